```python
import math
import jax, jax.numpy as jnp
from jax import lax
import numpy as np

D_MODEL = 2048
BATCH = 4
SEQ = 2048
DEPTH = 1
DEC_BATCH = 32
DEC_SEQ = 1
PAST_LEN = 16384
PAGE_SIZE = 128

CHUNK = 128
A_GROUPS = 8
A_GROUP_DIM = D_MODEL // 16
A_WIDTH = A_GROUPS * A_GROUP_DIM
N_HEADS = 8
N_KV_HEADS = 2
HEAD_DIM = D_MODEL // 16
B_WIDTH = N_HEADS * HEAD_DIM
KV_WIDTH = N_KV_HEADS * HEAD_DIM
IDX_HEADS = 16
IDX_DIM = 64
TOPK_MAX = 256
ROPE_THETA = 10000.0
EPS = 1e-6
Q_BLOCK = 128
NEG = -1e30

SPLITS = (A_WIDTH, A_WIDTH, A_WIDTH,
          B_WIDTH, KV_WIDTH, KV_WIDTH, B_WIDTH,
          IDX_HEADS * IDX_DIM, IDX_DIM, IDX_HEADS,
          D_MODEL, D_MODEL)
IN_WIDTH = sum(SPLITS)

kernel_name = "gmlp_dsa_hybrid_step"


def rmsnorm(x, g):
    xf = x.astype(jnp.float32)
    y = xf * lax.rsqrt(jnp.mean(xf * xf, axis=-1, keepdims=True) + EPS)
    return (y * g.astype(jnp.float32)).astype(x.dtype)


def layernorm(x, g, b):
    xf = x.astype(jnp.float32)
    mu = jnp.mean(xf, axis=-1, keepdims=True)
    var = jnp.mean(jnp.square(xf - mu), axis=-1, keepdims=True)
    y = (xf - mu) * lax.rsqrt(var + EPS)
    return (y * g.astype(jnp.float32) + b.astype(jnp.float32)).astype(x.dtype)


def rope(x, pos):
    half = x.shape[-1] // 2
    inv = ROPE_THETA ** (-jnp.arange(half, dtype=jnp.float32) / half)
    ang = pos.astype(jnp.float32)[:, None] * inv[None, :]
    cos = jnp.cos(ang)[:, None, :]
    sin = jnp.sin(ang)[:, None, :]
    xf = x.astype(jnp.float32)
    x1, x2 = xf[..., :half], xf[..., half:]
    return jnp.concatenate([x1 * cos - x2 * sin, x2 * cos + x1 * sin], axis=-1).astype(x.dtype)


def project(x, pos, pre_g, w_in):
    B, T, _ = x.shape
    xn = rmsnorm(x, pre_g)
    h = xn @ w_in
    offsets = np.cumsum(SPLITS)[:-1].tolist()
    ua, va, za, q, k, v, zb, qi, ki, wi, ga, gb = jnp.split(h, offsets, axis=-1)
    q = rope(q.reshape(B, T, N_HEADS, HEAD_DIM), pos)
    k = rope(k.reshape(B, T, N_KV_HEADS, HEAD_DIM), pos)
    v = v.reshape(B, T, N_KV_HEADS, HEAD_DIM)
    qi = rope(qi.reshape(B, T, IDX_HEADS, IDX_DIM), pos)
    ki = rope(ki.reshape(B, T, 1, IDX_DIM), pos)[:, :, 0]
    wi = wi * (IDX_HEADS ** -0.5)
    return ua, va, za, q, k, v, zb, qi, ki, wi, ga, gb


def spatial_gating(u, v, z, ln_g, ln_b, ws, bs):
    B, T, _ = v.shape
    vn = layernorm(v, ln_g, ln_b)
    n_chunks = -(-T // CHUNK)
    Tp = n_chunks * CHUNK
    vp = jnp.pad(vn, ((0, 0), (0, Tp - T), (0, 0))).reshape(B, n_chunks, CHUNK, A_GROUPS, A_GROUP_DIM)
    causal = jnp.tril(jnp.ones((CHUNK, CHUNK), dtype=bool))
    wm = jnp.where(causal[None], ws, 0.0).astype(v.dtype)
    s = jnp.einsum('gts,bcsgd->bctgd', wm, vp) + bs.T[None, None, :, :, None]
    s = s.reshape(B, Tp, A_WIDTH)[:, :T]
    y = u * s * jax.nn.silu(z)
    last = ((T - 1) // CHUNK) * CHUNK
    return y, vn[:, last:]


def indexer_scores(qi, wi, ki_all, qpos):
    logits = jnp.einsum('bthd,bsd->bths', qi, ki_all).astype(jnp.float32) * (IDX_DIM ** -0.5)
    score = jnp.einsum('bth,bths->bts', wi.astype(jnp.float32), jax.nn.relu(logits))
    kpos = jnp.arange(ki_all.shape[1])
    mask = kpos[None, None, :] <= qpos[None, :, None]
    return jnp.where(mask, score, -jnp.inf)


def sparse_attend(q, ks, vs, valid):
    B, T = q.shape[:2]
    qg = q.reshape(B, T, N_KV_HEADS, N_HEADS // N_KV_HEADS, HEAD_DIM)
    sc = jnp.einsum('btkgd,btskd->btkgs', qg, ks).astype(jnp.float32) * (HEAD_DIM ** -0.5)
    sc = jnp.where(valid[:, :, None, None, :], sc, NEG)
    p = jax.nn.softmax(sc, axis=-1).astype(vs.dtype)
    o = jnp.einsum('btkgs,btskd->btkgd', p, vs)
    return o.reshape(B, T, B_WIDTH)


def take_rows(a, i):
    return jax.vmap(lambda ab, ib: ab[ib])(a, i)


def prompt_attention(q, k, v, qi, ki, wi):
    B, S = q.shape[:2]
    top_k = min(TOPK_MAX, S // 4)
    n_blocks = S // Q_BLOCK

    def block(bi):
        s0 = bi * Q_BLOCK
        qb = lax.dynamic_slice_in_dim(q, s0, Q_BLOCK, axis=1)
        qib = lax.dynamic_slice_in_dim(qi, s0, Q_BLOCK, axis=1)
        wib = lax.dynamic_slice_in_dim(wi, s0, Q_BLOCK, axis=1)
        qpos = s0 + jnp.arange(Q_BLOCK)
        score = indexer_scores(qib, wib, ki, qpos)
        _, idx = lax.top_k(score, top_k)
        valid = idx <= qpos[None, :, None]
        return sparse_attend(qb, take_rows(k, idx), take_rows(v, idx), valid)

    outs = lax.map(block, jnp.arange(n_blocks))
    return jnp.moveaxis(outs, 0, 1).reshape(B, S, B_WIDTH)


def sample_attention(q, k, v, qi, ki, wi, ck, cv, cki, page_table):
    DB, DS = q.shape[:2]
    past = page_table.shape[1] * PAGE_SIZE
    L = past + DS
    top_k = min(TOPK_MAX, L // 4)
    past_ki = cki[page_table].reshape(DB, past, IDX_DIM)
    ki_all = jnp.concatenate([past_ki, ki], axis=1)
    qpos = past + jnp.arange(DS)
    score = indexer_scores(qi, wi, ki_all, qpos)
    _, idx = lax.top_k(score, top_k)
    valid = idx <= qpos[None, :, None]
    is_past = idx < past
    pidx = jnp.minimum(idx, past - 1)
    phys = jnp.take_along_axis(page_table, (pidx // PAGE_SIZE).reshape(DB, -1), axis=1).reshape(pidx.shape)
    off = pidx % PAGE_SIZE
    nidx = jnp.clip(idx - past, 0, DS - 1)
    sel = is_past[..., None, None]
    ks = jnp.where(sel, ck[phys, off], take_rows(k, nidx))
    vs = jnp.where(sel, cv[phys, off], take_rows(v, nidx))
    return sparse_attend(q, ks, vs, valid)


def merge(x, ya, ob, zb, ga, gb, w_oa, w_ob, w_out, post_g):
    pa = ya @ w_oa
    pb = (ob * jax.nn.silu(zb)) @ w_ob
    m = jax.nn.sigmoid(ga) * pa + jax.nn.sigmoid(gb) * pb
    return x + rmsnorm(m @ w_out, post_g)


def setup_inputs(seed: int = 0) -> dict:
    key = jax.random.key(seed)
    ks = jax.random.split(key, 20)
    f = jnp.float32
    n_pages = PAST_LEN // PAGE_SIZE
    n_pool = (5 * DEC_BATCH * n_pages + 3) // 4

    def nrm(k, shape, scale):
        return jax.random.normal(k, shape, f) * scale

    x_prompt = nrm(ks[0], (BATCH, SEQ, D_MODEL), 1.0)
    x_sample = nrm(ks[1], (DEC_BATCH, DEC_SEQ, D_MODEL), 1.0)
    cache_k = nrm(ks[2], (DEPTH, n_pool, PAGE_SIZE, N_KV_HEADS, HEAD_DIM), 1.0)
    cache_v = nrm(ks[3], (DEPTH, n_pool, PAGE_SIZE, N_KV_HEADS, HEAD_DIM), 1.0)
    cache_kidx = nrm(ks[4], (DEPTH, n_pool, PAGE_SIZE, IDX_DIM), 1.0)
    page_table = jax.random.permutation(ks[5], n_pool)[:DEC_BATCH * n_pages].reshape(DEC_BATCH, n_pages).astype(jnp.int32)
    pre_g = 1.0 + nrm(ks[6], (DEPTH, D_MODEL), 0.02)
    w_in = nrm(ks[7], (DEPTH, D_MODEL, IN_WIDTH), D_MODEL ** -0.5)
    a_ln_g = 1.0 + nrm(ks[8], (DEPTH, A_WIDTH), 0.02)
    a_ln_b = nrm(ks[9], (DEPTH, A_WIDTH), 0.02)
    a_ws = nrm(ks[10], (DEPTH, A_GROUPS, CHUNK, CHUNK), CHUNK ** -0.5)
    a_bs = 1.0 + nrm(ks[11], (DEPTH, A_GROUPS, CHUNK), 0.02)
    w_oa = nrm(ks[12], (DEPTH, A_WIDTH, D_MODEL), A_WIDTH ** -0.5)
    w_ob = nrm(ks[13], (DEPTH, B_WIDTH, D_MODEL), B_WIDTH ** -0.5)
    w_out = nrm(ks[14], (DEPTH, D_MODEL, D_MODEL), D_MODEL ** -0.5)
    post_g = 1.0 + nrm(ks[15], (DEPTH, D_MODEL), 0.02)
    return {"x_prompt": x_prompt, "x_sample": x_sample, "cache_k": cache_k, "cache_v": cache_v,
            "cache_kidx": cache_kidx, "page_table": page_table, "pre_g": pre_g, "w_in": w_in,
            "a_ln_g": a_ln_g, "a_ln_b": a_ln_b, "a_ws": a_ws, "a_bs": a_bs, "w_oa": w_oa,
            "w_ob": w_ob, "w_out": w_out, "post_g": post_g}


def reference(x_prompt, x_sample, cache_k, cache_v, cache_kidx, page_table, pre_g, w_in,
              a_ln_g, a_ln_b, a_ws, a_bs, w_oa, w_ob, w_out, post_g):
    S = x_prompt.shape[1]
    DS = x_sample.shape[1]
    past = page_table.shape[1] * PAGE_SIZE
    pos_p = jnp.arange(S)
    pos_s = past + jnp.arange(DS)
    hp, hs = x_prompt, x_sample
    kp_l, vp_l, kip_l, gvp_l = [], [], [], []
    ks_l, vs_l, kis_l, gvs_l = [], [], [], []
    for l in range(DEPTH):
        ua, va, za, q, k, v, zb, qi, ki, wi, ga, gb = project(hp, pos_p, pre_g[l], w_in[l])
        ya, gv = spatial_gating(ua, va, za, a_ln_g[l], a_ln_b[l], a_ws[l], a_bs[l])
        ob = prompt_attention(q, k, v, qi, ki, wi)
        hp = merge(hp, ya, ob, zb, ga, gb, w_oa[l], w_ob[l], w_out[l], post_g[l])
        kp_l.append(k); vp_l.append(v); kip_l.append(ki); gvp_l.append(gv)
        ua, va, za, q, k, v, zb, qi, ki, wi, ga, gb = project(hs, pos_s, pre_g[l], w_in[l])
        ya, gv = spatial_gating(ua, va, za, a_ln_g[l], a_ln_b[l], a_ws[l], a_bs[l])
        ob = sample_attention(q, k, v, qi, ki, wi, cache_k[l], cache_v[l], cache_kidx[l], page_table)
        hs = merge(hs, ya, ob, zb, ga, gb, w_oa[l], w_ob[l], w_out[l], post_g[l])
        ks_l.append(k); vs_l.append(v); kis_l.append(ki); gvs_l.append(gv)
    k_prompt = jnp.stack(kp_l, 0)
    v_prompt = jnp.stack(vp_l, 0)
    kidx_prompt = jnp.stack(kip_l, 0)
    gv_prompt = jnp.stack(gvp_l, 0)
    k_sample = jnp.stack(ks_l, 0)
    v_sample = jnp.stack(vs_l, 0)
    kidx_sample = jnp.stack(kis_l, 0)
    gv_sample = jnp.stack(gvs_l, 0)
    return (hp, hs, k_prompt, v_prompt, kidx_prompt, gv_prompt, k_sample, v_sample, kidx_sample, gv_sample)
```

```python
import functools

import jax
import jax.numpy as jnp
from jax import lax
from jax.experimental import pallas as pl
from jax.experimental.pallas import tpu as pltpu

F32 = jnp.float32
BF16 = jnp.bfloat16

D_MODEL = 2048
CHUNK = 128
A_GROUPS = 8
A_WIDTH = 1024
N_HEADS = 8
N_KV_HEADS = 2
HEAD_DIM = 128
B_WIDTH = 1024
KV_WIDTH = 256
IDX_HEADS = 16
IDX_DIM = 64
TOPK_MAX = 256
ROPE_THETA = 10000.0
EPS = 1e-6
PAGE_SIZE = 128
Q_BLOCK = 128
NEG = -1e30

LANES = 128
COL_TILE = 1024
T_UA, T_VA, T_ZA, T_Q, T_ZB, T_QI, T_GA, T_GB = 0, 1, 2, 3, 4, 5, 6, 8
N_MAIN_TILES = 10
TAIL_WIDTH = 640
VMEM_LIMIT = 56 * 1024 * 1024


def _cparams(sem):
    return pltpu.CompilerParams(dimension_semantics=sem, vmem_limit_bytes=VMEM_LIMIT)


def _rope_tables(pos, dim):
    half = dim // 2
    inv = ROPE_THETA ** (-jnp.arange(half, dtype=F32) / half)
    ang = pos.astype(F32)[:, None] * inv[None, :]
    cos = jnp.cos(ang)
    sin = jnp.sin(ang)
    reps = LANES // dim
    cos_t = jnp.tile(jnp.concatenate([cos, cos], axis=-1), (1, reps))
    sin_t = jnp.tile(jnp.concatenate([-sin, sin], axis=-1), (1, reps))
    return cos_t, sin_t


def _rope128(x, cos, sin):
    return x * cos + pltpu.roll(x, 64, 1) * sin


def _rope64(x, cos, sin):
    lane = lax.broadcasted_iota(jnp.int32, x.shape, 1)
    first = (lane % IDX_DIM) < (IDX_DIM // 2)
    partner = jnp.where(first, pltpu.roll(x, LANES - 32, 1), pltpu.roll(x, 32, 1))
    return x * cos + partner * sin


def _rmsnorm_rows(xf, g):
    ms = jnp.mean(xf * xf, axis=-1, keepdims=True)
    return xf * lax.rsqrt(ms + EPS) * g


def _proj_main_body(x_ref, g_ref, w_ref, cq_ref, sq_ref, ci_ref, si_ref, lng_ref, lnb_ref,
                    h_ref, gv_ref, xn_ref, *, gv_rows, tiles_per_seq):
    i = pl.program_id(0)
    j = pl.program_id(1)

    @pl.when(j == 0)
    def _():
        xn_ref[...] = _rmsnorm_rows(x_ref[...], g_ref[...]).astype(BF16)

    acc = jnp.dot(xn_ref[...], w_ref[...], preferred_element_type=F32)
    tm = acc.shape[0]

    @pl.when(j == T_UA)
    def _():
        h_ref[...] = acc.astype(BF16)

    @pl.when(j == T_VA)
    def _():
        mu = jnp.mean(acc, axis=-1, keepdims=True)
        d = acc - mu
        var = jnp.mean(d * d, axis=-1, keepdims=True)
        vn = d * lax.rsqrt(var + EPS) * lng_ref[...] + lnb_ref[...]
        h_ref[...] = vn.astype(BF16)

        @pl.when(i % tiles_per_seq == tiles_per_seq - 1)
        def _():
            gv_ref[...] = vn[tm - gv_rows:, :]

    @pl.when((j == T_ZA) | (j == T_ZB))
    def _():
        h_ref[...] = (acc * jax.nn.sigmoid(acc)).astype(BF16)

    @pl.when(j == T_Q)
    def _():
        cos = cq_ref[...]
        sin = sq_ref[...]
        for h in range(COL_TILE // LANES):
            sl = slice(h * LANES, (h + 1) * LANES)
            h_ref[:, sl] = _rope128(acc[:, sl], cos, sin).astype(BF16)

    @pl.when(j == T_QI)
    def _():
        cos = ci_ref[...]
        sin = si_ref[...]
        for h in range(COL_TILE // LANES):
            sl = slice(h * LANES, (h + 1) * LANES)
            h_ref[:, sl] = _rope64(acc[:, sl], cos, sin).astype(BF16)

    @pl.when(j >= T_GA)
    def _():
        h_ref[...] = jax.nn.sigmoid(acc).astype(BF16)


def _proj_main(x2, pre_g, w_main, cq, sq, ci, si, ln_g, ln_b, *, tm, seq_rows, gv_rows):
    m = x2.shape[0]
    tiles_per_seq = seq_rows // tm
    n_seq = m // seq_rows
    body = functools.partial(_proj_main_body, gv_rows=gv_rows, tiles_per_seq=tiles_per_seq)
    tab = pl.BlockSpec((tm, LANES), lambda i, j: (i % tiles_per_seq, 0))
    row = lambda n: pl.BlockSpec((1, n), lambda i, j: (0, 0))
    return pl.pallas_call(
        body,
        grid=(m // tm, N_MAIN_TILES),
        in_specs=[
            pl.BlockSpec((tm, D_MODEL), lambda i, j: (i, 0)),
            row(D_MODEL),
            pl.BlockSpec((D_MODEL, COL_TILE), lambda i, j: (0, j)),
            tab, tab, tab, tab,
            row(A_WIDTH), row(A_WIDTH),
        ],
        out_specs=[
            pl.BlockSpec((tm, COL_TILE), lambda i, j: (i, j)),
            pl.BlockSpec((gv_rows, A_WIDTH), lambda i, j: (i // tiles_per_seq, 0)),
        ],
        out_shape=[
            jax.ShapeDtypeStruct((m, N_MAIN_TILES * COL_TILE), BF16),
            jax.ShapeDtypeStruct((n_seq * gv_rows, A_WIDTH), F32),
        ],
        scratch_shapes=[pltpu.VMEM((tm, D_MODEL), BF16)],
        compiler_params=_cparams(("arbitrary", "arbitrary")),
        name="proj_main",
    )(x2, pre_g, w_main, cq, sq, ci, si, ln_g, ln_b)


def _proj_tail_body(x_ref, g_ref, w_ref, ck_ref, sk_ref, ci_ref, si_ref,
                    k_ref, v_ref, ki_ref, kb_ref, vb_ref, kie_ref, kio_ref, ws_ref):
    xn = _rmsnorm_rows(x_ref[...], g_ref[...]).astype(BF16)
    acc = jnp.dot(xn, w_ref[...], preferred_element_type=F32)
    cos = ck_ref[...]
    sin = sk_ref[...]
    for kh in range(N_KV_HEADS):
        sl = slice(kh * HEAD_DIM, (kh + 1) * HEAD_DIM)
        r = _rope128(acc[:, sl], cos, sin)
        k_ref[:, sl] = r
        kb_ref[:, sl] = r.astype(BF16)
    v = acc[:, KV_WIDTH:2 * KV_WIDTH]
    v_ref[...] = v
    vb_ref[...] = v.astype(BF16)
    t = acc[:, 2 * KV_WIDTH:2 * KV_WIDTH + LANES]
    r = _rope64(t, ci_ref[...], si_ref[...])
    ki_ref[...] = r[:, :IDX_DIM]
    lane = lax.broadcasted_iota(jnp.int32, r.shape, 1)
    ke = jnp.where(lane < IDX_DIM, r, 0.0)
    kie_ref[...] = ke.astype(BF16)
    kio_ref[...] = pltpu.roll(ke, IDX_DIM, 1).astype(BF16)
    ws_ref[...] = t[:, IDX_DIM:IDX_DIM + IDX_HEADS] * (IDX_HEADS ** -0.5 * IDX_DIM ** -0.5)


def _proj_tail(x2, pre_g, w_tail, ck, sk, ci, si, *, tm, seq_rows):
    m = x2.shape[0]
    tiles_per_seq = seq_rows // tm
    tab = pl.BlockSpec((tm, LANES), lambda i: (i % tiles_per_seq, 0))
    blk = lambda n: pl.BlockSpec((tm, n), lambda i: (i, 0))
    return pl.pallas_call(
        _proj_tail_body,
        grid=(m // tm,),
        in_specs=[
            blk(D_MODEL),
            pl.BlockSpec((1, D_MODEL), lambda i: (0, 0)),
            pl.BlockSpec((D_MODEL, TAIL_WIDTH), lambda i: (0, 0)),
            tab, tab, tab, tab,
        ],
        out_specs=[blk(KV_WIDTH), blk(KV_WIDTH), blk(IDX_DIM), blk(KV_WIDTH), blk(KV_WIDTH),
                   blk(LANES), blk(LANES), blk(IDX_HEADS)],
        out_shape=[
            jax.ShapeDtypeStruct((m, KV_WIDTH), F32),
            jax.ShapeDtypeStruct((m, KV_WIDTH), F32),
            jax.ShapeDtypeStruct((m, IDX_DIM), F32),
            jax.ShapeDtypeStruct((m, KV_WIDTH), BF16),
            jax.ShapeDtypeStruct((m, KV_WIDTH), BF16),
            jax.ShapeDtypeStruct((m, LANES), BF16),
            jax.ShapeDtypeStruct((m, LANES), BF16),
            jax.ShapeDtypeStruct((m, IDX_HEADS), F32),
        ],
        compiler_params=_cparams(("arbitrary",)),
        name="proj_tail",
    )(x2, pre_g, w_tail, ck, sk, ci, si)


def _gate_body(u_ref, vn_ref, sz_ref, ws_ref, bst_ref, y_ref, *, n_chunks):
    rr = lax.broadcasted_iota(jnp.int32, (CHUNK, CHUNK), 0)
    cc = lax.broadcasted_iota(jnp.int32, (CHUNK, CHUNK), 1)
    tril = cc <= rr
    for g in range(A_GROUPS):
        wm = jnp.where(tril, ws_ref[g], 0.0).astype(BF16)
        b = bst_ref[:, g:g + 1]
        cs = slice(g * LANES, (g + 1) * LANES)
        for c in range(n_chunks):
            rs = slice(c * CHUNK, (c + 1) * CHUNK)
            s = jnp.dot(wm, vn_ref[rs, cs], preferred_element_type=F32) + b
            y = u_ref[rs, cs].astype(F32) * s * sz_ref[rs, cs].astype(F32)
            y_ref[rs, cs] = y.astype(BF16)


def _gate(h_main, a_ws, bs_t, *, tm):
    m = h_main.shape[0]
    body = functools.partial(_gate_body, n_chunks=tm // CHUNK)
    col = lambda t: pl.BlockSpec((tm, COL_TILE), lambda i, t=t: (i, t))
    return pl.pallas_call(
        body,
        grid=(m // tm,),
        in_specs=[col(T_UA), col(T_VA), col(T_ZA),
                  pl.BlockSpec((A_GROUPS, CHUNK, CHUNK), lambda i: (0, 0, 0)),
                  pl.BlockSpec((CHUNK, A_GROUPS), lambda i: (0, 0))],
        out_specs=pl.BlockSpec((tm, A_WIDTH), lambda i: (i, 0)),
        out_shape=jax.ShapeDtypeStruct((m, A_WIDTH), BF16),
        compiler_params=_cparams(("arbitrary",)),
        name="gate_prompt",
    )(h_main, h_main, h_main, a_ws, bs_t)


def _gate_row_body(u_ref, vn_ref, sz_ref, w0_ref, b0_ref, y_ref):
    s = vn_ref[...] * w0_ref[...] + b0_ref[...]
    y_ref[...] = (u_ref[...].astype(F32) * s * sz_ref[...].astype(F32)).astype(BF16)


def _gate_row(h_main, vn, w0, b0):
    m = h_main.shape[0]
    col = lambda t: pl.BlockSpec((m, COL_TILE), lambda i, t=t: (0, t))
    full = lambda r: pl.BlockSpec((r, A_WIDTH), lambda i: (0, 0))
    return pl.pallas_call(
        _gate_row_body,
        grid=(1,),
        in_specs=[col(T_UA), full(m), col(T_ZA), full(1), full(1)],
        out_specs=full(m),
        out_shape=jax.ShapeDtypeStruct((m, A_WIDTH), BF16),
        compiler_params=_cparams(("arbitrary",)),
        name="gate_sample",
    )(h_main, vn, h_main, w0, b0)


def _lane_bcast(col, rows):
    return jnp.broadcast_to(col, (rows, LANES))


def _select_bias(sc_ref, extra_ref, kp, row_min, row_max, *, n_cols):
    rows = sc_ref.shape[0]
    n_tiles = n_cols // LANES
    extra = None if extra_ref is None else extra_ref[...]

    def count(pred):
        acc = jnp.zeros((rows, LANES), F32)
        for c in range(n_tiles):
            acc = acc + jnp.where(pred(sc_ref[:, c * LANES:(c + 1) * LANES], c), 1.0, 0.0)
        tot = jnp.sum(acc, axis=1, keepdims=True)
        return _lane_bcast(tot, rows)

    def count_ge(x):
        c = count(lambda s, _: s >= x)
        if extra is not None:
            c = c + jnp.where(extra >= x, 1.0, 0.0)
        return c

    c_max = count_ge(row_max)
    top = c_max >= kp
    lo0 = jnp.where(top, row_max, row_min)
    c0 = jnp.where(top, c_max, count_ge(row_min))

    def active(lo, hi, c_lo):
        mid = 0.5 * lo + 0.5 * hi
        return (c_lo != kp) & (mid > lo) & (mid < hi)

    def cond(st):
        lo, hi, c_lo, it = st
        return (jnp.max(jnp.where(active(lo, hi, c_lo), 1.0, 0.0)) > 0.0) & (it < 400)

    def step(st):
        lo, hi, c_lo, it = st
        mid = 0.5 * lo + 0.5 * hi
        c = count_ge(mid)
        ge = c >= kp
        act = active(lo, hi, c_lo)
        lo_n = jnp.where(act & ge, mid, lo)
        hi_n = jnp.where(act & (~ge), mid, hi)
        c_n = jnp.where(act & ge, c, c_lo)
        return lo_n, hi_n, c_n, it + 1

    lo, _, c_lo, _ = lax.while_loop(cond, step, (lo0, row_max, c0, jnp.int32(0)))

    exact = jnp.max(jnp.where(c_lo != kp, 1.0, 0.0)) == 0.0

    @pl.when(exact)
    def _():
        for c in range(n_tiles):
            sl = slice(c * LANES, (c + 1) * LANES)
            sc_ref[:, sl] = jnp.where(sc_ref[:, sl] >= lo, 0.0, NEG)
        if extra is not None:
            extra_ref[...] = jnp.where(extra >= lo, 0.0, NEG)

    @pl.when(jnp.logical_not(exact))
    def _():
        n_gt = count(lambda s, _: s > lo)
        if extra is not None:
            n_gt = n_gt + jnp.where(extra > lo, 1.0, 0.0)
        need = kp - n_gt
        lane = lax.broadcasted_iota(jnp.int32, (rows, LANES), 1).astype(F32)

        def count_eq_upto(jx):
            c = count(lambda s, c: (s == lo) & (lane + float(c * LANES) <= jx))
            if extra is not None:
                c = c + jnp.where((extra == lo) & (jx >= float(n_cols)), 1.0, 0.0)
            return c

        last = n_cols if extra is not None else n_cols - 1
        j_lo = jnp.full((rows, LANES), -1.0, F32)
        j_hi = jnp.full((rows, LANES), float(last), F32)

        def jstep(_, st):
            a, b = st
            mid = jnp.floor(0.5 * (a + b))
            ok = count_eq_upto(mid) >= need
            return jnp.where(ok, a, mid), jnp.where(ok, mid, b)

        n_steps = max(1, (n_cols + 1).bit_length())
        _, j_hi = lax.fori_loop(0, n_steps, jstep, (j_lo, j_hi))
        for c in range(n_tiles):
            sl = slice(c * LANES, (c + 1) * LANES)
            s = sc_ref[:, sl]
            keep = (s > lo) | ((s == lo) & (lane + float(c * LANES) <= j_hi))
            sc_ref[:, sl] = jnp.where(keep, 0.0, NEG)
        if extra is not None:
            keep = (extra > lo) | ((extra == lo) & (j_hi >= float(n_cols)))
            extra_ref[...] = jnp.where(keep, 0.0, NEG)


def _prompt_attn_block(nk, qi_ref, ws_ref, kie_ref, kio_ref, q_ref, kb_ref, vb_ref, sz_ref,
                       o_ref, sc_ref, *, top_k, key_chunk):
    qb = pl.program_id(1)
    rows = Q_BLOCK
    n_pairs = IDX_HEADS // 2
    pos = qb * Q_BLOCK + lax.broadcasted_iota(jnp.int32, (rows, LANES), 0)

    qs = jnp.concatenate([qi_ref[:, p * LANES:(p + 1) * LANES] for p in range(n_pairs)], axis=0)
    wcols = [_lane_bcast(ws_ref[:, h:h + 1], rows) for h in range(IDX_HEADS)]
    nt = (((1,), (1,)), ((), ()))
    rmax = jnp.full((rows, LANES), -jnp.inf, F32)
    rmin = jnp.full((rows, LANES), jnp.inf, F32)
    for kc in range(0, nk, key_chunk):
        le = lax.dot_general(qs, kie_ref[kc:kc + key_chunk, :], nt, preferred_element_type=F32)
        lo = lax.dot_general(qs, kio_ref[kc:kc + key_chunk, :], nt, preferred_element_type=F32)
        for c in range(key_chunk // LANES):
            cs = slice(c * LANES, (c + 1) * LANES)
            acc = jnp.zeros((rows, LANES), F32)
            for p in range(n_pairs):
                rs = slice(p * rows, (p + 1) * rows)
                acc = acc + jnp.maximum(le[rs, cs], 0.0) * wcols[2 * p]
                acc = acc + jnp.maximum(lo[rs, cs], 0.0) * wcols[2 * p + 1]
            kpos = kc + c * LANES + lax.broadcasted_iota(jnp.int32, (rows, LANES), 1)
            causal = kpos <= pos
            rmax = jnp.maximum(rmax, jnp.where(causal, acc, -jnp.inf))
            rmin = jnp.minimum(rmin, jnp.where(causal, acc, jnp.inf))
            sc_ref[:, kc + c * LANES:kc + (c + 1) * LANES] = jnp.where(causal, acc, -jnp.inf)

    row_max = _lane_bcast(jnp.max(rmax, axis=1, keepdims=True), rows)
    row_min = _lane_bcast(jnp.min(rmin, axis=1, keepdims=True), rows)
    kp = jnp.minimum(pos + 1, top_k).astype(F32)
    _select_bias(sc_ref, None, kp, row_min, row_max, n_cols=nk)

    grp = N_HEADS // N_KV_HEADS
    bias = jnp.concatenate([sc_ref[:, :nk]] * grp, axis=0)
    for kh in range(N_KV_HEADS):
        qh = jnp.concatenate(
            [q_ref[:, (kh * grp + g) * HEAD_DIM:(kh * grp + g + 1) * HEAD_DIM] for g in range(grp)], axis=0)
        ks = slice(kh * HEAD_DIM, (kh + 1) * HEAD_DIM)
        s = lax.dot_general(qh, kb_ref[0:nk, ks], nt, preferred_element_type=F32)
        s = s * (HEAD_DIM ** -0.5) + bias
        m = jnp.max(s, axis=-1, keepdims=True)
        p = jnp.exp(s - m)
        l = jnp.sum(p, axis=-1, keepdims=True)
        o = jnp.dot(p.astype(BF16), vb_ref[0:nk, ks], preferred_element_type=F32) / l
        for g in range(grp):
            hs = slice((kh * grp + g) * HEAD_DIM, (kh * grp + g + 1) * HEAD_DIM)
            o_ref[:, hs] = (o[g * rows:(g + 1) * rows, :] * sz_ref[:, hs].astype(F32)).astype(BF16)


def _prompt_attn_body(*refs, seq, top_k, n_buckets, key_chunk):
    qb = pl.program_id(1)
    span = seq // n_buckets
    per = span // Q_BLOCK
    for c in range(n_buckets):
        @pl.when(qb // per == c)
        def _(c=c):
            _prompt_attn_block((c + 1) * span, *refs, top_k=top_k, key_chunk=key_chunk)


def _prompt_attn(h_main, ws, kie, kio, kb, vb, *, batch, seq):
    top_k = min(TOPK_MAX, seq // 4)
    n_qb = seq // Q_BLOCK
    key_chunk = min(512, seq)
    n_buckets = max(1, seq // 512)
    body = functools.partial(_prompt_attn_body, seq=seq, top_k=top_k, n_buckets=n_buckets, key_chunk=key_chunk)
    col = lambda t: pl.BlockSpec((Q_BLOCK, COL_TILE), lambda b, q, t=t: (b * n_qb + q, t))
    seqblk = lambda n: pl.BlockSpec((seq, n), lambda b, q: (b, 0))
    return pl.pallas_call(
        body,
        grid=(batch, n_qb),
        in_specs=[col(T_QI),
                  pl.BlockSpec((Q_BLOCK, IDX_HEADS), lambda b, q: (b * n_qb + q, 0)),
                  seqblk(LANES), seqblk(LANES),
                  col(T_Q), seqblk(KV_WIDTH), seqblk(KV_WIDTH), col(T_ZB)],
        out_specs=pl.BlockSpec((Q_BLOCK, B_WIDTH), lambda b, q: (b * n_qb + q, 0)),
        out_shape=jax.ShapeDtypeStruct((batch * seq, B_WIDTH), BF16),
        scratch_shapes=[pltpu.VMEM((Q_BLOCK, seq), F32)],
        compiler_params=_cparams(("arbitrary", "arbitrary")),
        name="attn_prompt",
    )(h_main, ws, kie, kio, h_main, kb, vb, h_main)


def _merge_body(x_ref, ya_ref, ob_ref, ga_ref, gb_ref, woa_ref, wob_ref, wout_ref, pg_ref, o_ref):
    pa = jnp.dot(ya_ref[...], woa_ref[...], preferred_element_type=F32)
    pb = jnp.dot(ob_ref[...], wob_ref[...], preferred_element_type=F32)
    mix = ga_ref[...].astype(F32) * pa + gb_ref[...].astype(F32) * pb
    r = jnp.dot(mix.astype(BF16), wout_ref[...], preferred_element_type=F32)
    o_ref[...] = x_ref[...] + _rmsnorm_rows(r, pg_ref[...])


def _merge(x2, ya, ob, h_main, w_oa, w_ob, w_out, post_g, *, tm):
    m = x2.shape[0]
    const = lambda r, c: pl.BlockSpec((r, c), lambda i: (0, 0), pipeline_mode=pl.Buffered(1))
    return pl.pallas_call(
        _merge_body,
        grid=(m // tm,),
        in_specs=[
            pl.BlockSpec((tm, D_MODEL), lambda i: (i, 0)),
            pl.BlockSpec((tm, A_WIDTH), lambda i: (i, 0)),
            pl.BlockSpec((tm, B_WIDTH), lambda i: (i, 0)),
            pl.BlockSpec((tm, D_MODEL), lambda i: (i, T_GA // 2)),
            pl.BlockSpec((tm, D_MODEL), lambda i: (i, T_GB // 2)),
            const(A_WIDTH, D_MODEL), const(B_WIDTH, D_MODEL), const(D_MODEL, D_MODEL),
            const(1, D_MODEL),
        ],
        out_specs=pl.BlockSpec((tm, D_MODEL), lambda i: (i, 0)),
        out_shape=jax.ShapeDtypeStruct((m, D_MODEL), F32),
        compiler_params=_cparams(("arbitrary",)),
        name="merge",
    )(x2, ya, ob, h_main, h_main, w_oa, w_ob, w_out, post_g)


_NT = (((1,), (1,)), ((), ()))
SCORE_PAGES = 8
ATTN_PAGES = 16


def _sample_scores_body(pt_ref, q_ref, w_ref, *rest):
    del pt_ref
    page_refs, o_ref = rest[:-1], rest[-1]
    q = q_ref[0]
    w = w_ref[0]
    for r, page_ref in enumerate(page_refs):
        kp = page_ref[0].astype(BF16)
        logit = lax.dot_general(q, kp, _NT, preferred_element_type=F32)
        o_ref[0, :, r * PAGE_SIZE:(r + 1) * PAGE_SIZE] = jnp.sum(
            jnp.maximum(logit, 0.0) * w, axis=0, keepdims=True)


def _sample_scores(page_table, qi3, ws3, kidx_pages):
    db, n_pages = page_table.shape
    steps = n_pages // SCORE_PAGES
    page = lambda r: pl.BlockSpec((1, PAGE_SIZE, IDX_DIM),
                                  lambda b, g, pt, r=r: (pt[b, g * SCORE_PAGES + r], 0, 0))
    grid_spec = pltpu.PrefetchScalarGridSpec(
        num_scalar_prefetch=1,
        grid=(db, steps),
        in_specs=[pl.BlockSpec((1, IDX_HEADS, IDX_DIM), lambda b, g, pt: (b, 0, 0)),
                  pl.BlockSpec((1, IDX_HEADS, 1), lambda b, g, pt: (b, 0, 0))]
                 + [page(r) for r in range(SCORE_PAGES)],
        out_specs=pl.BlockSpec((1, 1, SCORE_PAGES * PAGE_SIZE), lambda b, g, pt: (b, 0, g)),
    )
    return pl.pallas_call(
        _sample_scores_body,
        grid_spec=grid_spec,
        out_shape=jax.ShapeDtypeStruct((db, 1, n_pages * PAGE_SIZE), F32),
        compiler_params=_cparams(("arbitrary", "arbitrary")),
        name="scores_sample",
    )(page_table, qi3, ws3, *([kidx_pages] * SCORE_PAGES))


def _sample_select_body(sc_ref, qi_ref, kie_ref, ws_ref, bias_ref, bnew_ref, *, top_k):
    rows, past = sc_ref.shape
    lane = lax.broadcasted_iota(jnp.int32, (rows, LANES), 1)
    ki = kie_ref[...].astype(F32)
    ki = ki + pltpu.roll(ki, IDX_DIM, 1)
    s_new = jnp.zeros((rows, 1), F32)
    for p in range(IDX_HEADS // 2):
        prod = qi_ref[:, p * LANES:(p + 1) * LANES].astype(F32) * ki
        l_even = jnp.sum(jnp.where(lane < IDX_DIM, prod, 0.0), axis=1, keepdims=True)
        l_odd = jnp.sum(jnp.where(lane >= IDX_DIM, prod, 0.0), axis=1, keepdims=True)
        s_new = s_new + jnp.maximum(l_even, 0.0) * ws_ref[:, 2 * p:2 * p + 1]
        s_new = s_new + jnp.maximum(l_odd, 0.0) * ws_ref[:, 2 * p + 1:2 * p + 2]
    extra = _lane_bcast(s_new, rows)
    bnew_ref[...] = extra
    rmax = extra
    rmin = extra
    for c in range(past // LANES):
        sl = slice(c * LANES, (c + 1) * LANES)
        s = sc_ref[:, sl]
        bias_ref[:, sl] = s
        rmax = jnp.maximum(rmax, s)
        rmin = jnp.minimum(rmin, s)
    row_max = _lane_bcast(jnp.max(rmax, axis=1, keepdims=True), rows)
    row_min = _lane_bcast(jnp.min(rmin, axis=1, keepdims=True), rows)
    kp = jnp.full((rows, LANES), float(top_k), F32)
    _select_bias(bias_ref, bnew_ref, kp, row_min, row_max, n_cols=past)


def _sample_select(scores, h_main, kie, ws, *, top_k):
    db, past = scores.shape
    full = lambda r, c: pl.BlockSpec((r, c), lambda i: (0, 0))
    return pl.pallas_call(
        functools.partial(_sample_select_body, top_k=top_k),
        grid=(1,),
        in_specs=[full(db, past),
                  pl.BlockSpec((db, COL_TILE), lambda i: (0, T_QI)),
                  full(db, LANES), full(db, IDX_HEADS)],
        out_specs=[full(db, past), full(db, LANES)],
        out_shape=[jax.ShapeDtypeStruct((db, past), F32), jax.ShapeDtypeStruct((db, LANES), F32)],
        compiler_params=_cparams(("arbitrary",)),
        name="select_sample",
    )(scores, h_main, kie, ws)


def _sample_attn_body(pt_ref, q_ref, bias_ref, bnew_ref, kn_ref, vn_ref, sz_ref, *rest):
    del pt_ref
    n = ATTN_PAGES
    k_refs, v_refs = rest[:n], rest[n:2 * n]
    o_ref, m_ref, l_ref, acc_ref = rest[2 * n:]
    g = pl.program_id(1)
    grp = N_HEADS // N_KV_HEADS
    scale = HEAD_DIM ** -0.5

    @pl.when(g == 0)
    def _():
        m_ref[...] = jnp.full(m_ref.shape, -jnp.inf, F32)
        l_ref[...] = jnp.zeros(l_ref.shape, F32)
        acc_ref[...] = jnp.zeros(acc_ref.shape, F32)

    q = q_ref[0]
    n_keys = n * PAGE_SIZE
    head_row = lax.broadcasted_iota(jnp.int32, (N_HEADS, n_keys), 0)
    head_row_o = lax.broadcasted_iota(jnp.int32, (N_HEADS, HEAD_DIM), 0)
    kcat = [jnp.concatenate([r[0, :, kh, :] for r in k_refs], axis=0).astype(BF16) for kh in range(N_KV_HEADS)]
    vcat = [jnp.concatenate([r[0, :, kh, :] for r in v_refs], axis=0).astype(BF16) for kh in range(N_KV_HEADS)]
    s0 = lax.dot_general(q, kcat[0], _NT, preferred_element_type=F32)
    s1 = lax.dot_general(q, kcat[1], _NT, preferred_element_type=F32)
    s = jnp.where(head_row < grp, s0, s1) * scale + bias_ref[0]
    m_old = m_ref[...]
    m_new = jnp.maximum(m_old, _lane_bcast(jnp.max(s, axis=-1, keepdims=True), N_HEADS))
    alpha = jnp.exp(m_old - m_new)
    p = jnp.exp(s - m_new[:, :1])
    l_ref[...] = alpha * l_ref[...] + _lane_bcast(jnp.sum(p, axis=-1, keepdims=True), N_HEADS)
    pb = p.astype(BF16)
    pv0 = jnp.dot(pb, vcat[0], preferred_element_type=F32)
    pv1 = jnp.dot(pb, vcat[1], preferred_element_type=F32)
    acc_ref[...] = alpha * acc_ref[...] + jnp.where(head_row_o < grp, pv0, pv1)
    m_ref[...] = m_new

    @pl.when(g == pl.num_programs(1) - 1)
    def _():
        qf = q.astype(F32)
        kn = kn_ref[0].astype(F32)
        vn = vn_ref[0].astype(F32)
        kn8 = jnp.where(head_row_o < grp, kn[:, :HEAD_DIM], kn[:, HEAD_DIM:])
        vn8 = jnp.where(head_row_o < grp, vn[:, :HEAD_DIM], vn[:, HEAD_DIM:])
        sn = _lane_bcast(jnp.sum(qf * kn8, axis=-1, keepdims=True), N_HEADS) * scale + bnew_ref[0]
        m_old2 = m_ref[...]
        m_fin = jnp.maximum(m_old2, sn)
        a2 = jnp.exp(m_old2 - m_fin)
        pn = jnp.exp(sn - m_fin)
        l_fin = a2 * l_ref[...] + pn
        o = (a2 * acc_ref[...] + pn * vn8) / l_fin
        o_ref[0] = (o * sz_ref[0].astype(F32)).astype(BF16)


def _sample_attn(page_table, q3, bias3, bnew3, kn3, vn3, sz3, k_pages, v_pages):
    db, n_pages = page_table.shape
    steps = n_pages // ATTN_PAGES
    page = lambda r: pl.BlockSpec((1, PAGE_SIZE, N_KV_HEADS, HEAD_DIM),
                                  lambda b, g, pt, r=r: (pt[b, g * ATTN_PAGES + r], 0, 0, 0))
    per_b = lambda r, c: pl.BlockSpec((1, r, c), lambda b, g, pt: (b, 0, 0))
    grid_spec = pltpu.PrefetchScalarGridSpec(
        num_scalar_prefetch=1,
        grid=(db, steps),
        in_specs=[per_b(N_HEADS, HEAD_DIM),
                  pl.BlockSpec((1, 1, ATTN_PAGES * PAGE_SIZE), lambda b, g, pt: (b, 0, g)),
                  per_b(1, LANES), per_b(1, KV_WIDTH), per_b(1, KV_WIDTH), per_b(N_HEADS, HEAD_DIM)]
                 + [page(r) for r in range(ATTN_PAGES)] * 2,
        out_specs=per_b(N_HEADS, HEAD_DIM),
        scratch_shapes=[pltpu.VMEM((N_HEADS, LANES), F32)] * 3,
    )
    return pl.pallas_call(
        _sample_attn_body,
        grid_spec=grid_spec,
        out_shape=jax.ShapeDtypeStruct((db, N_HEADS, HEAD_DIM), BF16),
        compiler_params=_cparams(("arbitrary", "arbitrary")),
        name="attn_sample",
    )(page_table, q3, bias3, bnew3, kn3, vn3, sz3,
      *([k_pages] * ATTN_PAGES), *([v_pages] * ATTN_PAGES))


def _col_tile(h_main, t, n=1):
    return h_main[:, t * COL_TILE:(t + n) * COL_TILE]


def kernel(x_prompt, x_sample, cache_k, cache_v, cache_kidx, page_table, pre_g, w_in, a_ln_g, a_ln_b,
           a_ws, a_bs, w_oa, w_ob, w_out, post_g):
    batch, seq, _ = x_prompt.shape
    db, ds, _ = x_sample.shape
    depth = w_in.shape[0]
    n_pages = page_table.shape[1]
    past = n_pages * PAGE_SIZE
    assert ds == 1 and seq % 512 == 0 and n_pages % ATTN_PAGES == 0
    top_k_s = min(TOPK_MAX, (past + ds) // 4)
    assert top_k_s <= past

    pos_p = jnp.arange(seq)
    pos_s = past + (jnp.arange(db * ds) % ds)
    tabs_p = _rope_tables(pos_p, HEAD_DIM) + _rope_tables(pos_p, IDX_DIM)
    tabs_s = _rope_tables(pos_s, HEAD_DIM) + _rope_tables(pos_s, IDX_DIM)

    hp = x_prompt.reshape(batch * seq, D_MODEL)
    hs = x_sample.reshape(db * ds, D_MODEL)
    outs = [[] for _ in range(8)]
    for l in range(depth):
        w = w_in[l]
        w_main = jnp.concatenate([w[:, 0:4096], w[:, 4608:6656], w[:, 6736:10832]], axis=1).astype(BF16)
        w_tail = jnp.concatenate(
            [w[:, 4096:4608], w[:, 6656:6736], jnp.zeros((D_MODEL, TAIL_WIDTH - 592), F32)], axis=1).astype(BF16)
        woa, wob, wout = w_oa[l].astype(BF16), w_ob[l].astype(BF16), w_out[l].astype(BF16)
        g_pre, g_post = pre_g[l][None], post_g[l][None]
        ln_g, ln_b = a_ln_g[l][None], a_ln_b[l][None]

        h_main, gv = _proj_main(hp, g_pre, w_main, *tabs_p, ln_g, ln_b, tm=512, seq_rows=seq, gv_rows=CHUNK)
        k, v, ki, kb, vb, kie, kio, ws = _proj_tail(hp, g_pre, w_tail, *tabs_p, tm=512, seq_rows=seq)
        ya = _gate(h_main, a_ws[l], a_bs[l].T, tm=512)
        ob = _prompt_attn(h_main, ws, kie, kio, kb, vb, batch=batch, seq=seq)
        hp = _merge(hp, ya, ob, h_main, woa, wob, wout, g_post, tm=256)
        outs[0].append(k.reshape(batch, seq, N_KV_HEADS, HEAD_DIM))
        outs[1].append(v.reshape(batch, seq, N_KV_HEADS, HEAD_DIM))
        outs[2].append(ki.reshape(batch, seq, IDX_DIM))
        outs[3].append(gv.reshape(batch, CHUNK, A_WIDTH))

        m_s = db * ds
        h_main, gv = _proj_main(hs, g_pre, w_main, *tabs_s, ln_g, ln_b, tm=m_s, seq_rows=m_s, gv_rows=m_s)
        k, v, ki, kb, vb, kie, kio, ws = _proj_tail(hs, g_pre, w_tail, *tabs_s, tm=m_s, seq_rows=m_s)
        w0 = jnp.repeat(a_ws[l][:, 0, 0], LANES)[None]
        b0 = jnp.repeat(a_bs[l][:, 0], LANES)[None]
        ya = _gate_row(h_main, gv, w0, b0)
        qi3 = _col_tile(h_main, T_QI).reshape(db, IDX_HEADS, IDX_DIM)
        scores = _sample_scores(page_table, qi3, ws.reshape(db, IDX_HEADS, 1), cache_kidx[l])
        bias, bnew = _sample_select(scores.reshape(db, past), h_main, kie, ws, top_k=top_k_s)
        ob = _sample_attn(page_table,
                          _col_tile(h_main, T_Q).reshape(db, N_HEADS, HEAD_DIM),
                          bias.reshape(db, 1, past), bnew.reshape(db, 1, LANES),
                          kb.reshape(db, 1, KV_WIDTH), vb.reshape(db, 1, KV_WIDTH),
                          _col_tile(h_main, T_ZB).reshape(db, N_HEADS, HEAD_DIM),
                          cache_k[l], cache_v[l]).reshape(db, B_WIDTH)
        hs = _merge(hs, ya, ob, h_main, woa, wob, wout, g_post, tm=m_s)
        outs[4].append(k.reshape(db, ds, N_KV_HEADS, HEAD_DIM))
        outs[5].append(v.reshape(db, ds, N_KV_HEADS, HEAD_DIM))
        outs[6].append(ki.reshape(db, ds, IDX_DIM))
        outs[7].append(gv.reshape(db, ds, A_WIDTH))

    st = [jnp.stack(o, axis=0) for o in outs]
    return (hp.reshape(batch, seq, D_MODEL), hs.reshape(db, ds, D_MODEL),
            st[0], st[1], st[2], st[3], st[4], st[5], st[6], st[7])
```

```python
import functools

import jax
import jax.numpy as jnp
from jax import lax
from jax.experimental import pallas as pl
from jax.experimental.pallas import tpu as pltpu

F32 = jnp.float32
BF16 = jnp.bfloat16

D_MODEL = 2048
CHUNK = 128
A_GROUPS = 8
A_WIDTH = 1024
N_HEADS = 8
N_KV_HEADS = 2
HEAD_DIM = 128
B_WIDTH = 1024
KV_WIDTH = 256
IDX_HEADS = 16
IDX_DIM = 64
TOPK_MAX = 256
ROPE_THETA = 10000.0
EPS = 1e-6
PAGE_SIZE = 128
Q_BLOCK = 128
NEG = -1e30

LANES = 128
COL_TILE = 1024
T_UA, T_VA, T_ZA, T_Q, T_ZB, T_QI, T_GA, T_GB = 0, 1, 2, 3, 4, 5, 6, 8
N_MAIN_TILES = 10
TAIL_WIDTH = 640
VMEM_LIMIT = 56 * 1024 * 1024


def _cparams(sem):
    return pltpu.CompilerParams(dimension_semantics=sem, vmem_limit_bytes=VMEM_LIMIT)


def _rope_tables(pos, dim):
    half = dim // 2
    inv = ROPE_THETA ** (-jnp.arange(half, dtype=F32) / half)
    ang = pos.astype(F32)[:, None] * inv[None, :]
    cos = jnp.cos(ang)
    sin = jnp.sin(ang)
    reps = LANES // dim
    cos_t = jnp.tile(jnp.concatenate([cos, cos], axis=-1), (1, reps))
    sin_t = jnp.tile(jnp.concatenate([-sin, sin], axis=-1), (1, reps))
    return cos_t, sin_t


def _rope128(x, cos, sin):
    return x * cos + pltpu.roll(x, 64, 1) * sin


def _rope64(x, cos, sin):
    lane = lax.broadcasted_iota(jnp.int32, x.shape, 1)
    first = (lane % IDX_DIM) < (IDX_DIM // 2)
    partner = jnp.where(first, pltpu.roll(x, LANES - 32, 1), pltpu.roll(x, 32, 1))
    return x * cos + partner * sin


def _rmsnorm_rows(xf, g):
    ms = jnp.mean(xf * xf, axis=-1, keepdims=True)
    return xf * lax.rsqrt(ms + EPS) * g


def _proj_main_body(x_ref, g_ref, w_ref, cq_ref, sq_ref, ci_ref, si_ref, lng_ref, lnb_ref,
                    h_ref, gv_ref, xn_ref, *, gv_rows, tiles_per_seq):
    i = pl.program_id(0)
    j = pl.program_id(1)

    @pl.when(j == 0)
    def _():
        xn_ref[...] = _rmsnorm_rows(x_ref[...], g_ref[...]).astype(BF16)

    acc = jnp.dot(xn_ref[...], w_ref[...], preferred_element_type=F32)
    tm = acc.shape[0]

    @pl.when(j == T_UA)
    def _():
        h_ref[...] = acc.astype(BF16)

    @pl.when(j == T_VA)
    def _():
        mu = jnp.mean(acc, axis=-1, keepdims=True)
        d = acc - mu
        var = jnp.mean(d * d, axis=-1, keepdims=True)
        vn = d * lax.rsqrt(var + EPS) * lng_ref[...] + lnb_ref[...]
        h_ref[...] = vn.astype(BF16)

        @pl.when(i % tiles_per_seq == tiles_per_seq - 1)
        def _():
            gv_ref[...] = vn[tm - gv_rows:, :]

    @pl.when((j == T_ZA) | (j == T_ZB))
    def _():
        h_ref[...] = (acc * jax.nn.sigmoid(acc)).astype(BF16)

    @pl.when(j == T_Q)
    def _():
        cos = cq_ref[...]
        sin = sq_ref[...]
        for h in range(COL_TILE // LANES):
            sl = slice(h * LANES, (h + 1) * LANES)
            h_ref[:, sl] = _rope128(acc[:, sl], cos, sin).astype(BF16)

    @pl.when(j == T_QI)
    def _():
        cos = ci_ref[...]
        sin = si_ref[...]
        for h in range(COL_TILE // LANES):
            sl = slice(h * LANES, (h + 1) * LANES)
            h_ref[:, sl] = _rope64(acc[:, sl], cos, sin).astype(BF16)

    @pl.when(j >= T_GA)
    def _():
        h_ref[...] = jax.nn.sigmoid(acc).astype(BF16)


def _proj_main(x2, pre_g, w_main, cq, sq, ci, si, ln_g, ln_b, *, tm, seq_rows, gv_rows):
    m = x2.shape[0]
    tiles_per_seq = seq_rows // tm
    n_seq = m // seq_rows
    body = functools.partial(_proj_main_body, gv_rows=gv_rows, tiles_per_seq=tiles_per_seq)
    tab = pl.BlockSpec((tm, LANES), lambda i, j: (i % tiles_per_seq, 0))
    row = lambda n: pl.BlockSpec((1, n), lambda i, j: (0, 0))
    return pl.pallas_call(
        body,
        grid=(m // tm, N_MAIN_TILES),
        in_specs=[
            pl.BlockSpec((tm, D_MODEL), lambda i, j: (i, 0)),
            row(D_MODEL),
            pl.BlockSpec((D_MODEL, COL_TILE), lambda i, j: (0, j)),
            tab, tab, tab, tab,
            row(A_WIDTH), row(A_WIDTH),
        ],
        out_specs=[
            pl.BlockSpec((tm, COL_TILE), lambda i, j: (i, j)),
            pl.BlockSpec((gv_rows, A_WIDTH), lambda i, j: (i // tiles_per_seq, 0)),
        ],
        out_shape=[
            jax.ShapeDtypeStruct((m, N_MAIN_TILES * COL_TILE), BF16),
            jax.ShapeDtypeStruct((n_seq * gv_rows, A_WIDTH), F32),
        ],
        scratch_shapes=[pltpu.VMEM((tm, D_MODEL), BF16)],
        compiler_params=_cparams(("arbitrary", "arbitrary")),
        name="proj_main",
    )(x2, pre_g, w_main, cq, sq, ci, si, ln_g, ln_b)


def _proj_tail_body(x_ref, g_ref, w_ref, ck_ref, sk_ref, ci_ref, si_ref,
                    k_ref, v_ref, ki_ref, kb_ref, vb_ref, kie_ref, kio_ref, ws_ref):
    xn = _rmsnorm_rows(x_ref[...], g_ref[...]).astype(BF16)
    acc = jnp.dot(xn, w_ref[...], preferred_element_type=F32)
    cos = ck_ref[...]
    sin = sk_ref[...]
    for kh in range(N_KV_HEADS):
        sl = slice(kh * HEAD_DIM, (kh + 1) * HEAD_DIM)
        r = _rope128(acc[:, sl], cos, sin)
        k_ref[:, sl] = r
        kb_ref[:, sl] = r.astype(BF16)
    v = acc[:, KV_WIDTH:2 * KV_WIDTH]
    v_ref[...] = v
    vb_ref[...] = v.astype(BF16)
    t = acc[:, 2 * KV_WIDTH:2 * KV_WIDTH + LANES]
    r = _rope64(t, ci_ref[...], si_ref[...])
    ki_ref[...] = r[:, :IDX_DIM]
    lane = lax.broadcasted_iota(jnp.int32, r.shape, 1)
    ke = jnp.where(lane < IDX_DIM, r, 0.0)
    kie_ref[...] = ke.astype(BF16)
    kio_ref[...] = pltpu.roll(ke, IDX_DIM, 1).astype(BF16)
    ws_ref[...] = t[:, IDX_DIM:IDX_DIM + IDX_HEADS] * (IDX_HEADS ** -0.5 * IDX_DIM ** -0.5)


def _proj_tail(x2, pre_g, w_tail, ck, sk, ci, si, *, tm, seq_rows):
    m = x2.shape[0]
    tiles_per_seq = seq_rows // tm
    tab = pl.BlockSpec((tm, LANES), lambda i: (i % tiles_per_seq, 0))
    blk = lambda n: pl.BlockSpec((tm, n), lambda i: (i, 0))
    return pl.pallas_call(
        _proj_tail_body,
        grid=(m // tm,),
        in_specs=[
            blk(D_MODEL),
            pl.BlockSpec((1, D_MODEL), lambda i: (0, 0)),
            pl.BlockSpec((D_MODEL, TAIL_WIDTH), lambda i: (0, 0)),
            tab, tab, tab, tab,
        ],
        out_specs=[blk(KV_WIDTH), blk(KV_WIDTH), blk(IDX_DIM), blk(KV_WIDTH), blk(KV_WIDTH),
                   blk(LANES), blk(LANES), blk(IDX_HEADS)],
        out_shape=[
            jax.ShapeDtypeStruct((m, KV_WIDTH), F32),
            jax.ShapeDtypeStruct((m, KV_WIDTH), F32),
            jax.ShapeDtypeStruct((m, IDX_DIM), F32),
            jax.ShapeDtypeStruct((m, KV_WIDTH), BF16),
            jax.ShapeDtypeStruct((m, KV_WIDTH), BF16),
            jax.ShapeDtypeStruct((m, LANES), BF16),
            jax.ShapeDtypeStruct((m, LANES), BF16),
            jax.ShapeDtypeStruct((m, IDX_HEADS), F32),
        ],
        compiler_params=_cparams(("arbitrary",)),
        name="proj_tail",
    )(x2, pre_g, w_tail, ck, sk, ci, si)


def _gate_body(u_ref, vn_ref, sz_ref, ws_ref, bst_ref, y_ref, *, n_chunks):
    rr = lax.broadcasted_iota(jnp.int32, (CHUNK, CHUNK), 0)
    cc = lax.broadcasted_iota(jnp.int32, (CHUNK, CHUNK), 1)
    tril = cc <= rr
    for g in range(A_GROUPS):
        wm = jnp.where(tril, ws_ref[g], 0.0).astype(BF16)
        b = bst_ref[:, g:g + 1]
        cs = slice(g * LANES, (g + 1) * LANES)
        for c in range(n_chunks):
            rs = slice(c * CHUNK, (c + 1) * CHUNK)
            s = jnp.dot(wm, vn_ref[rs, cs], preferred_element_type=F32) + b
            y = u_ref[rs, cs].astype(F32) * s * sz_ref[rs, cs].astype(F32)
            y_ref[rs, cs] = y.astype(BF16)


def _gate(h_main, a_ws, bs_t, *, tm):
    m = h_main.shape[0]
    body = functools.partial(_gate_body, n_chunks=tm // CHUNK)
    col = lambda t: pl.BlockSpec((tm, COL_TILE), lambda i, t=t: (i, t))
    return pl.pallas_call(
        body,
        grid=(m // tm,),
        in_specs=[col(T_UA), col(T_VA), col(T_ZA),
                  pl.BlockSpec((A_GROUPS, CHUNK, CHUNK), lambda i: (0, 0, 0)),
                  pl.BlockSpec((CHUNK, A_GROUPS), lambda i: (0, 0))],
        out_specs=pl.BlockSpec((tm, A_WIDTH), lambda i: (i, 0)),
        out_shape=jax.ShapeDtypeStruct((m, A_WIDTH), BF16),
        compiler_params=_cparams(("arbitrary",)),
        name="gate_prompt",
    )(h_main, h_main, h_main, a_ws, bs_t)


def _gate_row_body(u_ref, vn_ref, sz_ref, w0_ref, b0_ref, y_ref):
    s = vn_ref[...] * w0_ref[...] + b0_ref[...]
    y_ref[...] = (u_ref[...].astype(F32) * s * sz_ref[...].astype(F32)).astype(BF16)


def _gate_row(h_main, vn, w0, b0):
    m = h_main.shape[0]
    col = lambda t: pl.BlockSpec((m, COL_TILE), lambda i, t=t: (0, t))
    full = lambda r: pl.BlockSpec((r, A_WIDTH), lambda i: (0, 0))
    return pl.pallas_call(
        _gate_row_body,
        grid=(1,),
        in_specs=[col(T_UA), full(m), col(T_ZA), full(1), full(1)],
        out_specs=full(m),
        out_shape=jax.ShapeDtypeStruct((m, A_WIDTH), BF16),
        compiler_params=_cparams(("arbitrary",)),
        name="gate_sample",
    )(h_main, vn, h_main, w0, b0)


def _lane_bcast(col, rows):
    return jnp.broadcast_to(col, (rows, LANES))


def _select_bias(sc_ref, extra_ref, kp, row_min, row_max, *, n_cols):
    rows = sc_ref.shape[0]
    n_tiles = n_cols // LANES
    extra = None if extra_ref is None else extra_ref[...]

    def count(pred):
        acc = jnp.zeros((rows, LANES), F32)
        for c in range(n_tiles):
            acc = acc + jnp.where(pred(sc_ref[:, c * LANES:(c + 1) * LANES], c), 1.0, 0.0)
        tot = jnp.sum(acc, axis=1, keepdims=True)
        return _lane_bcast(tot, rows)

    def count_ge(x):
        c = count(lambda s, _: s >= x)
        if extra is not None:
            c = c + jnp.where(extra >= x, 1.0, 0.0)
        return c

    c_max = count_ge(row_max)
    top = c_max >= kp
    lo0 = jnp.where(top, row_max, row_min)
    c0 = jnp.where(top, c_max, count_ge(row_min))

    def active(lo, hi, c_lo):
        mid = 0.5 * lo + 0.5 * hi
        return (c_lo != kp) & (mid > lo) & (mid < hi)

    def cond(st):
        lo, hi, c_lo, it = st
        return (jnp.max(jnp.where(active(lo, hi, c_lo), 1.0, 0.0)) > 0.0) & (it < 400)

    def step(st):
        lo, hi, c_lo, it = st
        mid = 0.5 * lo + 0.5 * hi
        c = count_ge(mid)
        ge = c >= kp
        act = active(lo, hi, c_lo)
        lo_n = jnp.where(act & ge, mid, lo)
        hi_n = jnp.where(act & (~ge), mid, hi)
        c_n = jnp.where(act & ge, c, c_lo)
        return lo_n, hi_n, c_n, it + 1

    lo, _, c_lo, _ = lax.while_loop(cond, step, (lo0, row_max, c0, jnp.int32(0)))

    exact = jnp.max(jnp.where(c_lo != kp, 1.0, 0.0)) == 0.0

    @pl.when(exact)
    def _():
        for c in range(n_tiles):
            sl = slice(c * LANES, (c + 1) * LANES)
            sc_ref[:, sl] = jnp.where(sc_ref[:, sl] >= lo, 0.0, NEG)
        if extra is not None:
            extra_ref[...] = jnp.where(extra >= lo, 0.0, NEG)

    @pl.when(jnp.logical_not(exact))
    def _():
        n_gt = count(lambda s, _: s > lo)
        if extra is not None:
            n_gt = n_gt + jnp.where(extra > lo, 1.0, 0.0)
        need = kp - n_gt
        lane = lax.broadcasted_iota(jnp.int32, (rows, LANES), 1).astype(F32)

        def count_eq_upto(jx):
            c = count(lambda s, c: (s == lo) & (lane + float(c * LANES) <= jx))
            if extra is not None:
                c = c + jnp.where((extra == lo) & (jx >= float(n_cols)), 1.0, 0.0)
            return c

        last = n_cols if extra is not None else n_cols - 1
        j_lo = jnp.full((rows, LANES), -1.0, F32)
        j_hi = jnp.full((rows, LANES), float(last), F32)

        def jstep(_, st):
            a, b = st
            mid = jnp.floor(0.5 * (a + b))
            ok = count_eq_upto(mid) >= need
            return jnp.where(ok, a, mid), jnp.where(ok, mid, b)

        n_steps = max(1, (n_cols + 1).bit_length())
        _, j_hi = lax.fori_loop(0, n_steps, jstep, (j_lo, j_hi))
        for c in range(n_tiles):
            sl = slice(c * LANES, (c + 1) * LANES)
            s = sc_ref[:, sl]
            keep = (s > lo) | ((s == lo) & (lane + float(c * LANES) <= j_hi))
            sc_ref[:, sl] = jnp.where(keep, 0.0, NEG)
        if extra is not None:
            keep = (extra > lo) | ((extra == lo) & (j_hi >= float(n_cols)))
            extra_ref[...] = jnp.where(keep, 0.0, NEG)


def _prompt_attn_block(nk, qi_ref, ws_ref, kie_ref, kio_ref, q_ref, kb_ref, vb_ref, sz_ref,
                       o_ref, sc_ref, *, top_k, key_chunk):
    qb = pl.program_id(1)
    rows = Q_BLOCK
    n_pairs = IDX_HEADS // 2
    pos = qb * Q_BLOCK + lax.broadcasted_iota(jnp.int32, (rows, LANES), 0)

    qs = jnp.concatenate([qi_ref[:, p * LANES:(p + 1) * LANES] for p in range(n_pairs)], axis=0)
    wcols = [_lane_bcast(ws_ref[:, h:h + 1], rows) for h in range(IDX_HEADS)]
    nt = (((1,), (1,)), ((), ()))
    rmax = jnp.full((rows, LANES), -jnp.inf, F32)
    rmin = jnp.full((rows, LANES), jnp.inf, F32)
    for kc in range(0, nk, key_chunk):
        le = lax.dot_general(qs, kie_ref[kc:kc + key_chunk, :], nt, preferred_element_type=F32)
        lo = lax.dot_general(qs, kio_ref[kc:kc + key_chunk, :], nt, preferred_element_type=F32)
        for c in range(key_chunk // LANES):
            cs = slice(c * LANES, (c + 1) * LANES)
            acc = jnp.zeros((rows, LANES), F32)
            for p in range(n_pairs):
                rs = slice(p * rows, (p + 1) * rows)
                acc = acc + jnp.maximum(le[rs, cs], 0.0) * wcols[2 * p]
                acc = acc + jnp.maximum(lo[rs, cs], 0.0) * wcols[2 * p + 1]
            kpos = kc + c * LANES + lax.broadcasted_iota(jnp.int32, (rows, LANES), 1)
            causal = kpos <= pos
            rmax = jnp.maximum(rmax, jnp.where(causal, acc, -jnp.inf))
            rmin = jnp.minimum(rmin, jnp.where(causal, acc, jnp.inf))
            sc_ref[:, kc + c * LANES:kc + (c + 1) * LANES] = jnp.where(causal, acc, -jnp.inf)

    row_max = _lane_bcast(jnp.max(rmax, axis=1, keepdims=True), rows)
    row_min = _lane_bcast(jnp.min(rmin, axis=1, keepdims=True), rows)
    kp = jnp.minimum(pos + 1, top_k).astype(F32)
    _select_bias(sc_ref, None, kp, row_min, row_max, n_cols=nk)

    grp = N_HEADS // N_KV_HEADS
    bias = jnp.concatenate([sc_ref[:, :nk]] * grp, axis=0)
    for kh in range(N_KV_HEADS):
        qh = jnp.concatenate(
            [q_ref[:, (kh * grp + g) * HEAD_DIM:(kh * grp + g + 1) * HEAD_DIM] for g in range(grp)], axis=0)
        ks = slice(kh * HEAD_DIM, (kh + 1) * HEAD_DIM)
        s = lax.dot_general(qh, kb_ref[0:nk, ks], nt, preferred_element_type=F32)
        s = s * (HEAD_DIM ** -0.5) + bias
        m = jnp.max(s, axis=-1, keepdims=True)
        p = jnp.exp(s - m)
        l = jnp.sum(p, axis=-1, keepdims=True)
        o = jnp.dot(p.astype(BF16), vb_ref[0:nk, ks], preferred_element_type=F32) / l
        for g in range(grp):
            hs = slice((kh * grp + g) * HEAD_DIM, (kh * grp + g + 1) * HEAD_DIM)
            o_ref[:, hs] = (o[g * rows:(g + 1) * rows, :] * sz_ref[:, hs].astype(F32)).astype(BF16)


def _prompt_attn_body(*refs, seq, top_k, n_buckets, key_chunk):
    qb = pl.program_id(1)
    span = seq // n_buckets
    per = span // Q_BLOCK
    for c in range(n_buckets):
        @pl.when(qb // per == c)
        def _(c=c):
            _prompt_attn_block((c + 1) * span, *refs, top_k=top_k, key_chunk=key_chunk)


def _prompt_attn(h_main, ws, kie, kio, kb, vb, *, batch, seq):
    top_k = min(TOPK_MAX, seq // 4)
    n_qb = seq // Q_BLOCK
    key_chunk = min(512, seq)
    n_buckets = max(1, seq // 512)
    body = functools.partial(_prompt_attn_body, seq=seq, top_k=top_k, n_buckets=n_buckets, key_chunk=key_chunk)
    col = lambda t: pl.BlockSpec((Q_BLOCK, COL_TILE), lambda b, q, t=t: (b * n_qb + q, t))
    seqblk = lambda n: pl.BlockSpec((seq, n), lambda b, q: (b, 0))
    return pl.pallas_call(
        body,
        grid=(batch, n_qb),
        in_specs=[col(T_QI),
                  pl.BlockSpec((Q_BLOCK, IDX_HEADS), lambda b, q: (b * n_qb + q, 0)),
                  seqblk(LANES), seqblk(LANES),
                  col(T_Q), seqblk(KV_WIDTH), seqblk(KV_WIDTH), col(T_ZB)],
        out_specs=pl.BlockSpec((Q_BLOCK, B_WIDTH), lambda b, q: (b * n_qb + q, 0)),
        out_shape=jax.ShapeDtypeStruct((batch * seq, B_WIDTH), BF16),
        scratch_shapes=[pltpu.VMEM((Q_BLOCK, seq), F32)],
        compiler_params=_cparams(("arbitrary", "arbitrary")),
        name="attn_prompt",
    )(h_main, ws, kie, kio, h_main, kb, vb, h_main)


def _merge_body(x_ref, ya_ref, ob_ref, ga_ref, gb_ref, woa_ref, wob_ref, wout_ref, pg_ref, o_ref):
    pa = jnp.dot(ya_ref[...], woa_ref[...], preferred_element_type=F32)
    pb = jnp.dot(ob_ref[...], wob_ref[...], preferred_element_type=F32)
    mix = ga_ref[...].astype(F32) * pa + gb_ref[...].astype(F32) * pb
    r = jnp.dot(mix.astype(BF16), wout_ref[...], preferred_element_type=F32)
    o_ref[...] = x_ref[...] + _rmsnorm_rows(r, pg_ref[...])


def _merge(x2, ya, ob, h_main, w_oa, w_ob, w_out, post_g, *, tm):
    m = x2.shape[0]
    const = lambda r, c: pl.BlockSpec((r, c), lambda i: (0, 0), pipeline_mode=pl.Buffered(1))
    return pl.pallas_call(
        _merge_body,
        grid=(m // tm,),
        in_specs=[
            pl.BlockSpec((tm, D_MODEL), lambda i: (i, 0)),
            pl.BlockSpec((tm, A_WIDTH), lambda i: (i, 0)),
            pl.BlockSpec((tm, B_WIDTH), lambda i: (i, 0)),
            pl.BlockSpec((tm, D_MODEL), lambda i: (i, T_GA // 2)),
            pl.BlockSpec((tm, D_MODEL), lambda i: (i, T_GB // 2)),
            const(A_WIDTH, D_MODEL), const(B_WIDTH, D_MODEL), const(D_MODEL, D_MODEL),
            const(1, D_MODEL),
        ],
        out_specs=pl.BlockSpec((tm, D_MODEL), lambda i: (i, 0)),
        out_shape=jax.ShapeDtypeStruct((m, D_MODEL), F32),
        compiler_params=_cparams(("arbitrary",)),
        name="merge",
    )(x2, ya, ob, h_main, h_main, w_oa, w_ob, w_out, post_g)


_NT = (((1,), (1,)), ((), ()))
SCORE_CHUNK = 2048


def _sample_scores_body(pt_ref, q_ref, w_ref, kidx_hbm, o_ref, buf, sem):
    db, n_pages = pt_ref.shape
    past = n_pages * PAGE_SIZE

    def page_copy(b, p, slot):
        dst = buf.at[slot, :, pl.ds(pl.multiple_of(p * PAGE_SIZE, PAGE_SIZE), PAGE_SIZE)]
        return pltpu.make_async_copy(kidx_hbm.at[pt_ref[b, p]], dst, sem.at[slot])

    def start_all(b, slot):
        def f(p, c):
            page_copy(b, p, slot).start()
            return c
        lax.fori_loop(0, n_pages, f, 0)

    def wait_all(b, slot):
        def f(p, c):
            page_copy(b, p, slot).wait()
            return c
        lax.fori_loop(0, n_pages, f, 0)

    start_all(0, 0)

    def per_seq(b, c):
        slot = b % 2

        @pl.when(b + 1 < db)
        def _():
            start_all(b + 1, 1 - slot)

        wait_all(b, slot)
        q = q_ref[b]
        w = w_ref[b]
        for ch in range(past // SCORE_CHUNK):
            cs = slice(ch * SCORE_CHUNK, (ch + 1) * SCORE_CHUNK)
            logit = jnp.dot(q, buf[slot, :, cs].astype(BF16), preferred_element_type=F32)
            o_ref[pl.ds(b, 1), cs] = jnp.sum(jnp.maximum(logit, 0.0) * w, axis=0, keepdims=True)
        return c

    lax.fori_loop(0, db, per_seq, 0)


def _sample_scores(page_table, qi3, ws3, kidx_pages_t):
    db, n_pages = page_table.shape
    past = n_pages * PAGE_SIZE
    grid_spec = pltpu.PrefetchScalarGridSpec(
        num_scalar_prefetch=1,
        grid=(1,),
        in_specs=[pl.BlockSpec((db, IDX_HEADS, IDX_DIM), lambda i, pt: (0, 0, 0)),
                  pl.BlockSpec((db, IDX_HEADS, 1), lambda i, pt: (0, 0, 0)),
                  pl.BlockSpec(memory_space=pl.ANY)],
        out_specs=pl.BlockSpec((db, past), lambda i, pt: (0, 0)),
        scratch_shapes=[pltpu.VMEM((2, IDX_DIM, past), F32), pltpu.SemaphoreType.DMA((2,))],
    )
    return pl.pallas_call(
        _sample_scores_body,
        grid_spec=grid_spec,
        out_shape=jax.ShapeDtypeStruct((db, past), F32),
        compiler_params=_cparams(("arbitrary",)),
        name="scores_sample",
    )(page_table, qi3, ws3, kidx_pages_t)


def _sample_select_body(sc_ref, qi_ref, kie_ref, ws_ref, bias_ref, bnew_ref, *, top_k):
    rows, past = sc_ref.shape
    lane = lax.broadcasted_iota(jnp.int32, (rows, LANES), 1)
    ki = kie_ref[...].astype(F32)
    ki = ki + pltpu.roll(ki, IDX_DIM, 1)
    s_new = jnp.zeros((rows, 1), F32)
    for p in range(IDX_HEADS // 2):
        prod = qi_ref[:, p * LANES:(p + 1) * LANES].astype(F32) * ki
        l_even = jnp.sum(jnp.where(lane < IDX_DIM, prod, 0.0), axis=1, keepdims=True)
        l_odd = jnp.sum(jnp.where(lane >= IDX_DIM, prod, 0.0), axis=1, keepdims=True)
        s_new = s_new + jnp.maximum(l_even, 0.0) * ws_ref[:, 2 * p:2 * p + 1]
        s_new = s_new + jnp.maximum(l_odd, 0.0) * ws_ref[:, 2 * p + 1:2 * p + 2]
    extra = _lane_bcast(s_new, rows)
    bnew_ref[...] = extra
    rmax = extra
    rmin = extra
    for c in range(past // LANES):
        sl = slice(c * LANES, (c + 1) * LANES)
        s = sc_ref[:, sl]
        bias_ref[:, sl] = s
        rmax = jnp.maximum(rmax, s)
        rmin = jnp.minimum(rmin, s)
    row_max = _lane_bcast(jnp.max(rmax, axis=1, keepdims=True), rows)
    row_min = _lane_bcast(jnp.min(rmin, axis=1, keepdims=True), rows)
    kp = jnp.full((rows, LANES), float(top_k), F32)
    _select_bias(bias_ref, bnew_ref, kp, row_min, row_max, n_cols=past)


def _sample_select(scores, h_main, kie, ws, *, top_k):
    db, past = scores.shape
    full = lambda r, c: pl.BlockSpec((r, c), lambda i: (0, 0))
    return pl.pallas_call(
        functools.partial(_sample_select_body, top_k=top_k),
        grid=(1,),
        in_specs=[full(db, past),
                  pl.BlockSpec((db, COL_TILE), lambda i: (0, T_QI)),
                  full(db, LANES), full(db, IDX_HEADS)],
        out_specs=[full(db, past), full(db, LANES)],
        out_shape=[jax.ShapeDtypeStruct((db, past), F32), jax.ShapeDtypeStruct((db, LANES), F32)],
        compiler_params=_cparams(("arbitrary",)),
        name="select_sample",
    )(scores, h_main, kie, ws)


def _sample_compact_body(m_ref, mt_ref, idx_ref, *, n_slots):
    n_pages = m_ref.shape[1]
    one = lambda pred: jnp.where(pred, 1.0, 0.0)
    kept = m_ref[0] == 0.0
    kept_t = mt_ref[0] == 0.0
    ri = lax.broadcasted_iota(jnp.int32, (PAGE_SIZE, PAGE_SIZE), 0)
    ci = lax.broadcasted_iota(jnp.int32, (PAGE_SIZE, PAGE_SIZE), 1)
    rp = lax.broadcasted_iota(jnp.int32, (n_pages, n_pages), 0)
    cp = lax.broadcasted_iota(jnp.int32, (n_pages, n_pages), 1)
    plt = jnp.dot(one(ci <= ri).astype(BF16), one(kept_t).astype(BF16), preferred_element_type=F32)
    n_row = plt[PAGE_SIZE - 1:PAGE_SIZE, :]
    n_col = _lane_bcast(jnp.sum(one(kept), axis=1, keepdims=True), n_pages)
    e_col = jnp.dot(one(cp <= rp).astype(BF16), n_col.astype(BF16), preferred_element_type=F32)
    n_row8 = jnp.broadcast_to(n_row, (8, n_pages))
    e_row8 = jnp.dot(n_row8.astype(BF16), one(rp <= cp).astype(BF16), preferred_element_type=F32)
    off_row8 = e_row8 - n_row8
    n_total = e_col[n_pages - 1:n_pages, :]
    page_id = lax.broadcasted_iota(jnp.int32, (n_pages, LANES), 0).astype(F32)
    for jt in range(n_slots // LANES):
        j = (lax.broadcasted_iota(jnp.int32, (1, LANES), 1) + jt * LANES).astype(F32)
        page_j = jnp.sum(one(e_col <= j), axis=0, keepdims=True)
        pick = one(page_id == page_j).astype(BF16)
        prefix_j = jnp.dot(plt.astype(BF16), pick, preferred_element_type=F32)
        off_j = jnp.dot(off_row8.astype(BF16), pick, preferred_element_type=F32)[0:1]
        local_j = jnp.sum(one(prefix_j <= j - off_j), axis=0, keepdims=True)
        pos = page_j * float(PAGE_SIZE) + local_j
        idx_ref[0, :, jt * LANES:(jt + 1) * LANES] = jnp.where(j < n_total, pos, -1.0).astype(jnp.int32)


def _sample_compact(bias3, bias3_t, *, n_slots):
    db, n_pages, _ = bias3.shape
    return pl.pallas_call(
        functools.partial(_sample_compact_body, n_slots=n_slots),
        grid=(db,),
        in_specs=[pl.BlockSpec((1, n_pages, PAGE_SIZE), lambda b: (b, 0, 0)),
                  pl.BlockSpec((1, PAGE_SIZE, n_pages), lambda b: (b, 0, 0))],
        out_specs=pl.BlockSpec((1, 1, n_slots), lambda b: (b, 0, 0)),
        out_shape=jax.ShapeDtypeStruct((db, 1, n_slots), jnp.int32),
        compiler_params=_cparams(("arbitrary",)),
        name="compact_sample",
    )(bias3, bias3_t)


def _sample_attn_body(pt_ref, idx_ref, q_ref, slot_ref, bnew_ref, kn_ref, vn_ref, sz_ref, k_hbm, v_hbm,
                      o_ref, kbuf, vbuf, sem):
    b = pl.program_id(0)
    nb = pl.num_programs(0)
    n_slots = kbuf.shape[1]
    buf = b % 2
    grp = N_HEADS // N_KV_HEADS
    scale = HEAD_DIM ** -0.5

    def row_copies(seq, j, to):
        s = idx_ref[seq, j]
        page = pt_ref[seq, s // PAGE_SIZE]
        off = s % PAGE_SIZE
        return (pltpu.make_async_copy(k_hbm.at[page, off], kbuf.at[to, j], sem.at[0, to]),
                pltpu.make_async_copy(v_hbm.at[page, off], vbuf.at[to, j], sem.at[1, to]))

    def start_all(seq, to):
        def f(j, c):
            ck, cv = row_copies(seq, j, to)
            ck.start()
            cv.start()
            return c
        lax.fori_loop(0, n_slots, f, 0)

    def wait_all(seq, to):
        def f(j, c):
            ck, cv = row_copies(seq, j, to)
            ck.wait()
            cv.wait()
            return c
        lax.fori_loop(0, n_slots, f, 0)

    @pl.when(b == 0)
    def _():
        start_all(0, 0)

    @pl.when(b + 1 < nb)
    def _():
        start_all(b + 1, 1 - buf)

    wait_all(b, buf)

    q = q_ref[0]
    head_s = lax.broadcasted_iota(jnp.int32, (N_HEADS, n_slots), 0)
    head_o = lax.broadcasted_iota(jnp.int32, (N_HEADS, HEAD_DIM), 0)
    k0, k1 = (kbuf[buf, :, kh, :].astype(BF16) for kh in range(N_KV_HEADS))
    v0, v1 = (vbuf[buf, :, kh, :].astype(BF16) for kh in range(N_KV_HEADS))
    s0 = lax.dot_general(q, k0, _NT, preferred_element_type=F32)
    s1 = lax.dot_general(q, k1, _NT, preferred_element_type=F32)
    s = jnp.where(head_s < grp, s0, s1) * scale + jnp.where(slot_ref[0] >= 0, 0.0, NEG)
    kn = kn_ref[0].astype(F32)
    vn = vn_ref[0].astype(F32)
    kn8 = jnp.where(head_o < grp, kn[:, :HEAD_DIM], kn[:, HEAD_DIM:])
    vn8 = jnp.where(head_o < grp, vn[:, :HEAD_DIM], vn[:, HEAD_DIM:])
    s_new = _lane_bcast(jnp.sum(q.astype(F32) * kn8, axis=-1, keepdims=True), N_HEADS) * scale + bnew_ref[0]
    m = jnp.maximum(_lane_bcast(jnp.max(s, axis=-1, keepdims=True), N_HEADS), s_new)
    p = jnp.exp(s - m[:, :1])
    p_new = jnp.exp(s_new - m)
    l = _lane_bcast(jnp.sum(p, axis=-1, keepdims=True), N_HEADS) + p_new
    pb = p.astype(BF16)
    pv = jnp.where(head_o < grp,
                   jnp.dot(pb, v0, preferred_element_type=F32),
                   jnp.dot(pb, v1, preferred_element_type=F32))
    o = (pv + p_new * vn8) / l
    o_ref[0] = (o * sz_ref[0].astype(F32)).astype(BF16)


def _sample_attn(page_table, idx, q3, slots3, bnew3, kn3, vn3, sz3, k_pages, v_pages):
    db, n_slots = idx.shape
    per_b = lambda r, c: pl.BlockSpec((1, r, c), lambda b, pt, ix: (b, 0, 0))
    grid_spec = pltpu.PrefetchScalarGridSpec(
        num_scalar_prefetch=2,
        grid=(db,),
        in_specs=[per_b(N_HEADS, HEAD_DIM), per_b(1, n_slots), per_b(1, LANES),
                  per_b(1, KV_WIDTH), per_b(1, KV_WIDTH), per_b(N_HEADS, HEAD_DIM),
                  pl.BlockSpec(memory_space=pl.ANY), pl.BlockSpec(memory_space=pl.ANY)],
        out_specs=per_b(N_HEADS, HEAD_DIM),
        scratch_shapes=[pltpu.VMEM((2, n_slots, N_KV_HEADS, HEAD_DIM), F32),
                        pltpu.VMEM((2, n_slots, N_KV_HEADS, HEAD_DIM), F32),
                        pltpu.SemaphoreType.DMA((2, 2))],
    )
    return pl.pallas_call(
        _sample_attn_body,
        grid_spec=grid_spec,
        out_shape=jax.ShapeDtypeStruct((db, N_HEADS, HEAD_DIM), BF16),
        compiler_params=_cparams(("arbitrary",)),
        name="attn_sample",
    )(page_table, idx, q3, slots3, bnew3, kn3, vn3, sz3, k_pages, v_pages)


def _col_tile(h_main, t, n=1):
    return h_main[:, t * COL_TILE:(t + n) * COL_TILE]


def kernel(x_prompt, x_sample, cache_k, cache_v, cache_kidx, page_table, pre_g, w_in, a_ln_g, a_ln_b,
           a_ws, a_bs, w_oa, w_ob, w_out, post_g):
    batch, seq, _ = x_prompt.shape
    db, ds, _ = x_sample.shape
    depth = w_in.shape[0]
    n_pages = page_table.shape[1]
    past = n_pages * PAGE_SIZE
    assert ds == 1 and seq % 512 == 0 and past % SCORE_CHUNK == 0
    top_k_s = min(TOPK_MAX, (past + ds) // 4)
    assert top_k_s <= past and top_k_s % LANES == 0

    pos_p = jnp.arange(seq)
    pos_s = past + (jnp.arange(db * ds) % ds)
    tabs_p = _rope_tables(pos_p, HEAD_DIM) + _rope_tables(pos_p, IDX_DIM)
    tabs_s = _rope_tables(pos_s, HEAD_DIM) + _rope_tables(pos_s, IDX_DIM)

    hp = x_prompt.reshape(batch * seq, D_MODEL)
    hs = x_sample.reshape(db * ds, D_MODEL)
    outs = [[] for _ in range(8)]
    for l in range(depth):
        w = w_in[l]
        w_main = jnp.concatenate([w[:, 0:4096], w[:, 4608:6656], w[:, 6736:10832]], axis=1).astype(BF16)
        w_tail = jnp.concatenate(
            [w[:, 4096:4608], w[:, 6656:6736], jnp.zeros((D_MODEL, TAIL_WIDTH - 592), F32)], axis=1).astype(BF16)
        woa, wob, wout = w_oa[l].astype(BF16), w_ob[l].astype(BF16), w_out[l].astype(BF16)
        g_pre, g_post = pre_g[l][None], post_g[l][None]
        ln_g, ln_b = a_ln_g[l][None], a_ln_b[l][None]

        h_main, gv = _proj_main(hp, g_pre, w_main, *tabs_p, ln_g, ln_b, tm=512, seq_rows=seq, gv_rows=CHUNK)
        k, v, ki, kb, vb, kie, kio, ws = _proj_tail(hp, g_pre, w_tail, *tabs_p, tm=512, seq_rows=seq)
        ya = _gate(h_main, a_ws[l], a_bs[l].T, tm=512)
        ob = _prompt_attn(h_main, ws, kie, kio, kb, vb, batch=batch, seq=seq)
        hp = _merge(hp, ya, ob, h_main, woa, wob, wout, g_post, tm=256)
        outs[0].append(k.reshape(batch, seq, N_KV_HEADS, HEAD_DIM))
        outs[1].append(v.reshape(batch, seq, N_KV_HEADS, HEAD_DIM))
        outs[2].append(ki.reshape(batch, seq, IDX_DIM))
        outs[3].append(gv.reshape(batch, CHUNK, A_WIDTH))

        m_s = db * ds
        h_main, gv = _proj_main(hs, g_pre, w_main, *tabs_s, ln_g, ln_b, tm=m_s, seq_rows=m_s, gv_rows=m_s)
        k, v, ki, kb, vb, kie, kio, ws = _proj_tail(hs, g_pre, w_tail, *tabs_s, tm=m_s, seq_rows=m_s)
        w0 = jnp.repeat(a_ws[l][:, 0, 0], LANES)[None]
        b0 = jnp.repeat(a_bs[l][:, 0], LANES)[None]
        ya = _gate_row(h_main, gv, w0, b0)
        qi3 = _col_tile(h_main, T_QI).reshape(db, IDX_HEADS, IDX_DIM)
        kidx_t = jnp.swapaxes(cache_kidx[l], 1, 2)
        scores = _sample_scores(page_table, qi3, ws.reshape(db, IDX_HEADS, 1), kidx_t)
        bias, bnew = _sample_select(scores, h_main, kie, ws, top_k=top_k_s)
        bias3 = bias.reshape(db, n_pages, PAGE_SIZE)
        slots3 = _sample_compact(bias3, jnp.swapaxes(bias3, 1, 2), n_slots=top_k_s)
        ob = _sample_attn(page_table, jnp.maximum(slots3.reshape(db, top_k_s), 0),
                          _col_tile(h_main, T_Q).reshape(db, N_HEADS, HEAD_DIM),
                          slots3, bnew.reshape(db, 1, LANES),
                          kb.reshape(db, 1, KV_WIDTH), vb.reshape(db, 1, KV_WIDTH),
                          _col_tile(h_main, T_ZB).reshape(db, N_HEADS, HEAD_DIM),
                          cache_k[l], cache_v[l]).reshape(db, B_WIDTH)
        hs = _merge(hs, ya, ob, h_main, woa, wob, wout, g_post, tm=m_s)
        outs[4].append(k.reshape(db, ds, N_KV_HEADS, HEAD_DIM))
        outs[5].append(v.reshape(db, ds, N_KV_HEADS, HEAD_DIM))
        outs[6].append(ki.reshape(db, ds, IDX_DIM))
        outs[7].append(gv.reshape(db, ds, A_WIDTH))

    st = [jnp.stack(o, axis=0) for o in outs]
    return (hp.reshape(batch, seq, D_MODEL), hs.reshape(db, ds, D_MODEL),
            st[0], st[1], st[2], st[3], st[4], st[5], st[6], st[7])
```

```python
import functools

import jax
import jax.numpy as jnp
from jax import lax
from jax.experimental import pallas as pl
from jax.experimental.pallas import tpu as pltpu

F32 = jnp.float32
BF16 = jnp.bfloat16

D_MODEL = 2048
CHUNK = 128
A_GROUPS = 8
A_WIDTH = 1024
N_HEADS = 8
N_KV_HEADS = 2
HEAD_DIM = 128
B_WIDTH = 1024
KV_WIDTH = 256
IDX_HEADS = 16
IDX_DIM = 64
TOPK_MAX = 256
ROPE_THETA = 10000.0
EPS = 1e-6
PAGE_SIZE = 128
Q_BLOCK = 128
NEG = -1e30
LOG2_E = 1.4426950408889634

LANES = 128
COL_TILE = 1024
T_UA, T_VA, T_ZA, T_Q, T_ZB, T_QI, T_GA, T_GB = 0, 1, 2, 3, 4, 5, 6, 8
N_MAIN_TILES = 10
TAIL_WIDTH = 640
VMEM_LIMIT = 56 * 1024 * 1024


def _cparams(sem):
    return pltpu.CompilerParams(dimension_semantics=sem, vmem_limit_bytes=VMEM_LIMIT)


def _rope_tables(pos, dim):
    half = dim // 2
    inv = ROPE_THETA ** (-jnp.arange(half, dtype=F32) / half)
    ang = pos.astype(F32)[:, None] * inv[None, :]
    cos = jnp.cos(ang)
    sin = jnp.sin(ang)
    reps = LANES // dim
    cos_t = jnp.tile(jnp.concatenate([cos, cos], axis=-1), (1, reps))
    sin_t = jnp.tile(jnp.concatenate([-sin, sin], axis=-1), (1, reps))
    return cos_t, sin_t


def _rope128(x, cos, sin):
    return x * cos + pltpu.roll(x, 64, 1) * sin


def _rope64(x, cos, sin):
    lane = lax.broadcasted_iota(jnp.int32, x.shape, 1)
    first = (lane % IDX_DIM) < (IDX_DIM // 2)
    partner = jnp.where(first, pltpu.roll(x, LANES - 32, 1), pltpu.roll(x, 32, 1))
    return x * cos + partner * sin


def _rmsnorm_rows(xf, g):
    ms = jnp.mean(xf * xf, axis=-1, keepdims=True)
    return xf * lax.rsqrt(ms + EPS) * g


_TILE_KINDS = ("copy", "ln", "silu", "rope128", "silu", "rope64", "sigmoid", "sigmoid", "sigmoid", "sigmoid")


def _sigmoid(x):
    return 0.5 * jnp.tanh(0.5 * x) + 0.5


def _proj_main_body(x_ref, g_ref, w_ref, cq_ref, sq_ref, ci_ref, si_ref, lng_ref, lnb_ref,
                    h_ref, gv_ref, xn_ref, acc_ref, *, gv_rows, tiles_per_seq):
    i = pl.program_id(0)
    j = pl.program_id(1)
    tm = xn_ref.shape[0]

    def finish(kind, slot):
        acc = acc_ref[slot]
        if kind == "copy":
            h_ref[...] = acc.astype(BF16)
        elif kind == "ln":
            mu = jnp.mean(acc, axis=-1, keepdims=True)
            d = acc - mu
            var = jnp.mean(d * d, axis=-1, keepdims=True)
            vn = d * lax.rsqrt(var + EPS) * lng_ref[...] + lnb_ref[...]
            h_ref[...] = vn.astype(BF16)

            @pl.when(i % tiles_per_seq == tiles_per_seq - 1)
            def _():
                gv_ref[...] = vn[tm - gv_rows:, :]
        elif kind == "silu":
            h_ref[...] = (acc * _sigmoid(acc)).astype(BF16)
        elif kind == "sigmoid":
            h_ref[...] = _sigmoid(acc).astype(BF16)
        else:
            rope, cos, sin = ((_rope128, cq_ref[...], sq_ref[...]) if kind == "rope128"
                              else (_rope64, ci_ref[...], si_ref[...]))
            for h in range(COL_TILE // LANES):
                sl = slice(h * LANES, (h + 1) * LANES)
                h_ref[:, sl] = rope(acc[:, sl], cos, sin).astype(BF16)

    for step in range(N_MAIN_TILES + 1):
        @pl.when(j == step)
        def _(step=step):
            if step == 0:
                xn_ref[...] = _rmsnorm_rows(x_ref[...], g_ref[...]).astype(BF16)
            if step < N_MAIN_TILES:
                acc_ref[step % 2] = jnp.dot(xn_ref[...], w_ref[...], preferred_element_type=F32)
            if step > 0:
                finish(_TILE_KINDS[step - 1], (step - 1) % 2)


def _proj_main(x2, pre_g, w_main, cq, sq, ci, si, ln_g, ln_b, *, tm, seq_rows, gv_rows):
    m = x2.shape[0]
    tiles_per_seq = seq_rows // tm
    n_seq = m // seq_rows
    body = functools.partial(_proj_main_body, gv_rows=gv_rows, tiles_per_seq=tiles_per_seq)
    tab = pl.BlockSpec((tm, LANES), lambda i, j: (i % tiles_per_seq, 0))
    row = lambda n: pl.BlockSpec((1, n), lambda i, j: (0, 0))
    return pl.pallas_call(
        body,
        grid=(m // tm, N_MAIN_TILES + 1),
        in_specs=[
            pl.BlockSpec((tm, D_MODEL), lambda i, j: (i, 0)),
            row(D_MODEL),
            pl.BlockSpec((D_MODEL, COL_TILE), lambda i, j: (0, jnp.minimum(j, N_MAIN_TILES - 1))),
            tab, tab, tab, tab,
            row(A_WIDTH), row(A_WIDTH),
        ],
        out_specs=[
            pl.BlockSpec((tm, COL_TILE), lambda i, j: (i, jnp.maximum(j - 1, 0))),
            pl.BlockSpec((gv_rows, A_WIDTH), lambda i, j: (i // tiles_per_seq, 0)),
        ],
        out_shape=[
            jax.ShapeDtypeStruct((m, N_MAIN_TILES * COL_TILE), BF16),
            jax.ShapeDtypeStruct((n_seq * gv_rows, A_WIDTH), F32),
        ],
        scratch_shapes=[pltpu.VMEM((tm, D_MODEL), BF16), pltpu.VMEM((2, tm, COL_TILE), F32)],
        compiler_params=_cparams(("arbitrary", "arbitrary")),
        name="proj_main",
    )(x2, pre_g, w_main, cq, sq, ci, si, ln_g, ln_b)


def _proj_tail_body(x_ref, g_ref, w_ref, ck_ref, sk_ref, ci_ref, si_ref,
                    k_ref, v_ref, ki_ref, kb_ref, vb_ref, kie_ref, kio_ref, ws_ref):
    xn = _rmsnorm_rows(x_ref[...], g_ref[...]).astype(BF16)
    acc = jnp.dot(xn, w_ref[...], preferred_element_type=F32)
    cos = ck_ref[...]
    sin = sk_ref[...]
    for kh in range(N_KV_HEADS):
        sl = slice(kh * HEAD_DIM, (kh + 1) * HEAD_DIM)
        r = _rope128(acc[:, sl], cos, sin)
        k_ref[:, sl] = r
        kb_ref[:, sl] = r.astype(BF16)
    v = acc[:, KV_WIDTH:2 * KV_WIDTH]
    v_ref[...] = v
    vb_ref[...] = v.astype(BF16)
    t = acc[:, 2 * KV_WIDTH:2 * KV_WIDTH + LANES]
    r = _rope64(t, ci_ref[...], si_ref[...])
    ki_ref[...] = r[:, :IDX_DIM]
    lane = lax.broadcasted_iota(jnp.int32, r.shape, 1)
    ke = jnp.where(lane < IDX_DIM, r, 0.0)
    kie_ref[...] = ke.astype(BF16)
    kio_ref[...] = pltpu.roll(ke, IDX_DIM, 1).astype(BF16)
    ws_ref[...] = t[:, IDX_DIM:IDX_DIM + IDX_HEADS] * (IDX_HEADS ** -0.5 * IDX_DIM ** -0.5)


def _proj_tail(x2, pre_g, w_tail, ck, sk, ci, si, *, tm, seq_rows):
    m = x2.shape[0]
    tiles_per_seq = seq_rows // tm
    tab = pl.BlockSpec((tm, LANES), lambda i: (i % tiles_per_seq, 0))
    blk = lambda n: pl.BlockSpec((tm, n), lambda i: (i, 0))
    return pl.pallas_call(
        _proj_tail_body,
        grid=(m // tm,),
        in_specs=[
            blk(D_MODEL),
            pl.BlockSpec((1, D_MODEL), lambda i: (0, 0)),
            pl.BlockSpec((D_MODEL, TAIL_WIDTH), lambda i: (0, 0)),
            tab, tab, tab, tab,
        ],
        out_specs=[blk(KV_WIDTH), blk(KV_WIDTH), blk(IDX_DIM), blk(KV_WIDTH), blk(KV_WIDTH),
                   blk(LANES), blk(LANES), blk(IDX_HEADS)],
        out_shape=[
            jax.ShapeDtypeStruct((m, KV_WIDTH), F32),
            jax.ShapeDtypeStruct((m, KV_WIDTH), F32),
            jax.ShapeDtypeStruct((m, IDX_DIM), F32),
            jax.ShapeDtypeStruct((m, KV_WIDTH), BF16),
            jax.ShapeDtypeStruct((m, KV_WIDTH), BF16),
            jax.ShapeDtypeStruct((m, LANES), BF16),
            jax.ShapeDtypeStruct((m, LANES), BF16),
            jax.ShapeDtypeStruct((m, IDX_HEADS), F32),
        ],
        compiler_params=_cparams(("arbitrary",)),
        name="proj_tail",
    )(x2, pre_g, w_tail, ck, sk, ci, si)


def _gate_body(u_ref, vn_ref, sz_ref, ws_ref, bst_ref, y_ref, *, n_chunks):
    rr = lax.broadcasted_iota(jnp.int32, (CHUNK, CHUNK), 0)
    cc = lax.broadcasted_iota(jnp.int32, (CHUNK, CHUNK), 1)
    tril = cc <= rr
    for g in range(A_GROUPS):
        wm = jnp.where(tril, ws_ref[g], 0.0).astype(BF16)
        b = bst_ref[:, g:g + 1]
        cs = slice(g * LANES, (g + 1) * LANES)
        for c in range(n_chunks):
            rs = slice(c * CHUNK, (c + 1) * CHUNK)
            s = jnp.dot(wm, vn_ref[rs, cs], preferred_element_type=F32) + b
            y = u_ref[rs, cs].astype(F32) * s * sz_ref[rs, cs].astype(F32)
            y_ref[rs, cs] = y.astype(BF16)


def _gate(h_main, a_ws, bs_t, *, tm):
    m = h_main.shape[0]
    body = functools.partial(_gate_body, n_chunks=tm // CHUNK)
    col = lambda t: pl.BlockSpec((tm, COL_TILE), lambda i, t=t: (i, t))
    return pl.pallas_call(
        body,
        grid=(m // tm,),
        in_specs=[col(T_UA), col(T_VA), col(T_ZA),
                  pl.BlockSpec((A_GROUPS, CHUNK, CHUNK), lambda i: (0, 0, 0)),
                  pl.BlockSpec((CHUNK, A_GROUPS), lambda i: (0, 0))],
        out_specs=pl.BlockSpec((tm, A_WIDTH), lambda i: (i, 0)),
        out_shape=jax.ShapeDtypeStruct((m, A_WIDTH), BF16),
        compiler_params=_cparams(("arbitrary",)),
        name="gate_prompt",
    )(h_main, h_main, h_main, a_ws, bs_t)


def _gate_row_body(u_ref, vn_ref, sz_ref, w0_ref, b0_ref, y_ref):
    s = vn_ref[...] * w0_ref[...] + b0_ref[...]
    y_ref[...] = (u_ref[...].astype(F32) * s * sz_ref[...].astype(F32)).astype(BF16)


def _gate_row(h_main, vn, w0, b0):
    m = h_main.shape[0]
    col = lambda t: pl.BlockSpec((m, COL_TILE), lambda i, t=t: (0, t))
    full = lambda r: pl.BlockSpec((r, A_WIDTH), lambda i: (0, 0))
    return pl.pallas_call(
        _gate_row_body,
        grid=(1,),
        in_specs=[col(T_UA), full(m), col(T_ZA), full(1), full(1)],
        out_specs=full(m),
        out_shape=jax.ShapeDtypeStruct((m, A_WIDTH), BF16),
        compiler_params=_cparams(("arbitrary",)),
        name="gate_sample",
    )(h_main, vn, h_main, w0, b0)


def _lane_bcast(col, rows):
    return jnp.broadcast_to(col, (rows, LANES))


def _select_bias(sc_ref, extra_ref, kp, row_min, row_max, *, n_cols):
    rows = sc_ref.shape[0]
    n_tiles = n_cols // LANES
    extra = None if extra_ref is None else extra_ref[...]

    def count(pred):
        acc = jnp.zeros((rows, LANES), F32)
        for c in range(n_tiles):
            acc = acc + jnp.where(pred(sc_ref[:, c * LANES:(c + 1) * LANES], c), 1.0, 0.0)
        tot = jnp.sum(acc, axis=1, keepdims=True)
        return _lane_bcast(tot, rows)

    def count_ge(x):
        c = count(lambda s, _: s >= x)
        if extra is not None:
            c = c + jnp.where(extra >= x, 1.0, 0.0)
        return c

    c_max = count_ge(row_max)
    top = c_max >= kp
    lo0 = jnp.where(top, row_max, row_min)
    c0 = jnp.where(top, c_max, count_ge(row_min))

    def count_ge3(x1, x2, x3):
        a1 = jnp.zeros((rows, LANES), F32)
        a2 = jnp.zeros((rows, LANES), F32)
        a3 = jnp.zeros((rows, LANES), F32)
        for c in range(n_tiles):
            s = sc_ref[:, c * LANES:(c + 1) * LANES]
            a1 = a1 + jnp.where(s >= x1, 1.0, 0.0)
            a2 = a2 + jnp.where(s >= x2, 1.0, 0.0)
            a3 = a3 + jnp.where(s >= x3, 1.0, 0.0)
        res = []
        for a, x in ((a1, x1), (a2, x2), (a3, x3)):
            tot = _lane_bcast(jnp.sum(a, axis=1, keepdims=True), rows)
            if extra is not None:
                tot = tot + jnp.where(extra >= x, 1.0, 0.0)
            res.append(tot)
        return res

    def step(st):
        lo, hi, c_lo, _, it = st
        mid = 0.5 * lo + 0.5 * hi
        act = (c_lo != kp) & (mid > lo) & (mid < hi)
        any_act = jnp.max(jnp.where(act, 1.0, 0.0))
        clamp = lambda x: jnp.minimum(jnp.maximum(x, lo), hi)
        q1 = clamp(0.75 * lo + 0.25 * hi)
        q3 = clamp(0.25 * lo + 0.75 * hi)
        c1, c2, c3 = count_ge3(q1, mid, q3)
        g1, g2, g3 = c1 >= kp, c2 >= kp, c3 >= kp
        lo_n = jnp.where(g3, q3, jnp.where(g2, mid, jnp.where(g1, q1, lo)))
        c_n = jnp.where(g3, c3, jnp.where(g2, c2, jnp.where(g1, c1, c_lo)))
        hi_n = jnp.where(g3, hi, jnp.where(g2, q3, jnp.where(g1, mid, jnp.minimum(q1, mid))))
        return (jnp.where(act, lo_n, lo), jnp.where(act, hi_n, hi), jnp.where(act, c_n, c_lo),
                any_act, it + 1)

    def cond(st):
        return (st[3] > 0.0) & (st[4] < 400)

    lo, _, c_lo, _, _ = lax.while_loop(cond, step, (lo0, row_max, c0, jnp.float32(1.0), jnp.int32(0)))

    exact = jnp.max(jnp.where(c_lo != kp, 1.0, 0.0)) == 0.0

    @pl.when(exact)
    def _():
        for c in range(n_tiles):
            sl = slice(c * LANES, (c + 1) * LANES)
            sc_ref[:, sl] = jnp.where(sc_ref[:, sl] >= lo, 0.0, NEG)
        if extra is not None:
            extra_ref[...] = jnp.where(extra >= lo, 0.0, NEG)

    @pl.when(jnp.logical_not(exact))
    def _():
        n_gt = count(lambda s, _: s > lo)
        if extra is not None:
            n_gt = n_gt + jnp.where(extra > lo, 1.0, 0.0)
        need = kp - n_gt
        lane = lax.broadcasted_iota(jnp.int32, (rows, LANES), 1).astype(F32)

        def count_eq_upto(jx):
            c = count(lambda s, c: (s == lo) & (lane + float(c * LANES) <= jx))
            if extra is not None:
                c = c + jnp.where((extra == lo) & (jx >= float(n_cols)), 1.0, 0.0)
            return c

        last = n_cols if extra is not None else n_cols - 1
        j_lo = jnp.full((rows, LANES), -1.0, F32)
        j_hi = jnp.full((rows, LANES), float(last), F32)

        def jstep(_, st):
            a, b = st
            mid = jnp.floor(0.5 * (a + b))
            ok = count_eq_upto(mid) >= need
            return jnp.where(ok, a, mid), jnp.where(ok, mid, b)

        n_steps = max(1, (n_cols + 1).bit_length())
        _, j_hi = lax.fori_loop(0, n_steps, jstep, (j_lo, j_hi))
        for c in range(n_tiles):
            sl = slice(c * LANES, (c + 1) * LANES)
            s = sc_ref[:, sl]
            keep = (s > lo) | ((s == lo) & (lane + float(c * LANES) <= j_hi))
            sc_ref[:, sl] = jnp.where(keep, 0.0, NEG)
        if extra is not None:
            keep = (extra > lo) | ((extra == lo) & (j_hi >= float(n_cols)))
            extra_ref[...] = jnp.where(keep, 0.0, NEG)


def _prompt_attn_block(nk, qi_ref, ws_ref, kie_ref, kio_ref, q_ref, kb_ref, vb_ref, sz_ref,
                       o_ref, sc_ref, *, top_k, key_chunk):
    qb = pl.program_id(1)
    rows = Q_BLOCK
    n_pairs = IDX_HEADS // 2
    pos = qb * Q_BLOCK + lax.broadcasted_iota(jnp.int32, (rows, LANES), 0)

    qs = jnp.concatenate([qi_ref[:, p * LANES:(p + 1) * LANES] for p in range(n_pairs)], axis=0)
    wcols = [_lane_bcast(ws_ref[:, h:h + 1], rows) for h in range(IDX_HEADS)]
    nt = (((1,), (1,)), ((), ()))
    rmax = jnp.full((rows, LANES), -jnp.inf, F32)
    rmin = jnp.full((rows, LANES), jnp.inf, F32)
    for kc in range(0, nk, key_chunk):
        le = lax.dot_general(qs, kie_ref[kc:kc + key_chunk, :], nt, preferred_element_type=F32)
        lo = lax.dot_general(qs, kio_ref[kc:kc + key_chunk, :], nt, preferred_element_type=F32)
        for c in range(key_chunk // LANES):
            cs = slice(c * LANES, (c + 1) * LANES)
            acc = jnp.zeros((rows, LANES), F32)
            for p in range(n_pairs):
                rs = slice(p * rows, (p + 1) * rows)
                acc = acc + jnp.maximum(le[rs, cs], 0.0) * wcols[2 * p]
                acc = acc + jnp.maximum(lo[rs, cs], 0.0) * wcols[2 * p + 1]
            kpos = kc + c * LANES + lax.broadcasted_iota(jnp.int32, (rows, LANES), 1)
            causal = kpos <= pos
            rmax = jnp.maximum(rmax, jnp.where(causal, acc, -jnp.inf))
            rmin = jnp.minimum(rmin, jnp.where(causal, acc, jnp.inf))
            sc_ref[:, kc + c * LANES:kc + (c + 1) * LANES] = jnp.where(causal, acc, -jnp.inf)

    row_max = _lane_bcast(jnp.max(rmax, axis=1, keepdims=True), rows)
    row_min = _lane_bcast(jnp.min(rmin, axis=1, keepdims=True), rows)
    kp = jnp.minimum(pos + 1, top_k).astype(F32)
    _select_bias(sc_ref, None, kp, row_min, row_max, n_cols=nk)

    grp = N_HEADS // N_KV_HEADS
    bias = jnp.concatenate([sc_ref[:, :nk]] * grp, axis=0)
    for kh in range(N_KV_HEADS):
        qh = jnp.concatenate(
            [q_ref[:, (kh * grp + g) * HEAD_DIM:(kh * grp + g + 1) * HEAD_DIM] for g in range(grp)], axis=0)
        ks = slice(kh * HEAD_DIM, (kh + 1) * HEAD_DIM)
        s = lax.dot_general(qh, kb_ref[0:nk, ks], nt, preferred_element_type=F32) + bias
        m = jnp.max(s, axis=-1, keepdims=True)
        p = jnp.exp2((s - m) * (HEAD_DIM ** -0.5 * LOG2_E))
        l = jnp.sum(p, axis=-1, keepdims=True)
        o = jnp.dot(p.astype(BF16), vb_ref[0:nk, ks], preferred_element_type=F32) / l
        for g in range(grp):
            hs = slice((kh * grp + g) * HEAD_DIM, (kh * grp + g + 1) * HEAD_DIM)
            o_ref[:, hs] = (o[g * rows:(g + 1) * rows, :] * sz_ref[:, hs].astype(F32)).astype(BF16)


def _prompt_attn_body(*refs, seq, top_k, n_buckets, key_chunk):
    qb = pl.program_id(1)
    span = seq // n_buckets
    per = span // Q_BLOCK
    for c in range(n_buckets):
        @pl.when(qb // per == c)
        def _(c=c):
            _prompt_attn_block((c + 1) * span, *refs, top_k=top_k, key_chunk=key_chunk)


def _prompt_attn(h_main, ws, kie, kio, kb, vb, *, batch, seq):
    top_k = min(TOPK_MAX, seq // 4)
    n_qb = seq // Q_BLOCK
    key_chunk = min(512, seq)
    n_buckets = max(1, seq // 512)
    body = functools.partial(_prompt_attn_body, seq=seq, top_k=top_k, n_buckets=n_buckets, key_chunk=key_chunk)
    col = lambda t: pl.BlockSpec((Q_BLOCK, COL_TILE), lambda b, q, t=t: (b * n_qb + q, t))
    seqblk = lambda n: pl.BlockSpec((seq, n), lambda b, q: (b, 0))
    return pl.pallas_call(
        body,
        grid=(batch, n_qb),
        in_specs=[col(T_QI),
                  pl.BlockSpec((Q_BLOCK, IDX_HEADS), lambda b, q: (b * n_qb + q, 0)),
                  seqblk(LANES), seqblk(LANES),
                  col(T_Q), seqblk(KV_WIDTH), seqblk(KV_WIDTH), col(T_ZB)],
        out_specs=pl.BlockSpec((Q_BLOCK, B_WIDTH), lambda b, q: (b * n_qb + q, 0)),
        out_shape=jax.ShapeDtypeStruct((batch * seq, B_WIDTH), BF16),
        scratch_shapes=[pltpu.VMEM((Q_BLOCK, seq), F32)],
        compiler_params=_cparams(("arbitrary", "arbitrary")),
        name="attn_prompt",
    )(h_main, ws, kie, kio, h_main, kb, vb, h_main)


def _merge_body(x_ref, ya_ref, ob_ref, ga_ref, gb_ref, woa_ref, wob_ref, wout_ref, pg_ref, o_ref):
    pa = jnp.dot(ya_ref[...], woa_ref[...], preferred_element_type=F32)
    pb = jnp.dot(ob_ref[...], wob_ref[...], preferred_element_type=F32)
    mix = ga_ref[...].astype(F32) * pa + gb_ref[...].astype(F32) * pb
    r = jnp.dot(mix.astype(BF16), wout_ref[...], preferred_element_type=F32)
    o_ref[...] = x_ref[...] + _rmsnorm_rows(r, pg_ref[...])


def _merge(x2, ya, ob, h_main, w_oa, w_ob, w_out, post_g, *, tm):
    m = x2.shape[0]
    const = lambda r, c: pl.BlockSpec((r, c), lambda i: (0, 0), pipeline_mode=pl.Buffered(1))
    return pl.pallas_call(
        _merge_body,
        grid=(m // tm,),
        in_specs=[
            pl.BlockSpec((tm, D_MODEL), lambda i: (i, 0)),
            pl.BlockSpec((tm, A_WIDTH), lambda i: (i, 0)),
            pl.BlockSpec((tm, B_WIDTH), lambda i: (i, 0)),
            pl.BlockSpec((tm, D_MODEL), lambda i: (i, T_GA // 2)),
            pl.BlockSpec((tm, D_MODEL), lambda i: (i, T_GB // 2)),
            const(A_WIDTH, D_MODEL), const(B_WIDTH, D_MODEL), const(D_MODEL, D_MODEL),
            const(1, D_MODEL),
        ],
        out_specs=pl.BlockSpec((tm, D_MODEL), lambda i: (i, 0)),
        out_shape=jax.ShapeDtypeStruct((m, D_MODEL), F32),
        compiler_params=_cparams(("arbitrary",)),
        name="merge",
    )(x2, ya, ob, h_main, h_main, w_oa, w_ob, w_out, post_g)


_NT = (((1,), (1,)), ((), ()))
SCORE_CHUNK = 2048


def _sample_scores_body(pt_ref, q_ref, w_ref, kidx_hbm, o_ref, buf, sem):
    db, n_pages = pt_ref.shape
    past = n_pages * PAGE_SIZE

    def page_copy(b, p, slot):
        dst = buf.at[slot, :, pl.ds(pl.multiple_of(p * PAGE_SIZE, PAGE_SIZE), PAGE_SIZE)]
        return pltpu.make_async_copy(kidx_hbm.at[pt_ref[b, p]], dst, sem.at[slot])

    def start_all(b, slot):
        def f(p, c):
            page_copy(b, p, slot).start()
            return c
        lax.fori_loop(0, n_pages, f, 0, unroll=8)

    def wait_all(slot):
        for p in range(n_pages):
            dst = buf.at[slot, :, pl.ds(p * PAGE_SIZE, PAGE_SIZE)]
            pltpu.make_async_copy(kidx_hbm.at[0], dst, sem.at[slot]).wait()

    start_all(0, 0)

    def per_seq(b, c):
        slot = b % 2

        @pl.when(b + 1 < db)
        def _():
            start_all(b + 1, 1 - slot)

        wait_all(slot)
        q = q_ref[b]
        w = w_ref[b]
        for ch in range(past // SCORE_CHUNK):
            cs = slice(ch * SCORE_CHUNK, (ch + 1) * SCORE_CHUNK)
            logit = jnp.dot(q, buf[slot, :, cs].astype(BF16), preferred_element_type=F32)
            o_ref[pl.ds(b, 1), cs] = jnp.sum(jnp.maximum(logit, 0.0) * w, axis=0, keepdims=True)
        return c

    lax.fori_loop(0, db, per_seq, 0)


def _sample_scores(page_table, qi3, ws3, kidx_pages_t):
    db, n_pages = page_table.shape
    past = n_pages * PAGE_SIZE
    grid_spec = pltpu.PrefetchScalarGridSpec(
        num_scalar_prefetch=1,
        grid=(1,),
        in_specs=[pl.BlockSpec((db, IDX_HEADS, IDX_DIM), lambda i, pt: (0, 0, 0)),
                  pl.BlockSpec((db, IDX_HEADS, 1), lambda i, pt: (0, 0, 0)),
                  pl.BlockSpec(memory_space=pl.ANY)],
        out_specs=pl.BlockSpec((db, past), lambda i, pt: (0, 0)),
        scratch_shapes=[pltpu.VMEM((2, IDX_DIM, past), F32), pltpu.SemaphoreType.DMA((2,))],
    )
    return pl.pallas_call(
        _sample_scores_body,
        grid_spec=grid_spec,
        out_shape=jax.ShapeDtypeStruct((db, past), F32),
        compiler_params=_cparams(("arbitrary",)),
        name="scores_sample",
    )(page_table, qi3, ws3, kidx_pages_t)


def _sample_select_body(sc_ref, qi_ref, kie_ref, ws_ref, bias_ref, bnew_ref, *, top_k):
    rows, past = sc_ref.shape
    lane = lax.broadcasted_iota(jnp.int32, (rows, LANES), 1)
    ki = kie_ref[...].astype(F32)
    ki = ki + pltpu.roll(ki, IDX_DIM, 1)
    s_new = jnp.zeros((rows, 1), F32)
    for p in range(IDX_HEADS // 2):
        prod = qi_ref[:, p * LANES:(p + 1) * LANES].astype(F32) * ki
        l_even = jnp.sum(jnp.where(lane < IDX_DIM, prod, 0.0), axis=1, keepdims=True)
        l_odd = jnp.sum(jnp.where(lane >= IDX_DIM, prod, 0.0), axis=1, keepdims=True)
        s_new = s_new + jnp.maximum(l_even, 0.0) * ws_ref[:, 2 * p:2 * p + 1]
        s_new = s_new + jnp.maximum(l_odd, 0.0) * ws_ref[:, 2 * p + 1:2 * p + 2]
    extra = _lane_bcast(s_new, rows)
    bnew_ref[...] = extra
    rmax = extra
    rmin = extra
    for c in range(past // LANES):
        sl = slice(c * LANES, (c + 1) * LANES)
        s = sc_ref[:, sl]
        bias_ref[:, sl] = s
        rmax = jnp.maximum(rmax, s)
        rmin = jnp.minimum(rmin, s)
    row_max = _lane_bcast(jnp.max(rmax, axis=1, keepdims=True), rows)
    row_min = _lane_bcast(jnp.min(rmin, axis=1, keepdims=True), rows)
    kp = jnp.full((rows, LANES), float(top_k), F32)
    _select_bias(bias_ref, bnew_ref, kp, row_min, row_max, n_cols=past)


def _sample_select(scores, h_main, kie, ws, *, top_k):
    db, past = scores.shape
    full = lambda r, c: pl.BlockSpec((r, c), lambda i: (0, 0))
    return pl.pallas_call(
        functools.partial(_sample_select_body, top_k=top_k),
        grid=(1,),
        in_specs=[full(db, past),
                  pl.BlockSpec((db, COL_TILE), lambda i: (0, T_QI)),
                  full(db, LANES), full(db, IDX_HEADS)],
        out_specs=[full(db, past), full(db, LANES)],
        out_shape=[jax.ShapeDtypeStruct((db, past), F32), jax.ShapeDtypeStruct((db, LANES), F32)],
        compiler_params=_cparams(("arbitrary",)),
        name="select_sample",
    )(scores, h_main, kie, ws)


def _sample_compact_body(m_ref, mt_ref, idx_ref, *, n_slots):
    n_pages = m_ref.shape[1]
    one = lambda pred: jnp.where(pred, 1.0, 0.0)
    kept = m_ref[0] == 0.0
    kept_t = mt_ref[0] == 0.0
    ri = lax.broadcasted_iota(jnp.int32, (PAGE_SIZE, PAGE_SIZE), 0)
    ci = lax.broadcasted_iota(jnp.int32, (PAGE_SIZE, PAGE_SIZE), 1)
    rp = lax.broadcasted_iota(jnp.int32, (n_pages, n_pages), 0)
    cp = lax.broadcasted_iota(jnp.int32, (n_pages, n_pages), 1)
    plt = jnp.dot(one(ci <= ri).astype(BF16), one(kept_t).astype(BF16), preferred_element_type=F32)
    n_row = plt[PAGE_SIZE - 1:PAGE_SIZE, :]
    n_col = _lane_bcast(jnp.sum(one(kept), axis=1, keepdims=True), n_pages)
    e_col = jnp.dot(one(cp <= rp).astype(BF16), n_col.astype(BF16), preferred_element_type=F32)
    n_row8 = jnp.broadcast_to(n_row, (8, n_pages))
    e_row8 = jnp.dot(n_row8.astype(BF16), one(rp <= cp).astype(BF16), preferred_element_type=F32)
    off_row8 = e_row8 - n_row8
    n_total = e_col[n_pages - 1:n_pages, :]
    page_id = lax.broadcasted_iota(jnp.int32, (n_pages, LANES), 0).astype(F32)
    for jt in range(n_slots // LANES):
        j = (lax.broadcasted_iota(jnp.int32, (1, LANES), 1) + jt * LANES).astype(F32)
        page_j = jnp.sum(one(e_col <= j), axis=0, keepdims=True)
        pick = one(page_id == page_j).astype(BF16)
        prefix_j = jnp.dot(plt.astype(BF16), pick, preferred_element_type=F32)
        off_j = jnp.dot(off_row8.astype(BF16), pick, preferred_element_type=F32)[0:1]
        local_j = jnp.sum(one(prefix_j <= j - off_j), axis=0, keepdims=True)
        pos = page_j * float(PAGE_SIZE) + local_j
        idx_ref[0, :, jt * LANES:(jt + 1) * LANES] = jnp.where(j < n_total, pos, -1.0).astype(jnp.int32)


def _sample_compact(bias3, bias3_t, *, n_slots):
    db, n_pages, _ = bias3.shape
    return pl.pallas_call(
        functools.partial(_sample_compact_body, n_slots=n_slots),
        grid=(db,),
        in_specs=[pl.BlockSpec((1, n_pages, PAGE_SIZE), lambda b: (b, 0, 0)),
                  pl.BlockSpec((1, PAGE_SIZE, n_pages), lambda b: (b, 0, 0))],
        out_specs=pl.BlockSpec((1, 1, n_slots), lambda b: (b, 0, 0)),
        out_shape=jax.ShapeDtypeStruct((db, 1, n_slots), jnp.int32),
        compiler_params=_cparams(("arbitrary",)),
        name="compact_sample",
    )(bias3, bias3_t)


def _sample_attn_body(pt_ref, idx_ref, q_ref, slot_ref, bnew_ref, kn_ref, vn_ref, sz_ref, k_hbm, v_hbm,
                      o_ref, kbuf, vbuf, sem):
    b = pl.program_id(0)
    nb = pl.num_programs(0)
    n_slots = kbuf.shape[1]
    buf = b % 2
    grp = N_HEADS // N_KV_HEADS
    scale = HEAD_DIM ** -0.5

    def row_copies(seq, j, to):
        s = idx_ref[seq, j]
        page = pt_ref[seq, s // PAGE_SIZE]
        off = s % PAGE_SIZE
        return (pltpu.make_async_copy(k_hbm.at[page, off], kbuf.at[to, j], sem.at[0, to]),
                pltpu.make_async_copy(v_hbm.at[page, off], vbuf.at[to, j], sem.at[1, to]))

    def start_all(seq, to):
        def f(j, c):
            ck, cv = row_copies(seq, j, to)
            ck.start()
            cv.start()
            return c
        lax.fori_loop(0, n_slots, f, 0, unroll=8)

    def wait_all(to):
        for blk in range(n_slots // PAGE_SIZE):
            rows = pl.ds(blk * PAGE_SIZE, PAGE_SIZE)
            pltpu.make_async_copy(k_hbm.at[0], kbuf.at[to, rows], sem.at[0, to]).wait()
            pltpu.make_async_copy(v_hbm.at[0], vbuf.at[to, rows], sem.at[1, to]).wait()

    @pl.when(b == 0)
    def _():
        start_all(0, 0)

    @pl.when(b + 1 < nb)
    def _():
        start_all(b + 1, 1 - buf)

    wait_all(buf)

    q = q_ref[0]
    head_s = lax.broadcasted_iota(jnp.int32, (N_HEADS, n_slots), 0)
    head_o = lax.broadcasted_iota(jnp.int32, (N_HEADS, HEAD_DIM), 0)
    k0, k1 = (kbuf[buf, :, kh, :].astype(BF16) for kh in range(N_KV_HEADS))
    v0, v1 = (vbuf[buf, :, kh, :].astype(BF16) for kh in range(N_KV_HEADS))
    s0 = lax.dot_general(q, k0, _NT, preferred_element_type=F32)
    s1 = lax.dot_general(q, k1, _NT, preferred_element_type=F32)
    s = jnp.where(head_s < grp, s0, s1) * scale + jnp.where(slot_ref[0] >= 0, 0.0, NEG)
    kn = kn_ref[0].astype(F32)
    vn = vn_ref[0].astype(F32)
    kn8 = jnp.where(head_o < grp, kn[:, :HEAD_DIM], kn[:, HEAD_DIM:])
    vn8 = jnp.where(head_o < grp, vn[:, :HEAD_DIM], vn[:, HEAD_DIM:])
    s_new = _lane_bcast(jnp.sum(q.astype(F32) * kn8, axis=-1, keepdims=True), N_HEADS) * scale + bnew_ref[0]
    m = jnp.maximum(_lane_bcast(jnp.max(s, axis=-1, keepdims=True), N_HEADS), s_new)
    p = jnp.exp(s - m[:, :1])
    p_new = jnp.exp(s_new - m)
    l = _lane_bcast(jnp.sum(p, axis=-1, keepdims=True), N_HEADS) + p_new
    pb = p.astype(BF16)
    pv = jnp.where(head_o < grp,
                   jnp.dot(pb, v0, preferred_element_type=F32),
                   jnp.dot(pb, v1, preferred_element_type=F32))
    o = (pv + p_new * vn8) / l
    o_ref[0] = (o * sz_ref[0].astype(F32)).astype(BF16)


def _sample_attn(page_table, idx, q3, slots3, bnew3, kn3, vn3, sz3, k_pages, v_pages):
    db, n_slots = idx.shape
    per_b = lambda r, c: pl.BlockSpec((1, r, c), lambda b, pt, ix: (b, 0, 0))
    grid_spec = pltpu.PrefetchScalarGridSpec(
        num_scalar_prefetch=2,
        grid=(db,),
        in_specs=[per_b(N_HEADS, HEAD_DIM), per_b(1, n_slots), per_b(1, LANES),
                  per_b(1, KV_WIDTH), per_b(1, KV_WIDTH), per_b(N_HEADS, HEAD_DIM),
                  pl.BlockSpec(memory_space=pl.ANY), pl.BlockSpec(memory_space=pl.ANY)],
        out_specs=per_b(N_HEADS, HEAD_DIM),
        scratch_shapes=[pltpu.VMEM((2, n_slots, N_KV_HEADS, HEAD_DIM), F32),
                        pltpu.VMEM((2, n_slots, N_KV_HEADS, HEAD_DIM), F32),
                        pltpu.SemaphoreType.DMA((2, 2))],
    )
    return pl.pallas_call(
        _sample_attn_body,
        grid_spec=grid_spec,
        out_shape=jax.ShapeDtypeStruct((db, N_HEADS, HEAD_DIM), BF16),
        compiler_params=_cparams(("arbitrary",)),
        name="attn_sample",
    )(page_table, idx, q3, slots3, bnew3, kn3, vn3, sz3, k_pages, v_pages)


def _col_tile(h_main, t, n=1):
    return h_main[:, t * COL_TILE:(t + n) * COL_TILE]


def kernel(x_prompt, x_sample, cache_k, cache_v, cache_kidx, page_table, pre_g, w_in, a_ln_g, a_ln_b,
           a_ws, a_bs, w_oa, w_ob, w_out, post_g):
    batch, seq, _ = x_prompt.shape
    db, ds, _ = x_sample.shape
    depth = w_in.shape[0]
    n_pages = page_table.shape[1]
    past = n_pages * PAGE_SIZE
    assert ds == 1 and seq % 512 == 0 and past % SCORE_CHUNK == 0
    top_k_s = min(TOPK_MAX, (past + ds) // 4)
    assert top_k_s <= past and top_k_s % LANES == 0

    pos_p = jnp.arange(seq)
    pos_s = past + (jnp.arange(db * ds) % ds)
    tabs_p = _rope_tables(pos_p, HEAD_DIM) + _rope_tables(pos_p, IDX_DIM)
    tabs_s = _rope_tables(pos_s, HEAD_DIM) + _rope_tables(pos_s, IDX_DIM)

    hp = x_prompt.reshape(batch * seq, D_MODEL)
    hs = x_sample.reshape(db * ds, D_MODEL)
    outs = [[] for _ in range(8)]
    for l in range(depth):
        w = w_in[l]
        w_main = jnp.concatenate([w[:, 0:4096], w[:, 4608:6656], w[:, 6736:10832]], axis=1).astype(BF16)
        w_tail = jnp.concatenate(
            [w[:, 4096:4608], w[:, 6656:6736], jnp.zeros((D_MODEL, TAIL_WIDTH - 592), F32)], axis=1).astype(BF16)
        woa, wob, wout = w_oa[l].astype(BF16), w_ob[l].astype(BF16), w_out[l].astype(BF16)
        g_pre, g_post = pre_g[l][None], post_g[l][None]
        ln_g, ln_b = a_ln_g[l][None], a_ln_b[l][None]

        h_main, gv = _proj_main(hp, g_pre, w_main, *tabs_p, ln_g, ln_b, tm=512, seq_rows=seq, gv_rows=CHUNK)
        k, v, ki, kb, vb, kie, kio, ws = _proj_tail(hp, g_pre, w_tail, *tabs_p, tm=512, seq_rows=seq)
        ya = _gate(h_main, a_ws[l], a_bs[l].T, tm=512)
        ob = _prompt_attn(h_main, ws, kie, kio, kb, vb, batch=batch, seq=seq)
        hp = _merge(hp, ya, ob, h_main, woa, wob, wout, g_post, tm=256)
        outs[0].append(k.reshape(batch, seq, N_KV_HEADS, HEAD_DIM))
        outs[1].append(v.reshape(batch, seq, N_KV_HEADS, HEAD_DIM))
        outs[2].append(ki.reshape(batch, seq, IDX_DIM))
        outs[3].append(gv.reshape(batch, CHUNK, A_WIDTH))

        m_s = db * ds
        h_main, gv = _proj_main(hs, g_pre, w_main, *tabs_s, ln_g, ln_b, tm=m_s, seq_rows=m_s, gv_rows=m_s)
        k, v, ki, kb, vb, kie, kio, ws = _proj_tail(hs, g_pre, w_tail, *tabs_s, tm=m_s, seq_rows=m_s)
        w0 = jnp.repeat(a_ws[l][:, 0, 0], LANES)[None]
        b0 = jnp.repeat(a_bs[l][:, 0], LANES)[None]
        ya = _gate_row(h_main, gv, w0, b0)
        qi3 = _col_tile(h_main, T_QI).reshape(db, IDX_HEADS, IDX_DIM)
        kidx_t = jnp.swapaxes(cache_kidx[l], 1, 2)
        scores = _sample_scores(page_table, qi3, ws.reshape(db, IDX_HEADS, 1), kidx_t)
        bias, bnew = _sample_select(scores, h_main, kie, ws, top_k=top_k_s)
        bias3 = bias.reshape(db, n_pages, PAGE_SIZE)
        slots3 = _sample_compact(bias3, jnp.swapaxes(bias3, 1, 2), n_slots=top_k_s)
        ob = _sample_attn(page_table, jnp.maximum(slots3.reshape(db, top_k_s), 0),
                          _col_tile(h_main, T_Q).reshape(db, N_HEADS, HEAD_DIM),
                          slots3, bnew.reshape(db, 1, LANES),
                          kb.reshape(db, 1, KV_WIDTH), vb.reshape(db, 1, KV_WIDTH),
                          _col_tile(h_main, T_ZB).reshape(db, N_HEADS, HEAD_DIM),
                          cache_k[l], cache_v[l]).reshape(db, B_WIDTH)
        hs = _merge(hs, ya, ob, h_main, woa, wob, wout, g_post, tm=m_s)
        outs[4].append(k.reshape(db, ds, N_KV_HEADS, HEAD_DIM))
        outs[5].append(v.reshape(db, ds, N_KV_HEADS, HEAD_DIM))
        outs[6].append(ki.reshape(db, ds, IDX_DIM))
        outs[7].append(gv.reshape(db, ds, A_WIDTH))

    st = [jnp.stack(o, axis=0) for o in outs]
    return (hp.reshape(batch, seq, D_MODEL), hs.reshape(db, ds, D_MODEL),
            st[0], st[1], st[2], st[3], st[4], st[5], st[6], st[7])
```

```python
import functools

import jax
import jax.numpy as jnp
from jax import lax
from jax.experimental import pallas as pl
from jax.experimental.pallas import tpu as pltpu

F32 = jnp.float32
BF16 = jnp.bfloat16

D_MODEL = 2048
CHUNK = 128
A_GROUPS = 8
A_WIDTH = 1024
N_HEADS = 8
N_KV_HEADS = 2
HEAD_DIM = 128
B_WIDTH = 1024
KV_WIDTH = 256
IDX_HEADS = 16
IDX_DIM = 64
TOPK_MAX = 256
ROPE_THETA = 10000.0
EPS = 1e-6
PAGE_SIZE = 128
Q_BLOCK = 128
NEG = -1e30
LOG2_E = 1.4426950408889634

LANES = 128
BF16_SUBLANES = 16
_NT = (((1,), (1,)), ((), ()))
COL_TILE = 1024
T_UA, T_VA, T_ZA, T_Q, T_ZB, T_QI, T_GA, T_GB = 0, 1, 2, 3, 4, 5, 6, 8
N_MAIN_TILES = 10
ROW_K = 4 * COL_TILE
ROW_ZB = ROW_K + 2 * KV_WIDTH
ROW_KI = ROW_ZB + B_WIDTH + IDX_HEADS * IDX_DIM
ROW_GA = ROW_KI + IDX_DIM + IDX_HEADS
VMEM_LIMIT = 56 * 1024 * 1024


def _cparams(sem):
    return pltpu.CompilerParams(dimension_semantics=sem, vmem_limit_bytes=VMEM_LIMIT)


def _rope_tables(pos, dim):
    half = dim // 2
    inv = ROPE_THETA ** (-jnp.arange(half, dtype=F32) / half)
    ang = pos.astype(F32)[:, None] * inv[None, :]
    cos = jnp.cos(ang)
    sin = jnp.sin(ang)
    reps = LANES // dim
    cos_t = jnp.tile(jnp.concatenate([cos, cos], axis=-1), (1, reps))
    sin_t = jnp.tile(jnp.concatenate([-sin, sin], axis=-1), (1, reps))
    return cos_t, sin_t


def _rope128(x, cos, sin):
    return x * cos + pltpu.roll(x, 64, 1) * sin


def _rope64(x, cos, sin):
    lane = lax.broadcasted_iota(jnp.int32, x.shape, 1)
    first = (lane % IDX_DIM) < (IDX_DIM // 2)
    partner = jnp.where(first, pltpu.roll(x, LANES - 32, 1), pltpu.roll(x, 32, 1))
    return x * cos + partner * sin


def _rmsnorm_rows(xf, g):
    ms = jnp.mean(xf * xf, axis=-1, keepdims=True)
    return xf * lax.rsqrt(ms + EPS) * g


_TILE_KINDS = ("copy", "ln", "silu", "rope128", "silu", "rope64", "sigmoid", "sigmoid", "sigmoid", "sigmoid")


def _sigmoid(x):
    return 0.5 * jnp.tanh(0.5 * x) + 0.5


def _proj_main_body(x_ref, g_ref, w_ref, cq_ref, sq_ref, ci_ref, si_ref, lng_ref, lnb_ref,
                    h_ref, gv_ref, xn_ref, acc_ref, *, gv_rows, tiles_per_seq):
    i = pl.program_id(0)
    j = pl.program_id(1)
    tm = xn_ref.shape[0]

    def finish(kind, slot):
        acc = acc_ref[slot]
        if kind == "copy":
            h_ref[...] = acc.astype(BF16)
        elif kind == "ln":
            mu = jnp.mean(acc, axis=-1, keepdims=True)
            d = acc - mu
            var = jnp.mean(d * d, axis=-1, keepdims=True)
            vn = d * lax.rsqrt(var + EPS) * lng_ref[...] + lnb_ref[...]
            h_ref[...] = vn.astype(BF16)

            @pl.when(i % tiles_per_seq == tiles_per_seq - 1)
            def _():
                gv_ref[...] = vn[tm - gv_rows:, :]
        elif kind == "silu":
            h_ref[...] = (acc * _sigmoid(acc)).astype(BF16)
        elif kind == "sigmoid":
            h_ref[...] = _sigmoid(acc).astype(BF16)
        else:
            rope, cos, sin = ((_rope128, cq_ref[...], sq_ref[...]) if kind == "rope128"
                              else (_rope64, ci_ref[...], si_ref[...]))
            for h in range(COL_TILE // LANES):
                sl = slice(h * LANES, (h + 1) * LANES)
                h_ref[:, sl] = rope(acc[:, sl], cos, sin).astype(BF16)

    for step in range(N_MAIN_TILES + 1):
        @pl.when(j == step)
        def _(step=step):
            if step == 0:
                xn_ref[...] = _rmsnorm_rows(x_ref[...], g_ref[...]).astype(BF16)
            if step < N_MAIN_TILES:
                acc_ref[step % 2] = lax.dot_general(xn_ref[...], w_ref[...], _NT, preferred_element_type=F32)
            if step > 0:
                finish(_TILE_KINDS[step - 1], (step - 1) % 2)


def _main_tile_row(j):
    j = jnp.minimum(j, N_MAIN_TILES - 1)
    g = BF16_SUBLANES
    skip_kv = (ROW_ZB - T_ZB * COL_TILE) // g
    skip_idx = (ROW_GA - ROW_ZB - (T_GA - T_ZB) * COL_TILE) // g
    return (j * (COL_TILE // g) + jnp.where(j >= T_ZB, skip_kv, 0) + jnp.where(j >= T_GA, skip_idx, 0)) * g


def _proj_main(x2, pre_g, w_t, cq, sq, ci, si, ln_g, ln_b, *, tm, seq_rows, gv_rows):
    m = x2.shape[0]
    tiles_per_seq = seq_rows // tm
    n_seq = m // seq_rows
    body = functools.partial(_proj_main_body, gv_rows=gv_rows, tiles_per_seq=tiles_per_seq)
    tab = pl.BlockSpec((tm, LANES), lambda i, j: (i % tiles_per_seq, 0))
    row = lambda n: pl.BlockSpec((1, n), lambda i, j: (0, 0))
    return pl.pallas_call(
        body,
        grid=(m // tm, N_MAIN_TILES + 1),
        in_specs=[
            pl.BlockSpec((tm, D_MODEL), lambda i, j: (i, 0)),
            row(D_MODEL),
            pl.BlockSpec((pl.Element(COL_TILE), pl.Element(D_MODEL)), lambda i, j: (_main_tile_row(j), 0)),
            tab, tab, tab, tab,
            row(A_WIDTH), row(A_WIDTH),
        ],
        out_specs=[
            pl.BlockSpec((tm, COL_TILE), lambda i, j: (i, jnp.maximum(j - 1, 0))),
            pl.BlockSpec((gv_rows, A_WIDTH), lambda i, j: (i // tiles_per_seq, 0)),
        ],
        out_shape=[
            jax.ShapeDtypeStruct((m, N_MAIN_TILES * COL_TILE), BF16),
            jax.ShapeDtypeStruct((n_seq * gv_rows, A_WIDTH), F32),
        ],
        scratch_shapes=[pltpu.VMEM((tm, D_MODEL), BF16), pltpu.VMEM((2, tm, COL_TILE), F32)],
        compiler_params=_cparams(("arbitrary", "arbitrary")),
        name="proj_main",
    )(x2, pre_g, w_t, cq, sq, ci, si, ln_g, ln_b)


def _proj_tail_body(x_ref, g_ref, wkv_ref, wix_ref, ck_ref, sk_ref, ci_ref, si_ref,
                    k_ref, v_ref, ki_ref, kb_ref, vb_ref, kie_ref, kio_ref, ws_ref):
    xn = _rmsnorm_rows(x_ref[...], g_ref[...]).astype(BF16)
    acc = lax.dot_general(xn, wkv_ref[...], _NT, preferred_element_type=F32)
    cos = ck_ref[...]
    sin = sk_ref[...]
    for kh in range(N_KV_HEADS):
        sl = slice(kh * HEAD_DIM, (kh + 1) * HEAD_DIM)
        r = _rope128(acc[:, sl], cos, sin)
        k_ref[:, sl] = r
        kb_ref[:, sl] = r.astype(BF16)
    v = acc[:, KV_WIDTH:2 * KV_WIDTH]
    v_ref[...] = v
    vb_ref[...] = v.astype(BF16)
    t = lax.dot_general(xn, wix_ref[...], _NT, preferred_element_type=F32)
    r = _rope64(t, ci_ref[...], si_ref[...])
    ki_ref[...] = r[:, :IDX_DIM]
    lane = lax.broadcasted_iota(jnp.int32, r.shape, 1)
    ke = jnp.where(lane < IDX_DIM, r, 0.0)
    kie_ref[...] = ke.astype(BF16)
    kio_ref[...] = pltpu.roll(ke, IDX_DIM, 1).astype(BF16)
    ws_ref[...] = t[:, IDX_DIM:IDX_DIM + IDX_HEADS] * (IDX_HEADS ** -0.5 * IDX_DIM ** -0.5)


def _proj_tail(x2, pre_g, w_t, ck, sk, ci, si, *, tm, seq_rows):
    m = x2.shape[0]
    tiles_per_seq = seq_rows // tm
    tab = pl.BlockSpec((tm, LANES), lambda i: (i % tiles_per_seq, 0))
    blk = lambda n: pl.BlockSpec((tm, n), lambda i: (i, 0))
    return pl.pallas_call(
        _proj_tail_body,
        grid=(m // tm,),
        in_specs=[
            blk(D_MODEL),
            pl.BlockSpec((1, D_MODEL), lambda i: (0, 0)),
            pl.BlockSpec((2 * KV_WIDTH, D_MODEL), lambda i: (ROW_K // (2 * KV_WIDTH), 0)),
            pl.BlockSpec((LANES, D_MODEL), lambda i: (ROW_KI // LANES, 0)),
            tab, tab, tab, tab,
        ],
        out_specs=[blk(KV_WIDTH), blk(KV_WIDTH), blk(IDX_DIM), blk(KV_WIDTH), blk(KV_WIDTH),
                   blk(LANES), blk(LANES), blk(IDX_HEADS)],
        out_shape=[
            jax.ShapeDtypeStruct((m, KV_WIDTH), F32),
            jax.ShapeDtypeStruct((m, KV_WIDTH), F32),
            jax.ShapeDtypeStruct((m, IDX_DIM), F32),
            jax.ShapeDtypeStruct((m, KV_WIDTH), BF16),
            jax.ShapeDtypeStruct((m, KV_WIDTH), BF16),
            jax.ShapeDtypeStruct((m, LANES), BF16),
            jax.ShapeDtypeStruct((m, LANES), BF16),
            jax.ShapeDtypeStruct((m, IDX_HEADS), F32),
        ],
        compiler_params=_cparams(("arbitrary",)),
        name="proj_tail",
    )(x2, pre_g, w_t, w_t, ck, sk, ci, si)


def _gate_body(u_ref, vn_ref, sz_ref, ws_ref, bst_ref, y_ref, *, n_chunks):
    rr = lax.broadcasted_iota(jnp.int32, (CHUNK, CHUNK), 0)
    cc = lax.broadcasted_iota(jnp.int32, (CHUNK, CHUNK), 1)
    tril = cc <= rr
    for g in range(A_GROUPS):
        wm = jnp.where(tril, ws_ref[g], 0.0).astype(BF16)
        b = bst_ref[:, g:g + 1]
        cs = slice(g * LANES, (g + 1) * LANES)
        for c in range(n_chunks):
            rs = slice(c * CHUNK, (c + 1) * CHUNK)
            s = jnp.dot(wm, vn_ref[rs, cs], preferred_element_type=F32) + b
            y = u_ref[rs, cs].astype(F32) * s * sz_ref[rs, cs].astype(F32)
            y_ref[rs, cs] = y.astype(BF16)


def _gate(h_main, a_ws, bs_t, *, tm):
    m = h_main.shape[0]
    body = functools.partial(_gate_body, n_chunks=tm // CHUNK)
    col = lambda t: pl.BlockSpec((tm, COL_TILE), lambda i, t=t: (i, t))
    return pl.pallas_call(
        body,
        grid=(m // tm,),
        in_specs=[col(T_UA), col(T_VA), col(T_ZA),
                  pl.BlockSpec((A_GROUPS, CHUNK, CHUNK), lambda i: (0, 0, 0)),
                  pl.BlockSpec((CHUNK, A_GROUPS), lambda i: (0, 0))],
        out_specs=pl.BlockSpec((tm, A_WIDTH), lambda i: (i, 0)),
        out_shape=jax.ShapeDtypeStruct((m, A_WIDTH), BF16),
        compiler_params=_cparams(("arbitrary",)),
        name="gate_prompt",
    )(h_main, h_main, h_main, a_ws, bs_t)


def _gate_row_body(u_ref, vn_ref, sz_ref, w0_ref, b0_ref, y_ref):
    s = vn_ref[...] * w0_ref[...] + b0_ref[...]
    y_ref[...] = (u_ref[...].astype(F32) * s * sz_ref[...].astype(F32)).astype(BF16)


def _gate_row(h_main, vn, w0, b0):
    m = h_main.shape[0]
    col = lambda t: pl.BlockSpec((m, COL_TILE), lambda i, t=t: (0, t))
    full = lambda r: pl.BlockSpec((r, A_WIDTH), lambda i: (0, 0))
    return pl.pallas_call(
        _gate_row_body,
        grid=(1,),
        in_specs=[col(T_UA), full(m), col(T_ZA), full(1), full(1)],
        out_specs=full(m),
        out_shape=jax.ShapeDtypeStruct((m, A_WIDTH), BF16),
        compiler_params=_cparams(("arbitrary",)),
        name="gate_sample",
    )(h_main, vn, h_main, w0, b0)


def _lane_bcast(col, rows):
    return jnp.broadcast_to(col, (rows, LANES))


def _select_bias(sc_ref, extra_ref, kp, row_min, row_max, *, n_cols):
    rows = sc_ref.shape[0]
    n_tiles = n_cols // LANES
    extra = None if extra_ref is None else extra_ref[...]

    def count(pred):
        acc = jnp.zeros((rows, LANES), F32)
        for c in range(n_tiles):
            acc = acc + jnp.where(pred(sc_ref[:, c * LANES:(c + 1) * LANES], c), 1.0, 0.0)
        tot = jnp.sum(acc, axis=1, keepdims=True)
        return _lane_bcast(tot, rows)

    def count_ge(x):
        c = count(lambda s, _: s >= x)
        if extra is not None:
            c = c + jnp.where(extra >= x, 1.0, 0.0)
        return c

    c_max = count_ge(row_max)
    top = c_max >= kp
    lo0 = jnp.where(top, row_max, row_min)
    c0 = jnp.where(top, c_max, count_ge(row_min))

    def count_ge3(x1, x2, x3):
        a1 = jnp.zeros((rows, LANES), F32)
        a2 = jnp.zeros((rows, LANES), F32)
        a3 = jnp.zeros((rows, LANES), F32)
        for c in range(n_tiles):
            s = sc_ref[:, c * LANES:(c + 1) * LANES]
            a1 = a1 + jnp.where(s >= x1, 1.0, 0.0)
            a2 = a2 + jnp.where(s >= x2, 1.0, 0.0)
            a3 = a3 + jnp.where(s >= x3, 1.0, 0.0)
        res = []
        for a, x in ((a1, x1), (a2, x2), (a3, x3)):
            tot = _lane_bcast(jnp.sum(a, axis=1, keepdims=True), rows)
            if extra is not None:
                tot = tot + jnp.where(extra >= x, 1.0, 0.0)
            res.append(tot)
        return res

    def step(st):
        lo, hi, c_lo, _, it = st
        mid = 0.5 * lo + 0.5 * hi
        act = (c_lo != kp) & (mid > lo) & (mid < hi)
        any_act = jnp.max(jnp.where(act, 1.0, 0.0))
        clamp = lambda x: jnp.minimum(jnp.maximum(x, lo), hi)
        q1 = clamp(0.75 * lo + 0.25 * hi)
        q3 = clamp(0.25 * lo + 0.75 * hi)
        c1, c2, c3 = count_ge3(q1, mid, q3)
        g1, g2, g3 = c1 >= kp, c2 >= kp, c3 >= kp
        lo_n = jnp.where(g3, q3, jnp.where(g2, mid, jnp.where(g1, q1, lo)))
        c_n = jnp.where(g3, c3, jnp.where(g2, c2, jnp.where(g1, c1, c_lo)))
        hi_n = jnp.where(g3, hi, jnp.where(g2, q3, jnp.where(g1, mid, jnp.minimum(q1, mid))))
        return (jnp.where(act, lo_n, lo), jnp.where(act, hi_n, hi), jnp.where(act, c_n, c_lo),
                any_act, it + 1)

    def cond(st):
        return (st[3] > 0.0) & (st[4] < 400)

    lo, _, c_lo, _, _ = lax.while_loop(cond, step, (lo0, row_max, c0, jnp.float32(1.0), jnp.int32(0)))

    exact = jnp.max(jnp.where(c_lo != kp, 1.0, 0.0)) == 0.0

    @pl.when(exact)
    def _():
        for c in range(n_tiles):
            sl = slice(c * LANES, (c + 1) * LANES)
            sc_ref[:, sl] = jnp.where(sc_ref[:, sl] >= lo, 0.0, NEG)
        if extra is not None:
            extra_ref[...] = jnp.where(extra >= lo, 0.0, NEG)

    @pl.when(jnp.logical_not(exact))
    def _():
        n_gt = count(lambda s, _: s > lo)
        if extra is not None:
            n_gt = n_gt + jnp.where(extra > lo, 1.0, 0.0)
        need = kp - n_gt
        lane = lax.broadcasted_iota(jnp.int32, (rows, LANES), 1).astype(F32)

        def count_eq_upto(jx):
            c = count(lambda s, c: (s == lo) & (lane + float(c * LANES) <= jx))
            if extra is not None:
                c = c + jnp.where((extra == lo) & (jx >= float(n_cols)), 1.0, 0.0)
            return c

        last = n_cols if extra is not None else n_cols - 1
        j_lo = jnp.full((rows, LANES), -1.0, F32)
        j_hi = jnp.full((rows, LANES), float(last), F32)

        def jstep(_, st):
            a, b = st
            mid = jnp.floor(0.5 * (a + b))
            ok = count_eq_upto(mid) >= need
            return jnp.where(ok, a, mid), jnp.where(ok, mid, b)

        n_steps = max(1, (n_cols + 1).bit_length())
        _, j_hi = lax.fori_loop(0, n_steps, jstep, (j_lo, j_hi))
        for c in range(n_tiles):
            sl = slice(c * LANES, (c + 1) * LANES)
            s = sc_ref[:, sl]
            keep = (s > lo) | ((s == lo) & (lane + float(c * LANES) <= j_hi))
            sc_ref[:, sl] = jnp.where(keep, 0.0, NEG)
        if extra is not None:
            keep = (extra > lo) | ((extra == lo) & (j_hi >= float(n_cols)))
            extra_ref[...] = jnp.where(keep, 0.0, NEG)


def _prompt_attn_block(nk, qi_ref, ws_ref, kie_ref, kio_ref, q_ref, kb_ref, vb_ref, sz_ref,
                       o_ref, sc_ref, *, top_k, key_chunk):
    qb = pl.program_id(1)
    rows = Q_BLOCK
    n_pairs = IDX_HEADS // 2
    pos = qb * Q_BLOCK + lax.broadcasted_iota(jnp.int32, (rows, LANES), 0)

    qs = jnp.concatenate([qi_ref[:, p * LANES:(p + 1) * LANES] for p in range(n_pairs)], axis=0)
    wcols = [_lane_bcast(ws_ref[:, h:h + 1], rows) for h in range(IDX_HEADS)]
    nt = (((1,), (1,)), ((), ()))
    rmax = jnp.full((rows, LANES), -jnp.inf, F32)
    rmin = jnp.full((rows, LANES), jnp.inf, F32)
    for kc in range(0, nk, key_chunk):
        le = lax.dot_general(qs, kie_ref[kc:kc + key_chunk, :], nt, preferred_element_type=F32)
        lo = lax.dot_general(qs, kio_ref[kc:kc + key_chunk, :], nt, preferred_element_type=F32)
        for c in range(key_chunk // LANES):
            cs = slice(c * LANES, (c + 1) * LANES)
            acc = jnp.zeros((rows, LANES), F32)
            for p in range(n_pairs):
                rs = slice(p * rows, (p + 1) * rows)
                acc = acc + jnp.maximum(le[rs, cs], 0.0) * wcols[2 * p]
                acc = acc + jnp.maximum(lo[rs, cs], 0.0) * wcols[2 * p + 1]
            kpos = kc + c * LANES + lax.broadcasted_iota(jnp.int32, (rows, LANES), 1)
            causal = kpos <= pos
            rmax = jnp.maximum(rmax, jnp.where(causal, acc, -jnp.inf))
            rmin = jnp.minimum(rmin, jnp.where(causal, acc, jnp.inf))
            sc_ref[:, kc + c * LANES:kc + (c + 1) * LANES] = jnp.where(causal, acc, -jnp.inf)

    row_max = _lane_bcast(jnp.max(rmax, axis=1, keepdims=True), rows)
    row_min = _lane_bcast(jnp.min(rmin, axis=1, keepdims=True), rows)
    kp = jnp.minimum(pos + 1, top_k).astype(F32)
    _select_bias(sc_ref, None, kp, row_min, row_max, n_cols=nk)

    grp = N_HEADS // N_KV_HEADS
    bias = jnp.concatenate([sc_ref[:, :nk]] * grp, axis=0)
    for kh in range(N_KV_HEADS):
        qh = jnp.concatenate(
            [q_ref[:, (kh * grp + g) * HEAD_DIM:(kh * grp + g + 1) * HEAD_DIM] for g in range(grp)], axis=0)
        ks = slice(kh * HEAD_DIM, (kh + 1) * HEAD_DIM)
        s = lax.dot_general(qh, kb_ref[0:nk, ks], nt, preferred_element_type=F32) + bias
        m = jnp.max(s, axis=-1, keepdims=True)
        p = jnp.exp2((s - m) * (HEAD_DIM ** -0.5 * LOG2_E))
        l = jnp.sum(p, axis=-1, keepdims=True)
        o = jnp.dot(p.astype(BF16), vb_ref[0:nk, ks], preferred_element_type=F32) / l
        for g in range(grp):
            hs = slice((kh * grp + g) * HEAD_DIM, (kh * grp + g + 1) * HEAD_DIM)
            o_ref[:, hs] = (o[g * rows:(g + 1) * rows, :] * sz_ref[:, hs].astype(F32)).astype(BF16)


def _prompt_attn_body(*refs, seq, top_k, n_buckets, key_chunk):
    qb = pl.program_id(1)
    span = seq // n_buckets
    per = span // Q_BLOCK
    for c in range(n_buckets):
        @pl.when(qb // per == c)
        def _(c=c):
            _prompt_attn_block((c + 1) * span, *refs, top_k=top_k, key_chunk=key_chunk)


def _prompt_attn(h_main, ws, kie, kio, kb, vb, *, batch, seq):
    top_k = min(TOPK_MAX, seq // 4)
    n_qb = seq // Q_BLOCK
    key_chunk = min(256, seq)
    n_buckets = max(1, seq // 256)
    body = functools.partial(_prompt_attn_body, seq=seq, top_k=top_k, n_buckets=n_buckets, key_chunk=key_chunk)
    col = lambda t: pl.BlockSpec((Q_BLOCK, COL_TILE), lambda b, q, t=t: (b * n_qb + q, t))
    seqblk = lambda n: pl.BlockSpec((seq, n), lambda b, q: (b, 0))
    return pl.pallas_call(
        body,
        grid=(batch, n_qb),
        in_specs=[col(T_QI),
                  pl.BlockSpec((Q_BLOCK, IDX_HEADS), lambda b, q: (b * n_qb + q, 0)),
                  seqblk(LANES), seqblk(LANES),
                  col(T_Q), seqblk(KV_WIDTH), seqblk(KV_WIDTH), col(T_ZB)],
        out_specs=pl.BlockSpec((Q_BLOCK, B_WIDTH), lambda b, q: (b * n_qb + q, 0)),
        out_shape=jax.ShapeDtypeStruct((batch * seq, B_WIDTH), BF16),
        scratch_shapes=[pltpu.VMEM((Q_BLOCK, seq), F32)],
        compiler_params=_cparams(("arbitrary", "arbitrary")),
        name="attn_prompt",
    )(h_main, ws, kie, kio, h_main, kb, vb, h_main)


def _merge_body(x_ref, ya_ref, ob_ref, ga_ref, gb_ref, woa_ref, wob_ref, wout_ref, pg_ref, o_ref):
    pa = jnp.dot(ya_ref[...], woa_ref[...], preferred_element_type=F32)
    pb = jnp.dot(ob_ref[...], wob_ref[...], preferred_element_type=F32)
    mix = ga_ref[...].astype(F32) * pa + gb_ref[...].astype(F32) * pb
    r = jnp.dot(mix.astype(BF16), wout_ref[...], preferred_element_type=F32)
    o_ref[...] = x_ref[...] + _rmsnorm_rows(r, pg_ref[...])


def _merge(x2, ya, ob, h_main, w_oa, w_ob, w_out, post_g, *, tm):
    m = x2.shape[0]
    const = lambda r, c: pl.BlockSpec((r, c), lambda i: (0, 0), pipeline_mode=pl.Buffered(1))
    return pl.pallas_call(
        _merge_body,
        grid=(m // tm,),
        in_specs=[
            pl.BlockSpec((tm, D_MODEL), lambda i: (i, 0)),
            pl.BlockSpec((tm, A_WIDTH), lambda i: (i, 0)),
            pl.BlockSpec((tm, B_WIDTH), lambda i: (i, 0)),
            pl.BlockSpec((tm, D_MODEL), lambda i: (i, T_GA // 2)),
            pl.BlockSpec((tm, D_MODEL), lambda i: (i, T_GB // 2)),
            const(A_WIDTH, D_MODEL), const(B_WIDTH, D_MODEL), const(D_MODEL, D_MODEL),
            const(1, D_MODEL),
        ],
        out_specs=pl.BlockSpec((tm, D_MODEL), lambda i: (i, 0)),
        out_shape=jax.ShapeDtypeStruct((m, D_MODEL), F32),
        compiler_params=_cparams(("arbitrary",)),
        name="merge",
    )(x2, ya, ob, h_main, h_main, w_oa, w_ob, w_out, post_g)


SCORE_CHUNK = 2048


def _sample_scores_body(pt_ref, q_ref, w_ref, kidx_hbm, o_ref, buf, sem):
    db, n_pages = pt_ref.shape
    past = n_pages * PAGE_SIZE

    def page_copy(b, p, slot):
        dst = buf.at[slot, :, pl.ds(pl.multiple_of(p * PAGE_SIZE, PAGE_SIZE), PAGE_SIZE)]
        return pltpu.make_async_copy(kidx_hbm.at[pt_ref[b, p]], dst, sem.at[slot])

    def start_all(b, slot):
        def f(p, c):
            page_copy(b, p, slot).start()
            return c
        lax.fori_loop(0, n_pages, f, 0, unroll=8)

    def wait_all(slot):
        for p in range(n_pages):
            dst = buf.at[slot, :, pl.ds(p * PAGE_SIZE, PAGE_SIZE)]
            pltpu.make_async_copy(kidx_hbm.at[0], dst, sem.at[slot]).wait()

    start_all(0, 0)

    def per_seq(b, c):
        slot = b % 2

        @pl.when(b + 1 < db)
        def _():
            start_all(b + 1, 1 - slot)

        wait_all(slot)
        q = q_ref[b]
        w = w_ref[b]
        for ch in range(past // SCORE_CHUNK):
            cs = slice(ch * SCORE_CHUNK, (ch + 1) * SCORE_CHUNK)
            logit = jnp.dot(q, buf[slot, :, cs].astype(BF16), preferred_element_type=F32)
            o_ref[pl.ds(b, 1), cs] = jnp.sum(jnp.maximum(logit, 0.0) * w, axis=0, keepdims=True)
        return c

    lax.fori_loop(0, db, per_seq, 0)


def _sample_scores(page_table, qi3, ws3, kidx_pages_t):
    db, n_pages = page_table.shape
    past = n_pages * PAGE_SIZE
    grid_spec = pltpu.PrefetchScalarGridSpec(
        num_scalar_prefetch=1,
        grid=(1,),
        in_specs=[pl.BlockSpec((db, IDX_HEADS, IDX_DIM), lambda i, pt: (0, 0, 0)),
                  pl.BlockSpec((db, IDX_HEADS, 1), lambda i, pt: (0, 0, 0)),
                  pl.BlockSpec(memory_space=pl.ANY)],
        out_specs=pl.BlockSpec((db, past), lambda i, pt: (0, 0)),
        scratch_shapes=[pltpu.VMEM((2, IDX_DIM, past), F32), pltpu.SemaphoreType.DMA((2,))],
    )
    return pl.pallas_call(
        _sample_scores_body,
        grid_spec=grid_spec,
        out_shape=jax.ShapeDtypeStruct((db, past), F32),
        compiler_params=_cparams(("arbitrary",)),
        name="scores_sample",
    )(page_table, qi3, ws3, kidx_pages_t)


def _sample_select_body(sc_ref, qi_ref, kie_ref, ws_ref, bias_ref, bnew_ref, *, top_k):
    rows, past = sc_ref.shape
    lane = lax.broadcasted_iota(jnp.int32, (rows, LANES), 1)
    ki = kie_ref[...].astype(F32)
    ki = ki + pltpu.roll(ki, IDX_DIM, 1)
    s_new = jnp.zeros((rows, 1), F32)
    for p in range(IDX_HEADS // 2):
        prod = qi_ref[:, p * LANES:(p + 1) * LANES].astype(F32) * ki
        l_even = jnp.sum(jnp.where(lane < IDX_DIM, prod, 0.0), axis=1, keepdims=True)
        l_odd = jnp.sum(jnp.where(lane >= IDX_DIM, prod, 0.0), axis=1, keepdims=True)
        s_new = s_new + jnp.maximum(l_even, 0.0) * ws_ref[:, 2 * p:2 * p + 1]
        s_new = s_new + jnp.maximum(l_odd, 0.0) * ws_ref[:, 2 * p + 1:2 * p + 2]
    extra = _lane_bcast(s_new, rows)
    bnew_ref[...] = extra
    rmax = extra
    rmin = extra
    for c in range(past // LANES):
        sl = slice(c * LANES, (c + 1) * LANES)
        s = sc_ref[:, sl]
        bias_ref[:, sl] = s
        rmax = jnp.maximum(rmax, s)
        rmin = jnp.minimum(rmin, s)
    row_max = _lane_bcast(jnp.max(rmax, axis=1, keepdims=True), rows)
    row_min = _lane_bcast(jnp.min(rmin, axis=1, keepdims=True), rows)
    kp = jnp.full((rows, LANES), float(top_k), F32)
    _select_bias(bias_ref, bnew_ref, kp, row_min, row_max, n_cols=past)


def _sample_select(scores, h_main, kie, ws, *, top_k):
    db, past = scores.shape
    full = lambda r, c: pl.BlockSpec((r, c), lambda i: (0, 0))
    return pl.pallas_call(
        functools.partial(_sample_select_body, top_k=top_k),
        grid=(1,),
        in_specs=[full(db, past),
                  pl.BlockSpec((db, COL_TILE), lambda i: (0, T_QI)),
                  full(db, LANES), full(db, IDX_HEADS)],
        out_specs=[full(db, past), full(db, LANES)],
        out_shape=[jax.ShapeDtypeStruct((db, past), F32), jax.ShapeDtypeStruct((db, LANES), F32)],
        compiler_params=_cparams(("arbitrary",)),
        name="select_sample",
    )(scores, h_main, kie, ws)


def _sample_compact_body(m_ref, mt_ref, pt_ref, idx_ref, row_ref, *, n_slots):
    n_pages = m_ref.shape[1]
    pt = jnp.broadcast_to(pt_ref[0], (8, n_pages))
    pt_hi = (pt // PAGE_SIZE).astype(F32).astype(BF16)
    pt_lo = (pt % PAGE_SIZE).astype(F32).astype(BF16)
    one = lambda pred: jnp.where(pred, 1.0, 0.0)
    kept = m_ref[0] == 0.0
    kept_t = mt_ref[0] == 0.0
    ri = lax.broadcasted_iota(jnp.int32, (PAGE_SIZE, PAGE_SIZE), 0)
    ci = lax.broadcasted_iota(jnp.int32, (PAGE_SIZE, PAGE_SIZE), 1)
    rp = lax.broadcasted_iota(jnp.int32, (n_pages, n_pages), 0)
    cp = lax.broadcasted_iota(jnp.int32, (n_pages, n_pages), 1)
    plt = jnp.dot(one(ci <= ri).astype(BF16), one(kept_t).astype(BF16), preferred_element_type=F32)
    n_row = plt[PAGE_SIZE - 1:PAGE_SIZE, :]
    n_col = _lane_bcast(jnp.sum(one(kept), axis=1, keepdims=True), n_pages)
    e_col = jnp.dot(one(cp <= rp).astype(BF16), n_col.astype(BF16), preferred_element_type=F32)
    n_row8 = jnp.broadcast_to(n_row, (8, n_pages))
    e_row8 = jnp.dot(n_row8.astype(BF16), one(rp <= cp).astype(BF16), preferred_element_type=F32)
    off_row8 = e_row8 - n_row8
    n_total = e_col[n_pages - 1:n_pages, :]
    page_id = lax.broadcasted_iota(jnp.int32, (n_pages, LANES), 0).astype(F32)
    for jt in range(n_slots // LANES):
        j = (lax.broadcasted_iota(jnp.int32, (1, LANES), 1) + jt * LANES).astype(F32)
        page_j = jnp.sum(one(e_col <= j), axis=0, keepdims=True)
        pick = one(page_id == page_j).astype(BF16)
        prefix_j = jnp.dot(plt.astype(BF16), pick, preferred_element_type=F32)
        off_j = jnp.dot(off_row8.astype(BF16), pick, preferred_element_type=F32)[0:1]
        local_j = jnp.sum(one(prefix_j <= j - off_j), axis=0, keepdims=True)
        pos = page_j * float(PAGE_SIZE) + local_j
        phys = (jnp.dot(pt_hi, pick, preferred_element_type=F32)[0:1] * float(PAGE_SIZE)
                + jnp.dot(pt_lo, pick, preferred_element_type=F32)[0:1])
        row = phys * float(PAGE_SIZE) + local_j
        used = j < n_total
        sl = slice(jt * LANES, (jt + 1) * LANES)
        idx_ref[0, :, sl] = jnp.where(used, pos, -1.0).astype(jnp.int32)
        row_ref[0, :, sl] = jnp.where(used, row, 0.0).astype(jnp.int32)


def _sample_compact(bias3, bias3_t, pt3, *, n_slots):
    db, n_pages, _ = bias3.shape
    out = pl.BlockSpec((1, 1, n_slots), lambda b: (b, 0, 0))
    return pl.pallas_call(
        functools.partial(_sample_compact_body, n_slots=n_slots),
        grid=(db,),
        in_specs=[pl.BlockSpec((1, n_pages, PAGE_SIZE), lambda b: (b, 0, 0)),
                  pl.BlockSpec((1, PAGE_SIZE, n_pages), lambda b: (b, 0, 0)),
                  pl.BlockSpec((1, 1, n_pages), lambda b: (b, 0, 0))],
        out_specs=[out, out],
        out_shape=[jax.ShapeDtypeStruct((db, 1, n_slots), jnp.int32)] * 2,
        compiler_params=_cparams(("arbitrary",)),
        name="compact_sample",
    )(bias3, bias3_t, pt3)


def _sample_attn_body(row_ref, q_ref, slot_ref, bnew_ref, kn_ref, vn_ref, sz_ref, k_hbm, v_hbm,
                      o_ref, kbuf, vbuf, sem):
    b = pl.program_id(0)
    nb = pl.num_programs(0)
    n_slots = kbuf.shape[1]
    buf = b % 2
    grp = N_HEADS // N_KV_HEADS
    scale = HEAD_DIM ** -0.5

    def row_copies(seq, j, to):
        r = row_ref[seq, j]
        return (pltpu.make_async_copy(k_hbm.at[r], kbuf.at[to, j], sem.at[0, to]),
                pltpu.make_async_copy(v_hbm.at[r], vbuf.at[to, j], sem.at[1, to]))

    def start_all(seq, to):
        def f(j, c):
            ck, cv = row_copies(seq, j, to)
            ck.start()
            cv.start()
            return c
        lax.fori_loop(0, n_slots, f, 0, unroll=8)

    def wait_all(to):
        for blk in range(n_slots // PAGE_SIZE):
            rows = pl.ds(blk * PAGE_SIZE, PAGE_SIZE)
            src = pl.ds(0, PAGE_SIZE)
            pltpu.make_async_copy(k_hbm.at[src], kbuf.at[to, rows], sem.at[0, to]).wait()
            pltpu.make_async_copy(v_hbm.at[src], vbuf.at[to, rows], sem.at[1, to]).wait()

    @pl.when(b == 0)
    def _():
        start_all(0, 0)

    @pl.when(b + 1 < nb)
    def _():
        start_all(b + 1, 1 - buf)

    wait_all(buf)

    q = q_ref[0]
    head_s = lax.broadcasted_iota(jnp.int32, (N_HEADS, n_slots), 0)
    head_o = lax.broadcasted_iota(jnp.int32, (N_HEADS, HEAD_DIM), 0)
    k0, k1 = (kbuf[buf, :, kh, :].astype(BF16) for kh in range(N_KV_HEADS))
    v0, v1 = (vbuf[buf, :, kh, :].astype(BF16) for kh in range(N_KV_HEADS))
    s0 = lax.dot_general(q, k0, _NT, preferred_element_type=F32)
    s1 = lax.dot_general(q, k1, _NT, preferred_element_type=F32)
    s = jnp.where(head_s < grp, s0, s1) * scale + jnp.where(slot_ref[0] >= 0, 0.0, NEG)
    kn = kn_ref[0].astype(F32)
    vn = vn_ref[0].astype(F32)
    kn8 = jnp.where(head_o < grp, kn[:, :HEAD_DIM], kn[:, HEAD_DIM:])
    vn8 = jnp.where(head_o < grp, vn[:, :HEAD_DIM], vn[:, HEAD_DIM:])
    s_new = _lane_bcast(jnp.sum(q.astype(F32) * kn8, axis=-1, keepdims=True), N_HEADS) * scale + bnew_ref[0]
    m = jnp.maximum(_lane_bcast(jnp.max(s, axis=-1, keepdims=True), N_HEADS), s_new)
    p = jnp.exp(s - m[:, :1])
    p_new = jnp.exp(s_new - m)
    l = _lane_bcast(jnp.sum(p, axis=-1, keepdims=True), N_HEADS) + p_new
    pb = p.astype(BF16)
    pv = jnp.where(head_o < grp,
                   jnp.dot(pb, v0, preferred_element_type=F32),
                   jnp.dot(pb, v1, preferred_element_type=F32))
    o = (pv + p_new * vn8) / l
    o_ref[0] = (o * sz_ref[0].astype(F32)).astype(BF16)


def _sample_attn(rows, q3, slots3, bnew3, kn3, vn3, sz3, k_rows, v_rows):
    db, n_slots = rows.shape
    per_b = lambda r, c: pl.BlockSpec((1, r, c), lambda b, rw: (b, 0, 0))
    grid_spec = pltpu.PrefetchScalarGridSpec(
        num_scalar_prefetch=1,
        grid=(db,),
        in_specs=[per_b(N_HEADS, HEAD_DIM), per_b(1, n_slots), per_b(1, LANES),
                  per_b(1, KV_WIDTH), per_b(1, KV_WIDTH), per_b(N_HEADS, HEAD_DIM),
                  pl.BlockSpec(memory_space=pl.ANY), pl.BlockSpec(memory_space=pl.ANY)],
        out_specs=per_b(N_HEADS, HEAD_DIM),
        scratch_shapes=[pltpu.VMEM((2, n_slots, N_KV_HEADS, HEAD_DIM), F32),
                        pltpu.VMEM((2, n_slots, N_KV_HEADS, HEAD_DIM), F32),
                        pltpu.SemaphoreType.DMA((2, 2))],
    )
    return pl.pallas_call(
        _sample_attn_body,
        grid_spec=grid_spec,
        out_shape=jax.ShapeDtypeStruct((db, N_HEADS, HEAD_DIM), BF16),
        compiler_params=_cparams(("arbitrary",)),
        name="attn_sample",
    )(rows, q3, slots3, bnew3, kn3, vn3, sz3, k_rows, v_rows)


def _col_tile(h_main, t, n=1):
    return h_main[:, t * COL_TILE:(t + n) * COL_TILE]


def kernel(x_prompt, x_sample, cache_k, cache_v, cache_kidx, page_table, pre_g, w_in, a_ln_g, a_ln_b,
           a_ws, a_bs, w_oa, w_ob, w_out, post_g):
    batch, seq, _ = x_prompt.shape
    db, ds, _ = x_sample.shape
    depth = w_in.shape[0]
    n_pages = page_table.shape[1]
    past = n_pages * PAGE_SIZE
    assert ds == 1 and seq % 512 == 0 and past % SCORE_CHUNK == 0
    top_k_s = min(TOPK_MAX, (past + ds) // 4)
    assert top_k_s <= past and top_k_s % LANES == 0

    pos_p = jnp.arange(seq)
    pos_s = past + (jnp.arange(db * ds) % ds)
    tabs_p = _rope_tables(pos_p, HEAD_DIM) + _rope_tables(pos_p, IDX_DIM)
    tabs_s = _rope_tables(pos_s, HEAD_DIM) + _rope_tables(pos_s, IDX_DIM)

    hp = x_prompt.reshape(batch * seq, D_MODEL)
    hs = x_sample.reshape(db * ds, D_MODEL)
    outs = [[] for _ in range(8)]
    for l in range(depth):
        w_t = w_in[l].T.astype(BF16)
        woa, wob, wout = w_oa[l].astype(BF16), w_ob[l].astype(BF16), w_out[l].astype(BF16)
        g_pre, g_post = pre_g[l][None], post_g[l][None]
        ln_g, ln_b = a_ln_g[l][None], a_ln_b[l][None]

        h_main, gv = _proj_main(hp, g_pre, w_t, *tabs_p, ln_g, ln_b, tm=1024 if seq % 1024 == 0 else 512,
                                seq_rows=seq, gv_rows=CHUNK)
        k, v, ki, kb, vb, kie, kio, ws = _proj_tail(hp, g_pre, w_t, *tabs_p, tm=512, seq_rows=seq)
        ya = _gate(h_main, a_ws[l], a_bs[l].T, tm=512)
        ob = _prompt_attn(h_main, ws, kie, kio, kb, vb, batch=batch, seq=seq)
        hp = _merge(hp, ya, ob, h_main, woa, wob, wout, g_post, tm=256)
        outs[0].append(k.reshape(batch, seq, N_KV_HEADS, HEAD_DIM))
        outs[1].append(v.reshape(batch, seq, N_KV_HEADS, HEAD_DIM))
        outs[2].append(ki.reshape(batch, seq, IDX_DIM))
        outs[3].append(gv.reshape(batch, CHUNK, A_WIDTH))

        m_s = db * ds
        h_main, gv = _proj_main(hs, g_pre, w_t, *tabs_s, ln_g, ln_b, tm=m_s, seq_rows=m_s, gv_rows=m_s)
        k, v, ki, kb, vb, kie, kio, ws = _proj_tail(hs, g_pre, w_t, *tabs_s, tm=m_s, seq_rows=m_s)
        w0 = jnp.repeat(a_ws[l][:, 0, 0], LANES)[None]
        b0 = jnp.repeat(a_bs[l][:, 0], LANES)[None]
        ya = _gate_row(h_main, gv, w0, b0)
        qi3 = _col_tile(h_main, T_QI).reshape(db, IDX_HEADS, IDX_DIM)
        kidx_t = jnp.swapaxes(cache_kidx[l], 1, 2)
        scores = _sample_scores(page_table, qi3, ws.reshape(db, IDX_HEADS, 1), kidx_t)
        bias, bnew = _sample_select(scores, h_main, kie, ws, top_k=top_k_s)
        bias3 = bias.reshape(db, n_pages, PAGE_SIZE)
        slots3, rows3 = _sample_compact(bias3, jnp.swapaxes(bias3, 1, 2),
                                        page_table.reshape(db, 1, n_pages), n_slots=top_k_s)
        pool_rows = lambda c: c.reshape(-1, N_KV_HEADS, HEAD_DIM)
        ob = _sample_attn(rows3.reshape(db, top_k_s),
                          _col_tile(h_main, T_Q).reshape(db, N_HEADS, HEAD_DIM),
                          slots3, bnew.reshape(db, 1, LANES),
                          kb.reshape(db, 1, KV_WIDTH), vb.reshape(db, 1, KV_WIDTH),
                          _col_tile(h_main, T_ZB).reshape(db, N_HEADS, HEAD_DIM),
                          pool_rows(cache_k[l]), pool_rows(cache_v[l])).reshape(db, B_WIDTH)
        hs = _merge(hs, ya, ob, h_main, woa, wob, wout, g_post, tm=m_s)
        outs[4].append(k.reshape(db, ds, N_KV_HEADS, HEAD_DIM))
        outs[5].append(v.reshape(db, ds, N_KV_HEADS, HEAD_DIM))
        outs[6].append(ki.reshape(db, ds, IDX_DIM))
        outs[7].append(gv.reshape(db, ds, A_WIDTH))

    st = [jnp.stack(o, axis=0) for o in outs]
    return (hp.reshape(batch, seq, D_MODEL), hs.reshape(db, ds, D_MODEL),
            st[0], st[1], st[2], st[3], st[4], st[5], st[6], st[7])
```

```python
import functools

import jax
import jax.numpy as jnp
from jax import lax
from jax.experimental import pallas as pl
from jax.experimental.pallas import tpu as pltpu

F32 = jnp.float32
BF16 = jnp.bfloat16

D_MODEL = 2048
CHUNK = 128
A_GROUPS = 8
A_WIDTH = 1024
N_HEADS = 8
N_KV_HEADS = 2
HEAD_DIM = 128
B_WIDTH = 1024
KV_WIDTH = 256
IDX_HEADS = 16
IDX_DIM = 64
TOPK_MAX = 256
ROPE_THETA = 10000.0
EPS = 1e-6
PAGE_SIZE = 128
Q_BLOCK = 128
NEG = -1e30
LOG2_E = 1.4426950408889634

LANES = 128
BF16_SUBLANES = 16
_NT = (((1,), (1,)), ((), ()))
COL_TILE = 1024
T_UA, T_VA, T_ZA, T_Q, T_ZB, T_QI, T_GA, T_GB = 0, 1, 2, 3, 4, 5, 6, 8
N_MAIN_TILES = 10
ROW_K = 4 * COL_TILE
ROW_ZB = ROW_K + 2 * KV_WIDTH
ROW_KI = ROW_ZB + B_WIDTH + IDX_HEADS * IDX_DIM
ROW_GA = ROW_KI + IDX_DIM + IDX_HEADS
VMEM_LIMIT = 56 * 1024 * 1024


def _cparams(sem):
    return pltpu.CompilerParams(dimension_semantics=sem, vmem_limit_bytes=VMEM_LIMIT)


def _rope_tables(pos, dim):
    half = dim // 2
    inv = ROPE_THETA ** (-jnp.arange(half, dtype=F32) / half)
    ang = pos.astype(F32)[:, None] * inv[None, :]
    cos = jnp.cos(ang)
    sin = jnp.sin(ang)
    reps = LANES // dim
    cos_t = jnp.tile(jnp.concatenate([cos, cos], axis=-1), (1, reps))
    sin_t = jnp.tile(jnp.concatenate([-sin, sin], axis=-1), (1, reps))
    return cos_t, sin_t


def _rope128(x, cos, sin):
    return x * cos + pltpu.roll(x, 64, 1) * sin


def _rope64(x, cos, sin):
    lane = lax.broadcasted_iota(jnp.int32, x.shape, 1)
    first = (lane % IDX_DIM) < (IDX_DIM // 2)
    partner = jnp.where(first, pltpu.roll(x, LANES - 32, 1), pltpu.roll(x, 32, 1))
    return x * cos + partner * sin


def _rmsnorm_rows(xf, g):
    ms = jnp.mean(xf * xf, axis=-1, keepdims=True)
    return xf * lax.rsqrt(ms + EPS) * g


_TILE_KINDS = ("copy", "ln", "silu", "rope128", "silu", "rope64", "sigmoid", "sigmoid", "sigmoid", "sigmoid")


def _sigmoid(x):
    return 0.5 * jnp.tanh(0.5 * x) + 0.5


def _proj_main_body(x_ref, g_ref, w_ref, cq_ref, sq_ref, ci_ref, si_ref, lng_ref, lnb_ref,
                    h_ref, gv_ref, xn_ref, acc_ref, *, gv_rows, tiles_per_seq):
    i = pl.program_id(0)
    j = pl.program_id(1)
    tm = xn_ref.shape[0]

    def finish(kind, slot):
        acc = acc_ref[slot]
        if kind == "copy":
            h_ref[...] = acc.astype(BF16)
        elif kind == "ln":
            mu = jnp.mean(acc, axis=-1, keepdims=True)
            d = acc - mu
            var = jnp.mean(d * d, axis=-1, keepdims=True)
            vn = d * lax.rsqrt(var + EPS) * lng_ref[...] + lnb_ref[...]
            h_ref[...] = vn.astype(BF16)

            @pl.when(i % tiles_per_seq == tiles_per_seq - 1)
            def _():
                gv_ref[...] = vn[tm - gv_rows:, :]
        elif kind == "silu":
            h_ref[...] = (acc * _sigmoid(acc)).astype(BF16)
        elif kind == "sigmoid":
            h_ref[...] = _sigmoid(acc).astype(BF16)
        else:
            rope, cos, sin = ((_rope128, cq_ref[...], sq_ref[...]) if kind == "rope128"
                              else (_rope64, ci_ref[...], si_ref[...]))
            for h in range(COL_TILE // LANES):
                sl = slice(h * LANES, (h + 1) * LANES)
                h_ref[:, sl] = rope(acc[:, sl], cos, sin).astype(BF16)

    for step in range(N_MAIN_TILES + 1):
        @pl.when(j == step)
        def _(step=step):
            if step == 0:
                xn_ref[...] = _rmsnorm_rows(x_ref[...], g_ref[...]).astype(BF16)
            if step < N_MAIN_TILES:
                acc_ref[step % 2] = lax.dot_general(xn_ref[...], w_ref[...], _NT, preferred_element_type=F32)
            if step > 0:
                finish(_TILE_KINDS[step - 1], (step - 1) % 2)


def _main_tile_row(j):
    j = jnp.minimum(j, N_MAIN_TILES - 1)
    g = BF16_SUBLANES
    skip_kv = (ROW_ZB - T_ZB * COL_TILE) // g
    skip_idx = (ROW_GA - ROW_ZB - (T_GA - T_ZB) * COL_TILE) // g
    return (j * (COL_TILE // g) + jnp.where(j >= T_ZB, skip_kv, 0) + jnp.where(j >= T_GA, skip_idx, 0)) * g


def _proj_main(x2, pre_g, w_t, cq, sq, ci, si, ln_g, ln_b, *, tm, seq_rows, gv_rows):
    m = x2.shape[0]
    tiles_per_seq = seq_rows // tm
    n_seq = m // seq_rows
    body = functools.partial(_proj_main_body, gv_rows=gv_rows, tiles_per_seq=tiles_per_seq)
    tab = pl.BlockSpec((tm, LANES), lambda i, j: (i % tiles_per_seq, 0))
    row = lambda n: pl.BlockSpec((1, n), lambda i, j: (0, 0))
    return pl.pallas_call(
        body,
        grid=(m // tm, N_MAIN_TILES + 1),
        in_specs=[
            pl.BlockSpec((tm, D_MODEL), lambda i, j: (i, 0)),
            row(D_MODEL),
            pl.BlockSpec((pl.Element(COL_TILE), pl.Element(D_MODEL)), lambda i, j: (_main_tile_row(j), 0)),
            tab, tab, tab, tab,
            row(A_WIDTH), row(A_WIDTH),
        ],
        out_specs=[
            pl.BlockSpec((tm, COL_TILE), lambda i, j: (i, jnp.maximum(j - 1, 0))),
            pl.BlockSpec((gv_rows, A_WIDTH), lambda i, j: (i // tiles_per_seq, 0)),
        ],
        out_shape=[
            jax.ShapeDtypeStruct((m, N_MAIN_TILES * COL_TILE), BF16),
            jax.ShapeDtypeStruct((n_seq * gv_rows, A_WIDTH), F32),
        ],
        scratch_shapes=[pltpu.VMEM((tm, D_MODEL), BF16), pltpu.VMEM((2, tm, COL_TILE), F32)],
        compiler_params=_cparams(("arbitrary", "arbitrary")),
        name="proj_main",
    )(x2, pre_g, w_t, cq, sq, ci, si, ln_g, ln_b)


def _proj_tail_body(x_ref, g_ref, wkv_ref, wix_ref, ck_ref, sk_ref, ci_ref, si_ref,
                    k_ref, v_ref, ki_ref, kb_ref, vb_ref, kie_ref, kio_ref, ws_ref):
    xn = _rmsnorm_rows(x_ref[...], g_ref[...]).astype(BF16)
    acc = lax.dot_general(xn, wkv_ref[...], _NT, preferred_element_type=F32)
    cos = ck_ref[...]
    sin = sk_ref[...]
    for kh in range(N_KV_HEADS):
        sl = slice(kh * HEAD_DIM, (kh + 1) * HEAD_DIM)
        r = _rope128(acc[:, sl], cos, sin)
        k_ref[:, kh, :] = r
        kb_ref[:, sl] = r.astype(BF16)
        v_ref[:, kh, :] = acc[:, KV_WIDTH + kh * HEAD_DIM:KV_WIDTH + (kh + 1) * HEAD_DIM]
    vb_ref[...] = acc[:, KV_WIDTH:2 * KV_WIDTH].astype(BF16)
    t = lax.dot_general(xn, wix_ref[...], _NT, preferred_element_type=F32)
    r = _rope64(t, ci_ref[...], si_ref[...])
    ki_ref[...] = r[:, :IDX_DIM]
    lane = lax.broadcasted_iota(jnp.int32, r.shape, 1)
    ke = jnp.where(lane < IDX_DIM, r, 0.0)
    kie_ref[...] = ke.astype(BF16)
    kio_ref[...] = pltpu.roll(ke, IDX_DIM, 1).astype(BF16)
    ws_ref[...] = t[:, IDX_DIM:IDX_DIM + IDX_HEADS] * (IDX_HEADS ** -0.5 * IDX_DIM ** -0.5)


def _proj_tail(x2, pre_g, w_t, ck, sk, ci, si, *, tm, seq_rows):
    m = x2.shape[0]
    tiles_per_seq = seq_rows // tm
    tab = pl.BlockSpec((tm, LANES), lambda i: (i % tiles_per_seq, 0))
    blk = lambda n: pl.BlockSpec((tm, n), lambda i: (i, 0))
    kv_rows = pl.BlockSpec((tm, N_KV_HEADS, HEAD_DIM), lambda i: (i, 0, 0))
    return pl.pallas_call(
        _proj_tail_body,
        grid=(m // tm,),
        in_specs=[
            blk(D_MODEL),
            pl.BlockSpec((1, D_MODEL), lambda i: (0, 0)),
            pl.BlockSpec((2 * KV_WIDTH, D_MODEL), lambda i: (ROW_K // (2 * KV_WIDTH), 0)),
            pl.BlockSpec((LANES, D_MODEL), lambda i: (ROW_KI // LANES, 0)),
            tab, tab, tab, tab,
        ],
        out_specs=[kv_rows, kv_rows, blk(IDX_DIM), blk(KV_WIDTH), blk(KV_WIDTH),
                   blk(LANES), blk(LANES), blk(IDX_HEADS)],
        out_shape=[
            jax.ShapeDtypeStruct((m, N_KV_HEADS, HEAD_DIM), F32),
            jax.ShapeDtypeStruct((m, N_KV_HEADS, HEAD_DIM), F32),
            jax.ShapeDtypeStruct((m, IDX_DIM), F32),
            jax.ShapeDtypeStruct((m, KV_WIDTH), BF16),
            jax.ShapeDtypeStruct((m, KV_WIDTH), BF16),
            jax.ShapeDtypeStruct((m, LANES), BF16),
            jax.ShapeDtypeStruct((m, LANES), BF16),
            jax.ShapeDtypeStruct((m, IDX_HEADS), F32),
        ],
        compiler_params=_cparams(("arbitrary",)),
        name="proj_tail",
    )(x2, pre_g, w_t, w_t, ck, sk, ci, si)


def _gate_body(u_ref, vn_ref, sz_ref, ws_ref, bst_ref, y_ref, *, n_chunks):
    rr = lax.broadcasted_iota(jnp.int32, (CHUNK, CHUNK), 0)
    cc = lax.broadcasted_iota(jnp.int32, (CHUNK, CHUNK), 1)
    tril = cc <= rr
    for g in range(A_GROUPS):
        wm = jnp.where(tril, ws_ref[g], 0.0).astype(BF16)
        b = bst_ref[:, g:g + 1]
        cs = slice(g * LANES, (g + 1) * LANES)
        for c in range(n_chunks):
            rs = slice(c * CHUNK, (c + 1) * CHUNK)
            s = jnp.dot(wm, vn_ref[rs, cs], preferred_element_type=F32) + b
            y = u_ref[rs, cs].astype(F32) * s * sz_ref[rs, cs].astype(F32)
            y_ref[rs, cs] = y.astype(BF16)


def _gate(h_main, a_ws, bs_t, *, tm):
    m = h_main.shape[0]
    body = functools.partial(_gate_body, n_chunks=tm // CHUNK)
    col = lambda t: pl.BlockSpec((tm, COL_TILE), lambda i, t=t: (i, t))
    return pl.pallas_call(
        body,
        grid=(m // tm,),
        in_specs=[col(T_UA), col(T_VA), col(T_ZA),
                  pl.BlockSpec((A_GROUPS, CHUNK, CHUNK), lambda i: (0, 0, 0)),
                  pl.BlockSpec((CHUNK, A_GROUPS), lambda i: (0, 0))],
        out_specs=pl.BlockSpec((tm, A_WIDTH), lambda i: (i, 0)),
        out_shape=jax.ShapeDtypeStruct((m, A_WIDTH), BF16),
        compiler_params=_cparams(("arbitrary",)),
        name="gate_prompt",
    )(h_main, h_main, h_main, a_ws, bs_t)


def _gate_row_body(u_ref, vn_ref, sz_ref, w0_ref, b0_ref, y_ref):
    s = vn_ref[...] * w0_ref[...] + b0_ref[...]
    y_ref[...] = (u_ref[...].astype(F32) * s * sz_ref[...].astype(F32)).astype(BF16)


def _gate_row(h_main, vn, w0, b0):
    m = h_main.shape[0]
    col = lambda t: pl.BlockSpec((m, COL_TILE), lambda i, t=t: (0, t))
    full = lambda r: pl.BlockSpec((r, A_WIDTH), lambda i: (0, 0))
    return pl.pallas_call(
        _gate_row_body,
        grid=(1,),
        in_specs=[col(T_UA), full(m), col(T_ZA), full(1), full(1)],
        out_specs=full(m),
        out_shape=jax.ShapeDtypeStruct((m, A_WIDTH), BF16),
        compiler_params=_cparams(("arbitrary",)),
        name="gate_sample",
    )(h_main, vn, h_main, w0, b0)


def _lane_bcast(col, rows):
    return jnp.broadcast_to(col, (rows, LANES))


def _select_bias(sc_ref, extra_ref, kp, row_min, row_max, *, n_cols):
    rows = sc_ref.shape[0]
    n_tiles = n_cols // LANES
    extra = None if extra_ref is None else extra_ref[...]

    def count(pred):
        acc = jnp.zeros((rows, LANES), F32)
        for c in range(n_tiles):
            acc = acc + jnp.where(pred(sc_ref[:, c * LANES:(c + 1) * LANES], c), 1.0, 0.0)
        tot = jnp.sum(acc, axis=1, keepdims=True)
        return _lane_bcast(tot, rows)

    def count_ge(x):
        c = count(lambda s, _: s >= x)
        if extra is not None:
            c = c + jnp.where(extra >= x, 1.0, 0.0)
        return c

    c_max = count_ge(row_max)
    top = c_max >= kp
    lo0 = jnp.where(top, row_max, row_min)
    c0 = jnp.where(top, c_max, count_ge(row_min))

    def count_ge3(x1, x2, x3):
        a1 = jnp.zeros((rows, LANES), F32)
        a2 = jnp.zeros((rows, LANES), F32)
        a3 = jnp.zeros((rows, LANES), F32)
        for c in range(n_tiles):
            s = sc_ref[:, c * LANES:(c + 1) * LANES]
            a1 = a1 + jnp.where(s >= x1, 1.0, 0.0)
            a2 = a2 + jnp.where(s >= x2, 1.0, 0.0)
            a3 = a3 + jnp.where(s >= x3, 1.0, 0.0)
        res = []
        for a, x in ((a1, x1), (a2, x2), (a3, x3)):
            tot = _lane_bcast(jnp.sum(a, axis=1, keepdims=True), rows)
            if extra is not None:
                tot = tot + jnp.where(extra >= x, 1.0, 0.0)
            res.append(tot)
        return res

    def step(st):
        lo, hi, c_lo, _, it = st
        mid = 0.5 * lo + 0.5 * hi
        act = (c_lo != kp) & (mid > lo) & (mid < hi)
        any_act = jnp.max(jnp.where(act, 1.0, 0.0))
        clamp = lambda x: jnp.minimum(jnp.maximum(x, lo), hi)
        q1 = clamp(0.75 * lo + 0.25 * hi)
        q3 = clamp(0.25 * lo + 0.75 * hi)
        c1, c2, c3 = count_ge3(q1, mid, q3)
        g1, g2, g3 = c1 >= kp, c2 >= kp, c3 >= kp
        lo_n = jnp.where(g3, q3, jnp.where(g2, mid, jnp.where(g1, q1, lo)))
        c_n = jnp.where(g3, c3, jnp.where(g2, c2, jnp.where(g1, c1, c_lo)))
        hi_n = jnp.where(g3, hi, jnp.where(g2, q3, jnp.where(g1, mid, jnp.minimum(q1, mid))))
        return (jnp.where(act, lo_n, lo), jnp.where(act, hi_n, hi), jnp.where(act, c_n, c_lo),
                any_act, it + 1)

    def cond(st):
        return (st[3] > 0.0) & (st[4] < 400)

    lo, _, c_lo, _, _ = lax.while_loop(cond, step, (lo0, row_max, c0, jnp.float32(1.0), jnp.int32(0)))

    exact = jnp.max(jnp.where(c_lo != kp, 1.0, 0.0)) == 0.0

    @pl.when(exact)
    def _():
        for c in range(n_tiles):
            sl = slice(c * LANES, (c + 1) * LANES)
            sc_ref[:, sl] = jnp.where(sc_ref[:, sl] >= lo, 0.0, NEG)
        if extra is not None:
            extra_ref[...] = jnp.where(extra >= lo, 0.0, NEG)

    @pl.when(jnp.logical_not(exact))
    def _():
        n_gt = count(lambda s, _: s > lo)
        if extra is not None:
            n_gt = n_gt + jnp.where(extra > lo, 1.0, 0.0)
        need = kp - n_gt
        lane = lax.broadcasted_iota(jnp.int32, (rows, LANES), 1).astype(F32)

        def count_eq_upto(jx):
            c = count(lambda s, c: (s == lo) & (lane + float(c * LANES) <= jx))
            if extra is not None:
                c = c + jnp.where((extra == lo) & (jx >= float(n_cols)), 1.0, 0.0)
            return c

        last = n_cols if extra is not None else n_cols - 1
        j_lo = jnp.full((rows, LANES), -1.0, F32)
        j_hi = jnp.full((rows, LANES), float(last), F32)

        def jstep(_, st):
            a, b = st
            mid = jnp.floor(0.5 * (a + b))
            ok = count_eq_upto(mid) >= need
            return jnp.where(ok, a, mid), jnp.where(ok, mid, b)

        n_steps = max(1, (n_cols + 1).bit_length())
        _, j_hi = lax.fori_loop(0, n_steps, jstep, (j_lo, j_hi))
        for c in range(n_tiles):
            sl = slice(c * LANES, (c + 1) * LANES)
            s = sc_ref[:, sl]
            keep = (s > lo) | ((s == lo) & (lane + float(c * LANES) <= j_hi))
            sc_ref[:, sl] = jnp.where(keep, 0.0, NEG)
        if extra is not None:
            keep = (extra > lo) | ((extra == lo) & (j_hi >= float(n_cols)))
            extra_ref[...] = jnp.where(keep, 0.0, NEG)


def _prompt_attn_block(nk, qi_ref, ws_ref, kie_ref, kio_ref, q_ref, kb_ref, vb_ref, sz_ref,
                       o_ref, sc_ref, *, top_k, key_chunk):
    qb = pl.program_id(1)
    rows = Q_BLOCK
    n_pairs = IDX_HEADS // 2
    pos = qb * Q_BLOCK + lax.broadcasted_iota(jnp.int32, (rows, LANES), 0)

    qs = jnp.concatenate([qi_ref[:, p * LANES:(p + 1) * LANES] for p in range(n_pairs)], axis=0)
    wcols = [_lane_bcast(ws_ref[:, h:h + 1], rows) for h in range(IDX_HEADS)]
    nt = (((1,), (1,)), ((), ()))
    rmax = jnp.full((rows, LANES), -jnp.inf, F32)
    rmin = jnp.full((rows, LANES), jnp.inf, F32)
    for kc in range(0, nk, key_chunk):
        le = lax.dot_general(qs, kie_ref[kc:kc + key_chunk, :], nt, preferred_element_type=F32)
        lo = lax.dot_general(qs, kio_ref[kc:kc + key_chunk, :], nt, preferred_element_type=F32)
        for c in range(key_chunk // LANES):
            cs = slice(c * LANES, (c + 1) * LANES)
            acc = jnp.zeros((rows, LANES), F32)
            for p in range(n_pairs):
                rs = slice(p * rows, (p + 1) * rows)
                acc = acc + jnp.maximum(le[rs, cs], 0.0) * wcols[2 * p]
                acc = acc + jnp.maximum(lo[rs, cs], 0.0) * wcols[2 * p + 1]
            kpos = kc + c * LANES + lax.broadcasted_iota(jnp.int32, (rows, LANES), 1)
            causal = kpos <= pos
            rmax = jnp.maximum(rmax, jnp.where(causal, acc, -jnp.inf))
            rmin = jnp.minimum(rmin, jnp.where(causal, acc, jnp.inf))
            sc_ref[:, kc + c * LANES:kc + (c + 1) * LANES] = jnp.where(causal, acc, -jnp.inf)

    row_max = _lane_bcast(jnp.max(rmax, axis=1, keepdims=True), rows)
    row_min = _lane_bcast(jnp.min(rmin, axis=1, keepdims=True), rows)
    kp = jnp.minimum(pos + 1, top_k).astype(F32)
    _select_bias(sc_ref, None, kp, row_min, row_max, n_cols=nk)

    grp = N_HEADS // N_KV_HEADS
    bias = jnp.concatenate([sc_ref[:, :nk]] * grp, axis=0)
    for kh in range(N_KV_HEADS):
        qh = jnp.concatenate(
            [q_ref[:, (kh * grp + g) * HEAD_DIM:(kh * grp + g + 1) * HEAD_DIM] for g in range(grp)], axis=0)
        ks = slice(kh * HEAD_DIM, (kh + 1) * HEAD_DIM)
        s = lax.dot_general(qh, kb_ref[0:nk, ks], nt, preferred_element_type=F32) + bias
        m = jnp.max(s, axis=-1, keepdims=True)
        p = jnp.exp2((s - m) * (HEAD_DIM ** -0.5 * LOG2_E))
        l = jnp.sum(p, axis=-1, keepdims=True)
        o = jnp.dot(p.astype(BF16), vb_ref[0:nk, ks], preferred_element_type=F32) / l
        for g in range(grp):
            hs = slice((kh * grp + g) * HEAD_DIM, (kh * grp + g + 1) * HEAD_DIM)
            o_ref[:, hs] = (o[g * rows:(g + 1) * rows, :] * sz_ref[:, hs].astype(F32)).astype(BF16)


def _prompt_attn_body(*refs, seq, top_k, n_buckets, key_chunk):
    qb = pl.program_id(1)
    span = seq // n_buckets
    per = span // Q_BLOCK
    for c in range(n_buckets):
        @pl.when(qb // per == c)
        def _(c=c):
            _prompt_attn_block((c + 1) * span, *refs, top_k=top_k, key_chunk=key_chunk)


def _prompt_attn(h_main, ws, kie, kio, kb, vb, *, batch, seq):
    top_k = min(TOPK_MAX, seq // 4)
    n_qb = seq // Q_BLOCK
    key_chunk = min(512, seq)
    n_buckets = max(1, seq // 512)
    body = functools.partial(_prompt_attn_body, seq=seq, top_k=top_k, n_buckets=n_buckets, key_chunk=key_chunk)
    col = lambda t: pl.BlockSpec((Q_BLOCK, COL_TILE), lambda b, q, t=t: (b * n_qb + q, t))
    seqblk = lambda n: pl.BlockSpec((seq, n), lambda b, q: (b, 0))
    return pl.pallas_call(
        body,
        grid=(batch, n_qb),
        in_specs=[col(T_QI),
                  pl.BlockSpec((Q_BLOCK, IDX_HEADS), lambda b, q: (b * n_qb + q, 0)),
                  seqblk(LANES), seqblk(LANES),
                  col(T_Q), seqblk(KV_WIDTH), seqblk(KV_WIDTH), col(T_ZB)],
        out_specs=pl.BlockSpec((Q_BLOCK, B_WIDTH), lambda b, q: (b * n_qb + q, 0)),
        out_shape=jax.ShapeDtypeStruct((batch * seq, B_WIDTH), BF16),
        scratch_shapes=[pltpu.VMEM((Q_BLOCK, seq), F32)],
        compiler_params=_cparams(("arbitrary", "arbitrary")),
        name="attn_prompt",
    )(h_main, ws, kie, kio, h_main, kb, vb, h_main)


def _merge_body(x_ref, ya_ref, ob_ref, ga_ref, gb_ref, woa_ref, wob_ref, wout_ref, pg_ref, o_ref):
    pa = jnp.dot(ya_ref[...], woa_ref[...], preferred_element_type=F32)
    pb = jnp.dot(ob_ref[...], wob_ref[...], preferred_element_type=F32)
    mix = ga_ref[...].astype(F32) * pa + gb_ref[...].astype(F32) * pb
    r = jnp.dot(mix.astype(BF16), wout_ref[...], preferred_element_type=F32)
    o_ref[...] = x_ref[...] + _rmsnorm_rows(r, pg_ref[...])


def _merge(x2, ya, ob, h_main, w_oa, w_ob, w_out, post_g, *, tm):
    m = x2.shape[0]
    const = lambda r, c: pl.BlockSpec((r, c), lambda i: (0, 0), pipeline_mode=pl.Buffered(1))
    return pl.pallas_call(
        _merge_body,
        grid=(m // tm,),
        in_specs=[
            pl.BlockSpec((tm, D_MODEL), lambda i: (i, 0)),
            pl.BlockSpec((tm, A_WIDTH), lambda i: (i, 0)),
            pl.BlockSpec((tm, B_WIDTH), lambda i: (i, 0)),
            pl.BlockSpec((tm, D_MODEL), lambda i: (i, T_GA // 2)),
            pl.BlockSpec((tm, D_MODEL), lambda i: (i, T_GB // 2)),
            const(A_WIDTH, D_MODEL), const(B_WIDTH, D_MODEL), const(D_MODEL, D_MODEL),
            const(1, D_MODEL),
        ],
        out_specs=pl.BlockSpec((tm, D_MODEL), lambda i: (i, 0)),
        out_shape=jax.ShapeDtypeStruct((m, D_MODEL), F32),
        compiler_params=_cparams(("arbitrary",)),
        name="merge",
    )(x2, ya, ob, h_main, h_main, w_oa, w_ob, w_out, post_g)


SCORE_CHUNK = 2048


def _sample_scores_body(pt_ref, q_ref, w_ref, kidx_hbm, o_ref, buf, sem):
    db, n_pages = pt_ref.shape
    past = n_pages * PAGE_SIZE

    def page_copy(b, p, slot):
        dst = buf.at[slot, :, pl.ds(pl.multiple_of(p * PAGE_SIZE, PAGE_SIZE), PAGE_SIZE)]
        return pltpu.make_async_copy(kidx_hbm.at[pt_ref[b, p]], dst, sem.at[slot])

    def start_all(b, slot):
        def f(p, c):
            page_copy(b, p, slot).start()
            return c
        lax.fori_loop(0, n_pages, f, 0, unroll=8)

    def wait_all(slot):
        for p in range(n_pages):
            dst = buf.at[slot, :, pl.ds(p * PAGE_SIZE, PAGE_SIZE)]
            pltpu.make_async_copy(kidx_hbm.at[0], dst, sem.at[slot]).wait()

    start_all(0, 0)

    def per_seq(b, c):
        slot = b % 2

        @pl.when(b + 1 < db)
        def _():
            start_all(b + 1, 1 - slot)

        wait_all(slot)
        q = q_ref[b]
        w = w_ref[b]
        for ch in range(past // SCORE_CHUNK):
            cs = slice(ch * SCORE_CHUNK, (ch + 1) * SCORE_CHUNK)
            logit = jnp.dot(q, buf[slot, :, cs].astype(BF16), preferred_element_type=F32)
            o_ref[pl.ds(b, 1), cs] = jnp.sum(jnp.maximum(logit, 0.0) * w, axis=0, keepdims=True)
        return c

    lax.fori_loop(0, db, per_seq, 0)


def _sample_scores(page_table, qi3, ws3, kidx_pages_t):
    db, n_pages = page_table.shape
    past = n_pages * PAGE_SIZE
    grid_spec = pltpu.PrefetchScalarGridSpec(
        num_scalar_prefetch=1,
        grid=(1,),
        in_specs=[pl.BlockSpec((db, IDX_HEADS, IDX_DIM), lambda i, pt: (0, 0, 0)),
                  pl.BlockSpec((db, IDX_HEADS, 1), lambda i, pt: (0, 0, 0)),
                  pl.BlockSpec(memory_space=pl.ANY)],
        out_specs=pl.BlockSpec((db, past), lambda i, pt: (0, 0)),
        scratch_shapes=[pltpu.VMEM((2, IDX_DIM, past), F32), pltpu.SemaphoreType.DMA((2,))],
    )
    return pl.pallas_call(
        _sample_scores_body,
        grid_spec=grid_spec,
        out_shape=jax.ShapeDtypeStruct((db, past), F32),
        compiler_params=_cparams(("arbitrary",)),
        name="scores_sample",
    )(page_table, qi3, ws3, kidx_pages_t)


def _sample_select_body(sc_ref, qi_ref, kie_ref, ws_ref, bias_ref, bnew_ref, *, top_k):
    rows, past = sc_ref.shape
    lane = lax.broadcasted_iota(jnp.int32, (rows, LANES), 1)
    ki = kie_ref[...].astype(F32)
    ki = ki + pltpu.roll(ki, IDX_DIM, 1)
    s_new = jnp.zeros((rows, 1), F32)
    for p in range(IDX_HEADS // 2):
        prod = qi_ref[:, p * LANES:(p + 1) * LANES].astype(F32) * ki
        l_even = jnp.sum(jnp.where(lane < IDX_DIM, prod, 0.0), axis=1, keepdims=True)
        l_odd = jnp.sum(jnp.where(lane >= IDX_DIM, prod, 0.0), axis=1, keepdims=True)
        s_new = s_new + jnp.maximum(l_even, 0.0) * ws_ref[:, 2 * p:2 * p + 1]
        s_new = s_new + jnp.maximum(l_odd, 0.0) * ws_ref[:, 2 * p + 1:2 * p + 2]
    extra = _lane_bcast(s_new, rows)
    bnew_ref[...] = extra
    rmax = extra
    rmin = extra
    for c in range(past // LANES):
        sl = slice(c * LANES, (c + 1) * LANES)
        s = sc_ref[:, sl]
        bias_ref[:, sl] = s
        rmax = jnp.maximum(rmax, s)
        rmin = jnp.minimum(rmin, s)
    row_max = _lane_bcast(jnp.max(rmax, axis=1, keepdims=True), rows)
    row_min = _lane_bcast(jnp.min(rmin, axis=1, keepdims=True), rows)
    kp = jnp.full((rows, LANES), float(top_k), F32)
    _select_bias(bias_ref, bnew_ref, kp, row_min, row_max, n_cols=past)


def _sample_select(scores, h_main, kie, ws, *, top_k):
    db, past = scores.shape
    full = lambda r, c: pl.BlockSpec((r, c), lambda i: (0, 0))
    return pl.pallas_call(
        functools.partial(_sample_select_body, top_k=top_k),
        grid=(1,),
        in_specs=[full(db, past),
                  pl.BlockSpec((db, COL_TILE), lambda i: (0, T_QI)),
                  full(db, LANES), full(db, IDX_HEADS)],
        out_specs=[full(db, past), full(db, LANES)],
        out_shape=[jax.ShapeDtypeStruct((db, past), F32), jax.ShapeDtypeStruct((db, LANES), F32)],
        compiler_params=_cparams(("arbitrary",)),
        name="select_sample",
    )(scores, h_main, kie, ws)


def _sample_compact_body(m_ref, mt_ref, pt_ref, idx_ref, row_ref, *, n_slots):
    n_pages = m_ref.shape[1]
    pt = jnp.broadcast_to(pt_ref[0], (8, n_pages))
    pt_hi = (pt // PAGE_SIZE).astype(F32).astype(BF16)
    pt_lo = (pt % PAGE_SIZE).astype(F32).astype(BF16)
    one = lambda pred: jnp.where(pred, 1.0, 0.0)
    kept = m_ref[0] == 0.0
    kept_t = mt_ref[0] == 0.0
    ri = lax.broadcasted_iota(jnp.int32, (PAGE_SIZE, PAGE_SIZE), 0)
    ci = lax.broadcasted_iota(jnp.int32, (PAGE_SIZE, PAGE_SIZE), 1)
    rp = lax.broadcasted_iota(jnp.int32, (n_pages, n_pages), 0)
    cp = lax.broadcasted_iota(jnp.int32, (n_pages, n_pages), 1)
    plt = jnp.dot(one(ci <= ri).astype(BF16), one(kept_t).astype(BF16), preferred_element_type=F32)
    n_row = plt[PAGE_SIZE - 1:PAGE_SIZE, :]
    n_col = _lane_bcast(jnp.sum(one(kept), axis=1, keepdims=True), n_pages)
    e_col = jnp.dot(one(cp <= rp).astype(BF16), n_col.astype(BF16), preferred_element_type=F32)
    n_row8 = jnp.broadcast_to(n_row, (8, n_pages))
    e_row8 = jnp.dot(n_row8.astype(BF16), one(rp <= cp).astype(BF16), preferred_element_type=F32)
    off_row8 = e_row8 - n_row8
    n_total = e_col[n_pages - 1:n_pages, :]
    page_id = lax.broadcasted_iota(jnp.int32, (n_pages, LANES), 0).astype(F32)
    for jt in range(n_slots // LANES):
        j = (lax.broadcasted_iota(jnp.int32, (1, LANES), 1) + jt * LANES).astype(F32)
        page_j = jnp.sum(one(e_col <= j), axis=0, keepdims=True)
        pick = one(page_id == page_j).astype(BF16)
        prefix_j = jnp.dot(plt.astype(BF16), pick, preferred_element_type=F32)
        off_j = jnp.dot(off_row8.astype(BF16), pick, preferred_element_type=F32)[0:1]
        local_j = jnp.sum(one(prefix_j <= j - off_j), axis=0, keepdims=True)
        pos = page_j * float(PAGE_SIZE) + local_j
        phys = (jnp.dot(pt_hi, pick, preferred_element_type=F32)[0:1] * float(PAGE_SIZE)
                + jnp.dot(pt_lo, pick, preferred_element_type=F32)[0:1])
        row = phys * float(PAGE_SIZE) + local_j
        used = j < n_total
        sl = slice(jt * LANES, (jt + 1) * LANES)
        idx_ref[0, :, sl] = jnp.where(used, pos, -1.0).astype(jnp.int32)
        row_ref[0, :, sl] = jnp.where(used, row, 0.0).astype(jnp.int32)


def _sample_compact(bias3, bias3_t, pt3, *, n_slots):
    db, n_pages, _ = bias3.shape
    out = pl.BlockSpec((1, 1, n_slots), lambda b: (b, 0, 0))
    return pl.pallas_call(
        functools.partial(_sample_compact_body, n_slots=n_slots),
        grid=(db,),
        in_specs=[pl.BlockSpec((1, n_pages, PAGE_SIZE), lambda b: (b, 0, 0)),
                  pl.BlockSpec((1, PAGE_SIZE, n_pages), lambda b: (b, 0, 0)),
                  pl.BlockSpec((1, 1, n_pages), lambda b: (b, 0, 0))],
        out_specs=[out, out],
        out_shape=[jax.ShapeDtypeStruct((db, 1, n_slots), jnp.int32)] * 2,
        compiler_params=_cparams(("arbitrary",)),
        name="compact_sample",
    )(bias3, bias3_t, pt3)


def _sample_attn_body(row_ref, q_ref, slot_ref, bnew_ref, kn_ref, vn_ref, sz_ref, k_hbm, v_hbm,
                      o_ref, kbuf, vbuf, sem):
    b = pl.program_id(0)
    nb = pl.num_programs(0)
    n_slots = kbuf.shape[1]
    buf = b % 2
    grp = N_HEADS // N_KV_HEADS
    scale = HEAD_DIM ** -0.5

    def row_copies(seq, j, to):
        r = row_ref[seq, j]
        return (pltpu.make_async_copy(k_hbm.at[r], kbuf.at[to, j], sem.at[0, to]),
                pltpu.make_async_copy(v_hbm.at[r], vbuf.at[to, j], sem.at[1, to]))

    def start_all(seq, to):
        def f(j, c):
            ck, cv = row_copies(seq, j, to)
            ck.start()
            cv.start()
            return c
        lax.fori_loop(0, n_slots, f, 0, unroll=8)

    def wait_all(to):
        for blk in range(n_slots // PAGE_SIZE):
            rows = pl.ds(blk * PAGE_SIZE, PAGE_SIZE)
            src = pl.ds(0, PAGE_SIZE)
            pltpu.make_async_copy(k_hbm.at[src], kbuf.at[to, rows], sem.at[0, to]).wait()
            pltpu.make_async_copy(v_hbm.at[src], vbuf.at[to, rows], sem.at[1, to]).wait()

    @pl.when(b == 0)
    def _():
        start_all(0, 0)

    @pl.when(b + 1 < nb)
    def _():
        start_all(b + 1, 1 - buf)

    wait_all(buf)

    q = q_ref[0]
    head_s = lax.broadcasted_iota(jnp.int32, (N_HEADS, n_slots), 0)
    head_o = lax.broadcasted_iota(jnp.int32, (N_HEADS, HEAD_DIM), 0)
    k0, k1 = (kbuf[buf, :, kh, :].astype(BF16) for kh in range(N_KV_HEADS))
    v0, v1 = (vbuf[buf, :, kh, :].astype(BF16) for kh in range(N_KV_HEADS))
    s0 = lax.dot_general(q, k0, _NT, preferred_element_type=F32)
    s1 = lax.dot_general(q, k1, _NT, preferred_element_type=F32)
    s = jnp.where(head_s < grp, s0, s1) * scale + jnp.where(slot_ref[0] >= 0, 0.0, NEG)
    kn = kn_ref[0].astype(F32)
    vn = vn_ref[0].astype(F32)
    kn8 = jnp.where(head_o < grp, kn[:, :HEAD_DIM], kn[:, HEAD_DIM:])
    vn8 = jnp.where(head_o < grp, vn[:, :HEAD_DIM], vn[:, HEAD_DIM:])
    s_new = _lane_bcast(jnp.sum(q.astype(F32) * kn8, axis=-1, keepdims=True), N_HEADS) * scale + bnew_ref[0]
    m = jnp.maximum(_lane_bcast(jnp.max(s, axis=-1, keepdims=True), N_HEADS), s_new)
    p = jnp.exp(s - m[:, :1])
    p_new = jnp.exp(s_new - m)
    l = _lane_bcast(jnp.sum(p, axis=-1, keepdims=True), N_HEADS) + p_new
    pb = p.astype(BF16)
    pv = jnp.where(head_o < grp,
                   jnp.dot(pb, v0, preferred_element_type=F32),
                   jnp.dot(pb, v1, preferred_element_type=F32))
    o = (pv + p_new * vn8) / l
    o_ref[0] = (o * sz_ref[0].astype(F32)).astype(BF16)


def _sample_attn(rows, q3, slots3, bnew3, kn3, vn3, sz3, k_rows, v_rows):
    db, n_slots = rows.shape
    per_b = lambda r, c: pl.BlockSpec((1, r, c), lambda b, rw: (b, 0, 0))
    grid_spec = pltpu.PrefetchScalarGridSpec(
        num_scalar_prefetch=1,
        grid=(db,),
        in_specs=[per_b(N_HEADS, HEAD_DIM), per_b(1, n_slots), per_b(1, LANES),
                  per_b(1, KV_WIDTH), per_b(1, KV_WIDTH), per_b(N_HEADS, HEAD_DIM),
                  pl.BlockSpec(memory_space=pl.ANY), pl.BlockSpec(memory_space=pl.ANY)],
        out_specs=per_b(N_HEADS, HEAD_DIM),
        scratch_shapes=[pltpu.VMEM((2, n_slots, N_KV_HEADS, HEAD_DIM), F32),
                        pltpu.VMEM((2, n_slots, N_KV_HEADS, HEAD_DIM), F32),
                        pltpu.SemaphoreType.DMA((2, 2))],
    )
    return pl.pallas_call(
        _sample_attn_body,
        grid_spec=grid_spec,
        out_shape=jax.ShapeDtypeStruct((db, N_HEADS, HEAD_DIM), BF16),
        compiler_params=_cparams(("arbitrary",)),
        name="attn_sample",
    )(rows, q3, slots3, bnew3, kn3, vn3, sz3, k_rows, v_rows)


def _col_tile(h_main, t, n=1):
    return h_main[:, t * COL_TILE:(t + n) * COL_TILE]


def kernel(x_prompt, x_sample, cache_k, cache_v, cache_kidx, page_table, pre_g, w_in, a_ln_g, a_ln_b,
           a_ws, a_bs, w_oa, w_ob, w_out, post_g):
    batch, seq, _ = x_prompt.shape
    db, ds, _ = x_sample.shape
    depth = w_in.shape[0]
    n_pages = page_table.shape[1]
    past = n_pages * PAGE_SIZE
    assert ds == 1 and seq % 512 == 0 and past % SCORE_CHUNK == 0
    top_k_s = min(TOPK_MAX, (past + ds) // 4)
    assert top_k_s <= past and top_k_s % LANES == 0

    pos_p = jnp.arange(seq)
    pos_s = past + (jnp.arange(db * ds) % ds)
    tabs_p = _rope_tables(pos_p, HEAD_DIM) + _rope_tables(pos_p, IDX_DIM)
    tabs_s = _rope_tables(pos_s, HEAD_DIM) + _rope_tables(pos_s, IDX_DIM)

    hp = x_prompt.reshape(batch * seq, D_MODEL)
    hs = x_sample.reshape(db * ds, D_MODEL)
    outs = [[] for _ in range(8)]
    for l in range(depth):
        w_t = w_in[l].T.astype(BF16)
        woa, wob, wout = w_oa[l].astype(BF16), w_ob[l].astype(BF16), w_out[l].astype(BF16)
        g_pre, g_post = pre_g[l][None], post_g[l][None]
        ln_g, ln_b = a_ln_g[l][None], a_ln_b[l][None]

        h_main, gv = _proj_main(hp, g_pre, w_t, *tabs_p, ln_g, ln_b, tm=512, seq_rows=seq, gv_rows=CHUNK)
        k, v, ki, kb, vb, kie, kio, ws = _proj_tail(hp, g_pre, w_t, *tabs_p, tm=512, seq_rows=seq)
        ya = _gate(h_main, a_ws[l], a_bs[l].T, tm=512)
        ob = _prompt_attn(h_main, ws, kie, kio, kb, vb, batch=batch, seq=seq)
        hp = _merge(hp, ya, ob, h_main, woa, wob, wout, g_post, tm=256)
        outs[0].append(k.reshape(batch, seq, N_KV_HEADS, HEAD_DIM))
        outs[1].append(v.reshape(batch, seq, N_KV_HEADS, HEAD_DIM))
        outs[2].append(ki.reshape(batch, seq, IDX_DIM))
        outs[3].append(gv.reshape(batch, CHUNK, A_WIDTH))

        m_s = db * ds
        h_main, gv = _proj_main(hs, g_pre, w_t, *tabs_s, ln_g, ln_b, tm=m_s, seq_rows=m_s, gv_rows=m_s)
        k, v, ki, kb, vb, kie, kio, ws = _proj_tail(hs, g_pre, w_t, *tabs_s, tm=m_s, seq_rows=m_s)
        w0 = jnp.repeat(a_ws[l][:, 0, 0], LANES)[None]
        b0 = jnp.repeat(a_bs[l][:, 0], LANES)[None]
        ya = _gate_row(h_main, gv, w0, b0)
        qi3 = _col_tile(h_main, T_QI).reshape(db, IDX_HEADS, IDX_DIM)
        kidx_t = jnp.swapaxes(cache_kidx[l], 1, 2)
        scores = _sample_scores(page_table, qi3, ws.reshape(db, IDX_HEADS, 1), kidx_t)
        bias, bnew = _sample_select(scores, h_main, kie, ws, top_k=top_k_s)
        bias3 = bias.reshape(db, n_pages, PAGE_SIZE)
        slots3, rows3 = _sample_compact(bias3, jnp.swapaxes(bias3, 1, 2),
                                        page_table.reshape(db, 1, n_pages), n_slots=top_k_s)
        pool_rows = lambda c: c.reshape(-1, N_KV_HEADS, HEAD_DIM)
        ob = _sample_attn(rows3.reshape(db, top_k_s),
                          _col_tile(h_main, T_Q).reshape(db, N_HEADS, HEAD_DIM),
                          slots3, bnew.reshape(db, 1, LANES),
                          kb.reshape(db, 1, KV_WIDTH), vb.reshape(db, 1, KV_WIDTH),
                          _col_tile(h_main, T_ZB).reshape(db, N_HEADS, HEAD_DIM),
                          pool_rows(cache_k[l]), pool_rows(cache_v[l])).reshape(db, B_WIDTH)
        hs = _merge(hs, ya, ob, h_main, woa, wob, wout, g_post, tm=m_s)
        outs[4].append(k.reshape(db, ds, N_KV_HEADS, HEAD_DIM))
        outs[5].append(v.reshape(db, ds, N_KV_HEADS, HEAD_DIM))
        outs[6].append(ki.reshape(db, ds, IDX_DIM))
        outs[7].append(gv.reshape(db, ds, A_WIDTH))

    st = [jnp.stack(o, axis=0) for o in outs]
    return (hp.reshape(batch, seq, D_MODEL), hs.reshape(db, ds, D_MODEL),
            st[0], st[1], st[2], st[3], st[4], st[5], st[6], st[7])
```

```python
import functools

import jax
import jax.numpy as jnp
from jax import lax
from jax.experimental import pallas as pl
from jax.experimental.pallas import tpu as pltpu

F32 = jnp.float32
BF16 = jnp.bfloat16

D_MODEL = 2048
CHUNK = 128
A_GROUPS = 8
A_WIDTH = 1024
N_HEADS = 8
N_KV_HEADS = 2
HEAD_DIM = 128
B_WIDTH = 1024
KV_WIDTH = 256
IDX_HEADS = 16
IDX_DIM = 64
TOPK_MAX = 256
ROPE_THETA = 10000.0
EPS = 1e-6
PAGE_SIZE = 128
Q_BLOCK = 128
NEG = -1e30
LOG2_E = 1.4426950408889634

LANES = 128
BF16_SUBLANES = 16
NORM_STREAMS = 4
_NT = (((1,), (1,)), ((), ()))
COL_TILE = 1024
T_UA, T_VA, T_ZA, T_Q, T_ZB, T_QI, T_GA, T_GB = 0, 1, 2, 3, 4, 5, 6, 8
N_MAIN_TILES = 10
ROW_K = 4 * COL_TILE
ROW_ZB = ROW_K + 2 * KV_WIDTH
ROW_KI = ROW_ZB + B_WIDTH + IDX_HEADS * IDX_DIM
ROW_GA = ROW_KI + IDX_DIM + IDX_HEADS
VMEM_LIMIT = 56 * 1024 * 1024


def _cparams(sem):
    return pltpu.CompilerParams(dimension_semantics=sem, vmem_limit_bytes=VMEM_LIMIT)


def _rope_tables(pos, dim):
    half = dim // 2
    inv = ROPE_THETA ** (-jnp.arange(half, dtype=F32) / half)
    ang = pos.astype(F32)[:, None] * inv[None, :]
    cos = jnp.cos(ang)
    sin = jnp.sin(ang)
    reps = LANES // dim
    cos_t = jnp.tile(jnp.concatenate([cos, cos], axis=-1), (1, reps))
    sin_t = jnp.tile(jnp.concatenate([-sin, sin], axis=-1), (1, reps))
    return cos_t, sin_t


def _rope128(x, cos, sin):
    return x * cos + pltpu.roll(x, 64, 1) * sin


def _rope64(x, cos, sin):
    lane = lax.broadcasted_iota(jnp.int32, x.shape, 1)
    first = (lane % IDX_DIM) < (IDX_DIM // 2)
    partner = jnp.where(first, pltpu.roll(x, LANES - 32, 1), pltpu.roll(x, 32, 1))
    return x * cos + partner * sin


def _rmsnorm_rows(xf, g):
    ms = jnp.mean(xf * xf, axis=-1, keepdims=True)
    return xf * lax.rsqrt(ms + EPS) * g


_TILE_KINDS = ("copy", "ln", "silu", "rope128", "silu", "rope64", "sigmoid", "sigmoid", "sigmoid", "sigmoid")


def _sigmoid(x):
    return 0.5 * jnp.tanh(0.5 * x) + 0.5


def _prenorm_body(*refs):
    x_refs, g_ref, o_ref = refs[:-2], refs[-2], refs[-1]
    xs = [r[...] for r in x_refs]
    ms = sum(jnp.sum(x * x, axis=-1, keepdims=True) for x in xs) * (1.0 / D_MODEL)
    scale = lax.rsqrt(ms + EPS)
    w = xs[0].shape[1]
    for c, x in enumerate(xs):
        cs = slice(c * w, (c + 1) * w)
        o_ref[:, cs] = (x * scale * g_ref[:, cs]).astype(BF16)


def _prenorm(x2, g, *, tm):
    m = x2.shape[0]
    w = D_MODEL // NORM_STREAMS
    return pl.pallas_call(
        _prenorm_body,
        grid=(m // tm,),
        in_specs=[pl.BlockSpec((tm, w), lambda i, c=c: (i, c)) for c in range(NORM_STREAMS)]
                 + [pl.BlockSpec((1, D_MODEL), lambda i: (0, 0))],
        out_specs=pl.BlockSpec((tm, D_MODEL), lambda i: (i, 0)),
        out_shape=jax.ShapeDtypeStruct((m, D_MODEL), BF16),
        compiler_params=_cparams(("arbitrary",)),
        name="prenorm",
    )(*([x2] * NORM_STREAMS), g)


def _proj_main_body(xn_ref, w_ref, cq_ref, sq_ref, ci_ref, si_ref, lng_ref, lnb_ref,
                    h_ref, gv_ref, acc_ref, *, gv_rows, tiles_per_seq, n_row_tiles):
    j = pl.program_id(0)
    i = pl.program_id(1)
    tm = xn_ref.shape[0]

    def matmul():
        acc_ref[i % 2] = lax.dot_general(xn_ref[...], w_ref[...], _NT, preferred_element_type=F32)

    def finish(kind):
        acc = acc_ref[(i + 1) % 2]
        if kind == "copy":
            h_ref[...] = acc.astype(BF16)
        elif kind == "ln":
            mu = jnp.mean(acc, axis=-1, keepdims=True)
            d = acc - mu
            var = jnp.mean(d * d, axis=-1, keepdims=True)
            vn = d * lax.rsqrt(var + EPS) * lng_ref[...] + lnb_ref[...]
            h_ref[...] = vn.astype(BF16)

            @pl.when((i - 1) % tiles_per_seq == tiles_per_seq - 1)
            def _():
                gv_ref[...] = vn[tm - gv_rows:, :]
        elif kind == "silu":
            h_ref[...] = (acc * _sigmoid(acc)).astype(BF16)
        elif kind == "sigmoid":
            h_ref[...] = _sigmoid(acc).astype(BF16)
        else:
            rope, cos, sin = ((_rope128, cq_ref[...], sq_ref[...]) if kind == "rope128"
                              else (_rope64, ci_ref[...], si_ref[...]))
            for h in range(COL_TILE // LANES):
                sl = slice(h * LANES, (h + 1) * LANES)
                h_ref[:, sl] = rope(acc[:, sl], cos, sin).astype(BF16)

    pl.when(i == 0)(matmul)

    @pl.when((i == 0) & (j != T_VA))
    def _():
        gv_ref[...] = jnp.zeros(gv_ref.shape, F32)

    for kind in sorted(set(_TILE_KINDS)):
        is_kind = functools.reduce(jnp.logical_or, [j == t for t, k in enumerate(_TILE_KINDS) if k == kind])

        @pl.when(is_kind & (i > 0) & (i < n_row_tiles))
        def _(kind=kind):
            matmul()
            finish(kind)

        @pl.when(is_kind & (i == n_row_tiles))
        def _(kind=kind):
            finish(kind)


def _main_tile_row(j):
    j = jnp.minimum(j, N_MAIN_TILES - 1)
    g = BF16_SUBLANES
    skip_kv = (ROW_ZB - T_ZB * COL_TILE) // g
    skip_idx = (ROW_GA - ROW_ZB - (T_GA - T_ZB) * COL_TILE) // g
    return (j * (COL_TILE // g) + jnp.where(j >= T_ZB, skip_kv, 0) + jnp.where(j >= T_GA, skip_idx, 0)) * g


def _proj_main(xn, w_t, cq, sq, ci, si, ln_g, ln_b, *, tm, seq_rows, gv_rows):
    m = xn.shape[0]
    tiles_per_seq = seq_rows // tm
    n_seq = m // seq_rows
    n_row_tiles = m // tm
    body = functools.partial(_proj_main_body, gv_rows=gv_rows, tiles_per_seq=tiles_per_seq,
                             n_row_tiles=n_row_tiles)
    prev = lambda i: jnp.maximum(i - 1, 0)
    tab = pl.BlockSpec((tm, LANES), lambda j, i: (prev(i) % tiles_per_seq, 0))
    row = lambda n: pl.BlockSpec((1, n), lambda j, i: (0, 0))
    gv_block = lambda j, i: (jnp.where(j == T_VA, prev(i) // tiles_per_seq, n_seq + (j > T_VA)), 0)
    h_main, gv = pl.pallas_call(
        body,
        grid=(N_MAIN_TILES, n_row_tiles + 1),
        in_specs=[
            pl.BlockSpec((tm, D_MODEL), lambda j, i: (jnp.minimum(i, n_row_tiles - 1), 0)),
            pl.BlockSpec((pl.Element(COL_TILE), pl.Element(D_MODEL)), lambda j, i: (_main_tile_row(j), 0)),
            tab, tab, tab, tab,
            row(A_WIDTH), row(A_WIDTH),
        ],
        out_specs=[
            pl.BlockSpec((tm, COL_TILE), lambda j, i: (prev(i), j)),
            pl.BlockSpec((gv_rows, A_WIDTH), gv_block),
        ],
        out_shape=[
            jax.ShapeDtypeStruct((m, N_MAIN_TILES * COL_TILE), BF16),
            jax.ShapeDtypeStruct(((n_seq + 2) * gv_rows, A_WIDTH), F32),
        ],
        scratch_shapes=[pltpu.VMEM((2, tm, COL_TILE), F32)],
        compiler_params=_cparams(("arbitrary", "arbitrary")),
        name="proj_main",
    )(xn, w_t, cq, sq, ci, si, ln_g, ln_b)
    return h_main, gv[:n_seq * gv_rows]


def _proj_tail_body(xn_ref, wkv_ref, wix_ref, ck_ref, sk_ref, ci_ref, si_ref,
                    k_ref, v_ref, ki_ref, kb_ref, vb_ref, kie_ref, kio_ref, ws_ref):
    xn = xn_ref[...]
    acc = lax.dot_general(xn, wkv_ref[...], _NT, preferred_element_type=F32)
    cos = ck_ref[...]
    sin = sk_ref[...]
    for kh in range(N_KV_HEADS):
        sl = slice(kh * HEAD_DIM, (kh + 1) * HEAD_DIM)
        r = _rope128(acc[:, sl], cos, sin)
        k_ref[:, kh, :] = r
        kb_ref[:, sl] = r.astype(BF16)
        v_ref[:, kh, :] = acc[:, KV_WIDTH + kh * HEAD_DIM:KV_WIDTH + (kh + 1) * HEAD_DIM]
    vb_ref[...] = acc[:, KV_WIDTH:2 * KV_WIDTH].astype(BF16)
    t = lax.dot_general(xn, wix_ref[...], _NT, preferred_element_type=F32)
    r = _rope64(t, ci_ref[...], si_ref[...])
    ki_ref[...] = r[:, :IDX_DIM]
    lane = lax.broadcasted_iota(jnp.int32, r.shape, 1)
    ke = jnp.where(lane < IDX_DIM, r, 0.0)
    kie_ref[...] = ke.astype(BF16)
    kio_ref[...] = pltpu.roll(ke, IDX_DIM, 1).astype(BF16)
    ws_ref[...] = t[:, IDX_DIM:IDX_DIM + IDX_HEADS] * (IDX_HEADS ** -0.5 * IDX_DIM ** -0.5)


def _proj_tail(xn, w_t, ck, sk, ci, si, *, tm, seq_rows):
    m = xn.shape[0]
    tiles_per_seq = seq_rows // tm
    tab = pl.BlockSpec((tm, LANES), lambda i: (i % tiles_per_seq, 0))
    blk = lambda n: pl.BlockSpec((tm, n), lambda i: (i, 0))
    kv_rows = pl.BlockSpec((tm, N_KV_HEADS, HEAD_DIM), lambda i: (i, 0, 0))
    return pl.pallas_call(
        _proj_tail_body,
        grid=(m // tm,),
        in_specs=[
            blk(D_MODEL),
            pl.BlockSpec((2 * KV_WIDTH, D_MODEL), lambda i: (ROW_K // (2 * KV_WIDTH), 0)),
            pl.BlockSpec((LANES, D_MODEL), lambda i: (ROW_KI // LANES, 0)),
            tab, tab, tab, tab,
        ],
        out_specs=[kv_rows, kv_rows, blk(IDX_DIM), blk(KV_WIDTH), blk(KV_WIDTH),
                   blk(LANES), blk(LANES), blk(IDX_HEADS)],
        out_shape=[
            jax.ShapeDtypeStruct((m, N_KV_HEADS, HEAD_DIM), F32),
            jax.ShapeDtypeStruct((m, N_KV_HEADS, HEAD_DIM), F32),
            jax.ShapeDtypeStruct((m, IDX_DIM), F32),
            jax.ShapeDtypeStruct((m, KV_WIDTH), BF16),
            jax.ShapeDtypeStruct((m, KV_WIDTH), BF16),
            jax.ShapeDtypeStruct((m, LANES), BF16),
            jax.ShapeDtypeStruct((m, LANES), BF16),
            jax.ShapeDtypeStruct((m, IDX_HEADS), F32),
        ],
        compiler_params=_cparams(("arbitrary",)),
        name="proj_tail",
    )(xn, w_t, w_t, ck, sk, ci, si)


def _gate_body(u_ref, vn_ref, sz_ref, ws_ref, bst_ref, y_ref, *, n_chunks):
    rr = lax.broadcasted_iota(jnp.int32, (CHUNK, CHUNK), 0)
    cc = lax.broadcasted_iota(jnp.int32, (CHUNK, CHUNK), 1)
    tril = cc <= rr
    for g in range(A_GROUPS):
        wm = jnp.where(tril, ws_ref[g], 0.0).astype(BF16)
        b = bst_ref[:, g:g + 1]
        cs = slice(g * LANES, (g + 1) * LANES)
        for c in range(n_chunks):
            rs = slice(c * CHUNK, (c + 1) * CHUNK)
            s = jnp.dot(wm, vn_ref[rs, cs], preferred_element_type=F32) + b
            y = u_ref[rs, cs].astype(F32) * s * sz_ref[rs, cs].astype(F32)
            y_ref[rs, cs] = y.astype(BF16)


def _gate(h_main, a_ws, bs_t, *, tm):
    m = h_main.shape[0]
    body = functools.partial(_gate_body, n_chunks=tm // CHUNK)
    col = lambda t: pl.BlockSpec((tm, COL_TILE), lambda i, t=t: (i, t))
    return pl.pallas_call(
        body,
        grid=(m // tm,),
        in_specs=[col(T_UA), col(T_VA), col(T_ZA),
                  pl.BlockSpec((A_GROUPS, CHUNK, CHUNK), lambda i: (0, 0, 0)),
                  pl.BlockSpec((CHUNK, A_GROUPS), lambda i: (0, 0))],
        out_specs=pl.BlockSpec((tm, A_WIDTH), lambda i: (i, 0)),
        out_shape=jax.ShapeDtypeStruct((m, A_WIDTH), BF16),
        compiler_params=_cparams(("arbitrary",)),
        name="gate_prompt",
    )(h_main, h_main, h_main, a_ws, bs_t)


def _gate_row_body(u_ref, vn_ref, sz_ref, w0_ref, b0_ref, y_ref):
    s = vn_ref[...] * w0_ref[...] + b0_ref[...]
    y_ref[...] = (u_ref[...].astype(F32) * s * sz_ref[...].astype(F32)).astype(BF16)


def _gate_row(h_main, vn, w0, b0):
    m = h_main.shape[0]
    col = lambda t: pl.BlockSpec((m, COL_TILE), lambda i, t=t: (0, t))
    full = lambda r: pl.BlockSpec((r, A_WIDTH), lambda i: (0, 0))
    return pl.pallas_call(
        _gate_row_body,
        grid=(1,),
        in_specs=[col(T_UA), full(m), col(T_ZA), full(1), full(1)],
        out_specs=full(m),
        out_shape=jax.ShapeDtypeStruct((m, A_WIDTH), BF16),
        compiler_params=_cparams(("arbitrary",)),
        name="gate_sample",
    )(h_main, vn, h_main, w0, b0)


def _lane_bcast(col, rows):
    return jnp.broadcast_to(col, (rows, LANES))


def _select_bias(sc_ref, extra_ref, kp, row_min, row_max, *, n_cols):
    rows = sc_ref.shape[0]
    n_tiles = n_cols // LANES
    extra = None if extra_ref is None else extra_ref[...]

    def count(pred):
        acc = jnp.zeros((rows, LANES), F32)
        for c in range(n_tiles):
            acc = acc + jnp.where(pred(sc_ref[:, c * LANES:(c + 1) * LANES], c), 1.0, 0.0)
        tot = jnp.sum(acc, axis=1, keepdims=True)
        return _lane_bcast(tot, rows)

    def count_ge(x):
        c = count(lambda s, _: s >= x)
        if extra is not None:
            c = c + jnp.where(extra >= x, 1.0, 0.0)
        return c

    c_max = count_ge(row_max)
    top = c_max >= kp
    lo0 = jnp.where(top, row_max, row_min)
    c0 = jnp.where(top, c_max, count_ge(row_min))

    def count_ge3(x1, x2, x3):
        a1 = jnp.zeros((rows, LANES), F32)
        a2 = jnp.zeros((rows, LANES), F32)
        a3 = jnp.zeros((rows, LANES), F32)
        for c in range(n_tiles):
            s = sc_ref[:, c * LANES:(c + 1) * LANES]
            a1 = a1 + jnp.where(s >= x1, 1.0, 0.0)
            a2 = a2 + jnp.where(s >= x2, 1.0, 0.0)
            a3 = a3 + jnp.where(s >= x3, 1.0, 0.0)
        res = []
        for a, x in ((a1, x1), (a2, x2), (a3, x3)):
            tot = _lane_bcast(jnp.sum(a, axis=1, keepdims=True), rows)
            if extra is not None:
                tot = tot + jnp.where(extra >= x, 1.0, 0.0)
            res.append(tot)
        return res

    def step(st):
        lo, hi, c_lo, _, it = st
        mid = 0.5 * lo + 0.5 * hi
        act = (c_lo != kp) & (mid > lo) & (mid < hi)
        any_act = jnp.max(jnp.where(act, 1.0, 0.0))
        clamp = lambda x: jnp.minimum(jnp.maximum(x, lo), hi)
        q1 = clamp(0.75 * lo + 0.25 * hi)
        q3 = clamp(0.25 * lo + 0.75 * hi)
        c1, c2, c3 = count_ge3(q1, mid, q3)
        g1, g2, g3 = c1 >= kp, c2 >= kp, c3 >= kp
        lo_n = jnp.where(g3, q3, jnp.where(g2, mid, jnp.where(g1, q1, lo)))
        c_n = jnp.where(g3, c3, jnp.where(g2, c2, jnp.where(g1, c1, c_lo)))
        hi_n = jnp.where(g3, hi, jnp.where(g2, q3, jnp.where(g1, mid, jnp.minimum(q1, mid))))
        return (jnp.where(act, lo_n, lo), jnp.where(act, hi_n, hi), jnp.where(act, c_n, c_lo),
                any_act, it + 1)

    def cond(st):
        return (st[3] > 0.0) & (st[4] < 400)

    lo, _, c_lo, _, _ = lax.while_loop(cond, step, (lo0, row_max, c0, jnp.float32(1.0), jnp.int32(0)))

    exact = jnp.max(jnp.where(c_lo != kp, 1.0, 0.0)) == 0.0

    @pl.when(exact)
    def _():
        for c in range(n_tiles):
            sl = slice(c * LANES, (c + 1) * LANES)
            sc_ref[:, sl] = jnp.where(sc_ref[:, sl] >= lo, 0.0, NEG)
        if extra is not None:
            extra_ref[...] = jnp.where(extra >= lo, 0.0, NEG)

    @pl.when(jnp.logical_not(exact))
    def _():
        n_gt = count(lambda s, _: s > lo)
        if extra is not None:
            n_gt = n_gt + jnp.where(extra > lo, 1.0, 0.0)
        need = kp - n_gt
        lane = lax.broadcasted_iota(jnp.int32, (rows, LANES), 1).astype(F32)

        def count_eq_upto(jx):
            c = count(lambda s, c: (s == lo) & (lane + float(c * LANES) <= jx))
            if extra is not None:
                c = c + jnp.where((extra == lo) & (jx >= float(n_cols)), 1.0, 0.0)
            return c

        last = n_cols if extra is not None else n_cols - 1
        j_lo = jnp.full((rows, LANES), -1.0, F32)
        j_hi = jnp.full((rows, LANES), float(last), F32)

        def jstep(_, st):
            a, b = st
            mid = jnp.floor(0.5 * (a + b))
            ok = count_eq_upto(mid) >= need
            return jnp.where(ok, a, mid), jnp.where(ok, mid, b)

        n_steps = max(1, (n_cols + 1).bit_length())
        _, j_hi = lax.fori_loop(0, n_steps, jstep, (j_lo, j_hi))
        for c in range(n_tiles):
            sl = slice(c * LANES, (c + 1) * LANES)
            s = sc_ref[:, sl]
            keep = (s > lo) | ((s == lo) & (lane + float(c * LANES) <= j_hi))
            sc_ref[:, sl] = jnp.where(keep, 0.0, NEG)
        if extra is not None:
            keep = (extra > lo) | ((extra == lo) & (j_hi >= float(n_cols)))
            extra_ref[...] = jnp.where(keep, 0.0, NEG)


def _prompt_attn_block(nk, qi_ref, ws_ref, kie_ref, kio_ref, q_ref, kb_ref, vb_ref, sz_ref,
                       o_ref, sc_ref, *, top_k, key_chunk):
    qb = pl.program_id(1)
    rows = Q_BLOCK
    n_pairs = IDX_HEADS // 2
    pos = qb * Q_BLOCK + lax.broadcasted_iota(jnp.int32, (rows, LANES), 0)

    qs = jnp.concatenate([qi_ref[:, p * LANES:(p + 1) * LANES] for p in range(n_pairs)], axis=0)
    wcols = [_lane_bcast(ws_ref[:, h:h + 1], rows) for h in range(IDX_HEADS)]
    nt = (((1,), (1,)), ((), ()))
    rmax = jnp.full((rows, LANES), -jnp.inf, F32)
    rmin = jnp.full((rows, LANES), jnp.inf, F32)
    for kc in range(0, nk, key_chunk):
        le = lax.dot_general(qs, kie_ref[kc:kc + key_chunk, :], nt, preferred_element_type=F32)
        lo = lax.dot_general(qs, kio_ref[kc:kc + key_chunk, :], nt, preferred_element_type=F32)
        for c in range(key_chunk // LANES):
            cs = slice(c * LANES, (c + 1) * LANES)
            acc = jnp.zeros((rows, LANES), F32)
            for p in range(n_pairs):
                rs = slice(p * rows, (p + 1) * rows)
                acc = acc + jnp.maximum(le[rs, cs], 0.0) * wcols[2 * p]
                acc = acc + jnp.maximum(lo[rs, cs], 0.0) * wcols[2 * p + 1]
            kpos = kc + c * LANES + lax.broadcasted_iota(jnp.int32, (rows, LANES), 1)
            causal = kpos <= pos
            rmax = jnp.maximum(rmax, jnp.where(causal, acc, -jnp.inf))
            rmin = jnp.minimum(rmin, jnp.where(causal, acc, jnp.inf))
            sc_ref[:, kc + c * LANES:kc + (c + 1) * LANES] = jnp.where(causal, acc, -jnp.inf)

    row_max = _lane_bcast(jnp.max(rmax, axis=1, keepdims=True), rows)
    row_min = _lane_bcast(jnp.min(rmin, axis=1, keepdims=True), rows)
    kp = jnp.minimum(pos + 1, top_k).astype(F32)
    _select_bias(sc_ref, None, kp, row_min, row_max, n_cols=nk)

    grp = N_HEADS // N_KV_HEADS
    bias = jnp.concatenate([sc_ref[:, :nk]] * grp, axis=0)
    for kh in range(N_KV_HEADS):
        qh = jnp.concatenate(
            [q_ref[:, (kh * grp + g) * HEAD_DIM:(kh * grp + g + 1) * HEAD_DIM] for g in range(grp)], axis=0)
        ks = slice(kh * HEAD_DIM, (kh + 1) * HEAD_DIM)
        s = lax.dot_general(qh, kb_ref[0:nk, ks], nt, preferred_element_type=F32) + bias
        m = jnp.max(s, axis=-1, keepdims=True)
        p = jnp.exp2((s - m) * (HEAD_DIM ** -0.5 * LOG2_E))
        l = jnp.sum(p, axis=-1, keepdims=True)
        o = jnp.dot(p.astype(BF16), vb_ref[0:nk, ks], preferred_element_type=F32) / l
        for g in range(grp):
            hs = slice((kh * grp + g) * HEAD_DIM, (kh * grp + g + 1) * HEAD_DIM)
            o_ref[:, hs] = (o[g * rows:(g + 1) * rows, :] * sz_ref[:, hs].astype(F32)).astype(BF16)


def _prompt_attn_body(*refs, seq, top_k, n_buckets, key_chunk):
    qb = pl.program_id(1)
    span = seq // n_buckets
    per = span // Q_BLOCK
    for c in range(n_buckets):
        @pl.when(qb // per == c)
        def _(c=c):
            _prompt_attn_block((c + 1) * span, *refs, top_k=top_k, key_chunk=key_chunk)


def _prompt_attn(h_main, ws, kie, kio, kb, vb, *, batch, seq):
    top_k = min(TOPK_MAX, seq // 4)
    n_qb = seq // Q_BLOCK
    key_chunk = min(512, seq)
    n_buckets = max(1, seq // 512)
    body = functools.partial(_prompt_attn_body, seq=seq, top_k=top_k, n_buckets=n_buckets, key_chunk=key_chunk)
    col = lambda t: pl.BlockSpec((Q_BLOCK, COL_TILE), lambda b, q, t=t: (b * n_qb + q, t))
    seqblk = lambda n: pl.BlockSpec((seq, n), lambda b, q: (b, 0))
    return pl.pallas_call(
        body,
        grid=(batch, n_qb),
        in_specs=[col(T_QI),
                  pl.BlockSpec((Q_BLOCK, IDX_HEADS), lambda b, q: (b * n_qb + q, 0)),
                  seqblk(LANES), seqblk(LANES),
                  col(T_Q), seqblk(KV_WIDTH), seqblk(KV_WIDTH), col(T_ZB)],
        out_specs=pl.BlockSpec((Q_BLOCK, B_WIDTH), lambda b, q: (b * n_qb + q, 0)),
        out_shape=jax.ShapeDtypeStruct((batch * seq, B_WIDTH), BF16),
        scratch_shapes=[pltpu.VMEM((Q_BLOCK, seq), F32)],
        compiler_params=_cparams(("arbitrary", "arbitrary")),
        name="attn_prompt",
    )(h_main, ws, kie, kio, h_main, kb, vb, h_main)


def _merge_body(x_ref, ya_ref, ob_ref, ga_ref, gb_ref, woa_ref, wob_ref, wout_ref, pg_ref, o_ref):
    pa = jnp.dot(ya_ref[...], woa_ref[...], preferred_element_type=F32)
    pb = jnp.dot(ob_ref[...], wob_ref[...], preferred_element_type=F32)
    mix = ga_ref[...].astype(F32) * pa + gb_ref[...].astype(F32) * pb
    r = jnp.dot(mix.astype(BF16), wout_ref[...], preferred_element_type=F32)
    o_ref[...] = x_ref[...] + _rmsnorm_rows(r, pg_ref[...])


def _merge(x2, ya, ob, h_main, w_oa, w_ob, w_out, post_g, *, tm):
    m = x2.shape[0]
    const = lambda r, c: pl.BlockSpec((r, c), lambda i: (0, 0), pipeline_mode=pl.Buffered(1))
    return pl.pallas_call(
        _merge_body,
        grid=(m // tm,),
        in_specs=[
            pl.BlockSpec((tm, D_MODEL), lambda i: (i, 0)),
            pl.BlockSpec((tm, A_WIDTH), lambda i: (i, 0)),
            pl.BlockSpec((tm, B_WIDTH), lambda i: (i, 0)),
            pl.BlockSpec((tm, D_MODEL), lambda i: (i, T_GA // 2)),
            pl.BlockSpec((tm, D_MODEL), lambda i: (i, T_GB // 2)),
            const(A_WIDTH, D_MODEL), const(B_WIDTH, D_MODEL), const(D_MODEL, D_MODEL),
            const(1, D_MODEL),
        ],
        out_specs=pl.BlockSpec((tm, D_MODEL), lambda i: (i, 0)),
        out_shape=jax.ShapeDtypeStruct((m, D_MODEL), F32),
        compiler_params=_cparams(("arbitrary",)),
        name="merge",
    )(x2, ya, ob, h_main, h_main, w_oa, w_ob, w_out, post_g)


SCORE_CHUNK = 2048


def _sample_scores_body(pt_ref, q_ref, w_ref, kidx_hbm, o_ref, buf, sem):
    db, n_pages = pt_ref.shape
    past = n_pages * PAGE_SIZE

    def page_copy(b, p, slot):
        dst = buf.at[slot, :, pl.ds(pl.multiple_of(p * PAGE_SIZE, PAGE_SIZE), PAGE_SIZE)]
        return pltpu.make_async_copy(kidx_hbm.at[pt_ref[b, p]], dst, sem.at[slot])

    def start_all(b, slot):
        def f(p, c):
            page_copy(b, p, slot).start()
            return c
        lax.fori_loop(0, n_pages, f, 0, unroll=8)

    def wait_all(slot):
        for p in range(n_pages):
            dst = buf.at[slot, :, pl.ds(p * PAGE_SIZE, PAGE_SIZE)]
            pltpu.make_async_copy(kidx_hbm.at[0], dst, sem.at[slot]).wait()

    start_all(0, 0)

    def per_seq(b, c):
        slot = b % 2

        @pl.when(b + 1 < db)
        def _():
            start_all(b + 1, 1 - slot)

        wait_all(slot)
        q = q_ref[b]
        w = w_ref[b]
        for ch in range(past // SCORE_CHUNK):
            cs = slice(ch * SCORE_CHUNK, (ch + 1) * SCORE_CHUNK)
            logit = jnp.dot(q, buf[slot, :, cs].astype(BF16), preferred_element_type=F32)
            o_ref[pl.ds(b, 1), cs] = jnp.sum(jnp.maximum(logit, 0.0) * w, axis=0, keepdims=True)
        return c

    lax.fori_loop(0, db, per_seq, 0)


def _sample_scores(page_table, qi3, ws3, kidx_pages_t):
    db, n_pages = page_table.shape
    past = n_pages * PAGE_SIZE
    grid_spec = pltpu.PrefetchScalarGridSpec(
        num_scalar_prefetch=1,
        grid=(1,),
        in_specs=[pl.BlockSpec((db, IDX_HEADS, IDX_DIM), lambda i, pt: (0, 0, 0)),
                  pl.BlockSpec((db, IDX_HEADS, 1), lambda i, pt: (0, 0, 0)),
                  pl.BlockSpec(memory_space=pl.ANY)],
        out_specs=pl.BlockSpec((db, past), lambda i, pt: (0, 0)),
        scratch_shapes=[pltpu.VMEM((2, IDX_DIM, past), F32), pltpu.SemaphoreType.DMA((2,))],
    )
    return pl.pallas_call(
        _sample_scores_body,
        grid_spec=grid_spec,
        out_shape=jax.ShapeDtypeStruct((db, past), F32),
        compiler_params=_cparams(("arbitrary",)),
        name="scores_sample",
    )(page_table, qi3, ws3, kidx_pages_t)


def _sample_select_body(sc_ref, qi_ref, kie_ref, ws_ref, bias_ref, bnew_ref, *, top_k):
    rows, past = sc_ref.shape
    lane = lax.broadcasted_iota(jnp.int32, (rows, LANES), 1)
    ki = kie_ref[...].astype(F32)
    ki = ki + pltpu.roll(ki, IDX_DIM, 1)
    s_new = jnp.zeros((rows, 1), F32)
    for p in range(IDX_HEADS // 2):
        prod = qi_ref[:, p * LANES:(p + 1) * LANES].astype(F32) * ki
        l_even = jnp.sum(jnp.where(lane < IDX_DIM, prod, 0.0), axis=1, keepdims=True)
        l_odd = jnp.sum(jnp.where(lane >= IDX_DIM, prod, 0.0), axis=1, keepdims=True)
        s_new = s_new + jnp.maximum(l_even, 0.0) * ws_ref[:, 2 * p:2 * p + 1]
        s_new = s_new + jnp.maximum(l_odd, 0.0) * ws_ref[:, 2 * p + 1:2 * p + 2]
    extra = _lane_bcast(s_new, rows)
    bnew_ref[...] = extra
    rmax = extra
    rmin = extra
    for c in range(past // LANES):
        sl = slice(c * LANES, (c + 1) * LANES)
        s = sc_ref[:, sl]
        bias_ref[:, sl] = s
        rmax = jnp.maximum(rmax, s)
        rmin = jnp.minimum(rmin, s)
    row_max = _lane_bcast(jnp.max(rmax, axis=1, keepdims=True), rows)
    row_min = _lane_bcast(jnp.min(rmin, axis=1, keepdims=True), rows)
    kp = jnp.full((rows, LANES), float(top_k), F32)
    _select_bias(bias_ref, bnew_ref, kp, row_min, row_max, n_cols=past)


def _sample_select(scores, h_main, kie, ws, *, top_k):
    db, past = scores.shape
    full = lambda r, c: pl.BlockSpec((r, c), lambda i: (0, 0))
    return pl.pallas_call(
        functools.partial(_sample_select_body, top_k=top_k),
        grid=(1,),
        in_specs=[full(db, past),
                  pl.BlockSpec((db, COL_TILE), lambda i: (0, T_QI)),
                  full(db, LANES), full(db, IDX_HEADS)],
        out_specs=[full(db, past), full(db, LANES)],
        out_shape=[jax.ShapeDtypeStruct((db, past), F32), jax.ShapeDtypeStruct((db, LANES), F32)],
        compiler_params=_cparams(("arbitrary",)),
        name="select_sample",
    )(scores, h_main, kie, ws)


def _sample_compact_body(m_ref, mt_ref, pt_ref, idx_ref, row_ref, *, n_slots):
    n_pages = m_ref.shape[1]
    pt = jnp.broadcast_to(pt_ref[0], (8, n_pages))
    pt_hi = (pt // PAGE_SIZE).astype(F32).astype(BF16)
    pt_lo = (pt % PAGE_SIZE).astype(F32).astype(BF16)
    one = lambda pred: jnp.where(pred, 1.0, 0.0)
    kept = m_ref[0] == 0.0
    kept_t = mt_ref[0] == 0.0
    ri = lax.broadcasted_iota(jnp.int32, (PAGE_SIZE, PAGE_SIZE), 0)
    ci = lax.broadcasted_iota(jnp.int32, (PAGE_SIZE, PAGE_SIZE), 1)
    rp = lax.broadcasted_iota(jnp.int32, (n_pages, n_pages), 0)
    cp = lax.broadcasted_iota(jnp.int32, (n_pages, n_pages), 1)
    plt = jnp.dot(one(ci <= ri).astype(BF16), one(kept_t).astype(BF16), preferred_element_type=F32)
    n_row = plt[PAGE_SIZE - 1:PAGE_SIZE, :]
    n_col = _lane_bcast(jnp.sum(one(kept), axis=1, keepdims=True), n_pages)
    e_col = jnp.dot(one(cp <= rp).astype(BF16), n_col.astype(BF16), preferred_element_type=F32)
    n_row8 = jnp.broadcast_to(n_row, (8, n_pages))
    e_row8 = jnp.dot(n_row8.astype(BF16), one(rp <= cp).astype(BF16), preferred_element_type=F32)
    off_row8 = e_row8 - n_row8
    n_total = e_col[n_pages - 1:n_pages, :]
    page_id = lax.broadcasted_iota(jnp.int32, (n_pages, LANES), 0).astype(F32)
    for jt in range(n_slots // LANES):
        j = (lax.broadcasted_iota(jnp.int32, (1, LANES), 1) + jt * LANES).astype(F32)
        page_j = jnp.sum(one(e_col <= j), axis=0, keepdims=True)
        pick = one(page_id == page_j).astype(BF16)
        prefix_j = jnp.dot(plt.astype(BF16), pick, preferred_element_type=F32)
        off_j = jnp.dot(off_row8.astype(BF16), pick, preferred_element_type=F32)[0:1]
        local_j = jnp.sum(one(prefix_j <= j - off_j), axis=0, keepdims=True)
        pos = page_j * float(PAGE_SIZE) + local_j
        phys = (jnp.dot(pt_hi, pick, preferred_element_type=F32)[0:1] * float(PAGE_SIZE)
                + jnp.dot(pt_lo, pick, preferred_element_type=F32)[0:1])
        row = phys * float(PAGE_SIZE) + local_j
        used = j < n_total
        sl = slice(jt * LANES, (jt + 1) * LANES)
        idx_ref[0, :, sl] = jnp.where(used, pos, -1.0).astype(jnp.int32)
        row_ref[0, :, sl] = jnp.where(used, row, 0.0).astype(jnp.int32)


def _sample_compact(bias3, bias3_t, pt3, *, n_slots):
    db, n_pages, _ = bias3.shape
    out = pl.BlockSpec((1, 1, n_slots), lambda b: (b, 0, 0))
    return pl.pallas_call(
        functools.partial(_sample_compact_body, n_slots=n_slots),
        grid=(db,),
        in_specs=[pl.BlockSpec((1, n_pages, PAGE_SIZE), lambda b: (b, 0, 0)),
                  pl.BlockSpec((1, PAGE_SIZE, n_pages), lambda b: (b, 0, 0)),
                  pl.BlockSpec((1, 1, n_pages), lambda b: (b, 0, 0))],
        out_specs=[out, out],
        out_shape=[jax.ShapeDtypeStruct((db, 1, n_slots), jnp.int32)] * 2,
        compiler_params=_cparams(("arbitrary",)),
        name="compact_sample",
    )(bias3, bias3_t, pt3)


def _sample_attn_body(row_ref, q_ref, slot_ref, bnew_ref, kn_ref, vn_ref, sz_ref, k_hbm, v_hbm,
                      o_ref, kbuf, vbuf, sem):
    b = pl.program_id(0)
    nb = pl.num_programs(0)
    n_slots = kbuf.shape[1]
    buf = b % 2
    grp = N_HEADS // N_KV_HEADS
    scale = HEAD_DIM ** -0.5

    def row_copies(seq, j, to):
        r = row_ref[seq, j]
        return (pltpu.make_async_copy(k_hbm.at[r], kbuf.at[to, j], sem.at[0, to]),
                pltpu.make_async_copy(v_hbm.at[r], vbuf.at[to, j], sem.at[1, to]))

    def start_all(seq, to):
        def f(j, c):
            ck, cv = row_copies(seq, j, to)
            ck.start()
            cv.start()
            return c
        lax.fori_loop(0, n_slots, f, 0, unroll=8)

    def wait_all(to):
        for blk in range(n_slots // PAGE_SIZE):
            rows = pl.ds(blk * PAGE_SIZE, PAGE_SIZE)
            src = pl.ds(0, PAGE_SIZE)
            pltpu.make_async_copy(k_hbm.at[src], kbuf.at[to, rows], sem.at[0, to]).wait()
            pltpu.make_async_copy(v_hbm.at[src], vbuf.at[to, rows], sem.at[1, to]).wait()

    @pl.when(b == 0)
    def _():
        start_all(0, 0)

    @pl.when(b + 1 < nb)
    def _():
        start_all(b + 1, 1 - buf)

    wait_all(buf)

    q = q_ref[0]
    head_s = lax.broadcasted_iota(jnp.int32, (N_HEADS, n_slots), 0)
    head_o = lax.broadcasted_iota(jnp.int32, (N_HEADS, HEAD_DIM), 0)
    k0, k1 = (kbuf[buf, :, kh, :].astype(BF16) for kh in range(N_KV_HEADS))
    v0, v1 = (vbuf[buf, :, kh, :].astype(BF16) for kh in range(N_KV_HEADS))
    s0 = lax.dot_general(q, k0, _NT, preferred_element_type=F32)
    s1 = lax.dot_general(q, k1, _NT, preferred_element_type=F32)
    s = jnp.where(head_s < grp, s0, s1) * scale + jnp.where(slot_ref[0] >= 0, 0.0, NEG)
    kn = kn_ref[0].astype(F32)
    vn = vn_ref[0].astype(F32)
    kn8 = jnp.where(head_o < grp, kn[:, :HEAD_DIM], kn[:, HEAD_DIM:])
    vn8 = jnp.where(head_o < grp, vn[:, :HEAD_DIM], vn[:, HEAD_DIM:])
    s_new = _lane_bcast(jnp.sum(q.astype(F32) * kn8, axis=-1, keepdims=True), N_HEADS) * scale + bnew_ref[0]
    m = jnp.maximum(_lane_bcast(jnp.max(s, axis=-1, keepdims=True), N_HEADS), s_new)
    p = jnp.exp(s - m[:, :1])
    p_new = jnp.exp(s_new - m)
    l = _lane_bcast(jnp.sum(p, axis=-1, keepdims=True), N_HEADS) + p_new
    pb = p.astype(BF16)
    pv = jnp.where(head_o < grp,
                   jnp.dot(pb, v0, preferred_element_type=F32),
                   jnp.dot(pb, v1, preferred_element_type=F32))
    o = (pv + p_new * vn8) / l
    o_ref[0] = (o * sz_ref[0].astype(F32)).astype(BF16)


def _sample_attn(rows, q3, slots3, bnew3, kn3, vn3, sz3, k_rows, v_rows):
    db, n_slots = rows.shape
    per_b = lambda r, c: pl.BlockSpec((1, r, c), lambda b, rw: (b, 0, 0))
    grid_spec = pltpu.PrefetchScalarGridSpec(
        num_scalar_prefetch=1,
        grid=(db,),
        in_specs=[per_b(N_HEADS, HEAD_DIM), per_b(1, n_slots), per_b(1, LANES),
                  per_b(1, KV_WIDTH), per_b(1, KV_WIDTH), per_b(N_HEADS, HEAD_DIM),
                  pl.BlockSpec(memory_space=pl.ANY), pl.BlockSpec(memory_space=pl.ANY)],
        out_specs=per_b(N_HEADS, HEAD_DIM),
        scratch_shapes=[pltpu.VMEM((2, n_slots, N_KV_HEADS, HEAD_DIM), F32),
                        pltpu.VMEM((2, n_slots, N_KV_HEADS, HEAD_DIM), F32),
                        pltpu.SemaphoreType.DMA((2, 2))],
    )
    return pl.pallas_call(
        _sample_attn_body,
        grid_spec=grid_spec,
        out_shape=jax.ShapeDtypeStruct((db, N_HEADS, HEAD_DIM), BF16),
        compiler_params=_cparams(("arbitrary",)),
        name="attn_sample",
    )(rows, q3, slots3, bnew3, kn3, vn3, sz3, k_rows, v_rows)


def _col_tile(h_main, t, n=1):
    return h_main[:, t * COL_TILE:(t + n) * COL_TILE]


def kernel(x_prompt, x_sample, cache_k, cache_v, cache_kidx, page_table, pre_g, w_in, a_ln_g, a_ln_b,
           a_ws, a_bs, w_oa, w_ob, w_out, post_g):
    batch, seq, _ = x_prompt.shape
    db, ds, _ = x_sample.shape
    depth = w_in.shape[0]
    n_pages = page_table.shape[1]
    past = n_pages * PAGE_SIZE
    assert ds == 1 and seq % 512 == 0 and past % SCORE_CHUNK == 0
    top_k_s = min(TOPK_MAX, (past + ds) // 4)
    assert top_k_s <= past and top_k_s % LANES == 0

    pos_p = jnp.arange(seq)
    pos_s = past + (jnp.arange(db * ds) % ds)
    tabs_p = _rope_tables(pos_p, HEAD_DIM) + _rope_tables(pos_p, IDX_DIM)
    tabs_s = _rope_tables(pos_s, HEAD_DIM) + _rope_tables(pos_s, IDX_DIM)

    hp = x_prompt.reshape(batch * seq, D_MODEL)
    hs = x_sample.reshape(db * ds, D_MODEL)
    outs = [[] for _ in range(8)]
    for l in range(depth):
        w_t = w_in[l].T.astype(BF16)
        woa, wob, wout = w_oa[l].astype(BF16), w_ob[l].astype(BF16), w_out[l].astype(BF16)
        g_pre, g_post = pre_g[l][None], post_g[l][None]
        ln_g, ln_b = a_ln_g[l][None], a_ln_b[l][None]

        xn = _prenorm(hp, g_pre, tm=512)
        h_main, gv = _proj_main(xn, w_t, *tabs_p, ln_g, ln_b, tm=512, seq_rows=seq, gv_rows=CHUNK)
        k, v, ki, kb, vb, kie, kio, ws = _proj_tail(xn, w_t, *tabs_p, tm=512, seq_rows=seq)
        ya = _gate(h_main, a_ws[l], a_bs[l].T, tm=512)
        ob = _prompt_attn(h_main, ws, kie, kio, kb, vb, batch=batch, seq=seq)
        hp = _merge(hp, ya, ob, h_main, woa, wob, wout, g_post, tm=256)
        outs[0].append(k.reshape(batch, seq, N_KV_HEADS, HEAD_DIM))
        outs[1].append(v.reshape(batch, seq, N_KV_HEADS, HEAD_DIM))
        outs[2].append(ki.reshape(batch, seq, IDX_DIM))
        outs[3].append(gv.reshape(batch, CHUNK, A_WIDTH))

        m_s = db * ds
        xn = _prenorm(hs, g_pre, tm=m_s)
        h_main, gv = _proj_main(xn, w_t, *tabs_s, ln_g, ln_b, tm=m_s, seq_rows=m_s, gv_rows=m_s)
        k, v, ki, kb, vb, kie, kio, ws = _proj_tail(xn, w_t, *tabs_s, tm=m_s, seq_rows=m_s)
        w0 = jnp.repeat(a_ws[l][:, 0, 0], LANES)[None]
        b0 = jnp.repeat(a_bs[l][:, 0], LANES)[None]
        ya = _gate_row(h_main, gv, w0, b0)
        qi3 = _col_tile(h_main, T_QI).reshape(db, IDX_HEADS, IDX_DIM)
        kidx_t = jnp.swapaxes(cache_kidx[l], 1, 2)
        scores = _sample_scores(page_table, qi3, ws.reshape(db, IDX_HEADS, 1), kidx_t)
        bias, bnew = _sample_select(scores, h_main, kie, ws, top_k=top_k_s)
        bias3 = bias.reshape(db, n_pages, PAGE_SIZE)
        slots3, rows3 = _sample_compact(bias3, jnp.swapaxes(bias3, 1, 2),
                                        page_table.reshape(db, 1, n_pages), n_slots=top_k_s)
        pool_rows = lambda c: c.reshape(-1, N_KV_HEADS, HEAD_DIM)
        ob = _sample_attn(rows3.reshape(db, top_k_s),
                          _col_tile(h_main, T_Q).reshape(db, N_HEADS, HEAD_DIM),
                          slots3, bnew.reshape(db, 1, LANES),
                          kb.reshape(db, 1, KV_WIDTH), vb.reshape(db, 1, KV_WIDTH),
                          _col_tile(h_main, T_ZB).reshape(db, N_HEADS, HEAD_DIM),
                          pool_rows(cache_k[l]), pool_rows(cache_v[l])).reshape(db, B_WIDTH)
        hs = _merge(hs, ya, ob, h_main, woa, wob, wout, g_post, tm=m_s)
        outs[4].append(k.reshape(db, ds, N_KV_HEADS, HEAD_DIM))
        outs[5].append(v.reshape(db, ds, N_KV_HEADS, HEAD_DIM))
        outs[6].append(ki.reshape(db, ds, IDX_DIM))
        outs[7].append(gv.reshape(db, ds, A_WIDTH))

    st = [jnp.stack(o, axis=0) for o in outs]
    return (hp.reshape(batch, seq, D_MODEL), hs.reshape(db, ds, D_MODEL),
            st[0], st[1], st[2], st[3], st[4], st[5], st[6], st[7])
```

```python
import functools

import jax
import jax.numpy as jnp
from jax import lax
from jax.experimental import pallas as pl
from jax.experimental.pallas import tpu as pltpu

F32 = jnp.float32
BF16 = jnp.bfloat16

D_MODEL = 2048
CHUNK = 128
A_GROUPS = 8
A_WIDTH = 1024
N_HEADS = 8
N_KV_HEADS = 2
HEAD_DIM = 128
B_WIDTH = 1024
KV_WIDTH = 256
IDX_HEADS = 16
IDX_DIM = 64
TOPK_MAX = 256
ROPE_THETA = 10000.0
EPS = 1e-6
PAGE_SIZE = 128
Q_BLOCK = 128
NEG = -1e30
LOG2_E = 1.4426950408889634

LANES = 128
BF16_SUBLANES = 16
MXU_WIDTH = 256
NORM_STREAMS = 4
_NT = (((1,), (1,)), ((), ()))
COL_TILE = 1024
T_UA, T_VA, T_ZA, T_Q, T_ZB, T_QI, T_GA, T_GB = 0, 1, 2, 3, 4, 5, 6, 8
N_MAIN_TILES = 10
ROW_K = 4 * COL_TILE
ROW_ZB = ROW_K + 2 * KV_WIDTH
ROW_KI = ROW_ZB + B_WIDTH + IDX_HEADS * IDX_DIM
ROW_GA = ROW_KI + IDX_DIM + IDX_HEADS
VMEM_LIMIT = 56 * 1024 * 1024


def _cparams(sem):
    return pltpu.CompilerParams(dimension_semantics=sem, vmem_limit_bytes=VMEM_LIMIT)


def _rope_tables(pos, dim):
    half = dim // 2
    inv = ROPE_THETA ** (-jnp.arange(half, dtype=F32) / half)
    ang = pos.astype(F32)[:, None] * inv[None, :]
    cos = jnp.cos(ang)
    sin = jnp.sin(ang)
    reps = LANES // dim
    cos_t = jnp.tile(jnp.concatenate([cos, cos], axis=-1), (1, reps))
    sin_t = jnp.tile(jnp.concatenate([-sin, sin], axis=-1), (1, reps))
    return cos_t, sin_t


def _rope128(x, cos, sin):
    return x * cos + pltpu.roll(x, 64, 1) * sin


def _rope64(x, cos, sin):
    lane = lax.broadcasted_iota(jnp.int32, x.shape, 1)
    first = (lane % IDX_DIM) < (IDX_DIM // 2)
    partner = jnp.where(first, pltpu.roll(x, LANES - 32, 1), pltpu.roll(x, 32, 1))
    return x * cos + partner * sin


def _rmsnorm_rows(xf, g):
    ms = jnp.mean(xf * xf, axis=-1, keepdims=True)
    return xf * lax.rsqrt(ms + EPS) * g


_TILE_KINDS = ("copy", "ln", "silu", "rope128", "silu", "rope64", "sigmoid", "sigmoid", "sigmoid", "sigmoid")


def _sigmoid(x):
    return 0.5 * jnp.tanh(0.5 * x) + 0.5


def _prenorm_body(*refs):
    x_refs, g_ref, o_ref = refs[:-2], refs[-2], refs[-1]
    xs = [r[...] for r in x_refs]
    ms = sum(jnp.sum(x * x, axis=-1, keepdims=True) for x in xs) * (1.0 / D_MODEL)
    scale = lax.rsqrt(ms + EPS)
    w = xs[0].shape[1]
    for c, x in enumerate(xs):
        cs = slice(c * w, (c + 1) * w)
        o_ref[:, cs] = (x * scale * g_ref[:, cs]).astype(BF16)


def _prenorm(x2, g, *, tm):
    m = x2.shape[0]
    w = D_MODEL // NORM_STREAMS
    return pl.pallas_call(
        _prenorm_body,
        grid=(m // tm,),
        in_specs=[pl.BlockSpec((tm, w), lambda i, c=c: (i, c)) for c in range(NORM_STREAMS)]
                 + [pl.BlockSpec((1, D_MODEL), lambda i: (0, 0))],
        out_specs=pl.BlockSpec((tm, D_MODEL), lambda i: (i, 0)),
        out_shape=jax.ShapeDtypeStruct((m, D_MODEL), BF16),
        compiler_params=_cparams(("arbitrary",)),
        name="prenorm",
    )(*([x2] * NORM_STREAMS), g)


def _proj_main_body(xn_ref, w_ref, cq_ref, sq_ref, ci_ref, si_ref, lng_ref, lnb_ref,
                    h_ref, gv_ref, acc_ref, *, gv_rows, n_row_tiles):
    j = pl.program_id(0)
    i = pl.program_id(1)
    tm = xn_ref.shape[0]
    chunks = [slice(c * MXU_WIDTH, (c + 1) * MXU_WIDTH) for c in range(COL_TILE // MXU_WIDTH)]

    def finish_chunk(kind, cs, stats):
        acc = acc_ref[:, cs]
        if kind == "copy":
            h_ref[:, cs] = acc.astype(BF16)
        elif kind == "ln":
            mu, rstd = stats
            vn = (acc - mu) * rstd * lng_ref[:, cs] + lnb_ref[:, cs]
            h_ref[:, cs] = vn.astype(BF16)
            gv_ref[:, cs] = vn[tm - gv_rows:, :]
        elif kind == "silu":
            h_ref[:, cs] = (acc * _sigmoid(acc)).astype(BF16)
        elif kind == "sigmoid":
            h_ref[:, cs] = _sigmoid(acc).astype(BF16)
        else:
            rope, cos, sin = ((_rope128, cq_ref[...], sq_ref[...]) if kind == "rope128"
                              else (_rope64, ci_ref[...], si_ref[...]))
            for h in range(MXU_WIDTH // LANES):
                sl = slice(cs.start + h * LANES, cs.start + (h + 1) * LANES)
                h_ref[:, sl] = rope(acc[:, h * LANES:(h + 1) * LANES], cos, sin).astype(BF16)

    def run(kind, finish, matmul):
        stats = None
        if finish and kind == "ln":
            acc = acc_ref[...]
            mu = jnp.mean(acc, axis=-1, keepdims=True)
            d = acc - mu
            stats = (mu, lax.rsqrt(jnp.mean(d * d, axis=-1, keepdims=True) + EPS))
        for cs in chunks:
            if finish:
                finish_chunk(kind, cs, stats)
            if matmul:
                acc_ref[:, cs] = lax.dot_general(xn_ref[...], w_ref[cs, :], _NT, preferred_element_type=F32)

    @pl.when(i == 0)
    def _():
        run(None, False, True)

    @pl.when((i == 0) & (j != T_VA))
    def _():
        gv_ref[...] = jnp.zeros(gv_ref.shape, F32)

    for kind in sorted(set(_TILE_KINDS)):
        is_kind = functools.reduce(jnp.logical_or, [j == t for t, k in enumerate(_TILE_KINDS) if k == kind])

        @pl.when(is_kind & (i > 0) & (i < n_row_tiles))
        def _(kind=kind):
            run(kind, True, True)

        @pl.when(is_kind & (i == n_row_tiles))
        def _(kind=kind):
            run(kind, True, False)


def _main_tile_row(j):
    j = jnp.minimum(j, N_MAIN_TILES - 1)
    g = BF16_SUBLANES
    skip_kv = (ROW_ZB - T_ZB * COL_TILE) // g
    skip_idx = (ROW_GA - ROW_ZB - (T_GA - T_ZB) * COL_TILE) // g
    return (j * (COL_TILE // g) + jnp.where(j >= T_ZB, skip_kv, 0) + jnp.where(j >= T_GA, skip_idx, 0)) * g


def _proj_main(xn, w_t, cq, sq, ci, si, ln_g, ln_b, *, tm, seq_rows, gv_rows):
    m = xn.shape[0]
    tiles_per_seq = seq_rows // tm
    n_seq = m // seq_rows
    n_row_tiles = m // tm
    body = functools.partial(_proj_main_body, gv_rows=gv_rows, n_row_tiles=n_row_tiles)
    prev = lambda i: jnp.maximum(i - 1, 0)
    tab = pl.BlockSpec((tm, LANES), lambda j, i: (prev(i) % tiles_per_seq, 0))
    row = lambda n: pl.BlockSpec((1, n), lambda j, i: (0, 0))
    gv_block = lambda j, i: (jnp.where(j == T_VA, prev(i) // tiles_per_seq, n_seq + (j > T_VA)), 0)
    h_main, gv = pl.pallas_call(
        body,
        grid=(N_MAIN_TILES, n_row_tiles + 1),
        in_specs=[
            pl.BlockSpec((tm, D_MODEL), lambda j, i: (jnp.minimum(i, n_row_tiles - 1), 0)),
            pl.BlockSpec((pl.Element(COL_TILE), pl.Element(D_MODEL)), lambda j, i: (_main_tile_row(j), 0)),
            tab, tab, tab, tab,
            row(A_WIDTH), row(A_WIDTH),
        ],
        out_specs=[
            pl.BlockSpec((tm, COL_TILE), lambda j, i: (prev(i), j)),
            pl.BlockSpec((gv_rows, A_WIDTH), gv_block),
        ],
        out_shape=[
            jax.ShapeDtypeStruct((m, N_MAIN_TILES * COL_TILE), BF16),
            jax.ShapeDtypeStruct(((n_seq + 2) * gv_rows, A_WIDTH), F32),
        ],
        scratch_shapes=[pltpu.VMEM((tm, COL_TILE), F32)],
        compiler_params=_cparams(("arbitrary", "arbitrary")),
        name="proj_main",
    )(xn, w_t, cq, sq, ci, si, ln_g, ln_b)
    return h_main, gv[:n_seq * gv_rows]


def _proj_tail_body(xn_ref, wkv_ref, wix_ref, ck_ref, sk_ref, ci_ref, si_ref,
                    k_ref, v_ref, ki_ref, kb_ref, vb_ref, kie_ref, kio_ref, ws_ref):
    xn = xn_ref[...]
    acc = lax.dot_general(xn, wkv_ref[...], _NT, preferred_element_type=F32)
    cos = ck_ref[...]
    sin = sk_ref[...]
    for kh in range(N_KV_HEADS):
        sl = slice(kh * HEAD_DIM, (kh + 1) * HEAD_DIM)
        r = _rope128(acc[:, sl], cos, sin)
        k_ref[:, kh, :] = r
        kb_ref[:, sl] = r.astype(BF16)
        v_ref[:, kh, :] = acc[:, KV_WIDTH + kh * HEAD_DIM:KV_WIDTH + (kh + 1) * HEAD_DIM]
    vb_ref[...] = acc[:, KV_WIDTH:2 * KV_WIDTH].astype(BF16)
    t = lax.dot_general(xn, wix_ref[...], _NT, preferred_element_type=F32)
    r = _rope64(t, ci_ref[...], si_ref[...])
    ki_ref[...] = r[:, :IDX_DIM]
    lane = lax.broadcasted_iota(jnp.int32, r.shape, 1)
    ke = jnp.where(lane < IDX_DIM, r, 0.0)
    kie_ref[...] = ke.astype(BF16)
    kio_ref[...] = pltpu.roll(ke, IDX_DIM, 1).astype(BF16)
    ws_ref[...] = t[:, IDX_DIM:IDX_DIM + IDX_HEADS] * (IDX_HEADS ** -0.5 * IDX_DIM ** -0.5)


def _proj_tail(xn, w_t, ck, sk, ci, si, *, tm, seq_rows):
    m = xn.shape[0]
    tiles_per_seq = seq_rows // tm
    tab = pl.BlockSpec((tm, LANES), lambda i: (i % tiles_per_seq, 0))
    blk = lambda n: pl.BlockSpec((tm, n), lambda i: (i, 0))
    kv_rows = pl.BlockSpec((tm, N_KV_HEADS, HEAD_DIM), lambda i: (i, 0, 0))
    return pl.pallas_call(
        _proj_tail_body,
        grid=(m // tm,),
        in_specs=[
            blk(D_MODEL),
            pl.BlockSpec((2 * KV_WIDTH, D_MODEL), lambda i: (ROW_K // (2 * KV_WIDTH), 0)),
            pl.BlockSpec((LANES, D_MODEL), lambda i: (ROW_KI // LANES, 0)),
            tab, tab, tab, tab,
        ],
        out_specs=[kv_rows, kv_rows, blk(IDX_DIM), blk(KV_WIDTH), blk(KV_WIDTH),
                   blk(LANES), blk(LANES), blk(IDX_HEADS)],
        out_shape=[
            jax.ShapeDtypeStruct((m, N_KV_HEADS, HEAD_DIM), F32),
            jax.ShapeDtypeStruct((m, N_KV_HEADS, HEAD_DIM), F32),
            jax.ShapeDtypeStruct((m, IDX_DIM), F32),
            jax.ShapeDtypeStruct((m, KV_WIDTH), BF16),
            jax.ShapeDtypeStruct((m, KV_WIDTH), BF16),
            jax.ShapeDtypeStruct((m, LANES), BF16),
            jax.ShapeDtypeStruct((m, LANES), BF16),
            jax.ShapeDtypeStruct((m, IDX_HEADS), F32),
        ],
        compiler_params=_cparams(("arbitrary",)),
        name="proj_tail",
    )(xn, w_t, w_t, ck, sk, ci, si)


def _gate_body(u_ref, vn_ref, sz_ref, ws_ref, bst_ref, y_ref, *, n_chunks):
    rr = lax.broadcasted_iota(jnp.int32, (CHUNK, CHUNK), 0)
    cc = lax.broadcasted_iota(jnp.int32, (CHUNK, CHUNK), 1)
    tril = cc <= rr
    for g in range(A_GROUPS):
        wm = jnp.where(tril, ws_ref[g], 0.0).astype(BF16)
        b = bst_ref[:, g:g + 1]
        cs = slice(g * LANES, (g + 1) * LANES)
        for c in range(n_chunks):
            rs = slice(c * CHUNK, (c + 1) * CHUNK)
            s = jnp.dot(wm, vn_ref[rs, cs], preferred_element_type=F32) + b
            y = u_ref[rs, cs].astype(F32) * s * sz_ref[rs, cs].astype(F32)
            y_ref[rs, cs] = y.astype(BF16)


def _gate(h_main, a_ws, bs_t, *, tm):
    m = h_main.shape[0]
    body = functools.partial(_gate_body, n_chunks=tm // CHUNK)
    col = lambda t: pl.BlockSpec((tm, COL_TILE), lambda i, t=t: (i, t))
    return pl.pallas_call(
        body,
        grid=(m // tm,),
        in_specs=[col(T_UA), col(T_VA), col(T_ZA),
                  pl.BlockSpec((A_GROUPS, CHUNK, CHUNK), lambda i: (0, 0, 0)),
                  pl.BlockSpec((CHUNK, A_GROUPS), lambda i: (0, 0))],
        out_specs=pl.BlockSpec((tm, A_WIDTH), lambda i: (i, 0)),
        out_shape=jax.ShapeDtypeStruct((m, A_WIDTH), BF16),
        compiler_params=_cparams(("arbitrary",)),
        name="gate_prompt",
    )(h_main, h_main, h_main, a_ws, bs_t)


def _gate_row_body(u_ref, vn_ref, sz_ref, w0_ref, b0_ref, y_ref):
    s = vn_ref[...] * w0_ref[...] + b0_ref[...]
    y_ref[...] = (u_ref[...].astype(F32) * s * sz_ref[...].astype(F32)).astype(BF16)


def _gate_row(h_main, vn, w0, b0):
    m = h_main.shape[0]
    col = lambda t: pl.BlockSpec((m, COL_TILE), lambda i, t=t: (0, t))
    full = lambda r: pl.BlockSpec((r, A_WIDTH), lambda i: (0, 0))
    return pl.pallas_call(
        _gate_row_body,
        grid=(1,),
        in_specs=[col(T_UA), full(m), col(T_ZA), full(1), full(1)],
        out_specs=full(m),
        out_shape=jax.ShapeDtypeStruct((m, A_WIDTH), BF16),
        compiler_params=_cparams(("arbitrary",)),
        name="gate_sample",
    )(h_main, vn, h_main, w0, b0)


def _lane_bcast(col, rows):
    return jnp.broadcast_to(col, (rows, LANES))


def _select_bias(sc_ref, extra_ref, kp, row_min, row_max, *, n_cols):
    rows = sc_ref.shape[0]
    n_tiles = n_cols // LANES
    extra = None if extra_ref is None else extra_ref[...]

    def count(pred):
        acc = jnp.zeros((rows, LANES), F32)
        for c in range(n_tiles):
            acc = acc + jnp.where(pred(sc_ref[:, c * LANES:(c + 1) * LANES], c), 1.0, 0.0)
        tot = jnp.sum(acc, axis=1, keepdims=True)
        return _lane_bcast(tot, rows)

    def count_ge(x):
        c = count(lambda s, _: s >= x)
        if extra is not None:
            c = c + jnp.where(extra >= x, 1.0, 0.0)
        return c

    c_max = count_ge(row_max)
    top = c_max >= kp
    lo0 = jnp.where(top, row_max, row_min)
    c0 = jnp.where(top, c_max, count_ge(row_min))

    def count_ge3(x1, x2, x3):
        a1 = jnp.zeros((rows, LANES), F32)
        a2 = jnp.zeros((rows, LANES), F32)
        a3 = jnp.zeros((rows, LANES), F32)
        for c in range(n_tiles):
            s = sc_ref[:, c * LANES:(c + 1) * LANES]
            a1 = a1 + jnp.where(s >= x1, 1.0, 0.0)
            a2 = a2 + jnp.where(s >= x2, 1.0, 0.0)
            a3 = a3 + jnp.where(s >= x3, 1.0, 0.0)
        res = []
        for a, x in ((a1, x1), (a2, x2), (a3, x3)):
            tot = _lane_bcast(jnp.sum(a, axis=1, keepdims=True), rows)
            if extra is not None:
                tot = tot + jnp.where(extra >= x, 1.0, 0.0)
            res.append(tot)
        return res

    def step(st):
        lo, hi, c_lo, _, it = st
        mid = 0.5 * lo + 0.5 * hi
        act = (c_lo != kp) & (mid > lo) & (mid < hi)
        any_act = jnp.max(jnp.where(act, 1.0, 0.0))
        clamp = lambda x: jnp.minimum(jnp.maximum(x, lo), hi)
        q1 = clamp(0.75 * lo + 0.25 * hi)
        q3 = clamp(0.25 * lo + 0.75 * hi)
        c1, c2, c3 = count_ge3(q1, mid, q3)
        g1, g2, g3 = c1 >= kp, c2 >= kp, c3 >= kp
        lo_n = jnp.where(g3, q3, jnp.where(g2, mid, jnp.where(g1, q1, lo)))
        c_n = jnp.where(g3, c3, jnp.where(g2, c2, jnp.where(g1, c1, c_lo)))
        hi_n = jnp.where(g3, hi, jnp.where(g2, q3, jnp.where(g1, mid, jnp.minimum(q1, mid))))
        return (jnp.where(act, lo_n, lo), jnp.where(act, hi_n, hi), jnp.where(act, c_n, c_lo),
                any_act, it + 1)

    def cond(st):
        return (st[3] > 0.0) & (st[4] < 400)

    lo, _, c_lo, _, _ = lax.while_loop(cond, step, (lo0, row_max, c0, jnp.float32(1.0), jnp.int32(0)))

    exact = jnp.max(jnp.where(c_lo != kp, 1.0, 0.0)) == 0.0

    @pl.when(exact)
    def _():
        for c in range(n_tiles):
            sl = slice(c * LANES, (c + 1) * LANES)
            sc_ref[:, sl] = jnp.where(sc_ref[:, sl] >= lo, 0.0, NEG)
        if extra is not None:
            extra_ref[...] = jnp.where(extra >= lo, 0.0, NEG)

    @pl.when(jnp.logical_not(exact))
    def _():
        n_gt = count(lambda s, _: s > lo)
        if extra is not None:
            n_gt = n_gt + jnp.where(extra > lo, 1.0, 0.0)
        need = kp - n_gt
        lane = lax.broadcasted_iota(jnp.int32, (rows, LANES), 1).astype(F32)

        def count_eq_upto(jx):
            c = count(lambda s, c: (s == lo) & (lane + float(c * LANES) <= jx))
            if extra is not None:
                c = c + jnp.where((extra == lo) & (jx >= float(n_cols)), 1.0, 0.0)
            return c

        last = n_cols if extra is not None else n_cols - 1
        j_lo = jnp.full((rows, LANES), -1.0, F32)
        j_hi = jnp.full((rows, LANES), float(last), F32)

        def jstep(_, st):
            a, b = st
            mid = jnp.floor(0.5 * (a + b))
            ok = count_eq_upto(mid) >= need
            return jnp.where(ok, a, mid), jnp.where(ok, mid, b)

        n_steps = max(1, (n_cols + 1).bit_length())
        _, j_hi = lax.fori_loop(0, n_steps, jstep, (j_lo, j_hi))
        for c in range(n_tiles):
            sl = slice(c * LANES, (c + 1) * LANES)
            s = sc_ref[:, sl]
            keep = (s > lo) | ((s == lo) & (lane + float(c * LANES) <= j_hi))
            sc_ref[:, sl] = jnp.where(keep, 0.0, NEG)
        if extra is not None:
            keep = (extra > lo) | ((extra == lo) & (j_hi >= float(n_cols)))
            extra_ref[...] = jnp.where(keep, 0.0, NEG)


def _prompt_attn_block(nk, qi_ref, ws_ref, kie_ref, kio_ref, q_ref, kb_ref, vb_ref, sz_ref,
                       o_ref, sc_ref, *, top_k, key_chunk):
    qb = pl.program_id(1)
    rows = Q_BLOCK
    n_pairs = IDX_HEADS // 2
    pos = qb * Q_BLOCK + lax.broadcasted_iota(jnp.int32, (rows, LANES), 0)

    qs = jnp.concatenate([qi_ref[:, p * LANES:(p + 1) * LANES] for p in range(n_pairs)], axis=0)
    wcols = [_lane_bcast(ws_ref[:, h:h + 1], rows) for h in range(IDX_HEADS)]
    nt = (((1,), (1,)), ((), ()))
    rmax = jnp.full((rows, LANES), -jnp.inf, F32)
    rmin = jnp.full((rows, LANES), jnp.inf, F32)
    for kc in range(0, nk, key_chunk):
        le = lax.dot_general(qs, kie_ref[kc:kc + key_chunk, :], nt, preferred_element_type=F32)
        lo = lax.dot_general(qs, kio_ref[kc:kc + key_chunk, :], nt, preferred_element_type=F32)
        for c in range(key_chunk // LANES):
            cs = slice(c * LANES, (c + 1) * LANES)
            acc = jnp.zeros((rows, LANES), F32)
            for p in range(n_pairs):
                rs = slice(p * rows, (p + 1) * rows)
                acc = acc + jnp.maximum(le[rs, cs], 0.0) * wcols[2 * p]
                acc = acc + jnp.maximum(lo[rs, cs], 0.0) * wcols[2 * p + 1]
            kpos = kc + c * LANES + lax.broadcasted_iota(jnp.int32, (rows, LANES), 1)
            causal = kpos <= pos
            rmax = jnp.maximum(rmax, jnp.where(causal, acc, -jnp.inf))
            rmin = jnp.minimum(rmin, jnp.where(causal, acc, jnp.inf))
            sc_ref[:, kc + c * LANES:kc + (c + 1) * LANES] = jnp.where(causal, acc, -jnp.inf)

    row_max = _lane_bcast(jnp.max(rmax, axis=1, keepdims=True), rows)
    row_min = _lane_bcast(jnp.min(rmin, axis=1, keepdims=True), rows)
    kp = jnp.minimum(pos + 1, top_k).astype(F32)
    _select_bias(sc_ref, None, kp, row_min, row_max, n_cols=nk)

    grp = N_HEADS // N_KV_HEADS
    bias = jnp.concatenate([sc_ref[:, :nk]] * grp, axis=0)
    for kh in range(N_KV_HEADS):
        qh = jnp.concatenate(
            [q_ref[:, (kh * grp + g) * HEAD_DIM:(kh * grp + g + 1) * HEAD_DIM] for g in range(grp)], axis=0)
        ks = slice(kh * HEAD_DIM, (kh + 1) * HEAD_DIM)
        s = lax.dot_general(qh, kb_ref[0:nk, ks], nt, preferred_element_type=F32) + bias
        m = jnp.max(s, axis=-1, keepdims=True)
        p = jnp.exp2((s - m) * (HEAD_DIM ** -0.5 * LOG2_E))
        l = jnp.sum(p, axis=-1, keepdims=True)
        o = jnp.dot(p.astype(BF16), vb_ref[0:nk, ks], preferred_element_type=F32) / l
        for g in range(grp):
            hs = slice((kh * grp + g) * HEAD_DIM, (kh * grp + g + 1) * HEAD_DIM)
            o_ref[:, hs] = (o[g * rows:(g + 1) * rows, :] * sz_ref[:, hs].astype(F32)).astype(BF16)


def _prompt_attn_body(*refs, seq, top_k, n_buckets, key_chunk):
    qb = pl.program_id(1)
    span = seq // n_buckets
    per = span // Q_BLOCK
    for c in range(n_buckets):
        @pl.when(qb // per == c)
        def _(c=c):
            _prompt_attn_block((c + 1) * span, *refs, top_k=top_k, key_chunk=key_chunk)


def _prompt_attn(h_main, ws, kie, kio, kb, vb, *, batch, seq):
    top_k = min(TOPK_MAX, seq // 4)
    n_qb = seq // Q_BLOCK
    key_chunk = min(512, seq)
    n_buckets = max(1, seq // 512)
    body = functools.partial(_prompt_attn_body, seq=seq, top_k=top_k, n_buckets=n_buckets, key_chunk=key_chunk)
    col = lambda t: pl.BlockSpec((Q_BLOCK, COL_TILE), lambda b, q, t=t: (b * n_qb + q, t))
    seqblk = lambda n: pl.BlockSpec((seq, n), lambda b, q: (b, 0))
    return pl.pallas_call(
        body,
        grid=(batch, n_qb),
        in_specs=[col(T_QI),
                  pl.BlockSpec((Q_BLOCK, IDX_HEADS), lambda b, q: (b * n_qb + q, 0)),
                  seqblk(LANES), seqblk(LANES),
                  col(T_Q), seqblk(KV_WIDTH), seqblk(KV_WIDTH), col(T_ZB)],
        out_specs=pl.BlockSpec((Q_BLOCK, B_WIDTH), lambda b, q: (b * n_qb + q, 0)),
        out_shape=jax.ShapeDtypeStruct((batch * seq, B_WIDTH), BF16),
        scratch_shapes=[pltpu.VMEM((Q_BLOCK, seq), F32)],
        compiler_params=_cparams(("arbitrary", "arbitrary")),
        name="attn_prompt",
    )(h_main, ws, kie, kio, h_main, kb, vb, h_main)


def _merge_body(x_ref, ya_ref, ob_ref, ga_ref, gb_ref, woa_ref, wob_ref, wout_ref, pg_ref, o_ref):
    pa = jnp.dot(ya_ref[...], woa_ref[...], preferred_element_type=F32)
    pb = jnp.dot(ob_ref[...], wob_ref[...], preferred_element_type=F32)
    mix = ga_ref[...].astype(F32) * pa + gb_ref[...].astype(F32) * pb
    r = jnp.dot(mix.astype(BF16), wout_ref[...], preferred_element_type=F32)
    o_ref[...] = x_ref[...] + _rmsnorm_rows(r, pg_ref[...])


def _merge(x2, ya, ob, h_main, w_oa, w_ob, w_out, post_g, *, tm):
    m = x2.shape[0]
    const = lambda r, c: pl.BlockSpec((r, c), lambda i: (0, 0), pipeline_mode=pl.Buffered(1))
    return pl.pallas_call(
        _merge_body,
        grid=(m // tm,),
        in_specs=[
            pl.BlockSpec((tm, D_MODEL), lambda i: (i, 0)),
            pl.BlockSpec((tm, A_WIDTH), lambda i: (i, 0)),
            pl.BlockSpec((tm, B_WIDTH), lambda i: (i, 0)),
            pl.BlockSpec((tm, D_MODEL), lambda i: (i, T_GA // 2)),
            pl.BlockSpec((tm, D_MODEL), lambda i: (i, T_GB // 2)),
            const(A_WIDTH, D_MODEL), const(B_WIDTH, D_MODEL), const(D_MODEL, D_MODEL),
            const(1, D_MODEL),
        ],
        out_specs=pl.BlockSpec((tm, D_MODEL), lambda i: (i, 0)),
        out_shape=jax.ShapeDtypeStruct((m, D_MODEL), F32),
        compiler_params=_cparams(("arbitrary",)),
        name="merge",
    )(x2, ya, ob, h_main, h_main, w_oa, w_ob, w_out, post_g)


SCORE_CHUNK = 2048


def _sample_scores_body(pt_ref, q_ref, w_ref, kidx_hbm, o_ref, buf, sem):
    db, n_pages = pt_ref.shape
    past = n_pages * PAGE_SIZE

    def page_copy(b, p, slot):
        dst = buf.at[slot, :, pl.ds(pl.multiple_of(p * PAGE_SIZE, PAGE_SIZE), PAGE_SIZE)]
        return pltpu.make_async_copy(kidx_hbm.at[pt_ref[b, p]], dst, sem.at[slot])

    def start_all(b, slot):
        def f(p, c):
            page_copy(b, p, slot).start()
            return c
        lax.fori_loop(0, n_pages, f, 0, unroll=8)

    def wait_all(slot):
        for p in range(n_pages):
            dst = buf.at[slot, :, pl.ds(p * PAGE_SIZE, PAGE_SIZE)]
            pltpu.make_async_copy(kidx_hbm.at[0], dst, sem.at[slot]).wait()

    start_all(0, 0)

    def per_seq(b, c):
        slot = b % 2

        @pl.when(b + 1 < db)
        def _():
            start_all(b + 1, 1 - slot)

        wait_all(slot)
        q = q_ref[b]
        w = w_ref[b]
        for ch in range(past // SCORE_CHUNK):
            cs = slice(ch * SCORE_CHUNK, (ch + 1) * SCORE_CHUNK)
            logit = jnp.dot(q, buf[slot, :, cs].astype(BF16), preferred_element_type=F32)
            o_ref[pl.ds(b, 1), cs] = jnp.sum(jnp.maximum(logit, 0.0) * w, axis=0, keepdims=True)
        return c

    lax.fori_loop(0, db, per_seq, 0)


def _sample_scores(page_table, qi3, ws3, kidx_pages_t):
    db, n_pages = page_table.shape
    past = n_pages * PAGE_SIZE
    grid_spec = pltpu.PrefetchScalarGridSpec(
        num_scalar_prefetch=1,
        grid=(1,),
        in_specs=[pl.BlockSpec((db, IDX_HEADS, IDX_DIM), lambda i, pt: (0, 0, 0)),
                  pl.BlockSpec((db, IDX_HEADS, 1), lambda i, pt: (0, 0, 0)),
                  pl.BlockSpec(memory_space=pl.ANY)],
        out_specs=pl.BlockSpec((db, past), lambda i, pt: (0, 0)),
        scratch_shapes=[pltpu.VMEM((2, IDX_DIM, past), F32), pltpu.SemaphoreType.DMA((2,))],
    )
    return pl.pallas_call(
        _sample_scores_body,
        grid_spec=grid_spec,
        out_shape=jax.ShapeDtypeStruct((db, past), F32),
        compiler_params=_cparams(("arbitrary",)),
        name="scores_sample",
    )(page_table, qi3, ws3, kidx_pages_t)


def _sample_select_body(sc_ref, qi_ref, kie_ref, ws_ref, bias_ref, bnew_ref, *, top_k):
    rows, past = sc_ref.shape
    lane = lax.broadcasted_iota(jnp.int32, (rows, LANES), 1)
    ki = kie_ref[...].astype(F32)
    ki = ki + pltpu.roll(ki, IDX_DIM, 1)
    s_new = jnp.zeros((rows, 1), F32)
    for p in range(IDX_HEADS // 2):
        prod = qi_ref[:, p * LANES:(p + 1) * LANES].astype(F32) * ki
        l_even = jnp.sum(jnp.where(lane < IDX_DIM, prod, 0.0), axis=1, keepdims=True)
        l_odd = jnp.sum(jnp.where(lane >= IDX_DIM, prod, 0.0), axis=1, keepdims=True)
        s_new = s_new + jnp.maximum(l_even, 0.0) * ws_ref[:, 2 * p:2 * p + 1]
        s_new = s_new + jnp.maximum(l_odd, 0.0) * ws_ref[:, 2 * p + 1:2 * p + 2]
    extra = _lane_bcast(s_new, rows)
    bnew_ref[...] = extra
    rmax = extra
    rmin = extra
    for c in range(past // LANES):
        sl = slice(c * LANES, (c + 1) * LANES)
        s = sc_ref[:, sl]
        bias_ref[:, sl] = s
        rmax = jnp.maximum(rmax, s)
        rmin = jnp.minimum(rmin, s)
    row_max = _lane_bcast(jnp.max(rmax, axis=1, keepdims=True), rows)
    row_min = _lane_bcast(jnp.min(rmin, axis=1, keepdims=True), rows)
    kp = jnp.full((rows, LANES), float(top_k), F32)
    _select_bias(bias_ref, bnew_ref, kp, row_min, row_max, n_cols=past)


def _sample_select(scores, h_main, kie, ws, *, top_k):
    db, past = scores.shape
    full = lambda r, c: pl.BlockSpec((r, c), lambda i: (0, 0))
    return pl.pallas_call(
        functools.partial(_sample_select_body, top_k=top_k),
        grid=(1,),
        in_specs=[full(db, past),
                  pl.BlockSpec((db, COL_TILE), lambda i: (0, T_QI)),
                  full(db, LANES), full(db, IDX_HEADS)],
        out_specs=[full(db, past), full(db, LANES)],
        out_shape=[jax.ShapeDtypeStruct((db, past), F32), jax.ShapeDtypeStruct((db, LANES), F32)],
        compiler_params=_cparams(("arbitrary",)),
        name="select_sample",
    )(scores, h_main, kie, ws)


def _sample_compact_body(m_ref, mt_ref, pt_ref, idx_ref, row_ref, *, n_slots):
    n_pages = m_ref.shape[1]
    pt = jnp.broadcast_to(pt_ref[0], (8, n_pages))
    pt_hi = (pt // PAGE_SIZE).astype(F32).astype(BF16)
    pt_lo = (pt % PAGE_SIZE).astype(F32).astype(BF16)
    one = lambda pred: jnp.where(pred, 1.0, 0.0)
    kept = m_ref[0] == 0.0
    kept_t = mt_ref[0] == 0.0
    ri = lax.broadcasted_iota(jnp.int32, (PAGE_SIZE, PAGE_SIZE), 0)
    ci = lax.broadcasted_iota(jnp.int32, (PAGE_SIZE, PAGE_SIZE), 1)
    rp = lax.broadcasted_iota(jnp.int32, (n_pages, n_pages), 0)
    cp = lax.broadcasted_iota(jnp.int32, (n_pages, n_pages), 1)
    plt = jnp.dot(one(ci <= ri).astype(BF16), one(kept_t).astype(BF16), preferred_element_type=F32)
    n_row = plt[PAGE_SIZE - 1:PAGE_SIZE, :]
    n_col = _lane_bcast(jnp.sum(one(kept), axis=1, keepdims=True), n_pages)
    e_col = jnp.dot(one(cp <= rp).astype(BF16), n_col.astype(BF16), preferred_element_type=F32)
    n_row8 = jnp.broadcast_to(n_row, (8, n_pages))
    e_row8 = jnp.dot(n_row8.astype(BF16), one(rp <= cp).astype(BF16), preferred_element_type=F32)
    off_row8 = e_row8 - n_row8
    n_total = e_col[n_pages - 1:n_pages, :]
    page_id = lax.broadcasted_iota(jnp.int32, (n_pages, LANES), 0).astype(F32)
    for jt in range(n_slots // LANES):
        j = (lax.broadcasted_iota(jnp.int32, (1, LANES), 1) + jt * LANES).astype(F32)
        page_j = jnp.sum(one(e_col <= j), axis=0, keepdims=True)
        pick = one(page_id == page_j).astype(BF16)
        prefix_j = jnp.dot(plt.astype(BF16), pick, preferred_element_type=F32)
        off_j = jnp.dot(off_row8.astype(BF16), pick, preferred_element_type=F32)[0:1]
        local_j = jnp.sum(one(prefix_j <= j - off_j), axis=0, keepdims=True)
        pos = page_j * float(PAGE_SIZE) + local_j
        phys = (jnp.dot(pt_hi, pick, preferred_element_type=F32)[0:1] * float(PAGE_SIZE)
                + jnp.dot(pt_lo, pick, preferred_element_type=F32)[0:1])
        row = phys * float(PAGE_SIZE) + local_j
        used = j < n_total
        sl = slice(jt * LANES, (jt + 1) * LANES)
        idx_ref[0, :, sl] = jnp.where(used, pos, -1.0).astype(jnp.int32)
        row_ref[0, :, sl] = jnp.where(used, row, 0.0).astype(jnp.int32)


def _sample_compact(bias3, bias3_t, pt3, *, n_slots):
    db, n_pages, _ = bias3.shape
    out = pl.BlockSpec((1, 1, n_slots), lambda b: (b, 0, 0))
    return pl.pallas_call(
        functools.partial(_sample_compact_body, n_slots=n_slots),
        grid=(db,),
        in_specs=[pl.BlockSpec((1, n_pages, PAGE_SIZE), lambda b: (b, 0, 0)),
                  pl.BlockSpec((1, PAGE_SIZE, n_pages), lambda b: (b, 0, 0)),
                  pl.BlockSpec((1, 1, n_pages), lambda b: (b, 0, 0))],
        out_specs=[out, out],
        out_shape=[jax.ShapeDtypeStruct((db, 1, n_slots), jnp.int32)] * 2,
        compiler_params=_cparams(("arbitrary",)),
        name="compact_sample",
    )(bias3, bias3_t, pt3)


def _sample_attn_body(row_ref, q_ref, slot_ref, bnew_ref, kn_ref, vn_ref, sz_ref, k_hbm, v_hbm,
                      o_ref, kbuf, vbuf, sem):
    b = pl.program_id(0)
    nb = pl.num_programs(0)
    n_slots = kbuf.shape[1]
    buf = b % 2
    grp = N_HEADS // N_KV_HEADS
    scale = HEAD_DIM ** -0.5

    def row_copies(seq, j, to):
        r = row_ref[seq, j]
        return (pltpu.make_async_copy(k_hbm.at[r], kbuf.at[to, j], sem.at[0, to]),
                pltpu.make_async_copy(v_hbm.at[r], vbuf.at[to, j], sem.at[1, to]))

    def start_all(seq, to):
        def f(j, c):
            ck, cv = row_copies(seq, j, to)
            ck.start()
            cv.start()
            return c
        lax.fori_loop(0, n_slots, f, 0, unroll=8)

    def wait_all(to):
        for blk in range(n_slots // PAGE_SIZE):
            rows = pl.ds(blk * PAGE_SIZE, PAGE_SIZE)
            src = pl.ds(0, PAGE_SIZE)
            pltpu.make_async_copy(k_hbm.at[src], kbuf.at[to, rows], sem.at[0, to]).wait()
            pltpu.make_async_copy(v_hbm.at[src], vbuf.at[to, rows], sem.at[1, to]).wait()

    @pl.when(b == 0)
    def _():
        start_all(0, 0)

    @pl.when(b + 1 < nb)
    def _():
        start_all(b + 1, 1 - buf)

    wait_all(buf)

    q = q_ref[0]
    head_s = lax.broadcasted_iota(jnp.int32, (N_HEADS, n_slots), 0)
    head_o = lax.broadcasted_iota(jnp.int32, (N_HEADS, HEAD_DIM), 0)
    k0, k1 = (kbuf[buf, :, kh, :].astype(BF16) for kh in range(N_KV_HEADS))
    v0, v1 = (vbuf[buf, :, kh, :].astype(BF16) for kh in range(N_KV_HEADS))
    s0 = lax.dot_general(q, k0, _NT, preferred_element_type=F32)
    s1 = lax.dot_general(q, k1, _NT, preferred_element_type=F32)
    s = jnp.where(head_s < grp, s0, s1) * scale + jnp.where(slot_ref[0] >= 0, 0.0, NEG)
    kn = kn_ref[0].astype(F32)
    vn = vn_ref[0].astype(F32)
    kn8 = jnp.where(head_o < grp, kn[:, :HEAD_DIM], kn[:, HEAD_DIM:])
    vn8 = jnp.where(head_o < grp, vn[:, :HEAD_DIM], vn[:, HEAD_DIM:])
    s_new = _lane_bcast(jnp.sum(q.astype(F32) * kn8, axis=-1, keepdims=True), N_HEADS) * scale + bnew_ref[0]
    m = jnp.maximum(_lane_bcast(jnp.max(s, axis=-1, keepdims=True), N_HEADS), s_new)
    p = jnp.exp(s - m[:, :1])
    p_new = jnp.exp(s_new - m)
    l = _lane_bcast(jnp.sum(p, axis=-1, keepdims=True), N_HEADS) + p_new
    pb = p.astype(BF16)
    pv = jnp.where(head_o < grp,
                   jnp.dot(pb, v0, preferred_element_type=F32),
                   jnp.dot(pb, v1, preferred_element_type=F32))
    o = (pv + p_new * vn8) / l
    o_ref[0] = (o * sz_ref[0].astype(F32)).astype(BF16)


def _sample_attn(rows, q3, slots3, bnew3, kn3, vn3, sz3, k_rows, v_rows):
    db, n_slots = rows.shape
    per_b = lambda r, c: pl.BlockSpec((1, r, c), lambda b, rw: (b, 0, 0))
    grid_spec = pltpu.PrefetchScalarGridSpec(
        num_scalar_prefetch=1,
        grid=(db,),
        in_specs=[per_b(N_HEADS, HEAD_DIM), per_b(1, n_slots), per_b(1, LANES),
                  per_b(1, KV_WIDTH), per_b(1, KV_WIDTH), per_b(N_HEADS, HEAD_DIM),
                  pl.BlockSpec(memory_space=pl.ANY), pl.BlockSpec(memory_space=pl.ANY)],
        out_specs=per_b(N_HEADS, HEAD_DIM),
        scratch_shapes=[pltpu.VMEM((2, n_slots, N_KV_HEADS, HEAD_DIM), F32),
                        pltpu.VMEM((2, n_slots, N_KV_HEADS, HEAD_DIM), F32),
                        pltpu.SemaphoreType.DMA((2, 2))],
    )
    return pl.pallas_call(
        _sample_attn_body,
        grid_spec=grid_spec,
        out_shape=jax.ShapeDtypeStruct((db, N_HEADS, HEAD_DIM), BF16),
        compiler_params=_cparams(("arbitrary",)),
        name="attn_sample",
    )(rows, q3, slots3, bnew3, kn3, vn3, sz3, k_rows, v_rows)


def _col_tile(h_main, t, n=1):
    return h_main[:, t * COL_TILE:(t + n) * COL_TILE]


def kernel(x_prompt, x_sample, cache_k, cache_v, cache_kidx, page_table, pre_g, w_in, a_ln_g, a_ln_b,
           a_ws, a_bs, w_oa, w_ob, w_out, post_g):
    batch, seq, _ = x_prompt.shape
    db, ds, _ = x_sample.shape
    depth = w_in.shape[0]
    n_pages = page_table.shape[1]
    past = n_pages * PAGE_SIZE
    assert ds == 1 and seq % 512 == 0 and past % SCORE_CHUNK == 0
    top_k_s = min(TOPK_MAX, (past + ds) // 4)
    assert top_k_s <= past and top_k_s % LANES == 0

    pos_p = jnp.arange(seq)
    pos_s = past + (jnp.arange(db * ds) % ds)
    tabs_p = _rope_tables(pos_p, HEAD_DIM) + _rope_tables(pos_p, IDX_DIM)
    tabs_s = _rope_tables(pos_s, HEAD_DIM) + _rope_tables(pos_s, IDX_DIM)

    hp = x_prompt.reshape(batch * seq, D_MODEL)
    hs = x_sample.reshape(db * ds, D_MODEL)
    outs = [[] for _ in range(8)]
    for l in range(depth):
        w_t = w_in[l].T.astype(BF16)
        woa, wob, wout = w_oa[l].astype(BF16), w_ob[l].astype(BF16), w_out[l].astype(BF16)
        g_pre, g_post = pre_g[l][None], post_g[l][None]
        ln_g, ln_b = a_ln_g[l][None], a_ln_b[l][None]

        xn = _prenorm(hp, g_pre, tm=512)
        h_main, gv = _proj_main(xn, w_t, *tabs_p, ln_g, ln_b, tm=512, seq_rows=seq, gv_rows=CHUNK)
        k, v, ki, kb, vb, kie, kio, ws = _proj_tail(xn, w_t, *tabs_p, tm=512, seq_rows=seq)
        ya = _gate(h_main, a_ws[l], a_bs[l].T, tm=512)
        ob = _prompt_attn(h_main, ws, kie, kio, kb, vb, batch=batch, seq=seq)
        hp = _merge(hp, ya, ob, h_main, woa, wob, wout, g_post, tm=256)
        outs[0].append(k.reshape(batch, seq, N_KV_HEADS, HEAD_DIM))
        outs[1].append(v.reshape(batch, seq, N_KV_HEADS, HEAD_DIM))
        outs[2].append(ki.reshape(batch, seq, IDX_DIM))
        outs[3].append(gv.reshape(batch, CHUNK, A_WIDTH))

        m_s = db * ds
        xn = _prenorm(hs, g_pre, tm=m_s)
        h_main, gv = _proj_main(xn, w_t, *tabs_s, ln_g, ln_b, tm=m_s, seq_rows=m_s, gv_rows=m_s)
        k, v, ki, kb, vb, kie, kio, ws = _proj_tail(xn, w_t, *tabs_s, tm=m_s, seq_rows=m_s)
        w0 = jnp.repeat(a_ws[l][:, 0, 0], LANES)[None]
        b0 = jnp.repeat(a_bs[l][:, 0], LANES)[None]
        ya = _gate_row(h_main, gv, w0, b0)
        qi3 = _col_tile(h_main, T_QI).reshape(db, IDX_HEADS, IDX_DIM)
        kidx_t = jnp.swapaxes(cache_kidx[l], 1, 2)
        scores = _sample_scores(page_table, qi3, ws.reshape(db, IDX_HEADS, 1), kidx_t)
        bias, bnew = _sample_select(scores, h_main, kie, ws, top_k=top_k_s)
        bias3 = bias.reshape(db, n_pages, PAGE_SIZE)
        slots3, rows3 = _sample_compact(bias3, jnp.swapaxes(bias3, 1, 2),
                                        page_table.reshape(db, 1, n_pages), n_slots=top_k_s)
        pool_rows = lambda c: c.reshape(-1, N_KV_HEADS, HEAD_DIM)
        ob = _sample_attn(rows3.reshape(db, top_k_s),
                          _col_tile(h_main, T_Q).reshape(db, N_HEADS, HEAD_DIM),
                          slots3, bnew.reshape(db, 1, LANES),
                          kb.reshape(db, 1, KV_WIDTH), vb.reshape(db, 1, KV_WIDTH),
                          _col_tile(h_main, T_ZB).reshape(db, N_HEADS, HEAD_DIM),
                          pool_rows(cache_k[l]), pool_rows(cache_v[l])).reshape(db, B_WIDTH)
        hs = _merge(hs, ya, ob, h_main, woa, wob, wout, g_post, tm=m_s)
        outs[4].append(k.reshape(db, ds, N_KV_HEADS, HEAD_DIM))
        outs[5].append(v.reshape(db, ds, N_KV_HEADS, HEAD_DIM))
        outs[6].append(ki.reshape(db, ds, IDX_DIM))
        outs[7].append(gv.reshape(db, ds, A_WIDTH))

    st = [jnp.stack(o, axis=0) for o in outs]
    return (hp.reshape(batch, seq, D_MODEL), hs.reshape(db, ds, D_MODEL),
            st[0], st[1], st[2], st[3], st[4], st[5], st[6], st[7])
```

```python
import functools

import jax
import jax.numpy as jnp
from jax import lax
from jax.experimental import pallas as pl
from jax.experimental.pallas import tpu as pltpu

F32 = jnp.float32
BF16 = jnp.bfloat16

D_MODEL = 2048
CHUNK = 128
A_GROUPS = 8
A_WIDTH = 1024
N_HEADS = 8
N_KV_HEADS = 2
HEAD_DIM = 128
B_WIDTH = 1024
KV_WIDTH = 256
IDX_HEADS = 16
IDX_DIM = 64
TOPK_MAX = 256
ROPE_THETA = 10000.0
EPS = 1e-6
PAGE_SIZE = 128
Q_BLOCK = 128
NEG = -1e30
LOG2_E = 1.4426950408889634

LANES = 128
BF16_SUBLANES = 16
MXU_WIDTH = 256
NORM_STREAMS = 4
_NT = (((1,), (1,)), ((), ()))
COL_TILE = 1024
T_UA, T_VA, T_ZA, T_Q, T_ZB, T_QI, T_GA, T_GB = 0, 1, 2, 3, 4, 5, 6, 8
N_MAIN_TILES = 10
ROW_K = 4 * COL_TILE
ROW_ZB = ROW_K + 2 * KV_WIDTH
ROW_KI = ROW_ZB + B_WIDTH + IDX_HEADS * IDX_DIM
ROW_GA = ROW_KI + IDX_DIM + IDX_HEADS
VMEM_LIMIT = 56 * 1024 * 1024


def _cparams(sem):
    return pltpu.CompilerParams(dimension_semantics=sem, vmem_limit_bytes=VMEM_LIMIT)


def _rope_tables(pos, dim):
    half = dim // 2
    inv = ROPE_THETA ** (-jnp.arange(half, dtype=F32) / half)
    ang = pos.astype(F32)[:, None] * inv[None, :]
    cos = jnp.cos(ang)
    sin = jnp.sin(ang)
    reps = LANES // dim
    cos_t = jnp.tile(jnp.concatenate([cos, cos], axis=-1), (1, reps))
    sin_t = jnp.tile(jnp.concatenate([-sin, sin], axis=-1), (1, reps))
    return cos_t, sin_t


def _rope128(x, cos, sin):
    return x * cos + pltpu.roll(x, 64, 1) * sin


def _rope64(x, cos, sin):
    lane = lax.broadcasted_iota(jnp.int32, x.shape, 1)
    first = (lane % IDX_DIM) < (IDX_DIM // 2)
    partner = jnp.where(first, pltpu.roll(x, LANES - 32, 1), pltpu.roll(x, 32, 1))
    return x * cos + partner * sin


def _rmsnorm_rows(xf, g):
    ms = jnp.mean(xf * xf, axis=-1, keepdims=True)
    return xf * lax.rsqrt(ms + EPS) * g


_TILE_KINDS = ("copy", "ln", "silu", "rope128", "silu", "rope64", "sigmoid", "sigmoid", "sigmoid", "sigmoid")


def _sigmoid(x):
    return 0.5 * jnp.tanh(0.5 * x) + 0.5


def _prenorm_body(*refs):
    x_refs, g_ref, o_ref = refs[:-2], refs[-2], refs[-1]
    xs = [r[...] for r in x_refs]
    ms = sum(jnp.sum(x * x, axis=-1, keepdims=True) for x in xs) * (1.0 / D_MODEL)
    scale = lax.rsqrt(ms + EPS)
    w = xs[0].shape[1]
    for c, x in enumerate(xs):
        cs = slice(c * w, (c + 1) * w)
        o_ref[:, cs] = (x * scale * g_ref[:, cs]).astype(BF16)


def _prenorm(x2, g, *, tm):
    m = x2.shape[0]
    w = D_MODEL // NORM_STREAMS
    return pl.pallas_call(
        _prenorm_body,
        grid=(m // tm,),
        in_specs=[pl.BlockSpec((tm, w), lambda i, c=c: (i, c)) for c in range(NORM_STREAMS)]
                 + [pl.BlockSpec((1, D_MODEL), lambda i: (0, 0))],
        out_specs=pl.BlockSpec((tm, D_MODEL), lambda i: (i, 0)),
        out_shape=jax.ShapeDtypeStruct((m, D_MODEL), BF16),
        compiler_params=_cparams(("arbitrary",)),
        name="prenorm",
    )(*([x2] * NORM_STREAMS), g)


def _proj_main_body(xn_ref, w_ref, cq_ref, sq_ref, ci_ref, si_ref, lng_ref, lnb_ref,
                    h_ref, gv_ref, acc_ref, *, gv_rows, n_row_tiles):
    j = pl.program_id(0)
    i = pl.program_id(1)
    tm = xn_ref.shape[0]
    chunks = [slice(c * MXU_WIDTH, (c + 1) * MXU_WIDTH) for c in range(COL_TILE // MXU_WIDTH)]

    def finish_chunk(kind, cs, stats):
        acc = acc_ref[:, cs]
        if kind == "copy":
            h_ref[:, cs] = acc.astype(BF16)
        elif kind == "ln":
            mu, rstd = stats
            vn = (acc - mu) * rstd * lng_ref[:, cs] + lnb_ref[:, cs]
            h_ref[:, cs] = vn.astype(BF16)
            gv_ref[:, cs] = vn[tm - gv_rows:, :]
        elif kind == "silu":
            h_ref[:, cs] = (acc * _sigmoid(acc)).astype(BF16)
        elif kind == "sigmoid":
            h_ref[:, cs] = _sigmoid(acc).astype(BF16)
        else:
            rope, cos, sin = ((_rope128, cq_ref[...], sq_ref[...]) if kind == "rope128"
                              else (_rope64, ci_ref[...], si_ref[...]))
            for h in range(MXU_WIDTH // LANES):
                sl = slice(cs.start + h * LANES, cs.start + (h + 1) * LANES)
                h_ref[:, sl] = rope(acc[:, h * LANES:(h + 1) * LANES], cos, sin).astype(BF16)

    def run(kind, finish, matmul):
        stats = None
        if finish and kind == "ln":
            acc = acc_ref[...]
            mu = jnp.mean(acc, axis=-1, keepdims=True)
            d = acc - mu
            stats = (mu, lax.rsqrt(jnp.mean(d * d, axis=-1, keepdims=True) + EPS))
        for cs in chunks:
            if finish:
                finish_chunk(kind, cs, stats)
            if matmul:
                acc_ref[:, cs] = lax.dot_general(xn_ref[...], w_ref[cs, :], _NT, preferred_element_type=F32)

    @pl.when(i == 0)
    def _():
        run(None, False, True)

    @pl.when((i == 0) & (j != T_VA))
    def _():
        gv_ref[...] = jnp.zeros(gv_ref.shape, F32)

    for kind in sorted(set(_TILE_KINDS)):
        is_kind = functools.reduce(jnp.logical_or, [j == t for t, k in enumerate(_TILE_KINDS) if k == kind])

        @pl.when(is_kind & (i > 0) & (i < n_row_tiles))
        def _(kind=kind):
            run(kind, True, True)

        @pl.when(is_kind & (i == n_row_tiles))
        def _(kind=kind):
            run(kind, True, False)


def _main_tile_row(j):
    j = jnp.minimum(j, N_MAIN_TILES - 1)
    g = BF16_SUBLANES
    skip_kv = (ROW_ZB - T_ZB * COL_TILE) // g
    skip_idx = (ROW_GA - ROW_ZB - (T_GA - T_ZB) * COL_TILE) // g
    return (j * (COL_TILE // g) + jnp.where(j >= T_ZB, skip_kv, 0) + jnp.where(j >= T_GA, skip_idx, 0)) * g


def _proj_main(xn, w_t, cq, sq, ci, si, ln_g, ln_b, *, tm, seq_rows, gv_rows):
    m = xn.shape[0]
    tiles_per_seq = seq_rows // tm
    n_seq = m // seq_rows
    n_row_tiles = m // tm
    body = functools.partial(_proj_main_body, gv_rows=gv_rows, n_row_tiles=n_row_tiles)
    prev = lambda i: jnp.maximum(i - 1, 0)
    tab = pl.BlockSpec((tm, LANES), lambda j, i: (prev(i) % tiles_per_seq, 0))
    row = lambda n: pl.BlockSpec((1, n), lambda j, i: (0, 0))
    gv_block = lambda j, i: (jnp.where(j == T_VA, prev(i) // tiles_per_seq, n_seq + (j > T_VA)), 0)
    h_main, gv = pl.pallas_call(
        body,
        grid=(N_MAIN_TILES, n_row_tiles + 1),
        in_specs=[
            pl.BlockSpec((tm, D_MODEL), lambda j, i: (jnp.minimum(i, n_row_tiles - 1), 0)),
            pl.BlockSpec((pl.Element(COL_TILE), pl.Element(D_MODEL)), lambda j, i: (_main_tile_row(j), 0)),
            tab, tab, tab, tab,
            row(A_WIDTH), row(A_WIDTH),
        ],
        out_specs=[
            pl.BlockSpec((tm, COL_TILE), lambda j, i: (prev(i), j)),
            pl.BlockSpec((gv_rows, A_WIDTH), gv_block),
        ],
        out_shape=[
            jax.ShapeDtypeStruct((m, N_MAIN_TILES * COL_TILE), BF16),
            jax.ShapeDtypeStruct(((n_seq + 2) * gv_rows, A_WIDTH), F32),
        ],
        scratch_shapes=[pltpu.VMEM((tm, COL_TILE), F32)],
        compiler_params=_cparams(("arbitrary", "arbitrary")),
        name="proj_main",
    )(xn, w_t, cq, sq, ci, si, ln_g, ln_b)
    return h_main, gv[:n_seq * gv_rows]


def _proj_tail_body(xn_ref, wkv_ref, wix_ref, ck_ref, sk_ref, ci_ref, si_ref,
                    k_ref, v_ref, ki_ref, kb_ref, vb_ref, kie_ref, kio_ref, ws_ref, *t_refs):
    xn = xn_ref[...]
    acc = lax.dot_general(xn, wkv_ref[...], _NT, preferred_element_type=F32)
    cos = ck_ref[...]
    sin = sk_ref[...]
    for kh in range(N_KV_HEADS):
        sl = slice(kh * HEAD_DIM, (kh + 1) * HEAD_DIM)
        r = _rope128(acc[:, sl], cos, sin)
        k_ref[:, kh, :] = r
        kb_ref[:, sl] = r.astype(BF16)
        v_ref[:, kh, :] = acc[:, KV_WIDTH + kh * HEAD_DIM:KV_WIDTH + (kh + 1) * HEAD_DIM]
    vb_ref[...] = acc[:, KV_WIDTH:2 * KV_WIDTH].astype(BF16)
    t = lax.dot_general(xn, wix_ref[...], _NT, preferred_element_type=F32)
    r = _rope64(t, ci_ref[...], si_ref[...])
    ki_ref[...] = r[:, :IDX_DIM]
    lane = lax.broadcasted_iota(jnp.int32, r.shape, 1)
    ke = jnp.where(lane < IDX_DIM, r, 0.0)
    kie_ref[...] = ke.astype(BF16)
    kio_ref[...] = pltpu.roll(ke, IDX_DIM, 1).astype(BF16)
    w_scale = IDX_HEADS ** -0.5 * IDX_DIM ** -0.5
    ws_ref[...] = t[:, IDX_DIM:IDX_DIM + IDX_HEADS] * w_scale
    if t_refs:
        vb_t_ref, ws_t_ref = t_refs
        vb_t_ref[...] = acc[:, KV_WIDTH:2 * KV_WIDTH].T.astype(BF16)
        ws_t_ref[...] = t.T[IDX_DIM:IDX_DIM + IDX_HEADS, :] * w_scale


def _proj_tail(xn, w_t, ck, sk, ci, si, *, tm, seq_rows, with_transposed):
    m = xn.shape[0]
    blk_t = lambda n: pl.BlockSpec((n, tm), lambda i: (0, i))
    extra_specs = [blk_t(KV_WIDTH), blk_t(IDX_HEADS)] if with_transposed else []
    extra_shapes = ([jax.ShapeDtypeStruct((KV_WIDTH, m), BF16), jax.ShapeDtypeStruct((IDX_HEADS, m), F32)]
                    if with_transposed else [])
    tiles_per_seq = seq_rows // tm
    tab = pl.BlockSpec((tm, LANES), lambda i: (i % tiles_per_seq, 0))
    blk = lambda n: pl.BlockSpec((tm, n), lambda i: (i, 0))
    kv_rows = pl.BlockSpec((tm, N_KV_HEADS, HEAD_DIM), lambda i: (i, 0, 0))
    return pl.pallas_call(
        _proj_tail_body,
        grid=(m // tm,),
        in_specs=[
            blk(D_MODEL),
            pl.BlockSpec((2 * KV_WIDTH, D_MODEL), lambda i: (ROW_K // (2 * KV_WIDTH), 0)),
            pl.BlockSpec((LANES, D_MODEL), lambda i: (ROW_KI // LANES, 0)),
            tab, tab, tab, tab,
        ],
        out_specs=[kv_rows, kv_rows, blk(IDX_DIM), blk(KV_WIDTH), blk(KV_WIDTH),
                   blk(LANES), blk(LANES), blk(IDX_HEADS)] + extra_specs,
        out_shape=[
            jax.ShapeDtypeStruct((m, N_KV_HEADS, HEAD_DIM), F32),
            jax.ShapeDtypeStruct((m, N_KV_HEADS, HEAD_DIM), F32),
            jax.ShapeDtypeStruct((m, IDX_DIM), F32),
            jax.ShapeDtypeStruct((m, KV_WIDTH), BF16),
            jax.ShapeDtypeStruct((m, KV_WIDTH), BF16),
            jax.ShapeDtypeStruct((m, LANES), BF16),
            jax.ShapeDtypeStruct((m, LANES), BF16),
            jax.ShapeDtypeStruct((m, IDX_HEADS), F32),
        ] + extra_shapes,
        compiler_params=_cparams(("arbitrary",)),
        name="proj_tail",
    )(xn, w_t, w_t, ck, sk, ci, si)


def _gate_body(u_ref, vn_ref, sz_ref, ws_ref, bst_ref, y_ref, *, n_chunks):
    rr = lax.broadcasted_iota(jnp.int32, (CHUNK, CHUNK), 0)
    cc = lax.broadcasted_iota(jnp.int32, (CHUNK, CHUNK), 1)
    tril = cc <= rr
    for g in range(A_GROUPS):
        wm = jnp.where(tril, ws_ref[g], 0.0).astype(BF16)
        b = bst_ref[:, g:g + 1]
        cs = slice(g * LANES, (g + 1) * LANES)
        for c in range(n_chunks):
            rs = slice(c * CHUNK, (c + 1) * CHUNK)
            s = jnp.dot(wm, vn_ref[rs, cs], preferred_element_type=F32) + b
            y = u_ref[rs, cs].astype(F32) * s * sz_ref[rs, cs].astype(F32)
            y_ref[rs, cs] = y.astype(BF16)


def _gate(h_main, a_ws, bs_t, *, tm):
    m = h_main.shape[0]
    body = functools.partial(_gate_body, n_chunks=tm // CHUNK)
    col = lambda t: pl.BlockSpec((tm, COL_TILE), lambda i, t=t: (i, t))
    return pl.pallas_call(
        body,
        grid=(m // tm,),
        in_specs=[col(T_UA), col(T_VA), col(T_ZA),
                  pl.BlockSpec((A_GROUPS, CHUNK, CHUNK), lambda i: (0, 0, 0)),
                  pl.BlockSpec((CHUNK, A_GROUPS), lambda i: (0, 0))],
        out_specs=pl.BlockSpec((tm, A_WIDTH), lambda i: (i, 0)),
        out_shape=jax.ShapeDtypeStruct((m, A_WIDTH), BF16),
        compiler_params=_cparams(("arbitrary",)),
        name="gate_prompt",
    )(h_main, h_main, h_main, a_ws, bs_t)


def _gate_row_body(u_ref, vn_ref, sz_ref, w0_ref, b0_ref, y_ref):
    s = vn_ref[...] * w0_ref[...] + b0_ref[...]
    y_ref[...] = (u_ref[...].astype(F32) * s * sz_ref[...].astype(F32)).astype(BF16)


def _gate_row(h_main, vn, w0, b0):
    m = h_main.shape[0]
    col = lambda t: pl.BlockSpec((m, COL_TILE), lambda i, t=t: (0, t))
    full = lambda r: pl.BlockSpec((r, A_WIDTH), lambda i: (0, 0))
    return pl.pallas_call(
        _gate_row_body,
        grid=(1,),
        in_specs=[col(T_UA), full(m), col(T_ZA), full(1), full(1)],
        out_specs=full(m),
        out_shape=jax.ShapeDtypeStruct((m, A_WIDTH), BF16),
        compiler_params=_cparams(("arbitrary",)),
        name="gate_sample",
    )(h_main, vn, h_main, w0, b0)


def _lane_bcast(col, rows):
    return jnp.broadcast_to(col, (rows, LANES))


def _select_bias(sc_ref, extra_ref, kp, row_min, row_max, *, n_cols):
    rows = sc_ref.shape[0]
    n_tiles = n_cols // LANES
    extra = None if extra_ref is None else extra_ref[...]

    def count(pred):
        acc = jnp.zeros((rows, LANES), F32)
        for c in range(n_tiles):
            acc = acc + jnp.where(pred(sc_ref[:, c * LANES:(c + 1) * LANES], c), 1.0, 0.0)
        tot = jnp.sum(acc, axis=1, keepdims=True)
        return _lane_bcast(tot, rows)

    def count_ge(x):
        c = count(lambda s, _: s >= x)
        if extra is not None:
            c = c + jnp.where(extra >= x, 1.0, 0.0)
        return c

    c_max = count_ge(row_max)
    top = c_max >= kp
    lo0 = jnp.where(top, row_max, row_min)
    c0 = jnp.where(top, c_max, count_ge(row_min))

    def count_ge3(x1, x2, x3):
        a1 = jnp.zeros((rows, LANES), F32)
        a2 = jnp.zeros((rows, LANES), F32)
        a3 = jnp.zeros((rows, LANES), F32)
        for c in range(n_tiles):
            s = sc_ref[:, c * LANES:(c + 1) * LANES]
            a1 = a1 + jnp.where(s >= x1, 1.0, 0.0)
            a2 = a2 + jnp.where(s >= x2, 1.0, 0.0)
            a3 = a3 + jnp.where(s >= x3, 1.0, 0.0)
        res = []
        for a, x in ((a1, x1), (a2, x2), (a3, x3)):
            tot = _lane_bcast(jnp.sum(a, axis=1, keepdims=True), rows)
            if extra is not None:
                tot = tot + jnp.where(extra >= x, 1.0, 0.0)
            res.append(tot)
        return res

    def step(st):
        lo, hi, c_lo, _, it = st
        mid = 0.5 * lo + 0.5 * hi
        act = (c_lo != kp) & (mid > lo) & (mid < hi)
        any_act = jnp.max(jnp.where(act, 1.0, 0.0))
        clamp = lambda x: jnp.minimum(jnp.maximum(x, lo), hi)
        q1 = clamp(0.75 * lo + 0.25 * hi)
        q3 = clamp(0.25 * lo + 0.75 * hi)
        c1, c2, c3 = count_ge3(q1, mid, q3)
        g1, g2, g3 = c1 >= kp, c2 >= kp, c3 >= kp
        lo_n = jnp.where(g3, q3, jnp.where(g2, mid, jnp.where(g1, q1, lo)))
        c_n = jnp.where(g3, c3, jnp.where(g2, c2, jnp.where(g1, c1, c_lo)))
        hi_n = jnp.where(g3, hi, jnp.where(g2, q3, jnp.where(g1, mid, jnp.minimum(q1, mid))))
        return (jnp.where(act, lo_n, lo), jnp.where(act, hi_n, hi), jnp.where(act, c_n, c_lo),
                any_act, it + 1)

    def cond(st):
        return (st[3] > 0.0) & (st[4] < 400)

    lo, _, c_lo, _, _ = lax.while_loop(cond, step, (lo0, row_max, c0, jnp.float32(1.0), jnp.int32(0)))

    exact = jnp.max(jnp.where(c_lo != kp, 1.0, 0.0)) == 0.0

    @pl.when(exact)
    def _():
        for c in range(n_tiles):
            sl = slice(c * LANES, (c + 1) * LANES)
            sc_ref[:, sl] = jnp.where(sc_ref[:, sl] >= lo, 0.0, NEG)
        if extra is not None:
            extra_ref[...] = jnp.where(extra >= lo, 0.0, NEG)

    @pl.when(jnp.logical_not(exact))
    def _():
        n_gt = count(lambda s, _: s > lo)
        if extra is not None:
            n_gt = n_gt + jnp.where(extra > lo, 1.0, 0.0)
        need = kp - n_gt
        lane = lax.broadcasted_iota(jnp.int32, (rows, LANES), 1).astype(F32)

        def count_eq_upto(jx):
            c = count(lambda s, c: (s == lo) & (lane + float(c * LANES) <= jx))
            if extra is not None:
                c = c + jnp.where((extra == lo) & (jx >= float(n_cols)), 1.0, 0.0)
            return c

        last = n_cols if extra is not None else n_cols - 1
        j_lo = jnp.full((rows, LANES), -1.0, F32)
        j_hi = jnp.full((rows, LANES), float(last), F32)

        def jstep(_, st):
            a, b = st
            mid = jnp.floor(0.5 * (a + b))
            ok = count_eq_upto(mid) >= need
            return jnp.where(ok, a, mid), jnp.where(ok, mid, b)

        n_steps = max(1, (n_cols + 1).bit_length())
        _, j_hi = lax.fori_loop(0, n_steps, jstep, (j_lo, j_hi))
        for c in range(n_tiles):
            sl = slice(c * LANES, (c + 1) * LANES)
            s = sc_ref[:, sl]
            keep = (s > lo) | ((s == lo) & (lane + float(c * LANES) <= j_hi))
            sc_ref[:, sl] = jnp.where(keep, 0.0, NEG)
        if extra is not None:
            keep = (extra > lo) | ((extra == lo) & (j_hi >= float(n_cols)))
            extra_ref[...] = jnp.where(keep, 0.0, NEG)


def _select_bias_cols(sc_ref, kp, col_min, col_max, *, n_rows):
    n_tiles = n_rows // LANES
    n_pivots = 3 if n_rows <= 512 else 2 if n_rows <= 1024 else 1

    def counts(preds):
        accs = [jnp.zeros((LANES, LANES), F32) for _ in preds]
        for r in range(n_tiles):
            s = sc_ref[r * LANES:(r + 1) * LANES, :]
            accs = [a + jnp.where(p(s, r), 1.0, 0.0) for a, p in zip(accs, preds)]
        return [jnp.sum(a, axis=0, keepdims=True) for a in accs]

    count = lambda pred: counts([pred])[0]
    count_ge = lambda x: count(lambda s, _: s >= x)
    c_max = count_ge(col_max)
    top = c_max >= kp
    lo0 = jnp.where(top, col_max, col_min)
    c0 = jnp.where(top, c_max, count_ge(col_min))

    def step(st):
        lo, hi, c_lo, _, it = st
        mid = 0.5 * lo + 0.5 * hi
        act = (c_lo != kp) & (mid > lo) & (mid < hi)
        any_act = jnp.max(jnp.where(act, 1.0, 0.0))
        fr = [(k + 1) / (n_pivots + 1) for k in range(n_pivots)]
        piv = [mid if f == 0.5 else jnp.minimum(jnp.maximum((1.0 - f) * lo + f * hi, lo), hi) for f in fr]
        cs = counts([lambda s, _, x=x: s >= x for x in piv])
        lo_n, c_n, hi_n = lo, c_lo, functools.reduce(jnp.minimum, piv)
        for k in range(n_pivots):
            ge = cs[k] >= kp
            nxt = piv[k + 1] if k + 1 < n_pivots else hi
            lo_n = jnp.where(ge, piv[k], lo_n)
            c_n = jnp.where(ge, cs[k], c_n)
            hi_n = jnp.where(ge, nxt, hi_n)
        return jnp.where(act, lo_n, lo), jnp.where(act, hi_n, hi), jnp.where(act, c_n, c_lo), any_act, it + 1

    def cond(st):
        return (st[3] > 0.0) & (st[4] < 400)

    lo, _, c_lo, _, _ = lax.while_loop(cond, step, (lo0, col_max, c0, jnp.float32(1.0), jnp.int32(0)))
    exact = jnp.max(jnp.where(c_lo != kp, 1.0, 0.0)) == 0.0

    @pl.when(exact)
    def _():
        for r in range(n_tiles):
            rs = slice(r * LANES, (r + 1) * LANES)
            sc_ref[rs, :] = jnp.where(sc_ref[rs, :] >= lo, 0.0, NEG)

    @pl.when(jnp.logical_not(exact))
    def _():
        need = kp - count(lambda s, _: s > lo)
        key = lax.broadcasted_iota(jnp.int32, (LANES, LANES), 0).astype(F32)
        count_eq_upto = lambda jx: count(lambda s, r: (s == lo) & (key + float(r * LANES) <= jx))

        def jstep(_, st):
            a, b = st
            mid = jnp.floor(0.5 * (a + b))
            ok = count_eq_upto(mid) >= need
            return jnp.where(ok, a, mid), jnp.where(ok, mid, b)

        j_lo = jnp.full((1, LANES), -1.0, F32)
        j_hi = jnp.full((1, LANES), float(n_rows - 1), F32)
        _, j_hi = lax.fori_loop(0, max(1, n_rows.bit_length()), jstep, (j_lo, j_hi))
        for r in range(n_tiles):
            rs = slice(r * LANES, (r + 1) * LANES)
            s = sc_ref[rs, :]
            keep = (s > lo) | ((s == lo) & (key + float(r * LANES) <= j_hi))
            sc_ref[rs, :] = jnp.where(keep, 0.0, NEG)


def _prompt_attn_block(nk, qi_ref, wst_ref, kie_ref, kio_ref, q_ref, kb_ref, vt_ref, sz_ref,
                       o_ref, sc_ref, *, top_k, key_chunk):
    qb = pl.program_id(1)
    n_pairs = IDX_HEADS // 2
    grp = N_HEADS // N_KV_HEADS
    qpos = qb * Q_BLOCK + lax.broadcasted_iota(jnp.int32, (LANES, LANES), 1)
    key0 = lax.broadcasted_iota(jnp.int32, (LANES, LANES), 0)

    qs = jnp.concatenate([qi_ref[:, p * LANES:(p + 1) * LANES] for p in range(n_pairs)], axis=0)
    wrows = [wst_ref[h:h + 1, :] for h in range(IDX_HEADS)]
    cmax = jnp.full((LANES, LANES), -jnp.inf, F32)
    cmin = jnp.full((LANES, LANES), jnp.inf, F32)
    for kc in range(0, nk, key_chunk):
        le = lax.dot_general(kie_ref[kc:kc + key_chunk, :], qs, _NT, preferred_element_type=F32)
        lo = lax.dot_general(kio_ref[kc:kc + key_chunk, :], qs, _NT, preferred_element_type=F32)
        for r in range(key_chunk // LANES):
            rs = slice(r * LANES, (r + 1) * LANES)
            acc = jnp.zeros((LANES, LANES), F32)
            for p in range(n_pairs):
                cs = slice(p * LANES, (p + 1) * LANES)
                acc = acc + jnp.maximum(le[rs, cs], 0.0) * wrows[2 * p]
                acc = acc + jnp.maximum(lo[rs, cs], 0.0) * wrows[2 * p + 1]
            causal = key0 + (kc + r * LANES) <= qpos
            cmax = jnp.maximum(cmax, jnp.where(causal, acc, -jnp.inf))
            cmin = jnp.minimum(cmin, jnp.where(causal, acc, jnp.inf))
            sc_ref[kc + r * LANES:kc + (r + 1) * LANES, :] = jnp.where(causal, acc, -jnp.inf)

    col_max = jnp.max(cmax, axis=0, keepdims=True)
    col_min = jnp.min(cmin, axis=0, keepdims=True)
    kp = jnp.minimum(qpos[0:1, :] + 1, top_k).astype(F32)
    _select_bias_cols(sc_ref, kp, col_min, col_max, n_rows=nk)

    bias = jnp.concatenate([sc_ref[0:nk, :]] * grp, axis=1)
    for kh in range(N_KV_HEADS):
        qh = jnp.concatenate(
            [q_ref[:, (kh * grp + g) * HEAD_DIM:(kh * grp + g + 1) * HEAD_DIM] for g in range(grp)], axis=0)
        ks = slice(kh * HEAD_DIM, (kh + 1) * HEAD_DIM)
        s = lax.dot_general(kb_ref[0:nk, ks], qh, _NT, preferred_element_type=F32) + bias
        m = jnp.max(s, axis=0, keepdims=True)
        p = jnp.exp2((s - m) * (HEAD_DIM ** -0.5 * LOG2_E))
        l = jnp.sum(p, axis=0, keepdims=True)
        ot = jnp.dot(vt_ref[ks, 0:nk], p.astype(BF16), preferred_element_type=F32) / l
        for g in range(grp):
            hs = slice((kh * grp + g) * HEAD_DIM, (kh * grp + g + 1) * HEAD_DIM)
            o = ot[:, g * LANES:(g + 1) * LANES].T
            o_ref[:, hs] = (o * sz_ref[:, hs].astype(F32)).astype(BF16)


def _prompt_attn_body(*refs, seq, top_k, n_buckets, key_chunk):
    qb = pl.program_id(1)
    span = seq // n_buckets
    per = span // Q_BLOCK
    for c in range(n_buckets):
        @pl.when(qb // per == c)
        def _(c=c):
            _prompt_attn_block((c + 1) * span, *refs, top_k=top_k, key_chunk=key_chunk)


def _prompt_attn(h_main, ws_t, kie, kio, kb, vb_t, *, batch, seq):
    top_k = min(TOPK_MAX, seq // 4)
    n_qb = seq // Q_BLOCK
    key_chunk = min(512, seq)
    n_buckets = max(1, seq // 512)
    body = functools.partial(_prompt_attn_body, seq=seq, top_k=top_k, n_buckets=n_buckets, key_chunk=key_chunk)
    col = lambda t: pl.BlockSpec((Q_BLOCK, COL_TILE), lambda b, q, t=t: (b * n_qb + q, t))
    seqblk = lambda n: pl.BlockSpec((seq, n), lambda b, q: (b, 0))
    return pl.pallas_call(
        body,
        grid=(batch, n_qb),
        in_specs=[col(T_QI),
                  pl.BlockSpec((IDX_HEADS, Q_BLOCK), lambda b, q: (0, b * n_qb + q)),
                  seqblk(LANES), seqblk(LANES),
                  col(T_Q), seqblk(KV_WIDTH),
                  pl.BlockSpec((KV_WIDTH, seq), lambda b, q: (0, b)),
                  col(T_ZB)],
        out_specs=pl.BlockSpec((Q_BLOCK, B_WIDTH), lambda b, q: (b * n_qb + q, 0)),
        out_shape=jax.ShapeDtypeStruct((batch * seq, B_WIDTH), BF16),
        scratch_shapes=[pltpu.VMEM((seq, Q_BLOCK), F32)],
        compiler_params=_cparams(("arbitrary", "arbitrary")),
        name="attn_prompt",
    )(h_main, ws_t, kie, kio, h_main, kb, vb_t, h_main)


def _merge_body(x_ref, ya_ref, ob_ref, ga_ref, gb_ref, woa_ref, wob_ref, wout_ref, pg_ref, o_ref):
    pa = jnp.dot(ya_ref[...], woa_ref[...], preferred_element_type=F32)
    pb = jnp.dot(ob_ref[...], wob_ref[...], preferred_element_type=F32)
    mix = ga_ref[...].astype(F32) * pa + gb_ref[...].astype(F32) * pb
    r = jnp.dot(mix.astype(BF16), wout_ref[...], preferred_element_type=F32)
    o_ref[...] = x_ref[...] + _rmsnorm_rows(r, pg_ref[...])


def _merge(x2, ya, ob, h_main, w_oa, w_ob, w_out, post_g, *, tm):
    m = x2.shape[0]
    const = lambda r, c: pl.BlockSpec((r, c), lambda i: (0, 0), pipeline_mode=pl.Buffered(1))
    return pl.pallas_call(
        _merge_body,
        grid=(m // tm,),
        in_specs=[
            pl.BlockSpec((tm, D_MODEL), lambda i: (i, 0)),
            pl.BlockSpec((tm, A_WIDTH), lambda i: (i, 0)),
            pl.BlockSpec((tm, B_WIDTH), lambda i: (i, 0)),
            pl.BlockSpec((tm, D_MODEL), lambda i: (i, T_GA // 2)),
            pl.BlockSpec((tm, D_MODEL), lambda i: (i, T_GB // 2)),
            const(A_WIDTH, D_MODEL), const(B_WIDTH, D_MODEL), const(D_MODEL, D_MODEL),
            const(1, D_MODEL),
        ],
        out_specs=pl.BlockSpec((tm, D_MODEL), lambda i: (i, 0)),
        out_shape=jax.ShapeDtypeStruct((m, D_MODEL), F32),
        compiler_params=_cparams(("arbitrary",)),
        name="merge",
    )(x2, ya, ob, h_main, h_main, w_oa, w_ob, w_out, post_g)


SCORE_CHUNK = 2048


def _sample_scores_body(pt_ref, q_ref, w_ref, kidx_hbm, o_ref, buf, sem):
    db, n_pages = pt_ref.shape
    past = n_pages * PAGE_SIZE

    def page_copy(b, p, slot):
        dst = buf.at[slot, :, pl.ds(pl.multiple_of(p * PAGE_SIZE, PAGE_SIZE), PAGE_SIZE)]
        return pltpu.make_async_copy(kidx_hbm.at[pt_ref[b, p]], dst, sem.at[slot])

    def start_all(b, slot):
        def f(p, c):
            page_copy(b, p, slot).start()
            return c
        lax.fori_loop(0, n_pages, f, 0, unroll=8)

    def wait_all(slot):
        for p in range(n_pages):
            dst = buf.at[slot, :, pl.ds(p * PAGE_SIZE, PAGE_SIZE)]
            pltpu.make_async_copy(kidx_hbm.at[0], dst, sem.at[slot]).wait()

    start_all(0, 0)

    def per_seq(b, c):
        slot = b % 2

        @pl.when(b + 1 < db)
        def _():
            start_all(b + 1, 1 - slot)

        wait_all(slot)
        q = q_ref[b]
        w = w_ref[b]
        for ch in range(past // SCORE_CHUNK):
            cs = slice(ch * SCORE_CHUNK, (ch + 1) * SCORE_CHUNK)
            logit = jnp.dot(q, buf[slot, :, cs].astype(BF16), preferred_element_type=F32)
            o_ref[pl.ds(b, 1), cs] = jnp.sum(jnp.maximum(logit, 0.0) * w, axis=0, keepdims=True)
        return c

    lax.fori_loop(0, db, per_seq, 0)


def _sample_scores(page_table, qi3, ws3, kidx_pages_t):
    db, n_pages = page_table.shape
    past = n_pages * PAGE_SIZE
    grid_spec = pltpu.PrefetchScalarGridSpec(
        num_scalar_prefetch=1,
        grid=(1,),
        in_specs=[pl.BlockSpec((db, IDX_HEADS, IDX_DIM), lambda i, pt: (0, 0, 0)),
                  pl.BlockSpec((db, IDX_HEADS, 1), lambda i, pt: (0, 0, 0)),
                  pl.BlockSpec(memory_space=pl.ANY)],
        out_specs=pl.BlockSpec((db, past), lambda i, pt: (0, 0)),
        scratch_shapes=[pltpu.VMEM((2, IDX_DIM, past), F32), pltpu.SemaphoreType.DMA((2,))],
    )
    return pl.pallas_call(
        _sample_scores_body,
        grid_spec=grid_spec,
        out_shape=jax.ShapeDtypeStruct((db, past), F32),
        compiler_params=_cparams(("arbitrary",)),
        name="scores_sample",
    )(page_table, qi3, ws3, kidx_pages_t)


def _sample_select_body(sc_ref, qi_ref, kie_ref, ws_ref, bias_ref, bnew_ref, *, top_k):
    rows, past = sc_ref.shape
    lane = lax.broadcasted_iota(jnp.int32, (rows, LANES), 1)
    ki = kie_ref[...].astype(F32)
    ki = ki + pltpu.roll(ki, IDX_DIM, 1)
    s_new = jnp.zeros((rows, 1), F32)
    for p in range(IDX_HEADS // 2):
        prod = qi_ref[:, p * LANES:(p + 1) * LANES].astype(F32) * ki
        l_even = jnp.sum(jnp.where(lane < IDX_DIM, prod, 0.0), axis=1, keepdims=True)
        l_odd = jnp.sum(jnp.where(lane >= IDX_DIM, prod, 0.0), axis=1, keepdims=True)
        s_new = s_new + jnp.maximum(l_even, 0.0) * ws_ref[:, 2 * p:2 * p + 1]
        s_new = s_new + jnp.maximum(l_odd, 0.0) * ws_ref[:, 2 * p + 1:2 * p + 2]
    extra = _lane_bcast(s_new, rows)
    bnew_ref[...] = extra
    rmax = extra
    rmin = extra
    for c in range(past // LANES):
        sl = slice(c * LANES, (c + 1) * LANES)
        s = sc_ref[:, sl]
        bias_ref[:, sl] = s
        rmax = jnp.maximum(rmax, s)
        rmin = jnp.minimum(rmin, s)
    row_max = _lane_bcast(jnp.max(rmax, axis=1, keepdims=True), rows)
    row_min = _lane_bcast(jnp.min(rmin, axis=1, keepdims=True), rows)
    kp = jnp.full((rows, LANES), float(top_k), F32)
    _select_bias(bias_ref, bnew_ref, kp, row_min, row_max, n_cols=past)


def _sample_select(scores, h_main, kie, ws, *, top_k):
    db, past = scores.shape
    full = lambda r, c: pl.BlockSpec((r, c), lambda i: (0, 0))
    return pl.pallas_call(
        functools.partial(_sample_select_body, top_k=top_k),
        grid=(1,),
        in_specs=[full(db, past),
                  pl.BlockSpec((db, COL_TILE), lambda i: (0, T_QI)),
                  full(db, LANES), full(db, IDX_HEADS)],
        out_specs=[full(db, past), full(db, LANES)],
        out_shape=[jax.ShapeDtypeStruct((db, past), F32), jax.ShapeDtypeStruct((db, LANES), F32)],
        compiler_params=_cparams(("arbitrary",)),
        name="select_sample",
    )(scores, h_main, kie, ws)


def _sample_compact_body(m_ref, mt_ref, pt_ref, idx_ref, row_ref, *, n_slots):
    n_pages = m_ref.shape[1]
    pt = jnp.broadcast_to(pt_ref[0], (8, n_pages))
    pt_hi = (pt // PAGE_SIZE).astype(F32).astype(BF16)
    pt_lo = (pt % PAGE_SIZE).astype(F32).astype(BF16)
    one = lambda pred: jnp.where(pred, 1.0, 0.0)
    kept = m_ref[0] == 0.0
    kept_t = mt_ref[0] == 0.0
    ri = lax.broadcasted_iota(jnp.int32, (PAGE_SIZE, PAGE_SIZE), 0)
    ci = lax.broadcasted_iota(jnp.int32, (PAGE_SIZE, PAGE_SIZE), 1)
    rp = lax.broadcasted_iota(jnp.int32, (n_pages, n_pages), 0)
    cp = lax.broadcasted_iota(jnp.int32, (n_pages, n_pages), 1)
    plt = jnp.dot(one(ci <= ri).astype(BF16), one(kept_t).astype(BF16), preferred_element_type=F32)
    n_row = plt[PAGE_SIZE - 1:PAGE_SIZE, :]
    n_col = _lane_bcast(jnp.sum(one(kept), axis=1, keepdims=True), n_pages)
    e_col = jnp.dot(one(cp <= rp).astype(BF16), n_col.astype(BF16), preferred_element_type=F32)
    n_row8 = jnp.broadcast_to(n_row, (8, n_pages))
    e_row8 = jnp.dot(n_row8.astype(BF16), one(rp <= cp).astype(BF16), preferred_element_type=F32)
    off_row8 = e_row8 - n_row8
    n_total = e_col[n_pages - 1:n_pages, :]
    page_id = lax.broadcasted_iota(jnp.int32, (n_pages, LANES), 0).astype(F32)
    for jt in range(n_slots // LANES):
        j = (lax.broadcasted_iota(jnp.int32, (1, LANES), 1) + jt * LANES).astype(F32)
        page_j = jnp.sum(one(e_col <= j), axis=0, keepdims=True)
        pick = one(page_id == page_j).astype(BF16)
        prefix_j = jnp.dot(plt.astype(BF16), pick, preferred_element_type=F32)
        off_j = jnp.dot(off_row8.astype(BF16), pick, preferred_element_type=F32)[0:1]
        local_j = jnp.sum(one(prefix_j <= j - off_j), axis=0, keepdims=True)
        pos = page_j * float(PAGE_SIZE) + local_j
        phys = (jnp.dot(pt_hi, pick, preferred_element_type=F32)[0:1] * float(PAGE_SIZE)
                + jnp.dot(pt_lo, pick, preferred_element_type=F32)[0:1])
        row = phys * float(PAGE_SIZE) + local_j
        used = j < n_total
        sl = slice(jt * LANES, (jt + 1) * LANES)
        idx_ref[0, :, sl] = jnp.where(used, pos, -1.0).astype(jnp.int32)
        row_ref[0, :, sl] = jnp.where(used, row, 0.0).astype(jnp.int32)


def _sample_compact(bias3, bias3_t, pt3, *, n_slots):
    db, n_pages, _ = bias3.shape
    out = pl.BlockSpec((1, 1, n_slots), lambda b: (b, 0, 0))
    return pl.pallas_call(
        functools.partial(_sample_compact_body, n_slots=n_slots),
        grid=(db,),
        in_specs=[pl.BlockSpec((1, n_pages, PAGE_SIZE), lambda b: (b, 0, 0)),
                  pl.BlockSpec((1, PAGE_SIZE, n_pages), lambda b: (b, 0, 0)),
                  pl.BlockSpec((1, 1, n_pages), lambda b: (b, 0, 0))],
        out_specs=[out, out],
        out_shape=[jax.ShapeDtypeStruct((db, 1, n_slots), jnp.int32)] * 2,
        compiler_params=_cparams(("arbitrary",)),
        name="compact_sample",
    )(bias3, bias3_t, pt3)


def _sample_attn_body(row_ref, q_ref, slot_ref, bnew_ref, kn_ref, vn_ref, sz_ref, k_hbm, v_hbm,
                      o_ref, kbuf, vbuf, sem):
    b = pl.program_id(0)
    nb = pl.num_programs(0)
    n_slots = kbuf.shape[1]
    buf = b % 2
    grp = N_HEADS // N_KV_HEADS
    scale = HEAD_DIM ** -0.5

    def row_copies(seq, j, to):
        r = row_ref[seq, j]
        return (pltpu.make_async_copy(k_hbm.at[r], kbuf.at[to, j], sem.at[0, to]),
                pltpu.make_async_copy(v_hbm.at[r], vbuf.at[to, j], sem.at[1, to]))

    def start_all(seq, to):
        def f(j, c):
            ck, cv = row_copies(seq, j, to)
            ck.start()
            cv.start()
            return c
        lax.fori_loop(0, n_slots, f, 0, unroll=8)

    def wait_all(to):
        for blk in range(n_slots // PAGE_SIZE):
            rows = pl.ds(blk * PAGE_SIZE, PAGE_SIZE)
            src = pl.ds(0, PAGE_SIZE)
            pltpu.make_async_copy(k_hbm.at[src], kbuf.at[to, rows], sem.at[0, to]).wait()
            pltpu.make_async_copy(v_hbm.at[src], vbuf.at[to, rows], sem.at[1, to]).wait()

    @pl.when(b == 0)
    def _():
        start_all(0, 0)

    @pl.when(b + 1 < nb)
    def _():
        start_all(b + 1, 1 - buf)

    wait_all(buf)

    q = q_ref[0]
    head_s = lax.broadcasted_iota(jnp.int32, (N_HEADS, n_slots), 0)
    head_o = lax.broadcasted_iota(jnp.int32, (N_HEADS, HEAD_DIM), 0)
    k0, k1 = (kbuf[buf, :, kh, :].astype(BF16) for kh in range(N_KV_HEADS))
    v0, v1 = (vbuf[buf, :, kh, :].astype(BF16) for kh in range(N_KV_HEADS))
    s0 = lax.dot_general(q, k0, _NT, preferred_element_type=F32)
    s1 = lax.dot_general(q, k1, _NT, preferred_element_type=F32)
    s = jnp.where(head_s < grp, s0, s1) * scale + jnp.where(slot_ref[0] >= 0, 0.0, NEG)
    kn = kn_ref[0].astype(F32)
    vn = vn_ref[0].astype(F32)
    kn8 = jnp.where(head_o < grp, kn[:, :HEAD_DIM], kn[:, HEAD_DIM:])
    vn8 = jnp.where(head_o < grp, vn[:, :HEAD_DIM], vn[:, HEAD_DIM:])
    s_new = _lane_bcast(jnp.sum(q.astype(F32) * kn8, axis=-1, keepdims=True), N_HEADS) * scale + bnew_ref[0]
    m = jnp.maximum(_lane_bcast(jnp.max(s, axis=-1, keepdims=True), N_HEADS), s_new)
    p = jnp.exp(s - m[:, :1])
    p_new = jnp.exp(s_new - m)
    l = _lane_bcast(jnp.sum(p, axis=-1, keepdims=True), N_HEADS) + p_new
    pb = p.astype(BF16)
    pv = jnp.where(head_o < grp,
                   jnp.dot(pb, v0, preferred_element_type=F32),
                   jnp.dot(pb, v1, preferred_element_type=F32))
    o = (pv + p_new * vn8) / l
    o_ref[0] = (o * sz_ref[0].astype(F32)).astype(BF16)


def _sample_attn(rows, q3, slots3, bnew3, kn3, vn3, sz3, k_rows, v_rows):
    db, n_slots = rows.shape
    per_b = lambda r, c: pl.BlockSpec((1, r, c), lambda b, rw: (b, 0, 0))
    grid_spec = pltpu.PrefetchScalarGridSpec(
        num_scalar_prefetch=1,
        grid=(db,),
        in_specs=[per_b(N_HEADS, HEAD_DIM), per_b(1, n_slots), per_b(1, LANES),
                  per_b(1, KV_WIDTH), per_b(1, KV_WIDTH), per_b(N_HEADS, HEAD_DIM),
                  pl.BlockSpec(memory_space=pl.ANY), pl.BlockSpec(memory_space=pl.ANY)],
        out_specs=per_b(N_HEADS, HEAD_DIM),
        scratch_shapes=[pltpu.VMEM((2, n_slots, N_KV_HEADS, HEAD_DIM), F32),
                        pltpu.VMEM((2, n_slots, N_KV_HEADS, HEAD_DIM), F32),
                        pltpu.SemaphoreType.DMA((2, 2))],
    )
    return pl.pallas_call(
        _sample_attn_body,
        grid_spec=grid_spec,
        out_shape=jax.ShapeDtypeStruct((db, N_HEADS, HEAD_DIM), BF16),
        compiler_params=_cparams(("arbitrary",)),
        name="attn_sample",
    )(rows, q3, slots3, bnew3, kn3, vn3, sz3, k_rows, v_rows)


def _col_tile(h_main, t, n=1):
    return h_main[:, t * COL_TILE:(t + n) * COL_TILE]


def kernel(x_prompt, x_sample, cache_k, cache_v, cache_kidx, page_table, pre_g, w_in, a_ln_g, a_ln_b,
           a_ws, a_bs, w_oa, w_ob, w_out, post_g):
    batch, seq, _ = x_prompt.shape
    db, ds, _ = x_sample.shape
    depth = w_in.shape[0]
    n_pages = page_table.shape[1]
    past = n_pages * PAGE_SIZE
    assert ds == 1 and seq % 512 == 0 and past % SCORE_CHUNK == 0
    top_k_s = min(TOPK_MAX, (past + ds) // 4)
    assert top_k_s <= past and top_k_s % LANES == 0

    pos_p = jnp.arange(seq)
    pos_s = past + (jnp.arange(db * ds) % ds)
    tabs_p = _rope_tables(pos_p, HEAD_DIM) + _rope_tables(pos_p, IDX_DIM)
    tabs_s = _rope_tables(pos_s, HEAD_DIM) + _rope_tables(pos_s, IDX_DIM)

    hp = x_prompt.reshape(batch * seq, D_MODEL)
    hs = x_sample.reshape(db * ds, D_MODEL)
    outs = [[] for _ in range(8)]
    for l in range(depth):
        w_t = w_in[l].T.astype(BF16)
        woa, wob, wout = w_oa[l].astype(BF16), w_ob[l].astype(BF16), w_out[l].astype(BF16)
        g_pre, g_post = pre_g[l][None], post_g[l][None]
        ln_g, ln_b = a_ln_g[l][None], a_ln_b[l][None]

        xn = _prenorm(hp, g_pre, tm=512)
        h_main, gv = _proj_main(xn, w_t, *tabs_p, ln_g, ln_b, tm=512, seq_rows=seq, gv_rows=CHUNK)
        k, v, ki, kb, _, kie, kio, _, vb_t, ws_t = _proj_tail(xn, w_t, *tabs_p, tm=512, seq_rows=seq,
                                                              with_transposed=True)
        ya = _gate(h_main, a_ws[l], a_bs[l].T, tm=512)
        ob = _prompt_attn(h_main, ws_t, kie, kio, kb, vb_t, batch=batch, seq=seq)
        hp = _merge(hp, ya, ob, h_main, woa, wob, wout, g_post, tm=256)
        outs[0].append(k.reshape(batch, seq, N_KV_HEADS, HEAD_DIM))
        outs[1].append(v.reshape(batch, seq, N_KV_HEADS, HEAD_DIM))
        outs[2].append(ki.reshape(batch, seq, IDX_DIM))
        outs[3].append(gv.reshape(batch, CHUNK, A_WIDTH))

        m_s = db * ds
        xn = _prenorm(hs, g_pre, tm=m_s)
        h_main, gv = _proj_main(xn, w_t, *tabs_s, ln_g, ln_b, tm=m_s, seq_rows=m_s, gv_rows=m_s)
        k, v, ki, kb, vb, kie, kio, ws = _proj_tail(xn, w_t, *tabs_s, tm=m_s, seq_rows=m_s,
                                                    with_transposed=False)
        w0 = jnp.repeat(a_ws[l][:, 0, 0], LANES)[None]
        b0 = jnp.repeat(a_bs[l][:, 0], LANES)[None]
        ya = _gate_row(h_main, gv, w0, b0)
        qi3 = _col_tile(h_main, T_QI).reshape(db, IDX_HEADS, IDX_DIM)
        kidx_t = jnp.swapaxes(cache_kidx[l], 1, 2)
        scores = _sample_scores(page_table, qi3, ws.reshape(db, IDX_HEADS, 1), kidx_t)
        bias, bnew = _sample_select(scores, h_main, kie, ws, top_k=top_k_s)
        bias3 = bias.reshape(db, n_pages, PAGE_SIZE)
        slots3, rows3 = _sample_compact(bias3, jnp.swapaxes(bias3, 1, 2),
                                        page_table.reshape(db, 1, n_pages), n_slots=top_k_s)
        pool_rows = lambda c: c.reshape(-1, N_KV_HEADS, HEAD_DIM)
        ob = _sample_attn(rows3.reshape(db, top_k_s),
                          _col_tile(h_main, T_Q).reshape(db, N_HEADS, HEAD_DIM),
                          slots3, bnew.reshape(db, 1, LANES),
                          kb.reshape(db, 1, KV_WIDTH), vb.reshape(db, 1, KV_WIDTH),
                          _col_tile(h_main, T_ZB).reshape(db, N_HEADS, HEAD_DIM),
                          pool_rows(cache_k[l]), pool_rows(cache_v[l])).reshape(db, B_WIDTH)
        hs = _merge(hs, ya, ob, h_main, woa, wob, wout, g_post, tm=m_s)
        outs[4].append(k.reshape(db, ds, N_KV_HEADS, HEAD_DIM))
        outs[5].append(v.reshape(db, ds, N_KV_HEADS, HEAD_DIM))
        outs[6].append(ki.reshape(db, ds, IDX_DIM))
        outs[7].append(gv.reshape(db, ds, A_WIDTH))

    st = [jnp.stack(o, axis=0) for o in outs]
    return (hp.reshape(batch, seq, D_MODEL), hs.reshape(db, ds, D_MODEL),
            st[0], st[1], st[2], st[3], st[4], st[5], st[6], st[7])
```

```python
import functools

import jax
import jax.numpy as jnp
from jax import lax
from jax.experimental import pallas as pl
from jax.experimental.pallas import tpu as pltpu

F32 = jnp.float32
BF16 = jnp.bfloat16

D_MODEL = 2048
CHUNK = 128
A_GROUPS = 8
A_WIDTH = 1024
N_HEADS = 8
N_KV_HEADS = 2
HEAD_DIM = 128
B_WIDTH = 1024
KV_WIDTH = 256
IDX_HEADS = 16
IDX_DIM = 64
TOPK_MAX = 256
ROPE_THETA = 10000.0
EPS = 1e-6
PAGE_SIZE = 128
Q_BLOCK = 128
NEG = -1e30
LOG2_E = 1.4426950408889634

LANES = 128
BF16_SUBLANES = 16
MXU_WIDTH = 256
NORM_STREAMS = 4
_NT = (((1,), (1,)), ((), ()))
COL_TILE = 1024
T_UA, T_VA, T_ZA, T_Q, T_ZB, T_QI, T_GA, T_GB = 0, 1, 2, 3, 4, 5, 6, 8
N_MAIN_TILES = 10
ROW_K = 4 * COL_TILE
ROW_ZB = ROW_K + 2 * KV_WIDTH
ROW_KI = ROW_ZB + B_WIDTH + IDX_HEADS * IDX_DIM
ROW_GA = ROW_KI + IDX_DIM + IDX_HEADS
VMEM_LIMIT = 56 * 1024 * 1024


def _cparams(sem):
    return pltpu.CompilerParams(dimension_semantics=sem, vmem_limit_bytes=VMEM_LIMIT)


def _rope_tables(pos, dim):
    half = dim // 2
    inv = ROPE_THETA ** (-jnp.arange(half, dtype=F32) / half)
    ang = pos.astype(F32)[:, None] * inv[None, :]
    cos = jnp.cos(ang)
    sin = jnp.sin(ang)
    reps = LANES // dim
    cos_t = jnp.tile(jnp.concatenate([cos, cos], axis=-1), (1, reps))
    sin_t = jnp.tile(jnp.concatenate([-sin, sin], axis=-1), (1, reps))
    return cos_t, sin_t


def _rope128(x, cos, sin):
    return x * cos + pltpu.roll(x, 64, 1) * sin


def _rope64(x, cos, sin):
    lane = lax.broadcasted_iota(jnp.int32, x.shape, 1)
    first = (lane % IDX_DIM) < (IDX_DIM // 2)
    partner = jnp.where(first, pltpu.roll(x, LANES - 32, 1), pltpu.roll(x, 32, 1))
    return x * cos + partner * sin


def _rmsnorm_rows(xf, g):
    ms = jnp.mean(xf * xf, axis=-1, keepdims=True)
    return xf * lax.rsqrt(ms + EPS) * g


_TILE_KINDS = ("copy", "ln", "silu", "rope128", "silu", "rope64", "sigmoid", "sigmoid", "sigmoid", "sigmoid")


def _sigmoid(x):
    return 0.5 * jnp.tanh(0.5 * x) + 0.5


def _prenorm_body(*refs):
    x_refs, g_ref, o_ref = refs[:-2], refs[-2], refs[-1]
    xs = [r[...] for r in x_refs]
    ms = sum(jnp.sum(x * x, axis=-1, keepdims=True) for x in xs) * (1.0 / D_MODEL)
    scale = lax.rsqrt(ms + EPS)
    w = xs[0].shape[1]
    for c, x in enumerate(xs):
        cs = slice(c * w, (c + 1) * w)
        o_ref[:, cs] = (x * scale * g_ref[:, cs]).astype(BF16)


def _prenorm(x2, g, *, tm):
    m = x2.shape[0]
    w = D_MODEL // NORM_STREAMS
    return pl.pallas_call(
        _prenorm_body,
        grid=(m // tm,),
        in_specs=[pl.BlockSpec((tm, w), lambda i, c=c: (i, c)) for c in range(NORM_STREAMS)]
                 + [pl.BlockSpec((1, D_MODEL), lambda i: (0, 0))],
        out_specs=pl.BlockSpec((tm, D_MODEL), lambda i: (i, 0)),
        out_shape=jax.ShapeDtypeStruct((m, D_MODEL), BF16),
        compiler_params=_cparams(("arbitrary",)),
        name="prenorm",
    )(*([x2] * NORM_STREAMS), g)


def _proj_main_body(xn_ref, *refs, gv_rows, n_row_tiles, cast_weights):
    n_chunks = COL_TILE // MXU_WIDTH
    w_refs, refs = refs[:n_chunks], refs[n_chunks:]
    cq_ref, sq_ref, ci_ref, si_ref, lng_ref, lnb_ref, h_ref, gv_ref = refs[:8]
    wbf_ref = refs[8] if cast_weights else None
    acc_ref = refs[-1]
    j = pl.program_id(0)
    i = pl.program_id(1)
    tm = xn_ref.shape[0]
    chunks = [slice(c * MXU_WIDTH, (c + 1) * MXU_WIDTH) for c in range(n_chunks)]
    weights = lambda c: wbf_ref[chunks[c], :] if cast_weights else w_refs[c][...]

    def finish_chunk(kind, cs, stats):
        acc = acc_ref[:, cs]
        if kind == "copy":
            h_ref[:, cs] = acc.astype(BF16)
        elif kind == "ln":
            mu, rstd = stats
            vn = (acc - mu) * rstd * lng_ref[:, cs] + lnb_ref[:, cs]
            h_ref[:, cs] = vn.astype(BF16)
            gv_ref[:, cs] = vn[tm - gv_rows:, :]
        elif kind == "silu":
            h_ref[:, cs] = (acc * _sigmoid(acc)).astype(BF16)
        elif kind == "sigmoid":
            h_ref[:, cs] = _sigmoid(acc).astype(BF16)
        else:
            rope, cos, sin = ((_rope128, cq_ref[...], sq_ref[...]) if kind == "rope128"
                              else (_rope64, ci_ref[...], si_ref[...]))
            for h in range(MXU_WIDTH // LANES):
                sl = slice(cs.start + h * LANES, cs.start + (h + 1) * LANES)
                h_ref[:, sl] = rope(acc[:, h * LANES:(h + 1) * LANES], cos, sin).astype(BF16)

    def run(kind, finish, matmul):
        stats = None
        if finish and kind == "ln":
            acc = acc_ref[...]
            mu = jnp.mean(acc, axis=-1, keepdims=True)
            d = acc - mu
            stats = (mu, lax.rsqrt(jnp.mean(d * d, axis=-1, keepdims=True) + EPS))
        for c, cs in enumerate(chunks):
            if finish:
                finish_chunk(kind, cs, stats)
            if matmul:
                acc_ref[:, cs] = lax.dot_general(xn_ref[...], weights(c), _NT, preferred_element_type=F32)

    @pl.when(i == 0)
    def _():
        if cast_weights:
            for c, cs in enumerate(chunks):
                wbf_ref[cs, :] = w_refs[c][...].astype(BF16)
        run(None, False, True)

    @pl.when((i == 0) & (j != T_VA))
    def _():
        gv_ref[...] = jnp.zeros(gv_ref.shape, F32)

    for kind in sorted(set(_TILE_KINDS)):
        is_kind = functools.reduce(jnp.logical_or, [j == t for t, k in enumerate(_TILE_KINDS) if k == kind])

        @pl.when(is_kind & (i > 0) & (i < n_row_tiles))
        def _(kind=kind):
            run(kind, True, True)

        @pl.when(is_kind & (i == n_row_tiles))
        def _(kind=kind):
            run(kind, True, False)


def _main_tile_row(j, chunk):
    g = BF16_SUBLANES
    skip_kv = (ROW_ZB - T_ZB * COL_TILE) // g
    skip_idx = (ROW_GA - ROW_ZB - (T_GA - T_ZB) * COL_TILE) // g
    return (j * (COL_TILE // g) + chunk * (MXU_WIDTH // g)
            + jnp.where(j >= T_ZB, skip_kv, 0) + jnp.where(j >= T_GA, skip_idx, 0)) * g


def _proj_main(xn, w, cq, sq, ci, si, ln_g, ln_b, *, tm, seq_rows, gv_rows, cast_weights):
    m = xn.shape[0]
    tiles_per_seq = seq_rows // tm
    n_seq = m // seq_rows
    n_row_tiles = m // tm
    n_chunks = COL_TILE // MXU_WIDTH
    body = functools.partial(_proj_main_body, gv_rows=gv_rows, n_row_tiles=n_row_tiles,
                             cast_weights=cast_weights)
    if cast_weights:
        w_chunk = lambda c: pl.BlockSpec((pl.Element(MXU_WIDTH), pl.Element(D_MODEL)),
                                         lambda j, i: (_main_tile_row(j, c), 0))
    else:
        w_chunk = lambda c: pl.BlockSpec((MXU_WIDTH, D_MODEL), lambda j, i: (j * n_chunks + c, 0))
    w_out_spec = [pl.BlockSpec((COL_TILE, D_MODEL), lambda j, i: (j, 0))] if cast_weights else []
    w_out_shape = [jax.ShapeDtypeStruct((N_MAIN_TILES * COL_TILE, D_MODEL), BF16)] if cast_weights else []
    prev = lambda i: jnp.maximum(i - 1, 0)
    tab = pl.BlockSpec((tm, LANES), lambda j, i: (prev(i) % tiles_per_seq, 0))
    row = lambda n: pl.BlockSpec((1, n), lambda j, i: (0, 0))
    gv_block = lambda j, i: (jnp.where(j == T_VA, prev(i) // tiles_per_seq, n_seq + (j > T_VA)), 0)
    h_main, gv, *w_bf = pl.pallas_call(
        body,
        grid=(N_MAIN_TILES, n_row_tiles + 1),
        in_specs=[pl.BlockSpec((tm, D_MODEL), lambda j, i: (jnp.minimum(i, n_row_tiles - 1), 0))]
                 + [w_chunk(c) for c in range(n_chunks)]
                 + [tab, tab, tab, tab, row(A_WIDTH), row(A_WIDTH)],
        out_specs=[
            pl.BlockSpec((tm, COL_TILE), lambda j, i: (prev(i), j)),
            pl.BlockSpec((gv_rows, A_WIDTH), gv_block),
        ] + w_out_spec,
        out_shape=[
            jax.ShapeDtypeStruct((m, N_MAIN_TILES * COL_TILE), BF16),
            jax.ShapeDtypeStruct(((n_seq + 2) * gv_rows, A_WIDTH), F32),
        ] + w_out_shape,
        scratch_shapes=[pltpu.VMEM((tm, COL_TILE), F32)],
        compiler_params=_cparams(("arbitrary", "arbitrary")),
        name="proj_main",
    )(xn, *([w] * n_chunks), cq, sq, ci, si, ln_g, ln_b)
    return (h_main, gv[:n_seq * gv_rows], *w_bf)


def _proj_tail_body(xn_ref, wkv_ref, wix_ref, ck_ref, sk_ref, ci_ref, si_ref,
                    k_ref, v_ref, ki_ref, kb_ref, vb_ref, kie_ref, kio_ref, ws_ref, *t_refs):
    xn = xn_ref[...]
    acc = lax.dot_general(xn, wkv_ref[...].astype(BF16), _NT, preferred_element_type=F32)
    cos = ck_ref[...]
    sin = sk_ref[...]
    for kh in range(N_KV_HEADS):
        sl = slice(kh * HEAD_DIM, (kh + 1) * HEAD_DIM)
        r = _rope128(acc[:, sl], cos, sin)
        k_ref[:, kh, :] = r
        kb_ref[:, sl] = r.astype(BF16)
        v_ref[:, kh, :] = acc[:, KV_WIDTH + kh * HEAD_DIM:KV_WIDTH + (kh + 1) * HEAD_DIM]
    vb_ref[...] = acc[:, KV_WIDTH:2 * KV_WIDTH].astype(BF16)
    t = lax.dot_general(xn, wix_ref[...].astype(BF16), _NT, preferred_element_type=F32)
    r = _rope64(t, ci_ref[...], si_ref[...])
    ki_ref[...] = r[:, :IDX_DIM]
    lane = lax.broadcasted_iota(jnp.int32, r.shape, 1)
    ke = jnp.where(lane < IDX_DIM, r, 0.0)
    kie_ref[...] = ke.astype(BF16)
    kio_ref[...] = pltpu.roll(ke, IDX_DIM, 1).astype(BF16)
    w_scale = IDX_HEADS ** -0.5 * IDX_DIM ** -0.5
    ws_ref[...] = t[:, IDX_DIM:IDX_DIM + IDX_HEADS] * w_scale
    if t_refs:
        vb_t_ref, ws_t_ref = t_refs
        vb_t_ref[...] = acc[:, KV_WIDTH:2 * KV_WIDTH].T.astype(BF16)
        ws_t_ref[...] = t.T[IDX_DIM:IDX_DIM + IDX_HEADS, :] * w_scale


def _proj_tail(xn, w_t, ck, sk, ci, si, *, tm, seq_rows, with_transposed):
    m = xn.shape[0]
    blk_t = lambda n: pl.BlockSpec((n, tm), lambda i: (0, i))
    extra_specs = [blk_t(KV_WIDTH), blk_t(IDX_HEADS)] if with_transposed else []
    extra_shapes = ([jax.ShapeDtypeStruct((KV_WIDTH, m), BF16), jax.ShapeDtypeStruct((IDX_HEADS, m), F32)]
                    if with_transposed else [])
    tiles_per_seq = seq_rows // tm
    tab = pl.BlockSpec((tm, LANES), lambda i: (i % tiles_per_seq, 0))
    blk = lambda n: pl.BlockSpec((tm, n), lambda i: (i, 0))
    kv_rows = pl.BlockSpec((tm, N_KV_HEADS, HEAD_DIM), lambda i: (i, 0, 0))
    return pl.pallas_call(
        _proj_tail_body,
        grid=(m // tm,),
        in_specs=[
            blk(D_MODEL),
            pl.BlockSpec((2 * KV_WIDTH, D_MODEL), lambda i: (ROW_K // (2 * KV_WIDTH), 0)),
            pl.BlockSpec((LANES, D_MODEL), lambda i: (ROW_KI // LANES, 0)),
            tab, tab, tab, tab,
        ],
        out_specs=[kv_rows, kv_rows, blk(IDX_DIM), blk(KV_WIDTH), blk(KV_WIDTH),
                   blk(LANES), blk(LANES), blk(IDX_HEADS)] + extra_specs,
        out_shape=[
            jax.ShapeDtypeStruct((m, N_KV_HEADS, HEAD_DIM), F32),
            jax.ShapeDtypeStruct((m, N_KV_HEADS, HEAD_DIM), F32),
            jax.ShapeDtypeStruct((m, IDX_DIM), F32),
            jax.ShapeDtypeStruct((m, KV_WIDTH), BF16),
            jax.ShapeDtypeStruct((m, KV_WIDTH), BF16),
            jax.ShapeDtypeStruct((m, LANES), BF16),
            jax.ShapeDtypeStruct((m, LANES), BF16),
            jax.ShapeDtypeStruct((m, IDX_HEADS), F32),
        ] + extra_shapes,
        compiler_params=_cparams(("arbitrary",)),
        name="proj_tail",
    )(xn, w_t, w_t, ck, sk, ci, si)


def _gate_body(u_ref, vn_ref, sz_ref, ws_ref, bst_ref, y_ref, *, n_chunks):
    rr = lax.broadcasted_iota(jnp.int32, (CHUNK, CHUNK), 0)
    cc = lax.broadcasted_iota(jnp.int32, (CHUNK, CHUNK), 1)
    tril = cc <= rr
    for g in range(A_GROUPS):
        wm = jnp.where(tril, ws_ref[g], 0.0).astype(BF16)
        b = bst_ref[:, g:g + 1]
        cs = slice(g * LANES, (g + 1) * LANES)
        for c in range(n_chunks):
            rs = slice(c * CHUNK, (c + 1) * CHUNK)
            s = jnp.dot(wm, vn_ref[rs, cs], preferred_element_type=F32) + b
            y = u_ref[rs, cs].astype(F32) * s * sz_ref[rs, cs].astype(F32)
            y_ref[rs, cs] = y.astype(BF16)


def _gate(h_main, a_ws, bs_t, *, tm):
    m = h_main.shape[0]
    body = functools.partial(_gate_body, n_chunks=tm // CHUNK)
    col = lambda t: pl.BlockSpec((tm, COL_TILE), lambda i, t=t: (i, t))
    return pl.pallas_call(
        body,
        grid=(m // tm,),
        in_specs=[col(T_UA), col(T_VA), col(T_ZA),
                  pl.BlockSpec((A_GROUPS, CHUNK, CHUNK), lambda i: (0, 0, 0)),
                  pl.BlockSpec((CHUNK, A_GROUPS), lambda i: (0, 0))],
        out_specs=pl.BlockSpec((tm, A_WIDTH), lambda i: (i, 0)),
        out_shape=jax.ShapeDtypeStruct((m, A_WIDTH), BF16),
        compiler_params=_cparams(("arbitrary",)),
        name="gate_prompt",
    )(h_main, h_main, h_main, a_ws, bs_t)


def _gate_row_body(u_ref, vn_ref, sz_ref, w0_ref, b0_ref, y_ref):
    s = vn_ref[...] * w0_ref[...] + b0_ref[...]
    y_ref[...] = (u_ref[...].astype(F32) * s * sz_ref[...].astype(F32)).astype(BF16)


def _gate_row(h_main, vn, w0, b0):
    m = h_main.shape[0]
    col = lambda t: pl.BlockSpec((m, COL_TILE), lambda i, t=t: (0, t))
    full = lambda r: pl.BlockSpec((r, A_WIDTH), lambda i: (0, 0))
    return pl.pallas_call(
        _gate_row_body,
        grid=(1,),
        in_specs=[col(T_UA), full(m), col(T_ZA), full(1), full(1)],
        out_specs=full(m),
        out_shape=jax.ShapeDtypeStruct((m, A_WIDTH), BF16),
        compiler_params=_cparams(("arbitrary",)),
        name="gate_sample",
    )(h_main, vn, h_main, w0, b0)


def _lane_bcast(col, rows):
    return jnp.broadcast_to(col, (rows, LANES))


def _select_bias(sc_ref, extra_ref, kp, row_min, row_max, *, n_cols):
    rows = sc_ref.shape[0]
    n_tiles = n_cols // LANES
    extra = None if extra_ref is None else extra_ref[...]

    def count(pred):
        acc = jnp.zeros((rows, LANES), F32)
        for c in range(n_tiles):
            acc = acc + jnp.where(pred(sc_ref[:, c * LANES:(c + 1) * LANES], c), 1.0, 0.0)
        tot = jnp.sum(acc, axis=1, keepdims=True)
        return _lane_bcast(tot, rows)

    def count_ge(x):
        c = count(lambda s, _: s >= x)
        if extra is not None:
            c = c + jnp.where(extra >= x, 1.0, 0.0)
        return c

    c_max = count_ge(row_max)
    top = c_max >= kp
    lo0 = jnp.where(top, row_max, row_min)
    c0 = jnp.where(top, c_max, count_ge(row_min))

    def count_ge3(x1, x2, x3):
        a1 = jnp.zeros((rows, LANES), F32)
        a2 = jnp.zeros((rows, LANES), F32)
        a3 = jnp.zeros((rows, LANES), F32)
        for c in range(n_tiles):
            s = sc_ref[:, c * LANES:(c + 1) * LANES]
            a1 = a1 + jnp.where(s >= x1, 1.0, 0.0)
            a2 = a2 + jnp.where(s >= x2, 1.0, 0.0)
            a3 = a3 + jnp.where(s >= x3, 1.0, 0.0)
        res = []
        for a, x in ((a1, x1), (a2, x2), (a3, x3)):
            tot = _lane_bcast(jnp.sum(a, axis=1, keepdims=True), rows)
            if extra is not None:
                tot = tot + jnp.where(extra >= x, 1.0, 0.0)
            res.append(tot)
        return res

    def step(st):
        lo, hi, c_lo, _, it = st
        mid = 0.5 * lo + 0.5 * hi
        act = (c_lo != kp) & (mid > lo) & (mid < hi)
        any_act = jnp.max(jnp.where(act, 1.0, 0.0))
        clamp = lambda x: jnp.minimum(jnp.maximum(x, lo), hi)
        q1 = clamp(0.75 * lo + 0.25 * hi)
        q3 = clamp(0.25 * lo + 0.75 * hi)
        c1, c2, c3 = count_ge3(q1, mid, q3)
        g1, g2, g3 = c1 >= kp, c2 >= kp, c3 >= kp
        lo_n = jnp.where(g3, q3, jnp.where(g2, mid, jnp.where(g1, q1, lo)))
        c_n = jnp.where(g3, c3, jnp.where(g2, c2, jnp.where(g1, c1, c_lo)))
        hi_n = jnp.where(g3, hi, jnp.where(g2, q3, jnp.where(g1, mid, jnp.minimum(q1, mid))))
        return (jnp.where(act, lo_n, lo), jnp.where(act, hi_n, hi), jnp.where(act, c_n, c_lo),
                any_act, it + 1)

    def cond(st):
        return (st[3] > 0.0) & (st[4] < 400)

    lo, _, c_lo, _, _ = lax.while_loop(cond, step, (lo0, row_max, c0, jnp.float32(1.0), jnp.int32(0)))

    exact = jnp.max(jnp.where(c_lo != kp, 1.0, 0.0)) == 0.0

    @pl.when(exact)
    def _():
        for c in range(n_tiles):
            sl = slice(c * LANES, (c + 1) * LANES)
            sc_ref[:, sl] = jnp.where(sc_ref[:, sl] >= lo, 0.0, NEG)
        if extra is not None:
            extra_ref[...] = jnp.where(extra >= lo, 0.0, NEG)

    @pl.when(jnp.logical_not(exact))
    def _():
        n_gt = count(lambda s, _: s > lo)
        if extra is not None:
            n_gt = n_gt + jnp.where(extra > lo, 1.0, 0.0)
        need = kp - n_gt
        lane = lax.broadcasted_iota(jnp.int32, (rows, LANES), 1).astype(F32)

        def count_eq_upto(jx):
            c = count(lambda s, c: (s == lo) & (lane + float(c * LANES) <= jx))
            if extra is not None:
                c = c + jnp.where((extra == lo) & (jx >= float(n_cols)), 1.0, 0.0)
            return c

        last = n_cols if extra is not None else n_cols - 1
        j_lo = jnp.full((rows, LANES), -1.0, F32)
        j_hi = jnp.full((rows, LANES), float(last), F32)

        def jstep(_, st):
            a, b = st
            mid = jnp.floor(0.5 * (a + b))
            ok = count_eq_upto(mid) >= need
            return jnp.where(ok, a, mid), jnp.where(ok, mid, b)

        n_steps = max(1, (n_cols + 1).bit_length())
        _, j_hi = lax.fori_loop(0, n_steps, jstep, (j_lo, j_hi))
        for c in range(n_tiles):
            sl = slice(c * LANES, (c + 1) * LANES)
            s = sc_ref[:, sl]
            keep = (s > lo) | ((s == lo) & (lane + float(c * LANES) <= j_hi))
            sc_ref[:, sl] = jnp.where(keep, 0.0, NEG)
        if extra is not None:
            keep = (extra > lo) | ((extra == lo) & (j_hi >= float(n_cols)))
            extra_ref[...] = jnp.where(keep, 0.0, NEG)


def _select_bias_cols(sc_ref, kp, col_min, col_max, *, n_rows):
    n_tiles = n_rows // LANES
    n_pivots = 3 if n_rows <= 512 else 2 if n_rows <= 1024 else 1

    def counts(preds):
        accs = [jnp.zeros((LANES, LANES), F32) for _ in preds]
        for r in range(n_tiles):
            s = sc_ref[r * LANES:(r + 1) * LANES, :]
            accs = [a + jnp.where(p(s, r), 1.0, 0.0) for a, p in zip(accs, preds)]
        return [jnp.sum(a, axis=0, keepdims=True) for a in accs]

    count = lambda pred: counts([pred])[0]
    count_ge = lambda x: count(lambda s, _: s >= x)
    c_max = count_ge(col_max)
    top = c_max >= kp
    lo0 = jnp.where(top, col_max, col_min)
    c0 = jnp.where(top, c_max, count_ge(col_min))

    def step(st):
        lo, hi, c_lo, _, it = st
        mid = 0.5 * lo + 0.5 * hi
        act = (c_lo != kp) & (mid > lo) & (mid < hi)
        any_act = jnp.max(jnp.where(act, 1.0, 0.0))
        fr = [(k + 1) / (n_pivots + 1) for k in range(n_pivots)]
        piv = [mid if f == 0.5 else jnp.minimum(jnp.maximum((1.0 - f) * lo + f * hi, lo), hi) for f in fr]
        cs = counts([lambda s, _, x=x: s >= x for x in piv])
        lo_n, c_n, hi_n = lo, c_lo, functools.reduce(jnp.minimum, piv)
        for k in range(n_pivots):
            ge = cs[k] >= kp
            nxt = piv[k + 1] if k + 1 < n_pivots else hi
            lo_n = jnp.where(ge, piv[k], lo_n)
            c_n = jnp.where(ge, cs[k], c_n)
            hi_n = jnp.where(ge, nxt, hi_n)
        return jnp.where(act, lo_n, lo), jnp.where(act, hi_n, hi), jnp.where(act, c_n, c_lo), any_act, it + 1

    def cond(st):
        return (st[3] > 0.0) & (st[4] < 400)

    lo, _, c_lo, _, _ = lax.while_loop(cond, step, (lo0, col_max, c0, jnp.float32(1.0), jnp.int32(0)))
    exact = jnp.max(jnp.where(c_lo != kp, 1.0, 0.0)) == 0.0

    @pl.when(exact)
    def _():
        for r in range(n_tiles):
            rs = slice(r * LANES, (r + 1) * LANES)
            sc_ref[rs, :] = jnp.where(sc_ref[rs, :] >= lo, 0.0, NEG)

    @pl.when(jnp.logical_not(exact))
    def _():
        need = kp - count(lambda s, _: s > lo)
        key = lax.broadcasted_iota(jnp.int32, (LANES, LANES), 0).astype(F32)
        count_eq_upto = lambda jx: count(lambda s, r: (s == lo) & (key + float(r * LANES) <= jx))

        def jstep(_, st):
            a, b = st
            mid = jnp.floor(0.5 * (a + b))
            ok = count_eq_upto(mid) >= need
            return jnp.where(ok, a, mid), jnp.where(ok, mid, b)

        j_lo = jnp.full((1, LANES), -1.0, F32)
        j_hi = jnp.full((1, LANES), float(n_rows - 1), F32)
        _, j_hi = lax.fori_loop(0, max(1, n_rows.bit_length()), jstep, (j_lo, j_hi))
        for r in range(n_tiles):
            rs = slice(r * LANES, (r + 1) * LANES)
            s = sc_ref[rs, :]
            keep = (s > lo) | ((s == lo) & (key + float(r * LANES) <= j_hi))
            sc_ref[rs, :] = jnp.where(keep, 0.0, NEG)


def _prompt_attn_block(nk, qi_ref, wst_ref, kie_ref, kio_ref, q_ref, kb_ref, vt_ref, sz_ref,
                       o_ref, sc_ref, *, top_k, key_chunk):
    qb = pl.program_id(1)
    n_pairs = IDX_HEADS // 2
    grp = N_HEADS // N_KV_HEADS
    qpos = qb * Q_BLOCK + lax.broadcasted_iota(jnp.int32, (LANES, LANES), 1)
    key0 = lax.broadcasted_iota(jnp.int32, (LANES, LANES), 0)

    qs = jnp.concatenate([qi_ref[:, p * LANES:(p + 1) * LANES] for p in range(n_pairs)], axis=0)
    wrows = [wst_ref[h:h + 1, :] for h in range(IDX_HEADS)]
    cmax = jnp.full((LANES, LANES), -jnp.inf, F32)
    cmin = jnp.full((LANES, LANES), jnp.inf, F32)
    for kc in range(0, nk, key_chunk):
        le = lax.dot_general(kie_ref[kc:kc + key_chunk, :], qs, _NT, preferred_element_type=F32)
        lo = lax.dot_general(kio_ref[kc:kc + key_chunk, :], qs, _NT, preferred_element_type=F32)
        for r in range(key_chunk // LANES):
            rs = slice(r * LANES, (r + 1) * LANES)
            acc = jnp.zeros((LANES, LANES), F32)
            for p in range(n_pairs):
                cs = slice(p * LANES, (p + 1) * LANES)
                acc = acc + jnp.maximum(le[rs, cs], 0.0) * wrows[2 * p]
                acc = acc + jnp.maximum(lo[rs, cs], 0.0) * wrows[2 * p + 1]
            causal = key0 + (kc + r * LANES) <= qpos
            cmax = jnp.maximum(cmax, jnp.where(causal, acc, -jnp.inf))
            cmin = jnp.minimum(cmin, jnp.where(causal, acc, jnp.inf))
            sc_ref[kc + r * LANES:kc + (r + 1) * LANES, :] = jnp.where(causal, acc, -jnp.inf)

    col_max = jnp.max(cmax, axis=0, keepdims=True)
    col_min = jnp.min(cmin, axis=0, keepdims=True)
    kp = jnp.minimum(qpos[0:1, :] + 1, top_k).astype(F32)
    _select_bias_cols(sc_ref, kp, col_min, col_max, n_rows=nk)

    bias = jnp.concatenate([sc_ref[0:nk, :]] * grp, axis=1)
    for kh in range(N_KV_HEADS):
        qh = jnp.concatenate(
            [q_ref[:, (kh * grp + g) * HEAD_DIM:(kh * grp + g + 1) * HEAD_DIM] for g in range(grp)], axis=0)
        ks = slice(kh * HEAD_DIM, (kh + 1) * HEAD_DIM)
        s = lax.dot_general(kb_ref[0:nk, ks], qh, _NT, preferred_element_type=F32) + bias
        m = jnp.max(s, axis=0, keepdims=True)
        p = jnp.exp2((s - m) * (HEAD_DIM ** -0.5 * LOG2_E))
        l = jnp.sum(p, axis=0, keepdims=True)
        ot = jnp.dot(vt_ref[ks, 0:nk], p.astype(BF16), preferred_element_type=F32) / l
        for g in range(grp):
            hs = slice((kh * grp + g) * HEAD_DIM, (kh * grp + g + 1) * HEAD_DIM)
            o = ot[:, g * LANES:(g + 1) * LANES].T
            o_ref[:, hs] = (o * sz_ref[:, hs].astype(F32)).astype(BF16)


def _prompt_attn_body(*refs, seq, top_k, n_buckets, key_chunk):
    qb = pl.program_id(1)
    span = seq // n_buckets
    per = span // Q_BLOCK
    for c in range(n_buckets):
        @pl.when(qb // per == c)
        def _(c=c):
            _prompt_attn_block((c + 1) * span, *refs, top_k=top_k, key_chunk=key_chunk)


def _prompt_attn(h_main, ws_t, kie, kio, kb, vb_t, *, batch, seq):
    top_k = min(TOPK_MAX, seq // 4)
    n_qb = seq // Q_BLOCK
    key_chunk = min(512, seq)
    n_buckets = max(1, seq // 512)
    body = functools.partial(_prompt_attn_body, seq=seq, top_k=top_k, n_buckets=n_buckets, key_chunk=key_chunk)
    col = lambda t: pl.BlockSpec((Q_BLOCK, COL_TILE), lambda b, q, t=t: (b * n_qb + q, t))
    seqblk = lambda n: pl.BlockSpec((seq, n), lambda b, q: (b, 0))
    return pl.pallas_call(
        body,
        grid=(batch, n_qb),
        in_specs=[col(T_QI),
                  pl.BlockSpec((IDX_HEADS, Q_BLOCK), lambda b, q: (0, b * n_qb + q)),
                  seqblk(LANES), seqblk(LANES),
                  col(T_Q), seqblk(KV_WIDTH),
                  pl.BlockSpec((KV_WIDTH, seq), lambda b, q: (0, b)),
                  col(T_ZB)],
        out_specs=pl.BlockSpec((Q_BLOCK, B_WIDTH), lambda b, q: (b * n_qb + q, 0)),
        out_shape=jax.ShapeDtypeStruct((batch * seq, B_WIDTH), BF16),
        scratch_shapes=[pltpu.VMEM((seq, Q_BLOCK), F32)],
        compiler_params=_cparams(("arbitrary", "arbitrary")),
        name="attn_prompt",
    )(h_main, ws_t, kie, kio, h_main, kb, vb_t, h_main)


def _merge_body(x_ref, ya_ref, ob_ref, ga_ref, gb_ref, woa_ref, wob_ref, wout_ref, pg_ref, o_ref):
    pa = jnp.dot(ya_ref[...], woa_ref[...], preferred_element_type=F32)
    pb = jnp.dot(ob_ref[...], wob_ref[...], preferred_element_type=F32)
    mix = ga_ref[...].astype(F32) * pa + gb_ref[...].astype(F32) * pb
    r = jnp.dot(mix.astype(BF16), wout_ref[...], preferred_element_type=F32)
    o_ref[...] = x_ref[...] + _rmsnorm_rows(r, pg_ref[...])


def _merge(x2, ya, ob, h_main, w_oa, w_ob, w_out, post_g, *, tm):
    m = x2.shape[0]
    const = lambda r, c: pl.BlockSpec((r, c), lambda i: (0, 0), pipeline_mode=pl.Buffered(1))
    return pl.pallas_call(
        _merge_body,
        grid=(m // tm,),
        in_specs=[
            pl.BlockSpec((tm, D_MODEL), lambda i: (i, 0)),
            pl.BlockSpec((tm, A_WIDTH), lambda i: (i, 0)),
            pl.BlockSpec((tm, B_WIDTH), lambda i: (i, 0)),
            pl.BlockSpec((tm, D_MODEL), lambda i: (i, T_GA // 2)),
            pl.BlockSpec((tm, D_MODEL), lambda i: (i, T_GB // 2)),
            const(A_WIDTH, D_MODEL), const(B_WIDTH, D_MODEL), const(D_MODEL, D_MODEL),
            const(1, D_MODEL),
        ],
        out_specs=pl.BlockSpec((tm, D_MODEL), lambda i: (i, 0)),
        out_shape=jax.ShapeDtypeStruct((m, D_MODEL), F32),
        compiler_params=_cparams(("arbitrary",)),
        name="merge",
    )(x2, ya, ob, h_main, h_main, w_oa, w_ob, w_out, post_g)


SCORE_CHUNK = 2048
COMPACT_SEQS = 4


def _sample_scores_body(pt_ref, q_ref, w_ref, kidx_hbm, o_ref, buf, sem):
    db, n_pages = pt_ref.shape
    past = n_pages * PAGE_SIZE

    def page_copy(b, p, slot):
        dst = buf.at[slot, :, pl.ds(pl.multiple_of(p * PAGE_SIZE, PAGE_SIZE), PAGE_SIZE)]
        return pltpu.make_async_copy(kidx_hbm.at[pt_ref[b, p]], dst, sem.at[slot])

    def start_all(b, slot):
        def f(p, c):
            page_copy(b, p, slot).start()
            return c
        lax.fori_loop(0, n_pages, f, 0, unroll=8)

    def wait_all(slot):
        for p in range(n_pages):
            dst = buf.at[slot, :, pl.ds(p * PAGE_SIZE, PAGE_SIZE)]
            pltpu.make_async_copy(kidx_hbm.at[0], dst, sem.at[slot]).wait()

    start_all(0, 0)

    def per_seq(b, c):
        slot = b % 2

        @pl.when(b + 1 < db)
        def _():
            start_all(b + 1, 1 - slot)

        wait_all(slot)
        q = q_ref[b]
        w = w_ref[b]
        for ch in range(past // SCORE_CHUNK):
            cs = slice(ch * SCORE_CHUNK, (ch + 1) * SCORE_CHUNK)
            logit = jnp.dot(q, buf[slot, :, cs].astype(BF16), preferred_element_type=F32)
            o_ref[pl.ds(b, 1), cs] = jnp.sum(jnp.maximum(logit, 0.0) * w, axis=0, keepdims=True)
        return c

    lax.fori_loop(0, db, per_seq, 0)


def _sample_scores(page_table, qi3, ws3, kidx_pages_t):
    db, n_pages = page_table.shape
    past = n_pages * PAGE_SIZE
    grid_spec = pltpu.PrefetchScalarGridSpec(
        num_scalar_prefetch=1,
        grid=(1,),
        in_specs=[pl.BlockSpec((db, IDX_HEADS, IDX_DIM), lambda i, pt: (0, 0, 0)),
                  pl.BlockSpec((db, IDX_HEADS, 1), lambda i, pt: (0, 0, 0)),
                  pl.BlockSpec(memory_space=pl.ANY)],
        out_specs=pl.BlockSpec((db, past), lambda i, pt: (0, 0)),
        scratch_shapes=[pltpu.VMEM((2, IDX_DIM, past), F32), pltpu.SemaphoreType.DMA((2,))],
    )
    return pl.pallas_call(
        _sample_scores_body,
        grid_spec=grid_spec,
        out_shape=jax.ShapeDtypeStruct((db, past), F32),
        compiler_params=_cparams(("arbitrary",)),
        name="scores_sample",
    )(page_table, qi3, ws3, kidx_pages_t)


def _sample_select_body(sc_ref, qi_ref, kie_ref, ws_ref, bias_ref, bnew_ref, *, top_k):
    rows, past = sc_ref.shape
    lane = lax.broadcasted_iota(jnp.int32, (rows, LANES), 1)
    ki = kie_ref[...].astype(F32)
    ki = ki + pltpu.roll(ki, IDX_DIM, 1)
    s_new = jnp.zeros((rows, 1), F32)
    for p in range(IDX_HEADS // 2):
        prod = qi_ref[:, p * LANES:(p + 1) * LANES].astype(F32) * ki
        l_even = jnp.sum(jnp.where(lane < IDX_DIM, prod, 0.0), axis=1, keepdims=True)
        l_odd = jnp.sum(jnp.where(lane >= IDX_DIM, prod, 0.0), axis=1, keepdims=True)
        s_new = s_new + jnp.maximum(l_even, 0.0) * ws_ref[:, 2 * p:2 * p + 1]
        s_new = s_new + jnp.maximum(l_odd, 0.0) * ws_ref[:, 2 * p + 1:2 * p + 2]
    extra = _lane_bcast(s_new, rows)
    bnew_ref[...] = extra
    rmax = extra
    rmin = extra
    for c in range(past // LANES):
        sl = slice(c * LANES, (c + 1) * LANES)
        s = sc_ref[:, sl]
        bias_ref[:, sl] = s
        rmax = jnp.maximum(rmax, s)
        rmin = jnp.minimum(rmin, s)
    row_max = _lane_bcast(jnp.max(rmax, axis=1, keepdims=True), rows)
    row_min = _lane_bcast(jnp.min(rmin, axis=1, keepdims=True), rows)
    kp = jnp.full((rows, LANES), float(top_k), F32)
    _select_bias(bias_ref, bnew_ref, kp, row_min, row_max, n_cols=past)


def _sample_select(scores, h_main, kie, ws, *, top_k):
    db, past = scores.shape
    full = lambda r, c: pl.BlockSpec((r, c), lambda i: (0, 0))
    return pl.pallas_call(
        functools.partial(_sample_select_body, top_k=top_k),
        grid=(1,),
        in_specs=[full(db, past),
                  pl.BlockSpec((db, COL_TILE), lambda i: (0, T_QI)),
                  full(db, LANES), full(db, IDX_HEADS)],
        out_specs=[full(db, past), full(db, LANES)],
        out_shape=[jax.ShapeDtypeStruct((db, past), F32), jax.ShapeDtypeStruct((db, LANES), F32)],
        compiler_params=_cparams(("arbitrary",)),
        name="select_sample",
    )(scores, h_main, kie, ws)


def _sample_compact_body(m_ref, mt_ref, pt_ref, idx_ref, row_ref, *, n_slots):
    n_pages = m_ref.shape[1]
    for q in range(m_ref.shape[0]):
        _compact_one(m_ref.at[q], mt_ref.at[q], pt_ref.at[q], idx_ref.at[q], row_ref.at[q],
                     n_pages=n_pages, n_slots=n_slots)


def _compact_one(m_ref, mt_ref, pt_ref, idx_ref, row_ref, *, n_pages, n_slots):
    pt = jnp.broadcast_to(pt_ref[...], (8, n_pages))
    pt_hi = (pt // PAGE_SIZE).astype(F32).astype(BF16)
    pt_lo = (pt % PAGE_SIZE).astype(F32).astype(BF16)
    one = lambda pred: jnp.where(pred, 1.0, 0.0)
    kept = m_ref[...] == 0.0
    kept_t = mt_ref[...] == 0.0
    ri = lax.broadcasted_iota(jnp.int32, (PAGE_SIZE, PAGE_SIZE), 0)
    ci = lax.broadcasted_iota(jnp.int32, (PAGE_SIZE, PAGE_SIZE), 1)
    rp = lax.broadcasted_iota(jnp.int32, (n_pages, n_pages), 0)
    cp = lax.broadcasted_iota(jnp.int32, (n_pages, n_pages), 1)
    plt = jnp.dot(one(ci <= ri).astype(BF16), one(kept_t).astype(BF16), preferred_element_type=F32)
    n_row = plt[PAGE_SIZE - 1:PAGE_SIZE, :]
    n_col = _lane_bcast(jnp.sum(one(kept), axis=1, keepdims=True), n_pages)
    e_col = jnp.dot(one(cp <= rp).astype(BF16), n_col.astype(BF16), preferred_element_type=F32)
    n_row8 = jnp.broadcast_to(n_row, (8, n_pages))
    e_row8 = jnp.dot(n_row8.astype(BF16), one(rp <= cp).astype(BF16), preferred_element_type=F32)
    off_row8 = e_row8 - n_row8
    n_total = e_col[n_pages - 1:n_pages, :]
    page_id = lax.broadcasted_iota(jnp.int32, (n_pages, LANES), 0).astype(F32)
    for jt in range(n_slots // LANES):
        j = (lax.broadcasted_iota(jnp.int32, (1, LANES), 1) + jt * LANES).astype(F32)
        page_j = jnp.sum(one(e_col <= j), axis=0, keepdims=True)
        pick = one(page_id == page_j).astype(BF16)
        prefix_j = jnp.dot(plt.astype(BF16), pick, preferred_element_type=F32)
        off_j = jnp.dot(off_row8.astype(BF16), pick, preferred_element_type=F32)[0:1]
        local_j = jnp.sum(one(prefix_j <= j - off_j), axis=0, keepdims=True)
        pos = page_j * float(PAGE_SIZE) + local_j
        phys = (jnp.dot(pt_hi, pick, preferred_element_type=F32)[0:1] * float(PAGE_SIZE)
                + jnp.dot(pt_lo, pick, preferred_element_type=F32)[0:1])
        row = phys * float(PAGE_SIZE) + local_j
        used = j < n_total
        sl = slice(jt * LANES, (jt + 1) * LANES)
        idx_ref[:, sl] = jnp.where(used, pos, -1.0).astype(jnp.int32)
        row_ref[:, sl] = jnp.where(used, row, 0.0).astype(jnp.int32)


def _sample_compact(bias3, bias3_t, pt3, *, n_slots):
    db, n_pages, _ = bias3.shape
    per = COMPACT_SEQS if db % COMPACT_SEQS == 0 else 1
    out = pl.BlockSpec((per, 1, n_slots), lambda b: (b, 0, 0))
    return pl.pallas_call(
        functools.partial(_sample_compact_body, n_slots=n_slots),
        grid=(db // per,),
        in_specs=[pl.BlockSpec((per, n_pages, PAGE_SIZE), lambda b: (b, 0, 0)),
                  pl.BlockSpec((per, PAGE_SIZE, n_pages), lambda b: (b, 0, 0)),
                  pl.BlockSpec((per, 1, n_pages), lambda b: (b, 0, 0))],
        out_specs=[out, out],
        out_shape=[jax.ShapeDtypeStruct((db, 1, n_slots), jnp.int32)] * 2,
        compiler_params=_cparams(("arbitrary",)),
        name="compact_sample",
    )(bias3, bias3_t, pt3)


def _sample_attn_body(row_ref, q_ref, slot_ref, bnew_ref, kn_ref, vn_ref, sz_ref, k_hbm, v_hbm,
                      o_ref, kbuf, vbuf, sem):
    b = pl.program_id(0)
    nb = pl.num_programs(0)
    n_slots = kbuf.shape[1]
    buf = b % 2
    grp = N_HEADS // N_KV_HEADS
    scale = HEAD_DIM ** -0.5

    def row_copies(seq, j, to):
        r = row_ref[seq, j]
        return (pltpu.make_async_copy(k_hbm.at[r], kbuf.at[to, j], sem.at[0, to]),
                pltpu.make_async_copy(v_hbm.at[r], vbuf.at[to, j], sem.at[1, to]))

    def start_all(seq, to):
        def f(j, c):
            ck, cv = row_copies(seq, j, to)
            ck.start()
            cv.start()
            return c
        lax.fori_loop(0, n_slots, f, 0, unroll=8)

    def wait_all(to):
        for blk in range(n_slots // PAGE_SIZE):
            rows = pl.ds(blk * PAGE_SIZE, PAGE_SIZE)
            src = pl.ds(0, PAGE_SIZE)
            pltpu.make_async_copy(k_hbm.at[src], kbuf.at[to, rows], sem.at[0, to]).wait()
            pltpu.make_async_copy(v_hbm.at[src], vbuf.at[to, rows], sem.at[1, to]).wait()

    @pl.when(b == 0)
    def _():
        start_all(0, 0)

    @pl.when(b + 1 < nb)
    def _():
        start_all(b + 1, 1 - buf)

    wait_all(buf)

    q = q_ref[0]
    head_s = lax.broadcasted_iota(jnp.int32, (N_HEADS, n_slots), 0)
    head_o = lax.broadcasted_iota(jnp.int32, (N_HEADS, HEAD_DIM), 0)
    k0, k1 = (kbuf[buf, :, kh, :].astype(BF16) for kh in range(N_KV_HEADS))
    v0, v1 = (vbuf[buf, :, kh, :].astype(BF16) for kh in range(N_KV_HEADS))
    s0 = lax.dot_general(q, k0, _NT, preferred_element_type=F32)
    s1 = lax.dot_general(q, k1, _NT, preferred_element_type=F32)
    s = jnp.where(head_s < grp, s0, s1) * scale + jnp.where(slot_ref[0] >= 0, 0.0, NEG)
    kn = kn_ref[0].astype(F32)
    vn = vn_ref[0].astype(F32)
    kn8 = jnp.where(head_o < grp, kn[:, :HEAD_DIM], kn[:, HEAD_DIM:])
    vn8 = jnp.where(head_o < grp, vn[:, :HEAD_DIM], vn[:, HEAD_DIM:])
    s_new = _lane_bcast(jnp.sum(q.astype(F32) * kn8, axis=-1, keepdims=True), N_HEADS) * scale + bnew_ref[0]
    m = jnp.maximum(_lane_bcast(jnp.max(s, axis=-1, keepdims=True), N_HEADS), s_new)
    p = jnp.exp(s - m[:, :1])
    p_new = jnp.exp(s_new - m)
    l = _lane_bcast(jnp.sum(p, axis=-1, keepdims=True), N_HEADS) + p_new
    pb = p.astype(BF16)
    pv = jnp.where(head_o < grp,
                   jnp.dot(pb, v0, preferred_element_type=F32),
                   jnp.dot(pb, v1, preferred_element_type=F32))
    o = (pv + p_new * vn8) / l
    o_ref[0] = (o * sz_ref[0].astype(F32)).astype(BF16)


def _sample_attn(rows, q3, slots3, bnew3, kn3, vn3, sz3, k_rows, v_rows):
    db, n_slots = rows.shape
    per_b = lambda r, c: pl.BlockSpec((1, r, c), lambda b, rw: (b, 0, 0))
    grid_spec = pltpu.PrefetchScalarGridSpec(
        num_scalar_prefetch=1,
        grid=(db,),
        in_specs=[per_b(N_HEADS, HEAD_DIM), per_b(1, n_slots), per_b(1, LANES),
                  per_b(1, KV_WIDTH), per_b(1, KV_WIDTH), per_b(N_HEADS, HEAD_DIM),
                  pl.BlockSpec(memory_space=pl.ANY), pl.BlockSpec(memory_space=pl.ANY)],
        out_specs=per_b(N_HEADS, HEAD_DIM),
        scratch_shapes=[pltpu.VMEM((2, n_slots, N_KV_HEADS, HEAD_DIM), F32),
                        pltpu.VMEM((2, n_slots, N_KV_HEADS, HEAD_DIM), F32),
                        pltpu.SemaphoreType.DMA((2, 2))],
    )
    return pl.pallas_call(
        _sample_attn_body,
        grid_spec=grid_spec,
        out_shape=jax.ShapeDtypeStruct((db, N_HEADS, HEAD_DIM), BF16),
        compiler_params=_cparams(("arbitrary",)),
        name="attn_sample",
    )(rows, q3, slots3, bnew3, kn3, vn3, sz3, k_rows, v_rows)


def _col_tile(h_main, t, n=1):
    return h_main[:, t * COL_TILE:(t + n) * COL_TILE]


def kernel(x_prompt, x_sample, cache_k, cache_v, cache_kidx, page_table, pre_g, w_in, a_ln_g, a_ln_b,
           a_ws, a_bs, w_oa, w_ob, w_out, post_g):
    batch, seq, _ = x_prompt.shape
    db, ds, _ = x_sample.shape
    depth = w_in.shape[0]
    n_pages = page_table.shape[1]
    past = n_pages * PAGE_SIZE
    assert ds == 1 and seq % 512 == 0 and past % SCORE_CHUNK == 0
    top_k_s = min(TOPK_MAX, (past + ds) // 4)
    assert top_k_s <= past and top_k_s % LANES == 0

    pos_p = jnp.arange(seq)
    pos_s = past + (jnp.arange(db * ds) % ds)
    tabs_p = _rope_tables(pos_p, HEAD_DIM) + _rope_tables(pos_p, IDX_DIM)
    tabs_s = _rope_tables(pos_s, HEAD_DIM) + _rope_tables(pos_s, IDX_DIM)

    hp = x_prompt.reshape(batch * seq, D_MODEL)
    hs = x_sample.reshape(db * ds, D_MODEL)
    outs = [[] for _ in range(8)]
    for l in range(depth):
        w_t = w_in[l].T
        woa, wob, wout = w_oa[l].astype(BF16), w_ob[l].astype(BF16), w_out[l].astype(BF16)
        g_pre, g_post = pre_g[l][None], post_g[l][None]
        ln_g, ln_b = a_ln_g[l][None], a_ln_b[l][None]

        xn = _prenorm(hp, g_pre, tm=512)
        h_main, gv, w_tiles = _proj_main(xn, w_t, *tabs_p, ln_g, ln_b, tm=512, seq_rows=seq, gv_rows=CHUNK,
                                         cast_weights=True)
        k, v, ki, kb, _, kie, kio, _, vb_t, ws_t = _proj_tail(xn, w_t, *tabs_p, tm=512, seq_rows=seq,
                                                              with_transposed=True)
        ya = _gate(h_main, a_ws[l], a_bs[l].T, tm=512)
        ob = _prompt_attn(h_main, ws_t, kie, kio, kb, vb_t, batch=batch, seq=seq)
        hp = _merge(hp, ya, ob, h_main, woa, wob, wout, g_post, tm=256)
        outs[0].append(k.reshape(batch, seq, N_KV_HEADS, HEAD_DIM))
        outs[1].append(v.reshape(batch, seq, N_KV_HEADS, HEAD_DIM))
        outs[2].append(ki.reshape(batch, seq, IDX_DIM))
        outs[3].append(gv.reshape(batch, CHUNK, A_WIDTH))

        m_s = db * ds
        xn = _prenorm(hs, g_pre, tm=m_s)
        h_main, gv = _proj_main(xn, w_tiles, *tabs_s, ln_g, ln_b, tm=m_s, seq_rows=m_s, gv_rows=m_s,
                                cast_weights=False)
        k, v, ki, kb, vb, kie, kio, ws = _proj_tail(xn, w_t, *tabs_s, tm=m_s, seq_rows=m_s,
                                                    with_transposed=False)
        w0 = jnp.repeat(a_ws[l][:, 0, 0], LANES)[None]
        b0 = jnp.repeat(a_bs[l][:, 0], LANES)[None]
        ya = _gate_row(h_main, gv, w0, b0)
        qi3 = _col_tile(h_main, T_QI).reshape(db, IDX_HEADS, IDX_DIM)
        kidx_t = jnp.swapaxes(cache_kidx[l], 1, 2)
        scores = _sample_scores(page_table, qi3, ws.reshape(db, IDX_HEADS, 1), kidx_t)
        bias, bnew = _sample_select(scores, h_main, kie, ws, top_k=top_k_s)
        bias3 = bias.reshape(db, n_pages, PAGE_SIZE)
        slots3, rows3 = _sample_compact(bias3, jnp.swapaxes(bias3, 1, 2),
                                        page_table.reshape(db, 1, n_pages), n_slots=top_k_s)
        pool_rows = lambda c: c.reshape(-1, N_KV_HEADS, HEAD_DIM)
        ob = _sample_attn(rows3.reshape(db, top_k_s),
                          _col_tile(h_main, T_Q).reshape(db, N_HEADS, HEAD_DIM),
                          slots3, bnew.reshape(db, 1, LANES),
                          kb.reshape(db, 1, KV_WIDTH), vb.reshape(db, 1, KV_WIDTH),
                          _col_tile(h_main, T_ZB).reshape(db, N_HEADS, HEAD_DIM),
                          pool_rows(cache_k[l]), pool_rows(cache_v[l])).reshape(db, B_WIDTH)
        hs = _merge(hs, ya, ob, h_main, woa, wob, wout, g_post, tm=m_s)
        outs[4].append(k.reshape(db, ds, N_KV_HEADS, HEAD_DIM))
        outs[5].append(v.reshape(db, ds, N_KV_HEADS, HEAD_DIM))
        outs[6].append(ki.reshape(db, ds, IDX_DIM))
        outs[7].append(gv.reshape(db, ds, A_WIDTH))

    st = [jnp.stack(o, axis=0) for o in outs]
    return (hp.reshape(batch, seq, D_MODEL), hs.reshape(db, ds, D_MODEL),
            st[0], st[1], st[2], st[3], st[4], st[5], st[6], st[7])
```

```python
import functools

import jax
import jax.numpy as jnp
from jax import lax
from jax.experimental import pallas as pl
from jax.experimental.pallas import tpu as pltpu

F32 = jnp.float32
BF16 = jnp.bfloat16

D_MODEL = 2048
CHUNK = 128
A_GROUPS = 8
A_WIDTH = 1024
N_HEADS = 8
N_KV_HEADS = 2
HEAD_DIM = 128
B_WIDTH = 1024
KV_WIDTH = 256
IDX_HEADS = 16
IDX_DIM = 64
TOPK_MAX = 256
ROPE_THETA = 10000.0
EPS = 1e-6
PAGE_SIZE = 128
Q_BLOCK = 128
NEG = -1e30
LOG2_E = 1.4426950408889634

LANES = 128
BF16_SUBLANES = 16
MXU_WIDTH = 256
NORM_STREAMS = 4
_NT = (((1,), (1,)), ((), ()))
COL_TILE = 1024
T_UA, T_VA, T_ZA, T_Q, T_ZB, T_QI, T_GA, T_GB = 0, 1, 2, 3, 4, 5, 6, 8
N_MAIN_TILES = 10
ROW_K = 4 * COL_TILE
ROW_ZB = ROW_K + 2 * KV_WIDTH
ROW_KI = ROW_ZB + B_WIDTH + IDX_HEADS * IDX_DIM
ROW_GA = ROW_KI + IDX_DIM + IDX_HEADS
VMEM_LIMIT = 56 * 1024 * 1024


def _cparams(sem):
    return pltpu.CompilerParams(dimension_semantics=sem, vmem_limit_bytes=VMEM_LIMIT)


def _rope_tables(pos, dim):
    half = dim // 2
    inv = ROPE_THETA ** (-jnp.arange(half, dtype=F32) / half)
    ang = pos.astype(F32)[:, None] * inv[None, :]
    cos = jnp.cos(ang)
    sin = jnp.sin(ang)
    reps = LANES // dim
    cos_t = jnp.tile(jnp.concatenate([cos, cos], axis=-1), (1, reps))
    sin_t = jnp.tile(jnp.concatenate([-sin, sin], axis=-1), (1, reps))
    return cos_t, sin_t


def _rope128(x, cos, sin):
    return x * cos + pltpu.roll(x, 64, 1) * sin


def _rope64(x, cos, sin):
    lane = lax.broadcasted_iota(jnp.int32, x.shape, 1)
    first = (lane % IDX_DIM) < (IDX_DIM // 2)
    partner = jnp.where(first, pltpu.roll(x, LANES - 32, 1), pltpu.roll(x, 32, 1))
    return x * cos + partner * sin


def _rmsnorm_rows(xf, g):
    ms = jnp.mean(xf * xf, axis=-1, keepdims=True)
    return xf * lax.rsqrt(ms + EPS) * g


_TILE_KINDS = ("copy", "ln", "silu", "rope128", "silu", "rope64", "sigmoid", "sigmoid", "sigmoid", "sigmoid")


def _sigmoid(x):
    return 0.5 * jnp.tanh(0.5 * x) + 0.5


def _prenorm_body(*refs):
    x_refs, g_ref, o_ref = refs[:-2], refs[-2], refs[-1]
    xs = [r[...] for r in x_refs]
    ms = sum(jnp.sum(x * x, axis=-1, keepdims=True) for x in xs) * (1.0 / D_MODEL)
    scale = lax.rsqrt(ms + EPS)
    w = xs[0].shape[1]
    for c, x in enumerate(xs):
        cs = slice(c * w, (c + 1) * w)
        o_ref[:, cs] = (x * scale * g_ref[:, cs]).astype(BF16)


def _prenorm(x2, g, *, tm):
    m = x2.shape[0]
    w = D_MODEL // NORM_STREAMS
    return pl.pallas_call(
        _prenorm_body,
        grid=(m // tm,),
        in_specs=[pl.BlockSpec((tm, w), lambda i, c=c: (i, c)) for c in range(NORM_STREAMS)]
                 + [pl.BlockSpec((1, D_MODEL), lambda i: (0, 0))],
        out_specs=pl.BlockSpec((tm, D_MODEL), lambda i: (i, 0)),
        out_shape=jax.ShapeDtypeStruct((m, D_MODEL), BF16),
        compiler_params=_cparams(("arbitrary",)),
        name="prenorm",
    )(*([x2] * NORM_STREAMS), g)


def _proj_main_body(xn_ref, *refs, gv_rows, n_row_tiles, cast_weights):
    n_chunks = COL_TILE // MXU_WIDTH
    w_refs, refs = refs[:n_chunks], refs[n_chunks:]
    cq_ref, sq_ref, ci_ref, si_ref, lng_ref, lnb_ref, h_ref, gv_ref = refs[:8]
    wbf_ref = refs[8] if cast_weights else None
    acc_ref = refs[-1]
    j = pl.program_id(0)
    i = pl.program_id(1)
    tm = xn_ref.shape[0]
    chunks = [slice(c * MXU_WIDTH, (c + 1) * MXU_WIDTH) for c in range(n_chunks)]
    weights = lambda c: wbf_ref[chunks[c], :] if cast_weights else w_refs[c][...]

    def finish_chunk(kind, cs, stats):
        acc = acc_ref[:, cs]
        if kind == "copy":
            h_ref[:, cs] = acc.astype(BF16)
        elif kind == "ln":
            mu, rstd = stats
            vn = (acc - mu) * rstd * lng_ref[:, cs] + lnb_ref[:, cs]
            h_ref[:, cs] = vn.astype(BF16)
            gv_ref[:, cs] = vn[tm - gv_rows:, :]
        elif kind == "silu":
            h_ref[:, cs] = (acc * _sigmoid(acc)).astype(BF16)
        elif kind == "sigmoid":
            h_ref[:, cs] = _sigmoid(acc).astype(BF16)
        else:
            rope, cos, sin = ((_rope128, cq_ref[...], sq_ref[...]) if kind == "rope128"
                              else (_rope64, ci_ref[...], si_ref[...]))
            for h in range(MXU_WIDTH // LANES):
                sl = slice(cs.start + h * LANES, cs.start + (h + 1) * LANES)
                h_ref[:, sl] = rope(acc[:, h * LANES:(h + 1) * LANES], cos, sin).astype(BF16)

    def run(kind, finish, matmul):
        stats = None
        if finish and kind == "ln":
            acc = acc_ref[...]
            mu = jnp.mean(acc, axis=-1, keepdims=True)
            d = acc - mu
            stats = (mu, lax.rsqrt(jnp.mean(d * d, axis=-1, keepdims=True) + EPS))
        for c, cs in enumerate(chunks):
            if finish:
                finish_chunk(kind, cs, stats)
            if matmul:
                acc_ref[:, cs] = lax.dot_general(xn_ref[...], weights(c), _NT, preferred_element_type=F32)

    @pl.when(i == 0)
    def _():
        if cast_weights:
            for c, cs in enumerate(chunks):
                wbf_ref[cs, :] = w_refs[c][...].astype(BF16)
        run(None, False, True)

    @pl.when((i == 0) & (j != T_VA))
    def _():
        gv_ref[...] = jnp.zeros(gv_ref.shape, F32)

    for kind in sorted(set(_TILE_KINDS)):
        is_kind = functools.reduce(jnp.logical_or, [j == t for t, k in enumerate(_TILE_KINDS) if k == kind])

        @pl.when(is_kind & (i > 0) & (i < n_row_tiles))
        def _(kind=kind):
            run(kind, True, True)

        @pl.when(is_kind & (i == n_row_tiles))
        def _(kind=kind):
            run(kind, True, False)


def _main_tile_row(j, chunk):
    g = BF16_SUBLANES
    skip_kv = (ROW_ZB - T_ZB * COL_TILE) // g
    skip_idx = (ROW_GA - ROW_ZB - (T_GA - T_ZB) * COL_TILE) // g
    return (j * (COL_TILE // g) + chunk * (MXU_WIDTH // g)
            + jnp.where(j >= T_ZB, skip_kv, 0) + jnp.where(j >= T_GA, skip_idx, 0)) * g


def _proj_main(xn, w, cq, sq, ci, si, ln_g, ln_b, *, tm, seq_rows, gv_rows, cast_weights):
    m = xn.shape[0]
    tiles_per_seq = seq_rows // tm
    n_seq = m // seq_rows
    n_row_tiles = m // tm
    n_chunks = COL_TILE // MXU_WIDTH
    body = functools.partial(_proj_main_body, gv_rows=gv_rows, n_row_tiles=n_row_tiles,
                             cast_weights=cast_weights)
    if cast_weights:
        w_chunk = lambda c: pl.BlockSpec((pl.Element(MXU_WIDTH), pl.Element(D_MODEL)),
                                         lambda j, i: (_main_tile_row(j, c), 0))
    else:
        w_chunk = lambda c: pl.BlockSpec((MXU_WIDTH, D_MODEL), lambda j, i: (j * n_chunks + c, 0))
    w_out_spec = [pl.BlockSpec((COL_TILE, D_MODEL), lambda j, i: (j, 0))] if cast_weights else []
    w_out_shape = [jax.ShapeDtypeStruct((N_MAIN_TILES * COL_TILE, D_MODEL), BF16)] if cast_weights else []
    prev = lambda i: jnp.maximum(i - 1, 0)
    tab = pl.BlockSpec((tm, LANES), lambda j, i: (prev(i) % tiles_per_seq, 0))
    row = lambda n: pl.BlockSpec((1, n), lambda j, i: (0, 0))
    gv_block = lambda j, i: (jnp.where(j == T_VA, prev(i) // tiles_per_seq, n_seq + (j > T_VA)), 0)
    h_main, gv, *w_bf = pl.pallas_call(
        body,
        grid=(N_MAIN_TILES, n_row_tiles + 1),
        in_specs=[pl.BlockSpec((tm, D_MODEL), lambda j, i: (jnp.minimum(i, n_row_tiles - 1), 0))]
                 + [w_chunk(c) for c in range(n_chunks)]
                 + [tab, tab, tab, tab, row(A_WIDTH), row(A_WIDTH)],
        out_specs=[
            pl.BlockSpec((tm, COL_TILE), lambda j, i: (prev(i), j)),
            pl.BlockSpec((gv_rows, A_WIDTH), gv_block),
        ] + w_out_spec,
        out_shape=[
            jax.ShapeDtypeStruct((m, N_MAIN_TILES * COL_TILE), BF16),
            jax.ShapeDtypeStruct(((n_seq + 2) * gv_rows, A_WIDTH), F32),
        ] + w_out_shape,
        scratch_shapes=[pltpu.VMEM((tm, COL_TILE), F32)],
        compiler_params=_cparams(("arbitrary", "arbitrary")),
        name="proj_main",
    )(xn, *([w] * n_chunks), cq, sq, ci, si, ln_g, ln_b)
    return (h_main, gv[:n_seq * gv_rows], *w_bf)


def _proj_tail_body(xn_ref, wkv_ref, wix_ref, ck_ref, sk_ref, ci_ref, si_ref,
                    k_ref, v_ref, ki_ref, kb_ref, vb_ref, kie_ref, kio_ref, ws_ref, *t_refs):
    xn = xn_ref[...]
    acc = lax.dot_general(xn, wkv_ref[...].astype(BF16), _NT, preferred_element_type=F32)
    cos = ck_ref[...]
    sin = sk_ref[...]
    for kh in range(N_KV_HEADS):
        sl = slice(kh * HEAD_DIM, (kh + 1) * HEAD_DIM)
        r = _rope128(acc[:, sl], cos, sin)
        k_ref[:, kh, :] = r
        kb_ref[:, sl] = r.astype(BF16)
        v_ref[:, kh, :] = acc[:, KV_WIDTH + kh * HEAD_DIM:KV_WIDTH + (kh + 1) * HEAD_DIM]
    vb_ref[...] = acc[:, KV_WIDTH:2 * KV_WIDTH].astype(BF16)
    t = lax.dot_general(xn, wix_ref[...].astype(BF16), _NT, preferred_element_type=F32)
    r = _rope64(t, ci_ref[...], si_ref[...])
    ki_ref[...] = r[:, :IDX_DIM]
    lane = lax.broadcasted_iota(jnp.int32, r.shape, 1)
    ke = jnp.where(lane < IDX_DIM, r, 0.0)
    kie_ref[...] = ke.astype(BF16)
    kio_ref[...] = pltpu.roll(ke, IDX_DIM, 1).astype(BF16)
    w_scale = IDX_HEADS ** -0.5 * IDX_DIM ** -0.5
    ws_ref[...] = t[:, IDX_DIM:IDX_DIM + IDX_HEADS] * w_scale
    if t_refs:
        vb_t_ref, ws_t_ref = t_refs
        vb_t_ref[...] = acc[:, KV_WIDTH:2 * KV_WIDTH].T.astype(BF16)
        ws_t_ref[...] = t.T[IDX_DIM:IDX_DIM + IDX_HEADS, :] * w_scale


def _proj_tail(xn, w_t, ck, sk, ci, si, *, tm, seq_rows, with_transposed):
    m = xn.shape[0]
    blk_t = lambda n: pl.BlockSpec((n, tm), lambda i: (0, i))
    extra_specs = [blk_t(KV_WIDTH), blk_t(IDX_HEADS)] if with_transposed else []
    extra_shapes = ([jax.ShapeDtypeStruct((KV_WIDTH, m), BF16), jax.ShapeDtypeStruct((IDX_HEADS, m), F32)]
                    if with_transposed else [])
    tiles_per_seq = seq_rows // tm
    tab = pl.BlockSpec((tm, LANES), lambda i: (i % tiles_per_seq, 0))
    blk = lambda n: pl.BlockSpec((tm, n), lambda i: (i, 0))
    kv_rows = pl.BlockSpec((tm, N_KV_HEADS, HEAD_DIM), lambda i: (i, 0, 0))
    return pl.pallas_call(
        _proj_tail_body,
        grid=(m // tm,),
        in_specs=[
            blk(D_MODEL),
            pl.BlockSpec((2 * KV_WIDTH, D_MODEL), lambda i: (ROW_K // (2 * KV_WIDTH), 0)),
            pl.BlockSpec((LANES, D_MODEL), lambda i: (ROW_KI // LANES, 0)),
            tab, tab, tab, tab,
        ],
        out_specs=[kv_rows, kv_rows, blk(IDX_DIM), blk(KV_WIDTH), blk(KV_WIDTH),
                   blk(LANES), blk(LANES), blk(IDX_HEADS)] + extra_specs,
        out_shape=[
            jax.ShapeDtypeStruct((m, N_KV_HEADS, HEAD_DIM), F32),
            jax.ShapeDtypeStruct((m, N_KV_HEADS, HEAD_DIM), F32),
            jax.ShapeDtypeStruct((m, IDX_DIM), F32),
            jax.ShapeDtypeStruct((m, KV_WIDTH), BF16),
            jax.ShapeDtypeStruct((m, KV_WIDTH), BF16),
            jax.ShapeDtypeStruct((m, LANES), BF16),
            jax.ShapeDtypeStruct((m, LANES), BF16),
            jax.ShapeDtypeStruct((m, IDX_HEADS), F32),
        ] + extra_shapes,
        compiler_params=_cparams(("arbitrary",)),
        name="proj_tail",
    )(xn, w_t, w_t, ck, sk, ci, si)


def _gate_body(u_ref, vn_ref, sz_ref, ws_ref, bst_ref, y_ref, *, n_chunks):
    rr = lax.broadcasted_iota(jnp.int32, (CHUNK, CHUNK), 0)
    cc = lax.broadcasted_iota(jnp.int32, (CHUNK, CHUNK), 1)
    tril = cc <= rr
    for g in range(A_GROUPS):
        wm = jnp.where(tril, ws_ref[g], 0.0).astype(BF16)
        b = bst_ref[:, g:g + 1]
        cs = slice(g * LANES, (g + 1) * LANES)
        for c in range(n_chunks):
            rs = slice(c * CHUNK, (c + 1) * CHUNK)
            s = jnp.dot(wm, vn_ref[rs, cs], preferred_element_type=F32) + b
            y = u_ref[rs, cs].astype(F32) * s * sz_ref[rs, cs].astype(F32)
            y_ref[rs, cs] = y.astype(BF16)


def _gate(h_main, a_ws, bs_t, *, tm):
    m = h_main.shape[0]
    body = functools.partial(_gate_body, n_chunks=tm // CHUNK)
    col = lambda t: pl.BlockSpec((tm, COL_TILE), lambda i, t=t: (i, t))
    return pl.pallas_call(
        body,
        grid=(m // tm,),
        in_specs=[col(T_UA), col(T_VA), col(T_ZA),
                  pl.BlockSpec((A_GROUPS, CHUNK, CHUNK), lambda i: (0, 0, 0)),
                  pl.BlockSpec((CHUNK, A_GROUPS), lambda i: (0, 0))],
        out_specs=pl.BlockSpec((tm, A_WIDTH), lambda i: (i, 0)),
        out_shape=jax.ShapeDtypeStruct((m, A_WIDTH), BF16),
        compiler_params=_cparams(("arbitrary",)),
        name="gate_prompt",
    )(h_main, h_main, h_main, a_ws, bs_t)


def _gate_row_body(u_ref, vn_ref, sz_ref, w0_ref, b0_ref, y_ref):
    s = vn_ref[...] * w0_ref[...] + b0_ref[...]
    y_ref[...] = (u_ref[...].astype(F32) * s * sz_ref[...].astype(F32)).astype(BF16)


def _gate_row(h_main, vn, w0, b0):
    m = h_main.shape[0]
    col = lambda t: pl.BlockSpec((m, COL_TILE), lambda i, t=t: (0, t))
    full = lambda r: pl.BlockSpec((r, A_WIDTH), lambda i: (0, 0))
    return pl.pallas_call(
        _gate_row_body,
        grid=(1,),
        in_specs=[col(T_UA), full(m), col(T_ZA), full(1), full(1)],
        out_specs=full(m),
        out_shape=jax.ShapeDtypeStruct((m, A_WIDTH), BF16),
        compiler_params=_cparams(("arbitrary",)),
        name="gate_sample",
    )(h_main, vn, h_main, w0, b0)


def _lane_bcast(col, rows):
    return jnp.broadcast_to(col, (rows, LANES))


def _select_bias(sc_ref, extra_ref, kp, row_min, row_max, *, n_cols):
    rows = sc_ref.shape[0]
    n_tiles = n_cols // LANES
    extra = None if extra_ref is None else extra_ref[...]

    def count(pred):
        acc = jnp.zeros((rows, LANES), F32)
        for c in range(n_tiles):
            acc = acc + jnp.where(pred(sc_ref[:, c * LANES:(c + 1) * LANES], c), 1.0, 0.0)
        tot = jnp.sum(acc, axis=1, keepdims=True)
        return _lane_bcast(tot, rows)

    def count_ge(x):
        c = count(lambda s, _: s >= x)
        if extra is not None:
            c = c + jnp.where(extra >= x, 1.0, 0.0)
        return c

    c_max = count_ge(row_max)
    top = c_max >= kp
    lo0 = jnp.where(top, row_max, row_min)
    c0 = jnp.where(top, c_max, count_ge(row_min))

    def count_ge3(x1, x2, x3):
        a1 = jnp.zeros((rows, LANES), F32)
        a2 = jnp.zeros((rows, LANES), F32)
        a3 = jnp.zeros((rows, LANES), F32)
        for c in range(n_tiles):
            s = sc_ref[:, c * LANES:(c + 1) * LANES]
            a1 = a1 + jnp.where(s >= x1, 1.0, 0.0)
            a2 = a2 + jnp.where(s >= x2, 1.0, 0.0)
            a3 = a3 + jnp.where(s >= x3, 1.0, 0.0)
        res = []
        for a, x in ((a1, x1), (a2, x2), (a3, x3)):
            tot = _lane_bcast(jnp.sum(a, axis=1, keepdims=True), rows)
            if extra is not None:
                tot = tot + jnp.where(extra >= x, 1.0, 0.0)
            res.append(tot)
        return res

    def step(st):
        lo, hi, c_lo, _, it = st
        mid = 0.5 * lo + 0.5 * hi
        act = (c_lo != kp) & (mid > lo) & (mid < hi)
        any_act = jnp.max(jnp.where(act, 1.0, 0.0))
        clamp = lambda x: jnp.minimum(jnp.maximum(x, lo), hi)
        q1 = clamp(0.75 * lo + 0.25 * hi)
        q3 = clamp(0.25 * lo + 0.75 * hi)
        c1, c2, c3 = count_ge3(q1, mid, q3)
        g1, g2, g3 = c1 >= kp, c2 >= kp, c3 >= kp
        lo_n = jnp.where(g3, q3, jnp.where(g2, mid, jnp.where(g1, q1, lo)))
        c_n = jnp.where(g3, c3, jnp.where(g2, c2, jnp.where(g1, c1, c_lo)))
        hi_n = jnp.where(g3, hi, jnp.where(g2, q3, jnp.where(g1, mid, jnp.minimum(q1, mid))))
        return (jnp.where(act, lo_n, lo), jnp.where(act, hi_n, hi), jnp.where(act, c_n, c_lo),
                any_act, it + 1)

    def cond(st):
        return (st[3] > 0.0) & (st[4] < 400)

    lo, _, c_lo, _, _ = lax.while_loop(cond, step, (lo0, row_max, c0, jnp.float32(1.0), jnp.int32(0)))

    exact = jnp.max(jnp.where(c_lo != kp, 1.0, 0.0)) == 0.0

    @pl.when(exact)
    def _():
        for c in range(n_tiles):
            sl = slice(c * LANES, (c + 1) * LANES)
            sc_ref[:, sl] = jnp.where(sc_ref[:, sl] >= lo, 0.0, NEG)
        if extra is not None:
            extra_ref[...] = jnp.where(extra >= lo, 0.0, NEG)

    @pl.when(jnp.logical_not(exact))
    def _():
        n_gt = count(lambda s, _: s > lo)
        if extra is not None:
            n_gt = n_gt + jnp.where(extra > lo, 1.0, 0.0)
        need = kp - n_gt
        lane = lax.broadcasted_iota(jnp.int32, (rows, LANES), 1).astype(F32)

        def count_eq_upto(jx):
            c = count(lambda s, c: (s == lo) & (lane + float(c * LANES) <= jx))
            if extra is not None:
                c = c + jnp.where((extra == lo) & (jx >= float(n_cols)), 1.0, 0.0)
            return c

        last = n_cols if extra is not None else n_cols - 1
        j_lo = jnp.full((rows, LANES), -1.0, F32)
        j_hi = jnp.full((rows, LANES), float(last), F32)

        def jstep(_, st):
            a, b = st
            mid = jnp.floor(0.5 * (a + b))
            ok = count_eq_upto(mid) >= need
            return jnp.where(ok, a, mid), jnp.where(ok, mid, b)

        n_steps = max(1, (n_cols + 1).bit_length())
        _, j_hi = lax.fori_loop(0, n_steps, jstep, (j_lo, j_hi))
        for c in range(n_tiles):
            sl = slice(c * LANES, (c + 1) * LANES)
            s = sc_ref[:, sl]
            keep = (s > lo) | ((s == lo) & (lane + float(c * LANES) <= j_hi))
            sc_ref[:, sl] = jnp.where(keep, 0.0, NEG)
        if extra is not None:
            keep = (extra > lo) | ((extra == lo) & (j_hi >= float(n_cols)))
            extra_ref[...] = jnp.where(keep, 0.0, NEG)


def _select_bias_cols(sc_ref, kp, col_min, col_max, *, n_rows):
    n_tiles = n_rows // LANES
    n_pivots = 3 if n_rows <= 512 else 2 if n_rows <= 1024 else 1

    def counts(preds):
        accs = [jnp.zeros((LANES, LANES), F32) for _ in preds]
        for r in range(n_tiles):
            s = sc_ref[r * LANES:(r + 1) * LANES, :]
            accs = [a + jnp.where(p(s, r), 1.0, 0.0) for a, p in zip(accs, preds)]
        return [jnp.sum(a, axis=0, keepdims=True) for a in accs]

    count = lambda pred: counts([pred])[0]
    count_ge = lambda x: count(lambda s, _: s >= x)
    c_max = count_ge(col_max)
    top = c_max >= kp
    lo0 = jnp.where(top, col_max, col_min)
    c0 = jnp.where(top, c_max, count_ge(col_min))

    def step(st):
        lo, hi, c_lo, _, it = st
        mid = 0.5 * lo + 0.5 * hi
        act = (c_lo != kp) & (mid > lo) & (mid < hi)
        any_act = jnp.max(jnp.where(act, 1.0, 0.0))
        fr = [(k + 1) / (n_pivots + 1) for k in range(n_pivots)]
        piv = [mid if f == 0.5 else jnp.minimum(jnp.maximum((1.0 - f) * lo + f * hi, lo), hi) for f in fr]
        cs = counts([lambda s, _, x=x: s >= x for x in piv])
        lo_n, c_n, hi_n = lo, c_lo, functools.reduce(jnp.minimum, piv)
        for k in range(n_pivots):
            ge = cs[k] >= kp
            nxt = piv[k + 1] if k + 1 < n_pivots else hi
            lo_n = jnp.where(ge, piv[k], lo_n)
            c_n = jnp.where(ge, cs[k], c_n)
            hi_n = jnp.where(ge, nxt, hi_n)
        return jnp.where(act, lo_n, lo), jnp.where(act, hi_n, hi), jnp.where(act, c_n, c_lo), any_act, it + 1

    def cond(st):
        return (st[3] > 0.0) & (st[4] < 400)

    lo, _, c_lo, _, _ = lax.while_loop(cond, step, (lo0, col_max, c0, jnp.float32(1.0), jnp.int32(0)))
    exact = jnp.max(jnp.where(c_lo != kp, 1.0, 0.0)) == 0.0

    @pl.when(exact)
    def _():
        for r in range(n_tiles):
            rs = slice(r * LANES, (r + 1) * LANES)
            sc_ref[rs, :] = jnp.where(sc_ref[rs, :] >= lo, 0.0, NEG)

    @pl.when(jnp.logical_not(exact))
    def _():
        need = kp - count(lambda s, _: s > lo)
        key = lax.broadcasted_iota(jnp.int32, (LANES, LANES), 0).astype(F32)
        count_eq_upto = lambda jx: count(lambda s, r: (s == lo) & (key + float(r * LANES) <= jx))

        def jstep(_, st):
            a, b = st
            mid = jnp.floor(0.5 * (a + b))
            ok = count_eq_upto(mid) >= need
            return jnp.where(ok, a, mid), jnp.where(ok, mid, b)

        j_lo = jnp.full((1, LANES), -1.0, F32)
        j_hi = jnp.full((1, LANES), float(n_rows - 1), F32)
        _, j_hi = lax.fori_loop(0, max(1, n_rows.bit_length()), jstep, (j_lo, j_hi))
        for r in range(n_tiles):
            rs = slice(r * LANES, (r + 1) * LANES)
            s = sc_ref[rs, :]
            keep = (s > lo) | ((s == lo) & (key + float(r * LANES) <= j_hi))
            sc_ref[rs, :] = jnp.where(keep, 0.0, NEG)


def _prompt_attn_block(nk, qi_ref, wst_ref, kie_ref, kio_ref, q_ref, kb_ref, vt_ref, sz_ref,
                       o_ref, sc_ref, *, top_k, key_chunk):
    qb = pl.program_id(1)
    n_pairs = IDX_HEADS // 2
    grp = N_HEADS // N_KV_HEADS
    qpos = qb * Q_BLOCK + lax.broadcasted_iota(jnp.int32, (LANES, LANES), 1)
    key0 = lax.broadcasted_iota(jnp.int32, (LANES, LANES), 0)

    qs = jnp.concatenate([qi_ref[:, p * LANES:(p + 1) * LANES] for p in range(n_pairs)], axis=0)
    wrows = [wst_ref[h:h + 1, :] for h in range(IDX_HEADS)]
    cmax = jnp.full((LANES, LANES), -jnp.inf, F32)
    cmin = jnp.full((LANES, LANES), jnp.inf, F32)
    for kc in range(0, nk, key_chunk):
        le = lax.dot_general(kie_ref[kc:kc + key_chunk, :], qs, _NT, preferred_element_type=F32)
        lo = lax.dot_general(kio_ref[kc:kc + key_chunk, :], qs, _NT, preferred_element_type=F32)
        for r in range(key_chunk // LANES):
            rs = slice(r * LANES, (r + 1) * LANES)
            acc = jnp.zeros((LANES, LANES), F32)
            for p in range(n_pairs):
                cs = slice(p * LANES, (p + 1) * LANES)
                acc = acc + jnp.maximum(le[rs, cs], 0.0) * wrows[2 * p]
                acc = acc + jnp.maximum(lo[rs, cs], 0.0) * wrows[2 * p + 1]
            causal = key0 + (kc + r * LANES) <= qpos
            cmax = jnp.maximum(cmax, jnp.where(causal, acc, -jnp.inf))
            cmin = jnp.minimum(cmin, jnp.where(causal, acc, jnp.inf))
            sc_ref[kc + r * LANES:kc + (r + 1) * LANES, :] = jnp.where(causal, acc, -jnp.inf)

    col_max = jnp.max(cmax, axis=0, keepdims=True)
    col_min = jnp.min(cmin, axis=0, keepdims=True)
    kp = jnp.minimum(qpos[0:1, :] + 1, top_k).astype(F32)
    _select_bias_cols(sc_ref, kp, col_min, col_max, n_rows=nk)

    bias = jnp.concatenate([sc_ref[0:nk, :]] * grp, axis=1)
    for kh in range(N_KV_HEADS):
        qh = jnp.concatenate(
            [q_ref[:, (kh * grp + g) * HEAD_DIM:(kh * grp + g + 1) * HEAD_DIM] for g in range(grp)], axis=0)
        ks = slice(kh * HEAD_DIM, (kh + 1) * HEAD_DIM)
        s = lax.dot_general(kb_ref[0:nk, ks], qh, _NT, preferred_element_type=F32) + bias
        m = jnp.max(s, axis=0, keepdims=True)
        p = jnp.exp2((s - m) * (HEAD_DIM ** -0.5 * LOG2_E))
        l = jnp.sum(p, axis=0, keepdims=True)
        ot = jnp.dot(vt_ref[ks, 0:nk], p.astype(BF16), preferred_element_type=F32) / l
        for g in range(grp):
            hs = slice((kh * grp + g) * HEAD_DIM, (kh * grp + g + 1) * HEAD_DIM)
            o = ot[:, g * LANES:(g + 1) * LANES].T
            o_ref[:, hs] = (o * sz_ref[:, hs].astype(F32)).astype(BF16)


def _prompt_attn_body(*refs, seq, top_k, n_buckets, key_chunk):
    qb = pl.program_id(1)
    span = seq // n_buckets
    per = span // Q_BLOCK
    for c in range(n_buckets):
        @pl.when(qb // per == c)
        def _(c=c):
            _prompt_attn_block((c + 1) * span, *refs, top_k=top_k, key_chunk=key_chunk)


def _prompt_attn(h_main, ws_t, kie, kio, kb, vb_t, *, batch, seq):
    top_k = min(TOPK_MAX, seq // 4)
    n_qb = seq // Q_BLOCK
    key_chunk = min(512, seq)
    n_buckets = max(1, seq // 512)
    body = functools.partial(_prompt_attn_body, seq=seq, top_k=top_k, n_buckets=n_buckets, key_chunk=key_chunk)
    col = lambda t: pl.BlockSpec((Q_BLOCK, COL_TILE), lambda b, q, t=t: (b * n_qb + q, t))
    seqblk = lambda n: pl.BlockSpec((seq, n), lambda b, q: (b, 0))
    return pl.pallas_call(
        body,
        grid=(batch, n_qb),
        in_specs=[col(T_QI),
                  pl.BlockSpec((IDX_HEADS, Q_BLOCK), lambda b, q: (0, b * n_qb + q)),
                  seqblk(LANES), seqblk(LANES),
                  col(T_Q), seqblk(KV_WIDTH),
                  pl.BlockSpec((KV_WIDTH, seq), lambda b, q: (0, b)),
                  col(T_ZB)],
        out_specs=pl.BlockSpec((Q_BLOCK, B_WIDTH), lambda b, q: (b * n_qb + q, 0)),
        out_shape=jax.ShapeDtypeStruct((batch * seq, B_WIDTH), BF16),
        scratch_shapes=[pltpu.VMEM((seq, Q_BLOCK), F32)],
        compiler_params=_cparams(("arbitrary", "arbitrary")),
        name="attn_prompt",
    )(h_main, ws_t, kie, kio, h_main, kb, vb_t, h_main)


def _merge_body(x_ref, ya_ref, ob_ref, ga_ref, gb_ref, woa_ref, wob_ref, wout_ref, pg_ref, o_ref):
    pa = jnp.dot(ya_ref[...], woa_ref[...], preferred_element_type=F32)
    pb = jnp.dot(ob_ref[...], wob_ref[...], preferred_element_type=F32)
    mix = ga_ref[...].astype(F32) * pa + gb_ref[...].astype(F32) * pb
    r = jnp.dot(mix.astype(BF16), wout_ref[...], preferred_element_type=F32)
    o_ref[...] = x_ref[...] + _rmsnorm_rows(r, pg_ref[...])


def _merge(x2, ya, ob, h_main, w_oa, w_ob, w_out, post_g, *, tm):
    m = x2.shape[0]
    const = lambda r, c: pl.BlockSpec((r, c), lambda i: (0, 0), pipeline_mode=pl.Buffered(1))
    return pl.pallas_call(
        _merge_body,
        grid=(m // tm,),
        in_specs=[
            pl.BlockSpec((tm, D_MODEL), lambda i: (i, 0)),
            pl.BlockSpec((tm, A_WIDTH), lambda i: (i, 0)),
            pl.BlockSpec((tm, B_WIDTH), lambda i: (i, 0)),
            pl.BlockSpec((tm, D_MODEL), lambda i: (i, T_GA // 2)),
            pl.BlockSpec((tm, D_MODEL), lambda i: (i, T_GB // 2)),
            const(A_WIDTH, D_MODEL), const(B_WIDTH, D_MODEL), const(D_MODEL, D_MODEL),
            const(1, D_MODEL),
        ],
        out_specs=pl.BlockSpec((tm, D_MODEL), lambda i: (i, 0)),
        out_shape=jax.ShapeDtypeStruct((m, D_MODEL), F32),
        compiler_params=_cparams(("arbitrary",)),
        name="merge",
    )(x2, ya, ob, h_main, h_main, w_oa, w_ob, w_out, post_g)


SCORE_CHUNK = 2048
COMPACT_SEQS = 4
GATHER_BUFS = 3


def _sample_scores_body(pt_ref, q_ref, w_ref, kidx_hbm, o_ref, buf, sem):
    db, n_pages = pt_ref.shape
    past = n_pages * PAGE_SIZE

    def page_copy(b, p, slot):
        dst = buf.at[slot, :, pl.ds(pl.multiple_of(p * PAGE_SIZE, PAGE_SIZE), PAGE_SIZE)]
        return pltpu.make_async_copy(kidx_hbm.at[pt_ref[b, p]], dst, sem.at[slot])

    def start_all(b, slot):
        def f(p, c):
            page_copy(b, p, slot).start()
            return c
        lax.fori_loop(0, n_pages, f, 0, unroll=8)

    def wait_all(slot):
        for p in range(n_pages):
            dst = buf.at[slot, :, pl.ds(p * PAGE_SIZE, PAGE_SIZE)]
            pltpu.make_async_copy(kidx_hbm.at[0], dst, sem.at[slot]).wait()

    start_all(0, 0)

    def per_seq(b, c):
        slot = b % 2

        @pl.when(b + 1 < db)
        def _():
            start_all(b + 1, 1 - slot)

        wait_all(slot)
        q = q_ref[b]
        w = w_ref[b]
        for ch in range(past // SCORE_CHUNK):
            cs = slice(ch * SCORE_CHUNK, (ch + 1) * SCORE_CHUNK)
            logit = jnp.dot(q, buf[slot, :, cs].astype(BF16), preferred_element_type=F32)
            o_ref[pl.ds(b, 1), cs] = jnp.sum(jnp.maximum(logit, 0.0) * w, axis=0, keepdims=True)
        return c

    lax.fori_loop(0, db, per_seq, 0)


def _sample_scores(page_table, qi3, ws3, kidx_pages_t):
    db, n_pages = page_table.shape
    past = n_pages * PAGE_SIZE
    grid_spec = pltpu.PrefetchScalarGridSpec(
        num_scalar_prefetch=1,
        grid=(1,),
        in_specs=[pl.BlockSpec((db, IDX_HEADS, IDX_DIM), lambda i, pt: (0, 0, 0)),
                  pl.BlockSpec((db, IDX_HEADS, 1), lambda i, pt: (0, 0, 0)),
                  pl.BlockSpec(memory_space=pl.ANY)],
        out_specs=pl.BlockSpec((db, past), lambda i, pt: (0, 0)),
        scratch_shapes=[pltpu.VMEM((2, IDX_DIM, past), F32), pltpu.SemaphoreType.DMA((2,))],
    )
    return pl.pallas_call(
        _sample_scores_body,
        grid_spec=grid_spec,
        out_shape=jax.ShapeDtypeStruct((db, past), F32),
        compiler_params=_cparams(("arbitrary",)),
        name="scores_sample",
    )(page_table, qi3, ws3, kidx_pages_t)


def _sample_select_body(sc_ref, qi_ref, kie_ref, ws_ref, bias_ref, bnew_ref, *, top_k):
    rows, past = sc_ref.shape
    lane = lax.broadcasted_iota(jnp.int32, (rows, LANES), 1)
    ki = kie_ref[...].astype(F32)
    ki = ki + pltpu.roll(ki, IDX_DIM, 1)
    s_new = jnp.zeros((rows, 1), F32)
    for p in range(IDX_HEADS // 2):
        prod = qi_ref[:, p * LANES:(p + 1) * LANES].astype(F32) * ki
        l_even = jnp.sum(jnp.where(lane < IDX_DIM, prod, 0.0), axis=1, keepdims=True)
        l_odd = jnp.sum(jnp.where(lane >= IDX_DIM, prod, 0.0), axis=1, keepdims=True)
        s_new = s_new + jnp.maximum(l_even, 0.0) * ws_ref[:, 2 * p:2 * p + 1]
        s_new = s_new + jnp.maximum(l_odd, 0.0) * ws_ref[:, 2 * p + 1:2 * p + 2]
    extra = _lane_bcast(s_new, rows)
    bnew_ref[...] = extra
    rmax = extra
    rmin = extra
    for c in range(past // LANES):
        sl = slice(c * LANES, (c + 1) * LANES)
        s = sc_ref[:, sl]
        bias_ref[:, sl] = s
        rmax = jnp.maximum(rmax, s)
        rmin = jnp.minimum(rmin, s)
    row_max = _lane_bcast(jnp.max(rmax, axis=1, keepdims=True), rows)
    row_min = _lane_bcast(jnp.min(rmin, axis=1, keepdims=True), rows)
    kp = jnp.full((rows, LANES), float(top_k), F32)
    _select_bias(bias_ref, bnew_ref, kp, row_min, row_max, n_cols=past)


def _sample_select(scores, h_main, kie, ws, *, top_k):
    db, past = scores.shape
    full = lambda r, c: pl.BlockSpec((r, c), lambda i: (0, 0))
    return pl.pallas_call(
        functools.partial(_sample_select_body, top_k=top_k),
        grid=(1,),
        in_specs=[full(db, past),
                  pl.BlockSpec((db, COL_TILE), lambda i: (0, T_QI)),
                  full(db, LANES), full(db, IDX_HEADS)],
        out_specs=[full(db, past), full(db, LANES)],
        out_shape=[jax.ShapeDtypeStruct((db, past), F32), jax.ShapeDtypeStruct((db, LANES), F32)],
        compiler_params=_cparams(("arbitrary",)),
        name="select_sample",
    )(scores, h_main, kie, ws)


def _sample_compact_body(m_ref, mt_ref, pt_ref, idx_ref, row_ref, *, n_slots):
    n_pages = m_ref.shape[1]
    for q in range(m_ref.shape[0]):
        _compact_one(m_ref.at[q], mt_ref.at[q], pt_ref.at[q], idx_ref.at[q], row_ref.at[q],
                     n_pages=n_pages, n_slots=n_slots)


def _compact_one(m_ref, mt_ref, pt_ref, idx_ref, row_ref, *, n_pages, n_slots):
    pt = jnp.broadcast_to(pt_ref[...], (8, n_pages))
    pt_hi = (pt // PAGE_SIZE).astype(F32).astype(BF16)
    pt_lo = (pt % PAGE_SIZE).astype(F32).astype(BF16)
    one = lambda pred: jnp.where(pred, 1.0, 0.0)
    kept = m_ref[...] == 0.0
    kept_t = mt_ref[...] == 0.0
    ri = lax.broadcasted_iota(jnp.int32, (PAGE_SIZE, PAGE_SIZE), 0)
    ci = lax.broadcasted_iota(jnp.int32, (PAGE_SIZE, PAGE_SIZE), 1)
    rp = lax.broadcasted_iota(jnp.int32, (n_pages, n_pages), 0)
    cp = lax.broadcasted_iota(jnp.int32, (n_pages, n_pages), 1)
    plt = jnp.dot(one(ci <= ri).astype(BF16), one(kept_t).astype(BF16), preferred_element_type=F32)
    n_row = plt[PAGE_SIZE - 1:PAGE_SIZE, :]
    n_col = _lane_bcast(jnp.sum(one(kept), axis=1, keepdims=True), n_pages)
    e_col = jnp.dot(one(cp <= rp).astype(BF16), n_col.astype(BF16), preferred_element_type=F32)
    n_row8 = jnp.broadcast_to(n_row, (8, n_pages))
    e_row8 = jnp.dot(n_row8.astype(BF16), one(rp <= cp).astype(BF16), preferred_element_type=F32)
    off_row8 = e_row8 - n_row8
    n_total = e_col[n_pages - 1:n_pages, :]
    page_id = lax.broadcasted_iota(jnp.int32, (n_pages, LANES), 0).astype(F32)
    for jt in range(n_slots // LANES):
        j = (lax.broadcasted_iota(jnp.int32, (1, LANES), 1) + jt * LANES).astype(F32)
        page_j = jnp.sum(one(e_col <= j), axis=0, keepdims=True)
        pick = one(page_id == page_j).astype(BF16)
        prefix_j = jnp.dot(plt.astype(BF16), pick, preferred_element_type=F32)
        off_j = jnp.dot(off_row8.astype(BF16), pick, preferred_element_type=F32)[0:1]
        local_j = jnp.sum(one(prefix_j <= j - off_j), axis=0, keepdims=True)
        pos = page_j * float(PAGE_SIZE) + local_j
        phys = (jnp.dot(pt_hi, pick, preferred_element_type=F32)[0:1] * float(PAGE_SIZE)
                + jnp.dot(pt_lo, pick, preferred_element_type=F32)[0:1])
        row = phys * float(PAGE_SIZE) + local_j
        used = j < n_total
        sl = slice(jt * LANES, (jt + 1) * LANES)
        idx_ref[:, sl] = jnp.where(used, pos, -1.0).astype(jnp.int32)
        row_ref[:, sl] = jnp.where(used, row, 0.0).astype(jnp.int32)


def _sample_compact(bias3, bias3_t, pt3, *, n_slots):
    db, n_pages, _ = bias3.shape
    per = COMPACT_SEQS if db % COMPACT_SEQS == 0 else 1
    out = pl.BlockSpec((per, 1, n_slots), lambda b: (b, 0, 0))
    return pl.pallas_call(
        functools.partial(_sample_compact_body, n_slots=n_slots),
        grid=(db // per,),
        in_specs=[pl.BlockSpec((per, n_pages, PAGE_SIZE), lambda b: (b, 0, 0)),
                  pl.BlockSpec((per, PAGE_SIZE, n_pages), lambda b: (b, 0, 0)),
                  pl.BlockSpec((per, 1, n_pages), lambda b: (b, 0, 0))],
        out_specs=[out, out],
        out_shape=[jax.ShapeDtypeStruct((db, 1, n_slots), jnp.int32)] * 2,
        compiler_params=_cparams(("arbitrary",)),
        name="compact_sample",
    )(bias3, bias3_t, pt3)


def _sample_attn_body(row_ref, q_ref, slot_ref, bnew_ref, kn_ref, vn_ref, sz_ref, k_hbm, v_hbm, o_ref, *scratch):
    kbufs, vbufs, sem = scratch[:GATHER_BUFS], scratch[GATHER_BUFS:2 * GATHER_BUFS], scratch[-1]
    b = pl.program_id(0)
    nb = pl.num_programs(0)
    n_slots = kbufs[0].shape[0]
    ahead = GATHER_BUFS - 1
    grp = N_HEADS // N_KV_HEADS
    scale = HEAD_DIM ** -0.5

    def row_copies(seq, j, to):
        r = row_ref[seq, j]
        return (pltpu.make_async_copy(k_hbm.at[r], kbufs[to].at[j], sem.at[0, to]),
                pltpu.make_async_copy(v_hbm.at[r], vbufs[to].at[j], sem.at[1, to]))

    def start_all(seq, to, inline):
        def f(j, c):
            ck, cv = row_copies(seq, j, to)
            ck.start()
            cv.start()
            return c
        if inline:
            for j in range(n_slots):
                f(j, 0)
        else:
            lax.fori_loop(0, n_slots, f, 0, unroll=8)

    def wait_all(to):
        for blk in range(n_slots // PAGE_SIZE):
            rows = pl.ds(blk * PAGE_SIZE, PAGE_SIZE)
            src = pl.ds(0, PAGE_SIZE)
            pltpu.make_async_copy(k_hbm.at[src], kbufs[to].at[rows], sem.at[0, to]).wait()
            pltpu.make_async_copy(v_hbm.at[src], vbufs[to].at[rows], sem.at[1, to]).wait()

    def attend(buf):
        q = q_ref[0]
        head_s = lax.broadcasted_iota(jnp.int32, (N_HEADS, n_slots), 0)
        head_o = lax.broadcasted_iota(jnp.int32, (N_HEADS, HEAD_DIM), 0)
        k0, k1 = (kbufs[buf][:, kh, :].astype(BF16) for kh in range(N_KV_HEADS))
        v0, v1 = (vbufs[buf][:, kh, :].astype(BF16) for kh in range(N_KV_HEADS))
        s0 = lax.dot_general(q, k0, _NT, preferred_element_type=F32)
        s1 = lax.dot_general(q, k1, _NT, preferred_element_type=F32)
        s = jnp.where(head_s < grp, s0, s1) * scale + jnp.where(slot_ref[0] >= 0, 0.0, NEG)
        kn = kn_ref[0].astype(F32)
        vn = vn_ref[0].astype(F32)
        kn8 = jnp.where(head_o < grp, kn[:, :HEAD_DIM], kn[:, HEAD_DIM:])
        vn8 = jnp.where(head_o < grp, vn[:, :HEAD_DIM], vn[:, HEAD_DIM:])
        s_new = _lane_bcast(jnp.sum(q.astype(F32) * kn8, axis=-1, keepdims=True), N_HEADS) * scale + bnew_ref[0]
        m = jnp.maximum(_lane_bcast(jnp.max(s, axis=-1, keepdims=True), N_HEADS), s_new)
        p = jnp.exp(s - m[:, :1])
        p_new = jnp.exp(s_new - m)
        l = _lane_bcast(jnp.sum(p, axis=-1, keepdims=True), N_HEADS) + p_new
        pb = p.astype(BF16)
        pv = jnp.where(head_o < grp,
                       jnp.dot(pb, v0, preferred_element_type=F32),
                       jnp.dot(pb, v1, preferred_element_type=F32))
        o = (pv + p_new * vn8) / l
        o_ref[0] = (o * sz_ref[0].astype(F32)).astype(BF16)

    @pl.when(b == 0)
    def _():
        for first in range(ahead):
            @pl.when(first < nb)
            def _(first=first):
                start_all(first, first, inline=False)

    for buf in range(GATHER_BUFS):
        mine = b % GATHER_BUFS == buf

        @pl.when(mine & (b + ahead < nb))
        def _(buf=buf):
            wait_all(buf)
            start_all(b + ahead, (buf + ahead) % GATHER_BUFS, inline=True)
            attend(buf)

        @pl.when(mine & (b + ahead >= nb))
        def _(buf=buf):
            wait_all(buf)
            attend(buf)


def _sample_attn(rows, q3, slots3, bnew3, kn3, vn3, sz3, k_rows, v_rows):
    db, n_slots = rows.shape
    per_b = lambda r, c: pl.BlockSpec((1, r, c), lambda b, rw: (b, 0, 0))
    grid_spec = pltpu.PrefetchScalarGridSpec(
        num_scalar_prefetch=1,
        grid=(db,),
        in_specs=[per_b(N_HEADS, HEAD_DIM), per_b(1, n_slots), per_b(1, LANES),
                  per_b(1, KV_WIDTH), per_b(1, KV_WIDTH), per_b(N_HEADS, HEAD_DIM),
                  pl.BlockSpec(memory_space=pl.ANY), pl.BlockSpec(memory_space=pl.ANY)],
        out_specs=per_b(N_HEADS, HEAD_DIM),
        scratch_shapes=[pltpu.VMEM((n_slots, N_KV_HEADS, HEAD_DIM), F32)] * (2 * GATHER_BUFS)
                       + [pltpu.SemaphoreType.DMA((2, GATHER_BUFS))],
    )
    return pl.pallas_call(
        _sample_attn_body,
        grid_spec=grid_spec,
        out_shape=jax.ShapeDtypeStruct((db, N_HEADS, HEAD_DIM), BF16),
        compiler_params=_cparams(("arbitrary",)),
        name="attn_sample",
    )(rows, q3, slots3, bnew3, kn3, vn3, sz3, k_rows, v_rows)


def _col_tile(h_main, t, n=1):
    return h_main[:, t * COL_TILE:(t + n) * COL_TILE]


def kernel(x_prompt, x_sample, cache_k, cache_v, cache_kidx, page_table, pre_g, w_in, a_ln_g, a_ln_b,
           a_ws, a_bs, w_oa, w_ob, w_out, post_g):
    batch, seq, _ = x_prompt.shape
    db, ds, _ = x_sample.shape
    depth = w_in.shape[0]
    n_pages = page_table.shape[1]
    past = n_pages * PAGE_SIZE
    assert ds == 1 and seq % 512 == 0 and past % SCORE_CHUNK == 0
    top_k_s = min(TOPK_MAX, (past + ds) // 4)
    assert top_k_s <= past and top_k_s % LANES == 0

    pos_p = jnp.arange(seq)
    pos_s = past + (jnp.arange(db * ds) % ds)
    tabs_p = _rope_tables(pos_p, HEAD_DIM) + _rope_tables(pos_p, IDX_DIM)
    tabs_s = _rope_tables(pos_s, HEAD_DIM) + _rope_tables(pos_s, IDX_DIM)

    hp = x_prompt.reshape(batch * seq, D_MODEL)
    hs = x_sample.reshape(db * ds, D_MODEL)
    outs = [[] for _ in range(8)]
    for l in range(depth):
        w_t = w_in[l].T
        woa, wob, wout = w_oa[l].astype(BF16), w_ob[l].astype(BF16), w_out[l].astype(BF16)
        g_pre, g_post = pre_g[l][None], post_g[l][None]
        ln_g, ln_b = a_ln_g[l][None], a_ln_b[l][None]

        xn = _prenorm(hp, g_pre, tm=512)
        h_main, gv, w_tiles = _proj_main(xn, w_t, *tabs_p, ln_g, ln_b, tm=512, seq_rows=seq, gv_rows=CHUNK,
                                         cast_weights=True)
        k, v, ki, kb, _, kie, kio, _, vb_t, ws_t = _proj_tail(xn, w_t, *tabs_p, tm=512, seq_rows=seq,
                                                              with_transposed=True)
        ya = _gate(h_main, a_ws[l], a_bs[l].T, tm=512)
        ob = _prompt_attn(h_main, ws_t, kie, kio, kb, vb_t, batch=batch, seq=seq)
        hp = _merge(hp, ya, ob, h_main, woa, wob, wout, g_post, tm=256)
        outs[0].append(k.reshape(batch, seq, N_KV_HEADS, HEAD_DIM))
        outs[1].append(v.reshape(batch, seq, N_KV_HEADS, HEAD_DIM))
        outs[2].append(ki.reshape(batch, seq, IDX_DIM))
        outs[3].append(gv.reshape(batch, CHUNK, A_WIDTH))

        m_s = db * ds
        xn = _prenorm(hs, g_pre, tm=m_s)
        h_main, gv = _proj_main(xn, w_tiles, *tabs_s, ln_g, ln_b, tm=m_s, seq_rows=m_s, gv_rows=m_s,
                                cast_weights=False)
        k, v, ki, kb, vb, kie, kio, ws = _proj_tail(xn, w_t, *tabs_s, tm=m_s, seq_rows=m_s,
                                                    with_transposed=False)
        w0 = jnp.repeat(a_ws[l][:, 0, 0], LANES)[None]
        b0 = jnp.repeat(a_bs[l][:, 0], LANES)[None]
        ya = _gate_row(h_main, gv, w0, b0)
        qi3 = _col_tile(h_main, T_QI).reshape(db, IDX_HEADS, IDX_DIM)
        kidx_t = jnp.swapaxes(cache_kidx[l], 1, 2)
        scores = _sample_scores(page_table, qi3, ws.reshape(db, IDX_HEADS, 1), kidx_t)
        bias, bnew = _sample_select(scores, h_main, kie, ws, top_k=top_k_s)
        bias3 = bias.reshape(db, n_pages, PAGE_SIZE)
        slots3, rows3 = _sample_compact(bias3, jnp.swapaxes(bias3, 1, 2),
                                        page_table.reshape(db, 1, n_pages), n_slots=top_k_s)
        pool_rows = lambda c: c.reshape(-1, N_KV_HEADS, HEAD_DIM)
        ob = _sample_attn(rows3.reshape(db, top_k_s),
                          _col_tile(h_main, T_Q).reshape(db, N_HEADS, HEAD_DIM),
                          slots3, bnew.reshape(db, 1, LANES),
                          kb.reshape(db, 1, KV_WIDTH), vb.reshape(db, 1, KV_WIDTH),
                          _col_tile(h_main, T_ZB).reshape(db, N_HEADS, HEAD_DIM),
                          pool_rows(cache_k[l]), pool_rows(cache_v[l])).reshape(db, B_WIDTH)
        hs = _merge(hs, ya, ob, h_main, woa, wob, wout, g_post, tm=m_s)
        outs[4].append(k.reshape(db, ds, N_KV_HEADS, HEAD_DIM))
        outs[5].append(v.reshape(db, ds, N_KV_HEADS, HEAD_DIM))
        outs[6].append(ki.reshape(db, ds, IDX_DIM))
        outs[7].append(gv.reshape(db, ds, A_WIDTH))

    st = [jnp.stack(o, axis=0) for o in outs]
    return (hp.reshape(batch, seq, D_MODEL), hs.reshape(db, ds, D_MODEL),
            st[0], st[1], st[2], st[3], st[4], st[5], st[6], st[7])
```

```python
import functools

import jax
import jax.numpy as jnp
from jax import lax
from jax.experimental import pallas as pl
from jax.experimental.pallas import tpu as pltpu

F32 = jnp.float32
BF16 = jnp.bfloat16

D_MODEL = 2048
CHUNK = 128
A_GROUPS = 8
A_WIDTH = 1024
N_HEADS = 8
N_KV_HEADS = 2
HEAD_DIM = 128
B_WIDTH = 1024
KV_WIDTH = 256
IDX_HEADS = 16
IDX_DIM = 64
TOPK_MAX = 256
ROPE_THETA = 10000.0
EPS = 1e-6
PAGE_SIZE = 128
Q_BLOCK = 128
NEG = -1e30
LOG2_E = 1.4426950408889634

LANES = 128
BF16_SUBLANES = 16
MXU_WIDTH = 256
NORM_STREAMS = 4
_NT = (((1,), (1,)), ((), ()))
COL_TILE = 1024
T_UA, T_VA, T_ZA, T_Q, T_ZB, T_QI, T_GA, T_GB = 0, 1, 2, 3, 4, 5, 6, 8
N_MAIN_TILES = 10
ROW_K = 4 * COL_TILE
ROW_ZB = ROW_K + 2 * KV_WIDTH
ROW_KI = ROW_ZB + B_WIDTH + IDX_HEADS * IDX_DIM
ROW_GA = ROW_KI + IDX_DIM + IDX_HEADS
VMEM_LIMIT = 56 * 1024 * 1024


def _cparams(sem):
    return pltpu.CompilerParams(dimension_semantics=sem, vmem_limit_bytes=VMEM_LIMIT)


def _rope_tables(pos, dim):
    half = dim // 2
    inv = ROPE_THETA ** (-jnp.arange(half, dtype=F32) / half)
    ang = pos.astype(F32)[:, None] * inv[None, :]
    cos = jnp.cos(ang)
    sin = jnp.sin(ang)
    reps = LANES // dim
    cos_t = jnp.tile(jnp.concatenate([cos, cos], axis=-1), (1, reps))
    sin_t = jnp.tile(jnp.concatenate([-sin, sin], axis=-1), (1, reps))
    return cos_t, sin_t


def _rope128(x, cos, sin):
    return x * cos + pltpu.roll(x, 64, 1) * sin


def _rope64(x, cos, sin):
    lane = lax.broadcasted_iota(jnp.int32, x.shape, 1)
    first = (lane % IDX_DIM) < (IDX_DIM // 2)
    partner = jnp.where(first, pltpu.roll(x, LANES - 32, 1), pltpu.roll(x, 32, 1))
    return x * cos + partner * sin


def _rmsnorm_rows(xf, g):
    ms = jnp.mean(xf * xf, axis=-1, keepdims=True)
    return xf * lax.rsqrt(ms + EPS) * g


_TILE_KINDS = ("copy", "ln", "silu", "rope128", "silu", "rope64", "sigmoid", "sigmoid", "sigmoid", "sigmoid")


def _sigmoid(x):
    return 0.5 * jnp.tanh(0.5 * x) + 0.5


def _prenorm_rows(x_refs, g_ref, o_ref):
    xs = [r[...] for r in x_refs]
    ms = sum(jnp.sum(x * x, axis=-1, keepdims=True) for x in xs) * (1.0 / D_MODEL)
    scale = lax.rsqrt(ms + EPS)
    w = xs[0].shape[1]
    for c, x in enumerate(xs):
        cs = slice(c * w, (c + 1) * w)
        o_ref[:, cs] = (x * scale * g_ref[:, cs]).astype(BF16)


def _proj_main_body(xn_ref, *refs, gv_rows, n_row_tiles, cast_weights):
    n_chunks = COL_TILE // MXU_WIDTH
    w_refs, refs = refs[:n_chunks], refs[n_chunks:]
    cq_ref, sq_ref, ci_ref, si_ref, lng_ref, lnb_ref, h_ref, gv_ref = refs[:8]
    wbf_ref = refs[8] if cast_weights else None
    acc_ref = refs[-1]
    j = pl.program_id(0)
    i = pl.program_id(1)
    tm = xn_ref.shape[0]
    chunks = [slice(c * MXU_WIDTH, (c + 1) * MXU_WIDTH) for c in range(n_chunks)]
    weights = lambda c: wbf_ref[chunks[c], :] if cast_weights else w_refs[c][...]

    def finish_chunk(kind, cs, stats):
        acc = acc_ref[:, cs]
        if kind == "copy":
            h_ref[:, cs] = acc.astype(BF16)
        elif kind == "ln":
            mu, rstd = stats
            vn = (acc - mu) * rstd * lng_ref[:, cs] + lnb_ref[:, cs]
            h_ref[:, cs] = vn.astype(BF16)
            gv_ref[:, cs] = vn[tm - gv_rows:, :]
        elif kind == "silu":
            h_ref[:, cs] = (acc * _sigmoid(acc)).astype(BF16)
        elif kind == "sigmoid":
            h_ref[:, cs] = _sigmoid(acc).astype(BF16)
        else:
            rope, cos, sin = ((_rope128, cq_ref[...], sq_ref[...]) if kind == "rope128"
                              else (_rope64, ci_ref[...], si_ref[...]))
            for h in range(MXU_WIDTH // LANES):
                sl = slice(cs.start + h * LANES, cs.start + (h + 1) * LANES)
                h_ref[:, sl] = rope(acc[:, h * LANES:(h + 1) * LANES], cos, sin).astype(BF16)

    def run(kind, finish, matmul):
        stats = None
        if finish and kind == "ln":
            acc = acc_ref[...]
            mu = jnp.mean(acc, axis=-1, keepdims=True)
            d = acc - mu
            stats = (mu, lax.rsqrt(jnp.mean(d * d, axis=-1, keepdims=True) + EPS))
        for c, cs in enumerate(chunks):
            if finish:
                finish_chunk(kind, cs, stats)
            if matmul:
                acc_ref[:, cs] = lax.dot_general(xn_ref[...], weights(c), _NT, preferred_element_type=F32)

    @pl.when(i == 0)
    def _():
        if cast_weights:
            for c, cs in enumerate(chunks):
                wbf_ref[cs, :] = w_refs[c][...].astype(BF16)
        run(None, False, True)

    @pl.when((i == 0) & (j != T_VA))
    def _():
        gv_ref[...] = jnp.zeros(gv_ref.shape, F32)

    for kind in sorted(set(_TILE_KINDS)):
        is_kind = functools.reduce(jnp.logical_or, [j == t for t, k in enumerate(_TILE_KINDS) if k == kind])

        @pl.when(is_kind & (i > 0) & (i < n_row_tiles))
        def _(kind=kind):
            run(kind, True, True)

        @pl.when(is_kind & (i == n_row_tiles))
        def _(kind=kind):
            run(kind, True, False)


def _main_tile_row(j, chunk):
    g = BF16_SUBLANES
    skip_kv = (ROW_ZB - T_ZB * COL_TILE) // g
    skip_idx = (ROW_GA - ROW_ZB - (T_GA - T_ZB) * COL_TILE) // g
    return (j * (COL_TILE // g) + chunk * (MXU_WIDTH // g)
            + jnp.where(j >= T_ZB, skip_kv, 0) + jnp.where(j >= T_GA, skip_idx, 0)) * g


def _proj_main(xn, w, cq, sq, ci, si, ln_g, ln_b, *, tm, seq_rows, gv_rows, cast_weights):
    m = xn.shape[0]
    tiles_per_seq = seq_rows // tm
    n_seq = m // seq_rows
    n_row_tiles = m // tm
    n_chunks = COL_TILE // MXU_WIDTH
    body = functools.partial(_proj_main_body, gv_rows=gv_rows, n_row_tiles=n_row_tiles,
                             cast_weights=cast_weights)
    if cast_weights:
        w_chunk = lambda c: pl.BlockSpec((pl.Element(MXU_WIDTH), pl.Element(D_MODEL)),
                                         lambda j, i: (_main_tile_row(j, c), 0))
    else:
        w_chunk = lambda c: pl.BlockSpec((MXU_WIDTH, D_MODEL), lambda j, i: (j * n_chunks + c, 0))
    w_out_spec = [pl.BlockSpec((COL_TILE, D_MODEL), lambda j, i: (j, 0))] if cast_weights else []
    w_out_shape = [jax.ShapeDtypeStruct((N_MAIN_TILES * COL_TILE, D_MODEL), BF16)] if cast_weights else []
    prev = lambda i: jnp.maximum(i - 1, 0)
    tab = pl.BlockSpec((tm, LANES), lambda j, i: (prev(i) % tiles_per_seq, 0))
    row = lambda n: pl.BlockSpec((1, n), lambda j, i: (0, 0))
    gv_block = lambda j, i: (jnp.where(j == T_VA, prev(i) // tiles_per_seq, n_seq + (j > T_VA)), 0)
    h_main, gv, *w_bf = pl.pallas_call(
        body,
        grid=(N_MAIN_TILES, n_row_tiles + 1),
        in_specs=[pl.BlockSpec((tm, D_MODEL), lambda j, i: (jnp.minimum(i, n_row_tiles - 1), 0))]
                 + [w_chunk(c) for c in range(n_chunks)]
                 + [tab, tab, tab, tab, row(A_WIDTH), row(A_WIDTH)],
        out_specs=[
            pl.BlockSpec((tm, COL_TILE), lambda j, i: (prev(i), j)),
            pl.BlockSpec((gv_rows, A_WIDTH), gv_block),
        ] + w_out_spec,
        out_shape=[
            jax.ShapeDtypeStruct((m, N_MAIN_TILES * COL_TILE), BF16),
            jax.ShapeDtypeStruct(((n_seq + 2) * gv_rows, A_WIDTH), F32),
        ] + w_out_shape,
        scratch_shapes=[pltpu.VMEM((tm, COL_TILE), F32)],
        compiler_params=_cparams(("arbitrary", "arbitrary")),
        name="proj_main",
    )(xn, *([w] * n_chunks), cq, sq, ci, si, ln_g, ln_b)
    return (h_main, gv[:n_seq * gv_rows], *w_bf)


def _proj_tail_body(*refs):
    x_refs, refs = refs[:NORM_STREAMS], refs[NORM_STREAMS:]
    (g_ref, wkv_ref, wix_ref, ck_ref, sk_ref, ci_ref, si_ref,
     xn_ref, k_ref, v_ref, ki_ref, kb_ref, vb_ref, kie_ref, kio_ref, ws_ref, *t_refs) = refs
    _prenorm_rows(x_refs, g_ref, xn_ref)
    xn = xn_ref[...]
    acc = lax.dot_general(xn, wkv_ref[...].astype(BF16), _NT, preferred_element_type=F32)
    cos = ck_ref[...]
    sin = sk_ref[...]
    for kh in range(N_KV_HEADS):
        sl = slice(kh * HEAD_DIM, (kh + 1) * HEAD_DIM)
        r = _rope128(acc[:, sl], cos, sin)
        k_ref[:, kh, :] = r
        kb_ref[:, sl] = r.astype(BF16)
        v_ref[:, kh, :] = acc[:, KV_WIDTH + kh * HEAD_DIM:KV_WIDTH + (kh + 1) * HEAD_DIM]
    vb_ref[...] = acc[:, KV_WIDTH:2 * KV_WIDTH].astype(BF16)
    t = lax.dot_general(xn, wix_ref[...].astype(BF16), _NT, preferred_element_type=F32)
    r = _rope64(t, ci_ref[...], si_ref[...])
    ki_ref[...] = r[:, :IDX_DIM]
    lane = lax.broadcasted_iota(jnp.int32, r.shape, 1)
    ke = jnp.where(lane < IDX_DIM, r, 0.0)
    kie_ref[...] = ke.astype(BF16)
    kio_ref[...] = pltpu.roll(ke, IDX_DIM, 1).astype(BF16)
    w_scale = IDX_HEADS ** -0.5 * IDX_DIM ** -0.5
    ws_ref[...] = t[:, IDX_DIM:IDX_DIM + IDX_HEADS] * w_scale
    if t_refs:
        vb_t_ref, ws_t_ref = t_refs
        vb_t_ref[...] = acc[:, KV_WIDTH:2 * KV_WIDTH].T.astype(BF16)
        ws_t_ref[...] = t.T[IDX_DIM:IDX_DIM + IDX_HEADS, :] * w_scale


def _proj_tail(x2, pre_g, w_t, ck, sk, ci, si, *, tm, seq_rows, with_transposed):
    m = x2.shape[0]
    x_col = lambda c: pl.BlockSpec((tm, D_MODEL // NORM_STREAMS), lambda i: (i, c))
    blk_t = lambda n: pl.BlockSpec((n, tm), lambda i: (0, i))
    extra_specs = [blk_t(KV_WIDTH), blk_t(IDX_HEADS)] if with_transposed else []
    extra_shapes = ([jax.ShapeDtypeStruct((KV_WIDTH, m), BF16), jax.ShapeDtypeStruct((IDX_HEADS, m), F32)]
                    if with_transposed else [])
    tiles_per_seq = seq_rows // tm
    tab = pl.BlockSpec((tm, LANES), lambda i: (i % tiles_per_seq, 0))
    blk = lambda n: pl.BlockSpec((tm, n), lambda i: (i, 0))
    kv_rows = pl.BlockSpec((tm, N_KV_HEADS, HEAD_DIM), lambda i: (i, 0, 0))
    return pl.pallas_call(
        _proj_tail_body,
        grid=(m // tm,),
        in_specs=[x_col(c) for c in range(NORM_STREAMS)] + [
            pl.BlockSpec((1, D_MODEL), lambda i: (0, 0)),
            pl.BlockSpec((2 * KV_WIDTH, D_MODEL), lambda i: (ROW_K // (2 * KV_WIDTH), 0)),
            pl.BlockSpec((LANES, D_MODEL), lambda i: (ROW_KI // LANES, 0)),
            tab, tab, tab, tab,
        ],
        out_specs=[blk(D_MODEL), kv_rows, kv_rows, blk(IDX_DIM), blk(KV_WIDTH), blk(KV_WIDTH),
                   blk(LANES), blk(LANES), blk(IDX_HEADS)] + extra_specs,
        out_shape=[
            jax.ShapeDtypeStruct((m, D_MODEL), BF16),
            jax.ShapeDtypeStruct((m, N_KV_HEADS, HEAD_DIM), F32),
            jax.ShapeDtypeStruct((m, N_KV_HEADS, HEAD_DIM), F32),
            jax.ShapeDtypeStruct((m, IDX_DIM), F32),
            jax.ShapeDtypeStruct((m, KV_WIDTH), BF16),
            jax.ShapeDtypeStruct((m, KV_WIDTH), BF16),
            jax.ShapeDtypeStruct((m, LANES), BF16),
            jax.ShapeDtypeStruct((m, LANES), BF16),
            jax.ShapeDtypeStruct((m, IDX_HEADS), F32),
        ] + extra_shapes,
        compiler_params=_cparams(("arbitrary",)),
        name="proj_tail",
    )(*([x2] * NORM_STREAMS), pre_g, w_t, w_t, ck, sk, ci, si)


def _gate_body(u_ref, vn_ref, sz_ref, ws_ref, bst_ref, y_ref, *, n_chunks):
    rr = lax.broadcasted_iota(jnp.int32, (CHUNK, CHUNK), 0)
    cc = lax.broadcasted_iota(jnp.int32, (CHUNK, CHUNK), 1)
    tril = cc <= rr
    for g in range(A_GROUPS):
        wm = jnp.where(tril, ws_ref[g], 0.0).astype(BF16)
        b = bst_ref[:, g:g + 1]
        cs = slice(g * LANES, (g + 1) * LANES)
        for c in range(n_chunks):
            rs = slice(c * CHUNK, (c + 1) * CHUNK)
            s = jnp.dot(wm, vn_ref[rs, cs], preferred_element_type=F32) + b
            y = u_ref[rs, cs].astype(F32) * s * sz_ref[rs, cs].astype(F32)
            y_ref[rs, cs] = y.astype(BF16)


def _gate_row_body(u_ref, vn_ref, sz_ref, w0_ref, b0_ref, y_ref):
    s = vn_ref[...] * w0_ref[...] + b0_ref[...]
    y_ref[...] = (u_ref[...].astype(F32) * s * sz_ref[...].astype(F32)).astype(BF16)


def _gate_row(h_main, vn, w0, b0):
    m = h_main.shape[0]
    col = lambda t: pl.BlockSpec((m, COL_TILE), lambda i, t=t: (0, t))
    full = lambda r: pl.BlockSpec((r, A_WIDTH), lambda i: (0, 0))
    return pl.pallas_call(
        _gate_row_body,
        grid=(1,),
        in_specs=[col(T_UA), full(m), col(T_ZA), full(1), full(1)],
        out_specs=full(m),
        out_shape=jax.ShapeDtypeStruct((m, A_WIDTH), BF16),
        compiler_params=_cparams(("arbitrary",)),
        name="gate_sample",
    )(h_main, vn, h_main, w0, b0)


def _lane_bcast(col, rows):
    return jnp.broadcast_to(col, (rows, LANES))


def _select_bias(sc_ref, extra_ref, kp, row_min, row_max, *, n_cols):
    rows = sc_ref.shape[0]
    n_tiles = n_cols // LANES
    extra = None if extra_ref is None else extra_ref[...]

    def count(pred):
        acc = jnp.zeros((rows, LANES), F32)
        for c in range(n_tiles):
            acc = acc + jnp.where(pred(sc_ref[:, c * LANES:(c + 1) * LANES], c), 1.0, 0.0)
        tot = jnp.sum(acc, axis=1, keepdims=True)
        return _lane_bcast(tot, rows)

    def count_ge(x):
        c = count(lambda s, _: s >= x)
        if extra is not None:
            c = c + jnp.where(extra >= x, 1.0, 0.0)
        return c

    c_max = count_ge(row_max)
    top = c_max >= kp
    lo0 = jnp.where(top, row_max, row_min)
    c0 = jnp.where(top, c_max, count_ge(row_min))

    def count_ge3(x1, x2, x3):
        a1 = jnp.zeros((rows, LANES), F32)
        a2 = jnp.zeros((rows, LANES), F32)
        a3 = jnp.zeros((rows, LANES), F32)
        for c in range(n_tiles):
            s = sc_ref[:, c * LANES:(c + 1) * LANES]
            a1 = a1 + jnp.where(s >= x1, 1.0, 0.0)
            a2 = a2 + jnp.where(s >= x2, 1.0, 0.0)
            a3 = a3 + jnp.where(s >= x3, 1.0, 0.0)
        res = []
        for a, x in ((a1, x1), (a2, x2), (a3, x3)):
            tot = _lane_bcast(jnp.sum(a, axis=1, keepdims=True), rows)
            if extra is not None:
                tot = tot + jnp.where(extra >= x, 1.0, 0.0)
            res.append(tot)
        return res

    def step(st):
        lo, hi, c_lo, _, it = st
        mid = 0.5 * lo + 0.5 * hi
        act = (c_lo != kp) & (mid > lo) & (mid < hi)
        any_act = jnp.max(jnp.where(act, 1.0, 0.0))
        clamp = lambda x: jnp.minimum(jnp.maximum(x, lo), hi)
        q1 = clamp(0.75 * lo + 0.25 * hi)
        q3 = clamp(0.25 * lo + 0.75 * hi)
        c1, c2, c3 = count_ge3(q1, mid, q3)
        g1, g2, g3 = c1 >= kp, c2 >= kp, c3 >= kp
        lo_n = jnp.where(g3, q3, jnp.where(g2, mid, jnp.where(g1, q1, lo)))
        c_n = jnp.where(g3, c3, jnp.where(g2, c2, jnp.where(g1, c1, c_lo)))
        hi_n = jnp.where(g3, hi, jnp.where(g2, q3, jnp.where(g1, mid, jnp.minimum(q1, mid))))
        return (jnp.where(act, lo_n, lo), jnp.where(act, hi_n, hi), jnp.where(act, c_n, c_lo),
                any_act, it + 1)

    def cond(st):
        return (st[3] > 0.0) & (st[4] < 400)

    lo, _, c_lo, _, _ = lax.while_loop(cond, step, (lo0, row_max, c0, jnp.float32(1.0), jnp.int32(0)))

    exact = jnp.max(jnp.where(c_lo != kp, 1.0, 0.0)) == 0.0

    @pl.when(exact)
    def _():
        for c in range(n_tiles):
            sl = slice(c * LANES, (c + 1) * LANES)
            sc_ref[:, sl] = jnp.where(sc_ref[:, sl] >= lo, 0.0, NEG)
        if extra is not None:
            extra_ref[...] = jnp.where(extra >= lo, 0.0, NEG)

    @pl.when(jnp.logical_not(exact))
    def _():
        n_gt = count(lambda s, _: s > lo)
        if extra is not None:
            n_gt = n_gt + jnp.where(extra > lo, 1.0, 0.0)
        need = kp - n_gt
        lane = lax.broadcasted_iota(jnp.int32, (rows, LANES), 1).astype(F32)

        def count_eq_upto(jx):
            c = count(lambda s, c: (s == lo) & (lane + float(c * LANES) <= jx))
            if extra is not None:
                c = c + jnp.where((extra == lo) & (jx >= float(n_cols)), 1.0, 0.0)
            return c

        last = n_cols if extra is not None else n_cols - 1
        j_lo = jnp.full((rows, LANES), -1.0, F32)
        j_hi = jnp.full((rows, LANES), float(last), F32)

        def jstep(_, st):
            a, b = st
            mid = jnp.floor(0.5 * (a + b))
            ok = count_eq_upto(mid) >= need
            return jnp.where(ok, a, mid), jnp.where(ok, mid, b)

        n_steps = max(1, (n_cols + 1).bit_length())
        _, j_hi = lax.fori_loop(0, n_steps, jstep, (j_lo, j_hi))
        for c in range(n_tiles):
            sl = slice(c * LANES, (c + 1) * LANES)
            s = sc_ref[:, sl]
            keep = (s > lo) | ((s == lo) & (lane + float(c * LANES) <= j_hi))
            sc_ref[:, sl] = jnp.where(keep, 0.0, NEG)
        if extra is not None:
            keep = (extra > lo) | ((extra == lo) & (j_hi >= float(n_cols)))
            extra_ref[...] = jnp.where(keep, 0.0, NEG)


def _select_bias_cols(sc_ref, kp, col_min, col_max, *, n_rows):
    n_tiles = n_rows // LANES
    n_pivots = 3 if n_rows <= 512 else 2 if n_rows <= 1024 else 1

    def counts(preds):
        accs = [jnp.zeros((LANES, LANES), F32) for _ in preds]
        for r in range(n_tiles):
            s = sc_ref[r * LANES:(r + 1) * LANES, :]
            accs = [a + jnp.where(p(s, r), 1.0, 0.0) for a, p in zip(accs, preds)]
        return [jnp.sum(a, axis=0, keepdims=True) for a in accs]

    count = lambda pred: counts([pred])[0]
    count_ge = lambda x: count(lambda s, _: s >= x)
    c_max = count_ge(col_max)
    top = c_max >= kp
    lo0 = jnp.where(top, col_max, col_min)
    c0 = jnp.where(top, c_max, count_ge(col_min))

    def step(st):
        lo, hi, c_lo, _, it = st
        mid = 0.5 * lo + 0.5 * hi
        act = (c_lo != kp) & (mid > lo) & (mid < hi)
        any_act = jnp.max(jnp.where(act, 1.0, 0.0))
        fr = [(k + 1) / (n_pivots + 1) for k in range(n_pivots)]
        piv = [mid if f == 0.5 else jnp.minimum(jnp.maximum((1.0 - f) * lo + f * hi, lo), hi) for f in fr]
        cs = counts([lambda s, _, x=x: s >= x for x in piv])
        lo_n, c_n, hi_n = lo, c_lo, functools.reduce(jnp.minimum, piv)
        for k in range(n_pivots):
            ge = cs[k] >= kp
            nxt = piv[k + 1] if k + 1 < n_pivots else hi
            lo_n = jnp.where(ge, piv[k], lo_n)
            c_n = jnp.where(ge, cs[k], c_n)
            hi_n = jnp.where(ge, nxt, hi_n)
        return jnp.where(act, lo_n, lo), jnp.where(act, hi_n, hi), jnp.where(act, c_n, c_lo), any_act, it + 1

    def cond(st):
        return (st[3] > 0.0) & (st[4] < 400)

    lo, _, c_lo, _, _ = lax.while_loop(cond, step, (lo0, col_max, c0, jnp.float32(1.0), jnp.int32(0)))
    exact = jnp.max(jnp.where(c_lo != kp, 1.0, 0.0)) == 0.0

    @pl.when(exact)
    def _():
        for r in range(n_tiles):
            rs = slice(r * LANES, (r + 1) * LANES)
            sc_ref[rs, :] = jnp.where(sc_ref[rs, :] >= lo, 0.0, NEG)

    @pl.when(jnp.logical_not(exact))
    def _():
        need = kp - count(lambda s, _: s > lo)
        key = lax.broadcasted_iota(jnp.int32, (LANES, LANES), 0).astype(F32)
        count_eq_upto = lambda jx: count(lambda s, r: (s == lo) & (key + float(r * LANES) <= jx))

        def jstep(_, st):
            a, b = st
            mid = jnp.floor(0.5 * (a + b))
            ok = count_eq_upto(mid) >= need
            return jnp.where(ok, a, mid), jnp.where(ok, mid, b)

        j_lo = jnp.full((1, LANES), -1.0, F32)
        j_hi = jnp.full((1, LANES), float(n_rows - 1), F32)
        _, j_hi = lax.fori_loop(0, max(1, n_rows.bit_length()), jstep, (j_lo, j_hi))
        for r in range(n_tiles):
            rs = slice(r * LANES, (r + 1) * LANES)
            s = sc_ref[rs, :]
            keep = (s > lo) | ((s == lo) & (key + float(r * LANES) <= j_hi))
            sc_ref[rs, :] = jnp.where(keep, 0.0, NEG)


def _prompt_attn_block(nk, qi_ref, wst_ref, kie_ref, kio_ref, q_ref, kb_ref, vt_ref, sz_ref,
                       o_ref, sc_ref, *, top_k, key_chunk):
    qb = pl.program_id(1)
    n_pairs = IDX_HEADS // 2
    grp = N_HEADS // N_KV_HEADS
    qpos = qb * Q_BLOCK + lax.broadcasted_iota(jnp.int32, (LANES, LANES), 1)
    key0 = lax.broadcasted_iota(jnp.int32, (LANES, LANES), 0)

    qs = jnp.concatenate([qi_ref[:, p * LANES:(p + 1) * LANES] for p in range(n_pairs)], axis=0)
    wrows = [wst_ref[h:h + 1, :] for h in range(IDX_HEADS)]
    cmax = jnp.full((LANES, LANES), -jnp.inf, F32)
    cmin = jnp.full((LANES, LANES), jnp.inf, F32)
    for kc in range(0, nk, key_chunk):
        le = lax.dot_general(kie_ref[kc:kc + key_chunk, :], qs, _NT, preferred_element_type=F32)
        lo = lax.dot_general(kio_ref[kc:kc + key_chunk, :], qs, _NT, preferred_element_type=F32)
        for r in range(key_chunk // LANES):
            rs = slice(r * LANES, (r + 1) * LANES)
            acc = jnp.zeros((LANES, LANES), F32)
            for p in range(n_pairs):
                cs = slice(p * LANES, (p + 1) * LANES)
                acc = acc + jnp.maximum(le[rs, cs], 0.0) * wrows[2 * p]
                acc = acc + jnp.maximum(lo[rs, cs], 0.0) * wrows[2 * p + 1]
            causal = key0 + (kc + r * LANES) <= qpos
            cmax = jnp.maximum(cmax, jnp.where(causal, acc, -jnp.inf))
            cmin = jnp.minimum(cmin, jnp.where(causal, acc, jnp.inf))
            sc_ref[kc + r * LANES:kc + (r + 1) * LANES, :] = jnp.where(causal, acc, -jnp.inf)

    col_max = jnp.max(cmax, axis=0, keepdims=True)
    col_min = jnp.min(cmin, axis=0, keepdims=True)
    kp = jnp.minimum(qpos[0:1, :] + 1, top_k).astype(F32)
    _select_bias_cols(sc_ref, kp, col_min, col_max, n_rows=nk)

    bias = jnp.concatenate([sc_ref[0:nk, :]] * grp, axis=1)
    for kh in range(N_KV_HEADS):
        qh = jnp.concatenate(
            [q_ref[:, (kh * grp + g) * HEAD_DIM:(kh * grp + g + 1) * HEAD_DIM] for g in range(grp)], axis=0)
        ks = slice(kh * HEAD_DIM, (kh + 1) * HEAD_DIM)
        s = lax.dot_general(kb_ref[0:nk, ks], qh, _NT, preferred_element_type=F32) + bias
        m = jnp.max(s, axis=0, keepdims=True)
        p = jnp.exp2((s - m) * (HEAD_DIM ** -0.5 * LOG2_E))
        l = jnp.sum(p, axis=0, keepdims=True)
        ot = jnp.dot(vt_ref[ks, 0:nk], p.astype(BF16), preferred_element_type=F32) / l
        for g in range(grp):
            hs = slice((kh * grp + g) * HEAD_DIM, (kh * grp + g + 1) * HEAD_DIM)
            o = ot[:, g * LANES:(g + 1) * LANES].T
            o_ref[:, hs] = (o * sz_ref[:, hs].astype(F32)).astype(BF16)


def _prompt_attn_body(*refs, seq, top_k, n_buckets, key_chunk):
    qb = pl.program_id(1)
    span = seq // n_buckets
    per = span // Q_BLOCK
    for c in range(n_buckets):
        @pl.when(qb // per == c)
        def _(c=c):
            _prompt_attn_block((c + 1) * span, *refs, top_k=top_k, key_chunk=key_chunk)


def _prompt_attn(h_main, ws_t, kie, kio, kb, vb_t, *, batch, seq):
    top_k = min(TOPK_MAX, seq // 4)
    n_qb = seq // Q_BLOCK
    key_chunk = min(512, seq)
    n_buckets = max(1, seq // 512)
    body = functools.partial(_prompt_attn_body, seq=seq, top_k=top_k, n_buckets=n_buckets, key_chunk=key_chunk)
    col = lambda t: pl.BlockSpec((Q_BLOCK, COL_TILE), lambda b, q, t=t: (b * n_qb + q, t))
    seqblk = lambda n: pl.BlockSpec((seq, n), lambda b, q: (b, 0))
    return pl.pallas_call(
        body,
        grid=(batch, n_qb),
        in_specs=[col(T_QI),
                  pl.BlockSpec((IDX_HEADS, Q_BLOCK), lambda b, q: (0, b * n_qb + q)),
                  seqblk(LANES), seqblk(LANES),
                  col(T_Q), seqblk(KV_WIDTH),
                  pl.BlockSpec((KV_WIDTH, seq), lambda b, q: (0, b)),
                  col(T_ZB)],
        out_specs=pl.BlockSpec((Q_BLOCK, B_WIDTH), lambda b, q: (b * n_qb + q, 0)),
        out_shape=jax.ShapeDtypeStruct((batch * seq, B_WIDTH), BF16),
        scratch_shapes=[pltpu.VMEM((seq, Q_BLOCK), F32)],
        compiler_params=_cparams(("arbitrary", "arbitrary")),
        name="attn_prompt",
    )(h_main, ws_t, kie, kio, h_main, kb, vb_t, h_main)


def _merge_body(x_ref, *refs, fused_gate):
    if fused_gate:
        gate_refs, refs = refs[:5], refs[5:]
        ya_ref = refs[-1]
        _gate_body(*gate_refs, ya_ref, n_chunks=x_ref.shape[0] // CHUNK)
        ob_ref, ga_ref, gb_ref, woa_ref, wob_ref, wout_ref, pg_ref, o_ref = refs[:-1]
    else:
        ya_ref, ob_ref, ga_ref, gb_ref, woa_ref, wob_ref, wout_ref, pg_ref, o_ref = refs
    pa = jnp.dot(ya_ref[...], woa_ref[...], preferred_element_type=F32)
    pb = jnp.dot(ob_ref[...], wob_ref[...], preferred_element_type=F32)
    mix = ga_ref[...].astype(F32) * pa + gb_ref[...].astype(F32) * pb
    r = jnp.dot(mix.astype(BF16), wout_ref[...], preferred_element_type=F32)
    o_ref[...] = x_ref[...] + _rmsnorm_rows(r, pg_ref[...])


def _merge(x2, ya, ob, h_main, w_oa, w_ob, w_out, post_g, *, tm, gate=None):
    m = x2.shape[0]
    const = lambda r, c: pl.BlockSpec((r, c), lambda i: (0, 0), pipeline_mode=pl.Buffered(1))
    if gate is None:
        a_specs = [pl.BlockSpec((tm, A_WIDTH), lambda i: (i, 0))]
        a_args, scratch = [ya], []
    else:
        col = lambda t: pl.BlockSpec((tm, COL_TILE), lambda i, t=t: (i, t))
        a_specs = [col(T_UA), col(T_VA), col(T_ZA),
                   pl.BlockSpec((A_GROUPS, CHUNK, CHUNK), lambda i: (0, 0, 0)),
                   pl.BlockSpec((CHUNK, A_GROUPS), lambda i: (0, 0))]
        a_args, scratch = [h_main, h_main, h_main, *gate], [pltpu.VMEM((tm, A_WIDTH), BF16)]
    return pl.pallas_call(
        functools.partial(_merge_body, fused_gate=gate is not None),
        grid=(m // tm,),
        in_specs=[pl.BlockSpec((tm, D_MODEL), lambda i: (i, 0))] + a_specs + [
            pl.BlockSpec((tm, B_WIDTH), lambda i: (i, 0)),
            pl.BlockSpec((tm, D_MODEL), lambda i: (i, T_GA // 2)),
            pl.BlockSpec((tm, D_MODEL), lambda i: (i, T_GB // 2)),
            const(A_WIDTH, D_MODEL), const(B_WIDTH, D_MODEL), const(D_MODEL, D_MODEL),
            const(1, D_MODEL),
        ],
        out_specs=pl.BlockSpec((tm, D_MODEL), lambda i: (i, 0)),
        out_shape=jax.ShapeDtypeStruct((m, D_MODEL), F32),
        scratch_shapes=scratch,
        compiler_params=_cparams(("arbitrary",)),
        name="merge",
    )(x2, *a_args, ob, h_main, h_main, w_oa, w_ob, w_out, post_g)


SCORE_CHUNK = 2048
COMPACT_SEQS = 4
GATHER_BUFS = 3


def _sample_scores_body(pt_ref, q_ref, w_ref, kidx_hbm, o_ref, buf, sem):
    db, n_pages = pt_ref.shape
    past = n_pages * PAGE_SIZE

    def page_copy(b, p, slot):
        dst = buf.at[slot, :, pl.ds(pl.multiple_of(p * PAGE_SIZE, PAGE_SIZE), PAGE_SIZE)]
        return pltpu.make_async_copy(kidx_hbm.at[pt_ref[b, p]], dst, sem.at[slot])

    def start_all(b, slot):
        def f(p, c):
            page_copy(b, p, slot).start()
            return c
        lax.fori_loop(0, n_pages, f, 0, unroll=8)

    def wait_all(slot):
        for p in range(n_pages):
            dst = buf.at[slot, :, pl.ds(p * PAGE_SIZE, PAGE_SIZE)]
            pltpu.make_async_copy(kidx_hbm.at[0], dst, sem.at[slot]).wait()

    start_all(0, 0)

    def per_seq(b, c):
        slot = b % 2

        @pl.when(b + 1 < db)
        def _():
            start_all(b + 1, 1 - slot)

        wait_all(slot)
        q = q_ref[b]
        w = w_ref[b]
        for ch in range(past // SCORE_CHUNK):
            cs = slice(ch * SCORE_CHUNK, (ch + 1) * SCORE_CHUNK)
            logit = jnp.dot(q, buf[slot, :, cs].astype(BF16), preferred_element_type=F32)
            o_ref[pl.ds(b, 1), cs] = jnp.sum(jnp.maximum(logit, 0.0) * w, axis=0, keepdims=True)
        return c

    lax.fori_loop(0, db, per_seq, 0)


def _sample_scores(page_table, qi3, ws3, kidx_pages_t):
    db, n_pages = page_table.shape
    past = n_pages * PAGE_SIZE
    grid_spec = pltpu.PrefetchScalarGridSpec(
        num_scalar_prefetch=1,
        grid=(1,),
        in_specs=[pl.BlockSpec((db, IDX_HEADS, IDX_DIM), lambda i, pt: (0, 0, 0)),
                  pl.BlockSpec((db, IDX_HEADS, 1), lambda i, pt: (0, 0, 0)),
                  pl.BlockSpec(memory_space=pl.ANY)],
        out_specs=pl.BlockSpec((db, past), lambda i, pt: (0, 0)),
        scratch_shapes=[pltpu.VMEM((2, IDX_DIM, past), F32), pltpu.SemaphoreType.DMA((2,))],
    )
    return pl.pallas_call(
        _sample_scores_body,
        grid_spec=grid_spec,
        out_shape=jax.ShapeDtypeStruct((db, past), F32),
        compiler_params=_cparams(("arbitrary",)),
        name="scores_sample",
    )(page_table, qi3, ws3, kidx_pages_t)


def _sample_select_body(sc_ref, qi_ref, kie_ref, ws_ref, bias_ref, bnew_ref, *, top_k):
    rows, past = sc_ref.shape
    lane = lax.broadcasted_iota(jnp.int32, (rows, LANES), 1)
    ki = kie_ref[...].astype(F32)
    ki = ki + pltpu.roll(ki, IDX_DIM, 1)
    s_new = jnp.zeros((rows, 1), F32)
    for p in range(IDX_HEADS // 2):
        prod = qi_ref[:, p * LANES:(p + 1) * LANES].astype(F32) * ki
        l_even = jnp.sum(jnp.where(lane < IDX_DIM, prod, 0.0), axis=1, keepdims=True)
        l_odd = jnp.sum(jnp.where(lane >= IDX_DIM, prod, 0.0), axis=1, keepdims=True)
        s_new = s_new + jnp.maximum(l_even, 0.0) * ws_ref[:, 2 * p:2 * p + 1]
        s_new = s_new + jnp.maximum(l_odd, 0.0) * ws_ref[:, 2 * p + 1:2 * p + 2]
    extra = _lane_bcast(s_new, rows)
    bnew_ref[...] = extra
    rmax = extra
    rmin = extra
    for c in range(past // LANES):
        sl = slice(c * LANES, (c + 1) * LANES)
        s = sc_ref[:, sl]
        bias_ref[:, sl] = s
        rmax = jnp.maximum(rmax, s)
        rmin = jnp.minimum(rmin, s)
    row_max = _lane_bcast(jnp.max(rmax, axis=1, keepdims=True), rows)
    row_min = _lane_bcast(jnp.min(rmin, axis=1, keepdims=True), rows)
    kp = jnp.full((rows, LANES), float(top_k), F32)
    _select_bias(bias_ref, bnew_ref, kp, row_min, row_max, n_cols=past)


def _sample_select(scores, h_main, kie, ws, *, top_k):
    db, past = scores.shape
    full = lambda r, c: pl.BlockSpec((r, c), lambda i: (0, 0))
    return pl.pallas_call(
        functools.partial(_sample_select_body, top_k=top_k),
        grid=(1,),
        in_specs=[full(db, past),
                  pl.BlockSpec((db, COL_TILE), lambda i: (0, T_QI)),
                  full(db, LANES), full(db, IDX_HEADS)],
        out_specs=[full(db, past), full(db, LANES)],
        out_shape=[jax.ShapeDtypeStruct((db, past), F32), jax.ShapeDtypeStruct((db, LANES), F32)],
        compiler_params=_cparams(("arbitrary",)),
        name="select_sample",
    )(scores, h_main, kie, ws)


def _sample_compact_body(m_ref, mt_ref, pt_ref, idx_ref, row_ref, *, n_slots):
    n_pages = m_ref.shape[1]
    for q in range(m_ref.shape[0]):
        _compact_one(m_ref.at[q], mt_ref.at[q], pt_ref.at[q], idx_ref.at[q], row_ref.at[q],
                     n_pages=n_pages, n_slots=n_slots)


def _compact_one(m_ref, mt_ref, pt_ref, idx_ref, row_ref, *, n_pages, n_slots):
    pt = jnp.broadcast_to(pt_ref[...], (8, n_pages))
    pt_hi = (pt // PAGE_SIZE).astype(F32).astype(BF16)
    pt_lo = (pt % PAGE_SIZE).astype(F32).astype(BF16)
    one = lambda pred: jnp.where(pred, 1.0, 0.0)
    kept = m_ref[...] == 0.0
    kept_t = mt_ref[...] == 0.0
    ri = lax.broadcasted_iota(jnp.int32, (PAGE_SIZE, PAGE_SIZE), 0)
    ci = lax.broadcasted_iota(jnp.int32, (PAGE_SIZE, PAGE_SIZE), 1)
    rp = lax.broadcasted_iota(jnp.int32, (n_pages, n_pages), 0)
    cp = lax.broadcasted_iota(jnp.int32, (n_pages, n_pages), 1)
    plt = jnp.dot(one(ci <= ri).astype(BF16), one(kept_t).astype(BF16), preferred_element_type=F32)
    n_row = plt[PAGE_SIZE - 1:PAGE_SIZE, :]
    n_col = _lane_bcast(jnp.sum(one(kept), axis=1, keepdims=True), n_pages)
    e_col = jnp.dot(one(cp <= rp).astype(BF16), n_col.astype(BF16), preferred_element_type=F32)
    n_row8 = jnp.broadcast_to(n_row, (8, n_pages))
    e_row8 = jnp.dot(n_row8.astype(BF16), one(rp <= cp).astype(BF16), preferred_element_type=F32)
    off_row8 = e_row8 - n_row8
    n_total = e_col[n_pages - 1:n_pages, :]
    page_id = lax.broadcasted_iota(jnp.int32, (n_pages, LANES), 0).astype(F32)
    for jt in range(n_slots // LANES):
        j = (lax.broadcasted_iota(jnp.int32, (1, LANES), 1) + jt * LANES).astype(F32)
        page_j = jnp.sum(one(e_col <= j), axis=0, keepdims=True)
        pick = one(page_id == page_j).astype(BF16)
        prefix_j = jnp.dot(plt.astype(BF16), pick, preferred_element_type=F32)
        off_j = jnp.dot(off_row8.astype(BF16), pick, preferred_element_type=F32)[0:1]
        local_j = jnp.sum(one(prefix_j <= j - off_j), axis=0, keepdims=True)
        pos = page_j * float(PAGE_SIZE) + local_j
        phys = (jnp.dot(pt_hi, pick, preferred_element_type=F32)[0:1] * float(PAGE_SIZE)
                + jnp.dot(pt_lo, pick, preferred_element_type=F32)[0:1])
        row = phys * float(PAGE_SIZE) + local_j
        used = j < n_total
        sl = slice(jt * LANES, (jt + 1) * LANES)
        idx_ref[:, sl] = jnp.where(used, pos, -1.0).astype(jnp.int32)
        row_ref[:, sl] = jnp.where(used, row, 0.0).astype(jnp.int32)


def _sample_compact(bias3, bias3_t, pt3, *, n_slots):
    db, n_pages, _ = bias3.shape
    per = COMPACT_SEQS if db % COMPACT_SEQS == 0 else 1
    out = pl.BlockSpec((per, 1, n_slots), lambda b: (b, 0, 0))
    return pl.pallas_call(
        functools.partial(_sample_compact_body, n_slots=n_slots),
        grid=(db // per,),
        in_specs=[pl.BlockSpec((per, n_pages, PAGE_SIZE), lambda b: (b, 0, 0)),
                  pl.BlockSpec((per, PAGE_SIZE, n_pages), lambda b: (b, 0, 0)),
                  pl.BlockSpec((per, 1, n_pages), lambda b: (b, 0, 0))],
        out_specs=[out, out],
        out_shape=[jax.ShapeDtypeStruct((db, 1, n_slots), jnp.int32)] * 2,
        compiler_params=_cparams(("arbitrary",)),
        name="compact_sample",
    )(bias3, bias3_t, pt3)


def _sample_attn_body(row_ref, q_ref, slot_ref, bnew_ref, kn_ref, vn_ref, sz_ref, k_hbm, v_hbm, o_ref, *scratch):
    kbufs, vbufs, sem = scratch[:GATHER_BUFS], scratch[GATHER_BUFS:2 * GATHER_BUFS], scratch[-1]
    b = pl.program_id(0)
    nb = pl.num_programs(0)
    n_slots = kbufs[0].shape[0]
    ahead = GATHER_BUFS - 1
    grp = N_HEADS // N_KV_HEADS
    scale = HEAD_DIM ** -0.5

    def row_copies(seq, j, to):
        r = row_ref[seq, j]
        return (pltpu.make_async_copy(k_hbm.at[r], kbufs[to].at[j], sem.at[0, to]),
                pltpu.make_async_copy(v_hbm.at[r], vbufs[to].at[j], sem.at[1, to]))

    def start_all(seq, to, inline):
        def f(j, c):
            ck, cv = row_copies(seq, j, to)
            ck.start()
            cv.start()
            return c
        if inline:
            for j in range(n_slots):
                f(j, 0)
        else:
            lax.fori_loop(0, n_slots, f, 0, unroll=8)

    def wait_all(to):
        for blk in range(n_slots // PAGE_SIZE):
            rows = pl.ds(blk * PAGE_SIZE, PAGE_SIZE)
            src = pl.ds(0, PAGE_SIZE)
            pltpu.make_async_copy(k_hbm.at[src], kbufs[to].at[rows], sem.at[0, to]).wait()
            pltpu.make_async_copy(v_hbm.at[src], vbufs[to].at[rows], sem.at[1, to]).wait()

    def attend(buf):
        q = q_ref[0]
        head_s = lax.broadcasted_iota(jnp.int32, (N_HEADS, n_slots), 0)
        head_o = lax.broadcasted_iota(jnp.int32, (N_HEADS, HEAD_DIM), 0)
        k0, k1 = (kbufs[buf][:, kh, :].astype(BF16) for kh in range(N_KV_HEADS))
        v0, v1 = (vbufs[buf][:, kh, :].astype(BF16) for kh in range(N_KV_HEADS))
        s0 = lax.dot_general(q, k0, _NT, preferred_element_type=F32)
        s1 = lax.dot_general(q, k1, _NT, preferred_element_type=F32)
        s = jnp.where(head_s < grp, s0, s1) * scale + jnp.where(slot_ref[0] >= 0, 0.0, NEG)
        kn = kn_ref[0].astype(F32)
        vn = vn_ref[0].astype(F32)
        kn8 = jnp.where(head_o < grp, kn[:, :HEAD_DIM], kn[:, HEAD_DIM:])
        vn8 = jnp.where(head_o < grp, vn[:, :HEAD_DIM], vn[:, HEAD_DIM:])
        s_new = _lane_bcast(jnp.sum(q.astype(F32) * kn8, axis=-1, keepdims=True), N_HEADS) * scale + bnew_ref[0]
        m = jnp.maximum(_lane_bcast(jnp.max(s, axis=-1, keepdims=True), N_HEADS), s_new)
        p = jnp.exp(s - m[:, :1])
        p_new = jnp.exp(s_new - m)
        l = _lane_bcast(jnp.sum(p, axis=-1, keepdims=True), N_HEADS) + p_new
        pb = p.astype(BF16)
        pv = jnp.where(head_o < grp,
                       jnp.dot(pb, v0, preferred_element_type=F32),
                       jnp.dot(pb, v1, preferred_element_type=F32))
        o = (pv + p_new * vn8) / l
        o_ref[0] = (o * sz_ref[0].astype(F32)).astype(BF16)

    @pl.when(b == 0)
    def _():
        for first in range(ahead):
            @pl.when(first < nb)
            def _(first=first):
                start_all(first, first, inline=False)

    for buf in range(GATHER_BUFS):
        mine = b % GATHER_BUFS == buf

        @pl.when(mine & (b + ahead < nb))
        def _(buf=buf):
            wait_all(buf)
            start_all(b + ahead, (buf + ahead) % GATHER_BUFS, inline=True)
            attend(buf)

        @pl.when(mine & (b + ahead >= nb))
        def _(buf=buf):
            wait_all(buf)
            attend(buf)


def _sample_attn(rows, q3, slots3, bnew3, kn3, vn3, sz3, k_rows, v_rows):
    db, n_slots = rows.shape
    per_b = lambda r, c: pl.BlockSpec((1, r, c), lambda b, rw: (b, 0, 0))
    grid_spec = pltpu.PrefetchScalarGridSpec(
        num_scalar_prefetch=1,
        grid=(db,),
        in_specs=[per_b(N_HEADS, HEAD_DIM), per_b(1, n_slots), per_b(1, LANES),
                  per_b(1, KV_WIDTH), per_b(1, KV_WIDTH), per_b(N_HEADS, HEAD_DIM),
                  pl.BlockSpec(memory_space=pl.ANY), pl.BlockSpec(memory_space=pl.ANY)],
        out_specs=per_b(N_HEADS, HEAD_DIM),
        scratch_shapes=[pltpu.VMEM((n_slots, N_KV_HEADS, HEAD_DIM), F32)] * (2 * GATHER_BUFS)
                       + [pltpu.SemaphoreType.DMA((2, GATHER_BUFS))],
    )
    return pl.pallas_call(
        _sample_attn_body,
        grid_spec=grid_spec,
        out_shape=jax.ShapeDtypeStruct((db, N_HEADS, HEAD_DIM), BF16),
        compiler_params=_cparams(("arbitrary",)),
        name="attn_sample",
    )(rows, q3, slots3, bnew3, kn3, vn3, sz3, k_rows, v_rows)


def _col_tile(h_main, t, n=1):
    return h_main[:, t * COL_TILE:(t + n) * COL_TILE]


def kernel(x_prompt, x_sample, cache_k, cache_v, cache_kidx, page_table, pre_g, w_in, a_ln_g, a_ln_b,
           a_ws, a_bs, w_oa, w_ob, w_out, post_g):
    batch, seq, _ = x_prompt.shape
    db, ds, _ = x_sample.shape
    depth = w_in.shape[0]
    n_pages = page_table.shape[1]
    past = n_pages * PAGE_SIZE
    assert ds == 1 and seq % 512 == 0 and past % SCORE_CHUNK == 0
    top_k_s = min(TOPK_MAX, (past + ds) // 4)
    assert top_k_s <= past and top_k_s % LANES == 0

    pos_p = jnp.arange(seq)
    pos_s = past + (jnp.arange(db * ds) % ds)
    tabs_p = _rope_tables(pos_p, HEAD_DIM) + _rope_tables(pos_p, IDX_DIM)
    tabs_s = _rope_tables(pos_s, HEAD_DIM) + _rope_tables(pos_s, IDX_DIM)

    hp = x_prompt.reshape(batch * seq, D_MODEL)
    hs = x_sample.reshape(db * ds, D_MODEL)
    outs = [[] for _ in range(8)]
    for l in range(depth):
        w_t = w_in[l].T
        woa, wob, wout = w_oa[l].astype(BF16), w_ob[l].astype(BF16), w_out[l].astype(BF16)
        g_pre, g_post = pre_g[l][None], post_g[l][None]
        ln_g, ln_b = a_ln_g[l][None], a_ln_b[l][None]

        xn, k, v, ki, kb, _, kie, kio, _, vb_t, ws_t = _proj_tail(hp, g_pre, w_t, *tabs_p, tm=512, seq_rows=seq,
                                                                  with_transposed=True)
        h_main, gv, w_tiles = _proj_main(xn, w_t, *tabs_p, ln_g, ln_b, tm=512, seq_rows=seq, gv_rows=CHUNK,
                                         cast_weights=True)
        ob = _prompt_attn(h_main, ws_t, kie, kio, kb, vb_t, batch=batch, seq=seq)
        hp = _merge(hp, None, ob, h_main, woa, wob, wout, g_post, tm=256, gate=(a_ws[l], a_bs[l].T))
        outs[0].append(k.reshape(batch, seq, N_KV_HEADS, HEAD_DIM))
        outs[1].append(v.reshape(batch, seq, N_KV_HEADS, HEAD_DIM))
        outs[2].append(ki.reshape(batch, seq, IDX_DIM))
        outs[3].append(gv.reshape(batch, CHUNK, A_WIDTH))

        m_s = db * ds
        xn, k, v, ki, kb, vb, kie, kio, ws = _proj_tail(hs, g_pre, w_t, *tabs_s, tm=m_s, seq_rows=m_s,
                                                        with_transposed=False)
        h_main, gv = _proj_main(xn, w_tiles, *tabs_s, ln_g, ln_b, tm=m_s, seq_rows=m_s, gv_rows=m_s,
                                cast_weights=False)
        w0 = jnp.repeat(a_ws[l][:, 0, 0], LANES)[None]
        b0 = jnp.repeat(a_bs[l][:, 0], LANES)[None]
        ya = _gate_row(h_main, gv, w0, b0)
        qi3 = _col_tile(h_main, T_QI).reshape(db, IDX_HEADS, IDX_DIM)
        kidx_t = jnp.swapaxes(cache_kidx[l], 1, 2)
        scores = _sample_scores(page_table, qi3, ws.reshape(db, IDX_HEADS, 1), kidx_t)
        bias, bnew = _sample_select(scores, h_main, kie, ws, top_k=top_k_s)
        bias3 = bias.reshape(db, n_pages, PAGE_SIZE)
        slots3, rows3 = _sample_compact(bias3, jnp.swapaxes(bias3, 1, 2),
                                        page_table.reshape(db, 1, n_pages), n_slots=top_k_s)
        pool_rows = lambda c: c.reshape(-1, N_KV_HEADS, HEAD_DIM)
        ob = _sample_attn(rows3.reshape(db, top_k_s),
                          _col_tile(h_main, T_Q).reshape(db, N_HEADS, HEAD_DIM),
                          slots3, bnew.reshape(db, 1, LANES),
                          kb.reshape(db, 1, KV_WIDTH), vb.reshape(db, 1, KV_WIDTH),
                          _col_tile(h_main, T_ZB).reshape(db, N_HEADS, HEAD_DIM),
                          pool_rows(cache_k[l]), pool_rows(cache_v[l])).reshape(db, B_WIDTH)
        hs = _merge(hs, ya, ob, h_main, woa, wob, wout, g_post, tm=m_s)
        outs[4].append(k.reshape(db, ds, N_KV_HEADS, HEAD_DIM))
        outs[5].append(v.reshape(db, ds, N_KV_HEADS, HEAD_DIM))
        outs[6].append(ki.reshape(db, ds, IDX_DIM))
        outs[7].append(gv.reshape(db, ds, A_WIDTH))

    st = [jnp.stack(o, axis=0) for o in outs]
    return (hp.reshape(batch, seq, D_MODEL), hs.reshape(db, ds, D_MODEL),
            st[0], st[1], st[2], st[3], st[4], st[5], st[6], st[7])
```

```python
import functools

import jax
import jax.numpy as jnp
from jax import lax
from jax.experimental import pallas as pl
from jax.experimental.pallas import tpu as pltpu

F32 = jnp.float32
BF16 = jnp.bfloat16

D_MODEL = 2048
CHUNK = 128
A_GROUPS = 8
A_WIDTH = 1024
N_HEADS = 8
N_KV_HEADS = 2
HEAD_DIM = 128
B_WIDTH = 1024
KV_WIDTH = 256
IDX_HEADS = 16
IDX_DIM = 64
TOPK_MAX = 256
ROPE_THETA = 10000.0
EPS = 1e-6
PAGE_SIZE = 128
Q_BLOCK = 128
NEG = -1e30
LOG2_E = 1.4426950408889634

LANES = 128
BF16_SUBLANES = 16
MXU_WIDTH = 256
PROJ_ROWS = 512
MERGE_ROWS = 256
NORM_STREAMS = 4
_NT = (((1,), (1,)), ((), ()))
COL_TILE = 1024
T_UA, T_VA, T_ZA, T_Q, T_ZB, T_QI, T_GA, T_GB = 0, 1, 2, 3, 4, 5, 6, 8
N_MAIN_TILES = 10
ROW_K = 4 * COL_TILE
ROW_ZB = ROW_K + 2 * KV_WIDTH
ROW_KI = ROW_ZB + B_WIDTH + IDX_HEADS * IDX_DIM
ROW_GA = ROW_KI + IDX_DIM + IDX_HEADS
VMEM_LIMIT = 56 * 1024 * 1024


def _cparams(sem):
    return pltpu.CompilerParams(dimension_semantics=sem, vmem_limit_bytes=VMEM_LIMIT)


def _rope_tables(pos, dim):
    half = dim // 2
    inv = ROPE_THETA ** (-jnp.arange(half, dtype=F32) / half)
    ang = pos.astype(F32)[:, None] * inv[None, :]
    cos = jnp.cos(ang)
    sin = jnp.sin(ang)
    reps = LANES // dim
    cos_t = jnp.tile(jnp.concatenate([cos, cos], axis=-1), (1, reps))
    sin_t = jnp.tile(jnp.concatenate([-sin, sin], axis=-1), (1, reps))
    return cos_t, sin_t


def _rope128(x, cos, sin):
    return x * cos + pltpu.roll(x, 64, 1) * sin


def _rope64(x, cos, sin):
    lane = lax.broadcasted_iota(jnp.int32, x.shape, 1)
    first = (lane % IDX_DIM) < (IDX_DIM // 2)
    partner = jnp.where(first, pltpu.roll(x, LANES - 32, 1), pltpu.roll(x, 32, 1))
    return x * cos + partner * sin


def _rmsnorm_rows(xf, g):
    ms = jnp.mean(xf * xf, axis=-1, keepdims=True)
    return xf * lax.rsqrt(ms + EPS) * g


_TILE_KINDS = ("copy", "ln", "silu", "rope128", "silu", "rope64", "sigmoid", "sigmoid", "sigmoid", "sigmoid")


def _sigmoid(x):
    return 0.5 * jnp.tanh(0.5 * x) + 0.5


def _prenorm_rows(x_refs, g_ref, o_ref):
    xs = [r[...] for r in x_refs]
    ms = sum(jnp.sum(x * x, axis=-1, keepdims=True) for x in xs) * (1.0 / D_MODEL)
    scale = lax.rsqrt(ms + EPS)
    w = xs[0].shape[1]
    for c, x in enumerate(xs):
        cs = slice(c * w, (c + 1) * w)
        o_ref[:, cs] = (x * scale * g_ref[:, cs]).astype(BF16)


def _proj_main_body(xn_ref, *refs, gv_rows, n_row_tiles, cast_weights):
    n_chunks = COL_TILE // MXU_WIDTH
    w_refs, refs = refs[:n_chunks], refs[n_chunks:]
    cq_ref, sq_ref, ci_ref, si_ref, lng_ref, lnb_ref, h_ref, gv_ref = refs[:8]
    wbf_ref = refs[8] if cast_weights else None
    acc_ref = refs[-1]
    j = pl.program_id(0)
    i = pl.program_id(1)
    tm = xn_ref.shape[0]
    chunks = [slice(c * MXU_WIDTH, (c + 1) * MXU_WIDTH) for c in range(n_chunks)]
    weights = lambda c: wbf_ref[chunks[c], :] if cast_weights else w_refs[c][...]

    def finish_chunk(kind, cs, stats):
        acc = acc_ref[:, cs]
        if kind == "copy":
            h_ref[:, cs] = acc.astype(BF16)
        elif kind == "ln":
            mu, rstd = stats
            vn = (acc - mu) * rstd * lng_ref[:, cs] + lnb_ref[:, cs]
            h_ref[:, cs] = vn.astype(BF16)
            gv_ref[:, cs] = vn[tm - gv_rows:, :]
        elif kind == "silu":
            h_ref[:, cs] = (acc * _sigmoid(acc)).astype(BF16)
        elif kind == "sigmoid":
            h_ref[:, cs] = _sigmoid(acc).astype(BF16)
        else:
            rope, cos, sin = ((_rope128, cq_ref[...], sq_ref[...]) if kind == "rope128"
                              else (_rope64, ci_ref[...], si_ref[...]))
            for h in range(MXU_WIDTH // LANES):
                sl = slice(cs.start + h * LANES, cs.start + (h + 1) * LANES)
                h_ref[:, sl] = rope(acc[:, h * LANES:(h + 1) * LANES], cos, sin).astype(BF16)

    def run(kind, finish, matmul):
        stats = None
        if finish and kind == "ln":
            acc = acc_ref[...]
            mu = jnp.mean(acc, axis=-1, keepdims=True)
            d = acc - mu
            stats = (mu, lax.rsqrt(jnp.mean(d * d, axis=-1, keepdims=True) + EPS))
        for c, cs in enumerate(chunks):
            if finish:
                finish_chunk(kind, cs, stats)
            if matmul:
                acc_ref[:, cs] = lax.dot_general(xn_ref[...], weights(c), _NT, preferred_element_type=F32)

    @pl.when(i == 0)
    def _():
        if cast_weights:
            for c, cs in enumerate(chunks):
                wbf_ref[cs, :] = w_refs[c][...].astype(BF16)
        run(None, False, True)

    @pl.when((i == 0) & (j != T_VA))
    def _():
        gv_ref[...] = jnp.zeros(gv_ref.shape, F32)

    for kind in sorted(set(_TILE_KINDS)):
        is_kind = functools.reduce(jnp.logical_or, [j == t for t, k in enumerate(_TILE_KINDS) if k == kind])

        @pl.when(is_kind & (i > 0) & (i < n_row_tiles))
        def _(kind=kind):
            run(kind, True, True)

        @pl.when(is_kind & (i == n_row_tiles))
        def _(kind=kind):
            run(kind, True, False)


def _main_tile_row(j, chunk):
    g = BF16_SUBLANES
    skip_kv = (ROW_ZB - T_ZB * COL_TILE) // g
    skip_idx = (ROW_GA - ROW_ZB - (T_GA - T_ZB) * COL_TILE) // g
    return (j * (COL_TILE // g) + chunk * (MXU_WIDTH // g)
            + jnp.where(j >= T_ZB, skip_kv, 0) + jnp.where(j >= T_GA, skip_idx, 0)) * g


def _proj_main(xn, w, cq, sq, ci, si, ln_g, ln_b, *, tm, seq_rows, gv_rows, cast_weights):
    m = xn.shape[0]
    tiles_per_seq = seq_rows // tm
    n_seq = m // seq_rows
    n_row_tiles = m // tm
    n_chunks = COL_TILE // MXU_WIDTH
    body = functools.partial(_proj_main_body, gv_rows=gv_rows, n_row_tiles=n_row_tiles,
                             cast_weights=cast_weights)
    w_tile = lambda j, i: jnp.minimum(jnp.where(i == n_row_tiles, j + 1, j), N_MAIN_TILES - 1)
    if cast_weights:
        w_chunk = lambda c: pl.BlockSpec((pl.Element(MXU_WIDTH), pl.Element(D_MODEL)),
                                         lambda j, i: (_main_tile_row(w_tile(j, i), c), 0))
    else:
        w_chunk = lambda c: pl.BlockSpec((MXU_WIDTH, D_MODEL), lambda j, i: (w_tile(j, i) * n_chunks + c, 0))
    w_out_spec = [pl.BlockSpec((COL_TILE, D_MODEL), lambda j, i: (j, 0))] if cast_weights else []
    w_out_shape = [jax.ShapeDtypeStruct((N_MAIN_TILES * COL_TILE, D_MODEL), BF16)] if cast_weights else []
    prev = lambda i: jnp.maximum(i - 1, 0)
    tab = pl.BlockSpec((tm, LANES), lambda j, i: (prev(i) % tiles_per_seq, 0))
    row = lambda n: pl.BlockSpec((1, n), lambda j, i: (0, 0))
    gv_block = lambda j, i: (jnp.where(j == T_VA, prev(i) // tiles_per_seq, n_seq + (j > T_VA)), 0)
    h_main, gv, *w_bf = pl.pallas_call(
        body,
        grid=(N_MAIN_TILES, n_row_tiles + 1),
        in_specs=[pl.BlockSpec((tm, D_MODEL), lambda j, i: (jnp.minimum(i, n_row_tiles - 1), 0))]
                 + [w_chunk(c) for c in range(n_chunks)]
                 + [tab, tab, tab, tab, row(A_WIDTH), row(A_WIDTH)],
        out_specs=[
            pl.BlockSpec((tm, COL_TILE), lambda j, i: (prev(i), j)),
            pl.BlockSpec((gv_rows, A_WIDTH), gv_block),
        ] + w_out_spec,
        out_shape=[
            jax.ShapeDtypeStruct((m, N_MAIN_TILES * COL_TILE), BF16),
            jax.ShapeDtypeStruct(((n_seq + 2) * gv_rows, A_WIDTH), F32),
        ] + w_out_shape,
        scratch_shapes=[pltpu.VMEM((tm, COL_TILE), F32)],
        compiler_params=_cparams(("arbitrary", "arbitrary")),
        name="proj_main",
    )(xn, *([w] * n_chunks), cq, sq, ci, si, ln_g, ln_b)
    return (h_main, gv[:n_seq * gv_rows], *w_bf)


def _proj_tail_body(*refs, transposed):
    x_refs, refs = refs[:NORM_STREAMS], refs[NORM_STREAMS:]
    (g_ref, wkv_ref, wix_ref, ck_ref, sk_ref, ci_ref, si_ref,
     xn_ref, k_ref, v_ref, ki_ref, kb_ref, kie_ref, kio_ref, vb_ref, ws_ref) = refs
    _prenorm_rows(x_refs, g_ref, xn_ref)
    xn = xn_ref[...]
    acc = lax.dot_general(xn, wkv_ref[...].astype(BF16), _NT, preferred_element_type=F32)
    cos = ck_ref[...]
    sin = sk_ref[...]
    for kh in range(N_KV_HEADS):
        sl = slice(kh * HEAD_DIM, (kh + 1) * HEAD_DIM)
        r = _rope128(acc[:, sl], cos, sin)
        k_ref[:, kh, :] = r
        kb_ref[:, sl] = r.astype(BF16)
        v_ref[:, kh, :] = acc[:, KV_WIDTH + kh * HEAD_DIM:KV_WIDTH + (kh + 1) * HEAD_DIM]
    t = lax.dot_general(xn, wix_ref[...].astype(BF16), _NT, preferred_element_type=F32)
    r = _rope64(t, ci_ref[...], si_ref[...])
    ki_ref[...] = r[:, :IDX_DIM]
    lane = lax.broadcasted_iota(jnp.int32, r.shape, 1)
    ke = jnp.where(lane < IDX_DIM, r, 0.0)
    kie_ref[...] = ke.astype(BF16)
    kio_ref[...] = pltpu.roll(ke, IDX_DIM, 1).astype(BF16)
    w_scale = IDX_HEADS ** -0.5 * IDX_DIM ** -0.5
    v = acc[:, KV_WIDTH:2 * KV_WIDTH]
    if transposed:
        vb_ref[...] = v.T.astype(BF16)
        ws_ref[...] = t.T[IDX_DIM:IDX_DIM + IDX_HEADS, :] * w_scale
    else:
        vb_ref[...] = v.astype(BF16)
        ws_ref[...] = t[:, IDX_DIM:IDX_DIM + IDX_HEADS] * w_scale


def _proj_tail(x2, pre_g, w_t, ck, sk, ci, si, *, tm, seq_rows, transposed):
    m = x2.shape[0]
    x_col = lambda c: pl.BlockSpec((tm, D_MODEL // NORM_STREAMS), lambda i: (i, c))
    if transposed:
        vw_specs = [pl.BlockSpec((n, tm), lambda i: (0, i)) for n in (KV_WIDTH, IDX_HEADS)]
        vw_shapes = [jax.ShapeDtypeStruct((KV_WIDTH, m), BF16), jax.ShapeDtypeStruct((IDX_HEADS, m), F32)]
    else:
        vw_specs = [pl.BlockSpec((tm, n), lambda i: (i, 0)) for n in (KV_WIDTH, IDX_HEADS)]
        vw_shapes = [jax.ShapeDtypeStruct((m, KV_WIDTH), BF16), jax.ShapeDtypeStruct((m, IDX_HEADS), F32)]
    tiles_per_seq = seq_rows // tm
    tab = pl.BlockSpec((tm, LANES), lambda i: (i % tiles_per_seq, 0))
    blk = lambda n: pl.BlockSpec((tm, n), lambda i: (i, 0))
    kv_rows = pl.BlockSpec((tm, N_KV_HEADS, HEAD_DIM), lambda i: (i, 0, 0))
    return pl.pallas_call(
        functools.partial(_proj_tail_body, transposed=transposed),
        grid=(m // tm,),
        in_specs=[x_col(c) for c in range(NORM_STREAMS)] + [
            pl.BlockSpec((1, D_MODEL), lambda i: (0, 0)),
            pl.BlockSpec((2 * KV_WIDTH, D_MODEL), lambda i: (ROW_K // (2 * KV_WIDTH), 0)),
            pl.BlockSpec((LANES, D_MODEL), lambda i: (ROW_KI // LANES, 0)),
            tab, tab, tab, tab,
        ],
        out_specs=[blk(D_MODEL), kv_rows, kv_rows, blk(IDX_DIM), blk(KV_WIDTH),
                   blk(LANES), blk(LANES)] + vw_specs,
        out_shape=[
            jax.ShapeDtypeStruct((m, D_MODEL), BF16),
            jax.ShapeDtypeStruct((m, N_KV_HEADS, HEAD_DIM), F32),
            jax.ShapeDtypeStruct((m, N_KV_HEADS, HEAD_DIM), F32),
            jax.ShapeDtypeStruct((m, IDX_DIM), F32),
            jax.ShapeDtypeStruct((m, KV_WIDTH), BF16),
            jax.ShapeDtypeStruct((m, LANES), BF16),
            jax.ShapeDtypeStruct((m, LANES), BF16),
        ] + vw_shapes,
        compiler_params=_cparams(("arbitrary",)),
        name="proj_tail",
    )(*([x2] * NORM_STREAMS), pre_g, w_t, w_t, ck, sk, ci, si)


def _gate_body(u_ref, vn_ref, sz_ref, ws_ref, bst_ref, y_ref, *, n_chunks):
    rr = lax.broadcasted_iota(jnp.int32, (CHUNK, CHUNK), 0)
    cc = lax.broadcasted_iota(jnp.int32, (CHUNK, CHUNK), 1)
    tril = cc <= rr
    for g in range(A_GROUPS):
        wm = jnp.where(tril, ws_ref[g], 0.0).astype(BF16)
        b = bst_ref[:, g:g + 1]
        cs = slice(g * LANES, (g + 1) * LANES)
        for c in range(n_chunks):
            rs = slice(c * CHUNK, (c + 1) * CHUNK)
            s = jnp.dot(wm, vn_ref[rs, cs], preferred_element_type=F32) + b
            y = u_ref[rs, cs].astype(F32) * s * sz_ref[rs, cs].astype(F32)
            y_ref[rs, cs] = y.astype(BF16)


def _gate_row_body(u_ref, vn_ref, sz_ref, w0_ref, b0_ref, y_ref):
    s = vn_ref[...] * w0_ref[...] + b0_ref[...]
    y_ref[...] = (u_ref[...].astype(F32) * s * sz_ref[...].astype(F32)).astype(BF16)


def _gate_row(h_main, vn, w0, b0):
    m = h_main.shape[0]
    col = lambda t: pl.BlockSpec((m, COL_TILE), lambda i, t=t: (0, t))
    full = lambda r: pl.BlockSpec((r, A_WIDTH), lambda i: (0, 0))
    return pl.pallas_call(
        _gate_row_body,
        grid=(1,),
        in_specs=[col(T_UA), full(m), col(T_ZA), full(1), full(1)],
        out_specs=full(m),
        out_shape=jax.ShapeDtypeStruct((m, A_WIDTH), BF16),
        compiler_params=_cparams(("arbitrary",)),
        name="gate_sample",
    )(h_main, vn, h_main, w0, b0)


def _lane_bcast(col, rows):
    return jnp.broadcast_to(col, (rows, LANES))


def _select_bias(sc_ref, extra_ref, kp, row_min, row_max, *, n_cols):
    rows = sc_ref.shape[0]
    n_tiles = n_cols // LANES
    extra = None if extra_ref is None else extra_ref[...]

    def count(pred):
        acc = jnp.zeros((rows, LANES), F32)
        for c in range(n_tiles):
            acc = acc + jnp.where(pred(sc_ref[:, c * LANES:(c + 1) * LANES], c), 1.0, 0.0)
        tot = jnp.sum(acc, axis=1, keepdims=True)
        return _lane_bcast(tot, rows)

    def count_ge(x):
        c = count(lambda s, _: s >= x)
        if extra is not None:
            c = c + jnp.where(extra >= x, 1.0, 0.0)
        return c

    c_max = count_ge(row_max)
    top = c_max >= kp
    lo0 = jnp.where(top, row_max, row_min)
    c0 = jnp.where(top, c_max, count_ge(row_min))

    def count_ge3(x1, x2, x3):
        a1 = jnp.zeros((rows, LANES), F32)
        a2 = jnp.zeros((rows, LANES), F32)
        a3 = jnp.zeros((rows, LANES), F32)
        for c in range(n_tiles):
            s = sc_ref[:, c * LANES:(c + 1) * LANES]
            a1 = a1 + jnp.where(s >= x1, 1.0, 0.0)
            a2 = a2 + jnp.where(s >= x2, 1.0, 0.0)
            a3 = a3 + jnp.where(s >= x3, 1.0, 0.0)
        res = []
        for a, x in ((a1, x1), (a2, x2), (a3, x3)):
            tot = _lane_bcast(jnp.sum(a, axis=1, keepdims=True), rows)
            if extra is not None:
                tot = tot + jnp.where(extra >= x, 1.0, 0.0)
            res.append(tot)
        return res

    def step(st):
        lo, hi, c_lo, _, it = st
        mid = 0.5 * lo + 0.5 * hi
        act = (c_lo != kp) & (mid > lo) & (mid < hi)
        any_act = jnp.max(jnp.where(act, 1.0, 0.0))
        clamp = lambda x: jnp.minimum(jnp.maximum(x, lo), hi)
        q1 = clamp(0.75 * lo + 0.25 * hi)
        q3 = clamp(0.25 * lo + 0.75 * hi)
        c1, c2, c3 = count_ge3(q1, mid, q3)
        g1, g2, g3 = c1 >= kp, c2 >= kp, c3 >= kp
        lo_n = jnp.where(g3, q3, jnp.where(g2, mid, jnp.where(g1, q1, lo)))
        c_n = jnp.where(g3, c3, jnp.where(g2, c2, jnp.where(g1, c1, c_lo)))
        hi_n = jnp.where(g3, hi, jnp.where(g2, q3, jnp.where(g1, mid, jnp.minimum(q1, mid))))
        return (jnp.where(act, lo_n, lo), jnp.where(act, hi_n, hi), jnp.where(act, c_n, c_lo),
                any_act, it + 1)

    def cond(st):
        return (st[3] > 0.0) & (st[4] < 400)

    lo, _, c_lo, _, _ = lax.while_loop(cond, step, (lo0, row_max, c0, jnp.float32(1.0), jnp.int32(0)))

    exact = jnp.max(jnp.where(c_lo != kp, 1.0, 0.0)) == 0.0

    @pl.when(exact)
    def _():
        for c in range(n_tiles):
            sl = slice(c * LANES, (c + 1) * LANES)
            sc_ref[:, sl] = jnp.where(sc_ref[:, sl] >= lo, 0.0, NEG)
        if extra is not None:
            extra_ref[...] = jnp.where(extra >= lo, 0.0, NEG)

    @pl.when(jnp.logical_not(exact))
    def _():
        n_gt = count(lambda s, _: s > lo)
        if extra is not None:
            n_gt = n_gt + jnp.where(extra > lo, 1.0, 0.0)
        need = kp - n_gt
        lane = lax.broadcasted_iota(jnp.int32, (rows, LANES), 1).astype(F32)

        def count_eq_upto(jx):
            c = count(lambda s, c: (s == lo) & (lane + float(c * LANES) <= jx))
            if extra is not None:
                c = c + jnp.where((extra == lo) & (jx >= float(n_cols)), 1.0, 0.0)
            return c

        last = n_cols if extra is not None else n_cols - 1
        j_lo = jnp.full((rows, LANES), -1.0, F32)
        j_hi = jnp.full((rows, LANES), float(last), F32)

        def jstep(_, st):
            a, b = st
            mid = jnp.floor(0.5 * (a + b))
            ok = count_eq_upto(mid) >= need
            return jnp.where(ok, a, mid), jnp.where(ok, mid, b)

        n_steps = max(1, (n_cols + 1).bit_length())
        _, j_hi = lax.fori_loop(0, n_steps, jstep, (j_lo, j_hi))
        for c in range(n_tiles):
            sl = slice(c * LANES, (c + 1) * LANES)
            s = sc_ref[:, sl]
            keep = (s > lo) | ((s == lo) & (lane + float(c * LANES) <= j_hi))
            sc_ref[:, sl] = jnp.where(keep, 0.0, NEG)
        if extra is not None:
            keep = (extra > lo) | ((extra == lo) & (j_hi >= float(n_cols)))
            extra_ref[...] = jnp.where(keep, 0.0, NEG)


def _select_bias_cols(sc_ref, kp, col_min, col_max, *, n_rows):
    n_tiles = n_rows // LANES
    n_pivots = 3 if n_rows <= 512 else 2 if n_rows <= 1024 else 1

    def counts(preds):
        accs = [jnp.zeros((LANES, LANES), F32) for _ in preds]
        for r in range(n_tiles):
            s = sc_ref[r * LANES:(r + 1) * LANES, :]
            accs = [a + jnp.where(p(s, r), 1.0, 0.0) for a, p in zip(accs, preds)]
        return [jnp.sum(a, axis=0, keepdims=True) for a in accs]

    count = lambda pred: counts([pred])[0]
    count_ge = lambda x: count(lambda s, _: s >= x)
    c_max = count_ge(col_max)
    top = c_max >= kp
    lo0 = jnp.where(top, col_max, col_min)
    c0 = jnp.where(top, c_max, count_ge(col_min))

    def step(st):
        lo, hi, c_lo, _, it = st
        mid = 0.5 * lo + 0.5 * hi
        act = (c_lo != kp) & (mid > lo) & (mid < hi)
        any_act = jnp.max(jnp.where(act, 1.0, 0.0))
        fr = [(k + 1) / (n_pivots + 1) for k in range(n_pivots)]
        piv = [mid if f == 0.5 else jnp.minimum(jnp.maximum((1.0 - f) * lo + f * hi, lo), hi) for f in fr]
        cs = counts([lambda s, _, x=x: s >= x for x in piv])
        lo_n, c_n, hi_n = lo, c_lo, functools.reduce(jnp.minimum, piv)
        for k in range(n_pivots):
            ge = cs[k] >= kp
            nxt = piv[k + 1] if k + 1 < n_pivots else hi
            lo_n = jnp.where(ge, piv[k], lo_n)
            c_n = jnp.where(ge, cs[k], c_n)
            hi_n = jnp.where(ge, nxt, hi_n)
        return jnp.where(act, lo_n, lo), jnp.where(act, hi_n, hi), jnp.where(act, c_n, c_lo), any_act, it + 1

    def cond(st):
        return (st[3] > 0.0) & (st[4] < 400)

    lo, _, c_lo, _, _ = lax.while_loop(cond, step, (lo0, col_max, c0, jnp.float32(1.0), jnp.int32(0)))
    exact = jnp.max(jnp.where(c_lo != kp, 1.0, 0.0)) == 0.0

    @pl.when(exact)
    def _():
        for r in range(n_tiles):
            rs = slice(r * LANES, (r + 1) * LANES)
            sc_ref[rs, :] = jnp.where(sc_ref[rs, :] >= lo, 0.0, NEG)

    @pl.when(jnp.logical_not(exact))
    def _():
        need = kp - count(lambda s, _: s > lo)
        key = lax.broadcasted_iota(jnp.int32, (LANES, LANES), 0).astype(F32)
        count_eq_upto = lambda jx: count(lambda s, r: (s == lo) & (key + float(r * LANES) <= jx))

        def jstep(_, st):
            a, b = st
            mid = jnp.floor(0.5 * (a + b))
            ok = count_eq_upto(mid) >= need
            return jnp.where(ok, a, mid), jnp.where(ok, mid, b)

        j_lo = jnp.full((1, LANES), -1.0, F32)
        j_hi = jnp.full((1, LANES), float(n_rows - 1), F32)
        _, j_hi = lax.fori_loop(0, max(1, n_rows.bit_length()), jstep, (j_lo, j_hi))
        for r in range(n_tiles):
            rs = slice(r * LANES, (r + 1) * LANES)
            s = sc_ref[rs, :]
            keep = (s > lo) | ((s == lo) & (key + float(r * LANES) <= j_hi))
            sc_ref[rs, :] = jnp.where(keep, 0.0, NEG)


def _prompt_attn_block(nk, qi_ref, wst_ref, kie_ref, kio_ref, q_ref, kb_ref, vt_ref, sz_ref,
                       o_ref, sc_ref, *, top_k, key_chunk):
    qb = pl.program_id(1)
    n_pairs = IDX_HEADS // 2
    grp = N_HEADS // N_KV_HEADS
    qpos = qb * Q_BLOCK + lax.broadcasted_iota(jnp.int32, (LANES, LANES), 1)
    key0 = lax.broadcasted_iota(jnp.int32, (LANES, LANES), 0)

    qs = jnp.concatenate([qi_ref[:, p * LANES:(p + 1) * LANES] for p in range(n_pairs)], axis=0)
    wrows = [wst_ref[h:h + 1, :] for h in range(IDX_HEADS)]
    cmax = jnp.full((LANES, LANES), -jnp.inf, F32)
    cmin = jnp.full((LANES, LANES), jnp.inf, F32)
    for kc in range(0, nk, key_chunk):
        le = lax.dot_general(kie_ref[kc:kc + key_chunk, :], qs, _NT, preferred_element_type=F32)
        lo = lax.dot_general(kio_ref[kc:kc + key_chunk, :], qs, _NT, preferred_element_type=F32)
        for r in range(key_chunk // LANES):
            rs = slice(r * LANES, (r + 1) * LANES)
            acc = jnp.zeros((LANES, LANES), F32)
            for p in range(n_pairs):
                cs = slice(p * LANES, (p + 1) * LANES)
                acc = acc + jnp.maximum(le[rs, cs], 0.0) * wrows[2 * p]
                acc = acc + jnp.maximum(lo[rs, cs], 0.0) * wrows[2 * p + 1]
            causal = key0 + (kc + r * LANES) <= qpos
            cmax = jnp.maximum(cmax, jnp.where(causal, acc, -jnp.inf))
            cmin = jnp.minimum(cmin, jnp.where(causal, acc, jnp.inf))
            sc_ref[kc + r * LANES:kc + (r + 1) * LANES, :] = jnp.where(causal, acc, -jnp.inf)

    col_max = jnp.max(cmax, axis=0, keepdims=True)
    col_min = jnp.min(cmin, axis=0, keepdims=True)
    kp = jnp.minimum(qpos[0:1, :] + 1, top_k).astype(F32)
    _select_bias_cols(sc_ref, kp, col_min, col_max, n_rows=nk)

    bias = jnp.concatenate([sc_ref[0:nk, :]] * grp, axis=1)
    for kh in range(N_KV_HEADS):
        qh = jnp.concatenate(
            [q_ref[:, (kh * grp + g) * HEAD_DIM:(kh * grp + g + 1) * HEAD_DIM] for g in range(grp)], axis=0)
        ks = slice(kh * HEAD_DIM, (kh + 1) * HEAD_DIM)
        s = lax.dot_general(kb_ref[0:nk, ks], qh, _NT, preferred_element_type=F32) + bias
        m = jnp.max(s, axis=0, keepdims=True)
        p = jnp.exp2((s - m) * (HEAD_DIM ** -0.5 * LOG2_E))
        l = jnp.sum(p, axis=0, keepdims=True)
        ot = jnp.dot(vt_ref[ks, 0:nk], p.astype(BF16), preferred_element_type=F32) / l
        for g in range(grp):
            hs = slice((kh * grp + g) * HEAD_DIM, (kh * grp + g + 1) * HEAD_DIM)
            o = ot[:, g * LANES:(g + 1) * LANES].T
            o_ref[:, hs] = (o * sz_ref[:, hs].astype(F32)).astype(BF16)


def _prompt_attn_body(*refs, seq, top_k, n_buckets, key_chunk):
    qb = pl.program_id(1)
    span = seq // n_buckets
    per = span // Q_BLOCK
    for c in range(n_buckets):
        @pl.when(qb // per == c)
        def _(c=c):
            _prompt_attn_block((c + 1) * span, *refs, top_k=top_k, key_chunk=key_chunk)


def _prompt_attn(h_main, ws_t, kie, kio, kb, vb_t, *, batch, seq):
    top_k = min(TOPK_MAX, seq // 4)
    n_qb = seq // Q_BLOCK
    key_chunk = min(512, seq)
    n_buckets = max(1, seq // 512)
    body = functools.partial(_prompt_attn_body, seq=seq, top_k=top_k, n_buckets=n_buckets, key_chunk=key_chunk)
    col = lambda t: pl.BlockSpec((Q_BLOCK, COL_TILE), lambda b, q, t=t: (b * n_qb + q, t))
    seqblk = lambda n: pl.BlockSpec((seq, n), lambda b, q: (b, 0))
    return pl.pallas_call(
        body,
        grid=(batch, n_qb),
        in_specs=[col(T_QI),
                  pl.BlockSpec((IDX_HEADS, Q_BLOCK), lambda b, q: (0, b * n_qb + q)),
                  seqblk(LANES), seqblk(LANES),
                  col(T_Q), seqblk(KV_WIDTH),
                  pl.BlockSpec((KV_WIDTH, seq), lambda b, q: (0, b)),
                  col(T_ZB)],
        out_specs=pl.BlockSpec((Q_BLOCK, B_WIDTH), lambda b, q: (b * n_qb + q, 0)),
        out_shape=jax.ShapeDtypeStruct((batch * seq, B_WIDTH), BF16),
        scratch_shapes=[pltpu.VMEM((seq, Q_BLOCK), F32)],
        compiler_params=_cparams(("arbitrary", "arbitrary")),
        name="attn_prompt",
    )(h_main, ws_t, kie, kio, h_main, kb, vb_t, h_main)


def _merge_body(x_ref, *refs, fused_gate):
    if fused_gate:
        gate_refs, refs = refs[:5], refs[5:]
        ya_ref = refs[-1]
        _gate_body(*gate_refs, ya_ref, n_chunks=x_ref.shape[0] // CHUNK)
        ob_ref, ga_ref, gb_ref, woa_ref, wob_ref, wout_ref, pg_ref, o_ref = refs[:-1]
    else:
        ya_ref, ob_ref, ga_ref, gb_ref, woa_ref, wob_ref, wout_ref, pg_ref, o_ref = refs
    pa = jnp.dot(ya_ref[...], woa_ref[...], preferred_element_type=F32)
    pb = jnp.dot(ob_ref[...], wob_ref[...], preferred_element_type=F32)
    mix = ga_ref[...].astype(F32) * pa + gb_ref[...].astype(F32) * pb
    r = jnp.dot(mix.astype(BF16), wout_ref[...], preferred_element_type=F32)
    o_ref[...] = x_ref[...] + _rmsnorm_rows(r, pg_ref[...])


def _merge(x2, ya, ob, h_main, w_oa, w_ob, w_out, post_g, *, tm, gate=None):
    m = x2.shape[0]
    const = lambda r, c: pl.BlockSpec((r, c), lambda i: (0, 0), pipeline_mode=pl.Buffered(1))
    if gate is None:
        a_specs = [pl.BlockSpec((tm, A_WIDTH), lambda i: (i, 0))]
        a_args, scratch = [ya], []
    else:
        col = lambda t: pl.BlockSpec((tm, COL_TILE), lambda i, t=t: (i, t))
        a_specs = [col(T_UA), col(T_VA), col(T_ZA),
                   pl.BlockSpec((A_GROUPS, CHUNK, CHUNK), lambda i: (0, 0, 0)),
                   pl.BlockSpec((CHUNK, A_GROUPS), lambda i: (0, 0))]
        a_args, scratch = [h_main, h_main, h_main, *gate], [pltpu.VMEM((tm, A_WIDTH), BF16)]
    return pl.pallas_call(
        functools.partial(_merge_body, fused_gate=gate is not None),
        grid=(m // tm,),
        in_specs=[pl.BlockSpec((tm, D_MODEL), lambda i: (i, 0))] + a_specs + [
            pl.BlockSpec((tm, B_WIDTH), lambda i: (i, 0)),
            pl.BlockSpec((tm, D_MODEL), lambda i: (i, T_GA // 2)),
            pl.BlockSpec((tm, D_MODEL), lambda i: (i, T_GB // 2)),
            const(A_WIDTH, D_MODEL), const(B_WIDTH, D_MODEL), const(D_MODEL, D_MODEL),
            const(1, D_MODEL),
        ],
        out_specs=pl.BlockSpec((tm, D_MODEL), lambda i: (i, 0)),
        out_shape=jax.ShapeDtypeStruct((m, D_MODEL), F32),
        scratch_shapes=scratch,
        compiler_params=_cparams(("arbitrary",)),
        name="merge",
    )(x2, *a_args, ob, h_main, h_main, w_oa, w_ob, w_out, post_g)


SCORE_CHUNK = 2048
COMPACT_SEQS = 4
GATHER_BUFS = 3


def _sample_scores_body(pt_ref, q_ref, w_ref, kidx_hbm, o_ref, buf, sem):
    db, n_pages = pt_ref.shape
    past = n_pages * PAGE_SIZE

    def page_copy(b, p, slot):
        dst = buf.at[slot, :, pl.ds(pl.multiple_of(p * PAGE_SIZE, PAGE_SIZE), PAGE_SIZE)]
        return pltpu.make_async_copy(kidx_hbm.at[pt_ref[b, p]], dst, sem.at[slot])

    def start_all(b, slot):
        def f(p, c):
            page_copy(b, p, slot).start()
            return c
        lax.fori_loop(0, n_pages, f, 0, unroll=8)

    def wait_all(slot):
        for p in range(n_pages):
            dst = buf.at[slot, :, pl.ds(p * PAGE_SIZE, PAGE_SIZE)]
            pltpu.make_async_copy(kidx_hbm.at[0], dst, sem.at[slot]).wait()

    start_all(0, 0)

    def per_seq(b, c):
        slot = b % 2

        @pl.when(b + 1 < db)
        def _():
            start_all(b + 1, 1 - slot)

        wait_all(slot)
        q = q_ref[b]
        w = w_ref[b]
        for ch in range(past // SCORE_CHUNK):
            cs = slice(ch * SCORE_CHUNK, (ch + 1) * SCORE_CHUNK)
            logit = jnp.dot(q, buf[slot, :, cs].astype(BF16), preferred_element_type=F32)
            o_ref[pl.ds(b, 1), cs] = jnp.sum(jnp.maximum(logit, 0.0) * w, axis=0, keepdims=True)
        return c

    lax.fori_loop(0, db, per_seq, 0)


def _sample_scores(page_table, qi3, ws3, kidx_pages_t):
    db, n_pages = page_table.shape
    past = n_pages * PAGE_SIZE
    grid_spec = pltpu.PrefetchScalarGridSpec(
        num_scalar_prefetch=1,
        grid=(1,),
        in_specs=[pl.BlockSpec((db, IDX_HEADS, IDX_DIM), lambda i, pt: (0, 0, 0)),
                  pl.BlockSpec((db, IDX_HEADS, 1), lambda i, pt: (0, 0, 0)),
                  pl.BlockSpec(memory_space=pl.ANY)],
        out_specs=pl.BlockSpec((db, past), lambda i, pt: (0, 0)),
        scratch_shapes=[pltpu.VMEM((2, IDX_DIM, past), F32), pltpu.SemaphoreType.DMA((2,))],
    )
    return pl.pallas_call(
        _sample_scores_body,
        grid_spec=grid_spec,
        out_shape=jax.ShapeDtypeStruct((db, past), F32),
        compiler_params=_cparams(("arbitrary",)),
        name="scores_sample",
    )(page_table, qi3, ws3, kidx_pages_t)


def _sample_select_body(sc_ref, qi_ref, kie_ref, ws_ref, bias_ref, bnew_ref, *, top_k):
    rows, past = sc_ref.shape
    lane = lax.broadcasted_iota(jnp.int32, (rows, LANES), 1)
    ki = kie_ref[...].astype(F32)
    ki = ki + pltpu.roll(ki, IDX_DIM, 1)
    s_new = jnp.zeros((rows, 1), F32)
    for p in range(IDX_HEADS // 2):
        prod = qi_ref[:, p * LANES:(p + 1) * LANES].astype(F32) * ki
        l_even = jnp.sum(jnp.where(lane < IDX_DIM, prod, 0.0), axis=1, keepdims=True)
        l_odd = jnp.sum(jnp.where(lane >= IDX_DIM, prod, 0.0), axis=1, keepdims=True)
        s_new = s_new + jnp.maximum(l_even, 0.0) * ws_ref[:, 2 * p:2 * p + 1]
        s_new = s_new + jnp.maximum(l_odd, 0.0) * ws_ref[:, 2 * p + 1:2 * p + 2]
    extra = _lane_bcast(s_new, rows)
    bnew_ref[...] = extra
    rmax = extra
    rmin = extra
    for c in range(past // LANES):
        sl = slice(c * LANES, (c + 1) * LANES)
        s = sc_ref[:, sl]
        bias_ref[:, sl] = s
        rmax = jnp.maximum(rmax, s)
        rmin = jnp.minimum(rmin, s)
    row_max = _lane_bcast(jnp.max(rmax, axis=1, keepdims=True), rows)
    row_min = _lane_bcast(jnp.min(rmin, axis=1, keepdims=True), rows)
    kp = jnp.full((rows, LANES), float(top_k), F32)
    _select_bias(bias_ref, bnew_ref, kp, row_min, row_max, n_cols=past)


def _sample_select(scores, h_main, kie, ws, *, top_k):
    db, past = scores.shape
    full = lambda r, c: pl.BlockSpec((r, c), lambda i: (0, 0))
    return pl.pallas_call(
        functools.partial(_sample_select_body, top_k=top_k),
        grid=(1,),
        in_specs=[full(db, past),
                  pl.BlockSpec((db, COL_TILE), lambda i: (0, T_QI)),
                  full(db, LANES), full(db, IDX_HEADS)],
        out_specs=[full(db, past), full(db, LANES)],
        out_shape=[jax.ShapeDtypeStruct((db, past), F32), jax.ShapeDtypeStruct((db, LANES), F32)],
        compiler_params=_cparams(("arbitrary",)),
        name="select_sample",
    )(scores, h_main, kie, ws)


def _sample_compact_body(m_ref, mt_ref, pt_ref, idx_ref, row_ref, *, n_slots):
    n_pages = m_ref.shape[1]
    for q in range(m_ref.shape[0]):
        _compact_one(m_ref.at[q], mt_ref.at[q], pt_ref.at[q], idx_ref.at[q], row_ref.at[q],
                     n_pages=n_pages, n_slots=n_slots)


def _compact_one(m_ref, mt_ref, pt_ref, idx_ref, row_ref, *, n_pages, n_slots):
    pt = jnp.broadcast_to(pt_ref[...], (8, n_pages))
    pt_hi = (pt // PAGE_SIZE).astype(F32).astype(BF16)
    pt_lo = (pt % PAGE_SIZE).astype(F32).astype(BF16)
    one = lambda pred: jnp.where(pred, 1.0, 0.0)
    kept = m_ref[...] == 0.0
    kept_t = mt_ref[...] == 0.0
    ri = lax.broadcasted_iota(jnp.int32, (PAGE_SIZE, PAGE_SIZE), 0)
    ci = lax.broadcasted_iota(jnp.int32, (PAGE_SIZE, PAGE_SIZE), 1)
    rp = lax.broadcasted_iota(jnp.int32, (n_pages, n_pages), 0)
    cp = lax.broadcasted_iota(jnp.int32, (n_pages, n_pages), 1)
    plt = jnp.dot(one(ci <= ri).astype(BF16), one(kept_t).astype(BF16), preferred_element_type=F32)
    n_row = plt[PAGE_SIZE - 1:PAGE_SIZE, :]
    n_col = _lane_bcast(jnp.sum(one(kept), axis=1, keepdims=True), n_pages)
    e_col = jnp.dot(one(cp <= rp).astype(BF16), n_col.astype(BF16), preferred_element_type=F32)
    n_row8 = jnp.broadcast_to(n_row, (8, n_pages))
    e_row8 = jnp.dot(n_row8.astype(BF16), one(rp <= cp).astype(BF16), preferred_element_type=F32)
    off_row8 = e_row8 - n_row8
    n_total = e_col[n_pages - 1:n_pages, :]
    page_id = lax.broadcasted_iota(jnp.int32, (n_pages, LANES), 0).astype(F32)
    for jt in range(n_slots // LANES):
        j = (lax.broadcasted_iota(jnp.int32, (1, LANES), 1) + jt * LANES).astype(F32)
        page_j = jnp.sum(one(e_col <= j), axis=0, keepdims=True)
        pick = one(page_id == page_j).astype(BF16)
        prefix_j = jnp.dot(plt.astype(BF16), pick, preferred_element_type=F32)
        off_j = jnp.dot(off_row8.astype(BF16), pick, preferred_element_type=F32)[0:1]
        local_j = jnp.sum(one(prefix_j <= j - off_j), axis=0, keepdims=True)
        pos = page_j * float(PAGE_SIZE) + local_j
        phys = (jnp.dot(pt_hi, pick, preferred_element_type=F32)[0:1] * float(PAGE_SIZE)
                + jnp.dot(pt_lo, pick, preferred_element_type=F32)[0:1])
        row = phys * float(PAGE_SIZE) + local_j
        used = j < n_total
        sl = slice(jt * LANES, (jt + 1) * LANES)
        idx_ref[:, sl] = jnp.where(used, pos, -1.0).astype(jnp.int32)
        row_ref[:, sl] = jnp.where(used, row, 0.0).astype(jnp.int32)


def _sample_compact(bias3, bias3_t, pt3, *, n_slots):
    db, n_pages, _ = bias3.shape
    per = COMPACT_SEQS if db % COMPACT_SEQS == 0 else 1
    out = pl.BlockSpec((per, 1, n_slots), lambda b: (b, 0, 0))
    return pl.pallas_call(
        functools.partial(_sample_compact_body, n_slots=n_slots),
        grid=(db // per,),
        in_specs=[pl.BlockSpec((per, n_pages, PAGE_SIZE), lambda b: (b, 0, 0)),
                  pl.BlockSpec((per, PAGE_SIZE, n_pages), lambda b: (b, 0, 0)),
                  pl.BlockSpec((per, 1, n_pages), lambda b: (b, 0, 0))],
        out_specs=[out, out],
        out_shape=[jax.ShapeDtypeStruct((db, 1, n_slots), jnp.int32)] * 2,
        compiler_params=_cparams(("arbitrary",)),
        name="compact_sample",
    )(bias3, bias3_t, pt3)


def _sample_attn_body(row_ref, q_ref, slot_ref, bnew_ref, kn_ref, vn_ref, sz_ref, k_hbm, v_hbm, o_ref, *scratch):
    kbufs, vbufs, sem = scratch[:GATHER_BUFS], scratch[GATHER_BUFS:2 * GATHER_BUFS], scratch[-1]
    b = pl.program_id(0)
    nb = pl.num_programs(0)
    n_slots = kbufs[0].shape[0]
    ahead = GATHER_BUFS - 1
    grp = N_HEADS // N_KV_HEADS
    scale = HEAD_DIM ** -0.5

    def row_copies(seq, j, to):
        r = row_ref[seq, j]
        return (pltpu.make_async_copy(k_hbm.at[r], kbufs[to].at[j], sem.at[0, to]),
                pltpu.make_async_copy(v_hbm.at[r], vbufs[to].at[j], sem.at[1, to]))

    def start_all(seq, to, inline):
        def f(j, c):
            ck, cv = row_copies(seq, j, to)
            ck.start()
            cv.start()
            return c
        if inline:
            for j in range(n_slots):
                f(j, 0)
        else:
            lax.fori_loop(0, n_slots, f, 0, unroll=8)

    def wait_all(to):
        for blk in range(n_slots // PAGE_SIZE):
            rows = pl.ds(blk * PAGE_SIZE, PAGE_SIZE)
            src = pl.ds(0, PAGE_SIZE)
            pltpu.make_async_copy(k_hbm.at[src], kbufs[to].at[rows], sem.at[0, to]).wait()
            pltpu.make_async_copy(v_hbm.at[src], vbufs[to].at[rows], sem.at[1, to]).wait()

    def attend(buf):
        q = q_ref[0]
        head_s = lax.broadcasted_iota(jnp.int32, (N_HEADS, n_slots), 0)
        head_o = lax.broadcasted_iota(jnp.int32, (N_HEADS, HEAD_DIM), 0)
        k0, k1 = (kbufs[buf][:, kh, :].astype(BF16) for kh in range(N_KV_HEADS))
        v0, v1 = (vbufs[buf][:, kh, :].astype(BF16) for kh in range(N_KV_HEADS))
        s0 = lax.dot_general(q, k0, _NT, preferred_element_type=F32)
        s1 = lax.dot_general(q, k1, _NT, preferred_element_type=F32)
        s = jnp.where(head_s < grp, s0, s1) * scale + jnp.where(slot_ref[0] >= 0, 0.0, NEG)
        kn = kn_ref[0].astype(F32)
        vn = vn_ref[0].astype(F32)
        kn8 = jnp.where(head_o < grp, kn[:, :HEAD_DIM], kn[:, HEAD_DIM:])
        vn8 = jnp.where(head_o < grp, vn[:, :HEAD_DIM], vn[:, HEAD_DIM:])
        s_new = _lane_bcast(jnp.sum(q.astype(F32) * kn8, axis=-1, keepdims=True), N_HEADS) * scale + bnew_ref[0]
        m = jnp.maximum(_lane_bcast(jnp.max(s, axis=-1, keepdims=True), N_HEADS), s_new)
        p = jnp.exp(s - m[:, :1])
        p_new = jnp.exp(s_new - m)
        l = _lane_bcast(jnp.sum(p, axis=-1, keepdims=True), N_HEADS) + p_new
        pb = p.astype(BF16)
        pv = jnp.where(head_o < grp,
                       jnp.dot(pb, v0, preferred_element_type=F32),
                       jnp.dot(pb, v1, preferred_element_type=F32))
        o = (pv + p_new * vn8) / l
        o_ref[0] = (o * sz_ref[0].astype(F32)).astype(BF16)

    @pl.when(b == 0)
    def _():
        for first in range(ahead):
            @pl.when(first < nb)
            def _(first=first):
                start_all(first, first, inline=False)

    for buf in range(GATHER_BUFS):
        mine = b % GATHER_BUFS == buf

        @pl.when(mine & (b + ahead < nb))
        def _(buf=buf):
            wait_all(buf)
            start_all(b + ahead, (buf + ahead) % GATHER_BUFS, inline=True)
            attend(buf)

        @pl.when(mine & (b + ahead >= nb))
        def _(buf=buf):
            wait_all(buf)
            attend(buf)


def _sample_attn(rows, q3, slots3, bnew3, kn3, vn3, sz3, k_rows, v_rows):
    db, n_slots = rows.shape
    per_b = lambda r, c: pl.BlockSpec((1, r, c), lambda b, rw: (b, 0, 0))
    grid_spec = pltpu.PrefetchScalarGridSpec(
        num_scalar_prefetch=1,
        grid=(db,),
        in_specs=[per_b(N_HEADS, HEAD_DIM), per_b(1, n_slots), per_b(1, LANES),
                  per_b(1, KV_WIDTH), per_b(1, KV_WIDTH), per_b(N_HEADS, HEAD_DIM),
                  pl.BlockSpec(memory_space=pl.ANY), pl.BlockSpec(memory_space=pl.ANY)],
        out_specs=per_b(N_HEADS, HEAD_DIM),
        scratch_shapes=[pltpu.VMEM((n_slots, N_KV_HEADS, HEAD_DIM), F32)] * (2 * GATHER_BUFS)
                       + [pltpu.SemaphoreType.DMA((2, GATHER_BUFS))],
    )
    return pl.pallas_call(
        _sample_attn_body,
        grid_spec=grid_spec,
        out_shape=jax.ShapeDtypeStruct((db, N_HEADS, HEAD_DIM), BF16),
        compiler_params=_cparams(("arbitrary",)),
        name="attn_sample",
    )(rows, q3, slots3, bnew3, kn3, vn3, sz3, k_rows, v_rows)


def _col_tile(h_main, t, n=1):
    return h_main[:, t * COL_TILE:(t + n) * COL_TILE]


def kernel(x_prompt, x_sample, cache_k, cache_v, cache_kidx, page_table, pre_g, w_in, a_ln_g, a_ln_b,
           a_ws, a_bs, w_oa, w_ob, w_out, post_g):
    batch, seq, _ = x_prompt.shape
    db, ds, _ = x_sample.shape
    depth = w_in.shape[0]
    n_pages = page_table.shape[1]
    past = n_pages * PAGE_SIZE
    assert ds == 1 and seq % PROJ_ROWS == 0 and past % SCORE_CHUNK == 0 and Q_BLOCK == LANES
    top_k_s = min(TOPK_MAX, (past + ds) // 4)
    assert top_k_s <= past and top_k_s % LANES == 0

    pos_p = jnp.arange(seq)
    pos_s = past + (jnp.arange(db * ds) % ds)
    tabs_p = _rope_tables(pos_p, HEAD_DIM) + _rope_tables(pos_p, IDX_DIM)
    tabs_s = _rope_tables(pos_s, HEAD_DIM) + _rope_tables(pos_s, IDX_DIM)

    hp = x_prompt.reshape(batch * seq, D_MODEL)
    hs = x_sample.reshape(db * ds, D_MODEL)
    outs = [[] for _ in range(8)]
    for l in range(depth):
        w_t = w_in[l].T
        woa, wob, wout = w_oa[l].astype(BF16), w_ob[l].astype(BF16), w_out[l].astype(BF16)
        g_pre, g_post = pre_g[l][None], post_g[l][None]
        ln_g, ln_b = a_ln_g[l][None], a_ln_b[l][None]

        xn, k, v, ki, kb, kie, kio, vb_t, ws_t = _proj_tail(hp, g_pre, w_t, *tabs_p, tm=PROJ_ROWS, seq_rows=seq,
                                                            transposed=True)
        h_main, gv, w_tiles = _proj_main(xn, w_t, *tabs_p, ln_g, ln_b, tm=PROJ_ROWS, seq_rows=seq,
                                         gv_rows=CHUNK, cast_weights=True)
        ob = _prompt_attn(h_main, ws_t, kie, kio, kb, vb_t, batch=batch, seq=seq)
        hp = _merge(hp, None, ob, h_main, woa, wob, wout, g_post, tm=MERGE_ROWS, gate=(a_ws[l], a_bs[l].T))
        outs[0].append(k.reshape(batch, seq, N_KV_HEADS, HEAD_DIM))
        outs[1].append(v.reshape(batch, seq, N_KV_HEADS, HEAD_DIM))
        outs[2].append(ki.reshape(batch, seq, IDX_DIM))
        outs[3].append(gv.reshape(batch, CHUNK, A_WIDTH))

        m_s = db * ds
        xn, k, v, ki, kb, kie, kio, vb, ws = _proj_tail(hs, g_pre, w_t, *tabs_s, tm=m_s, seq_rows=m_s,
                                                        transposed=False)
        h_main, gv = _proj_main(xn, w_tiles, *tabs_s, ln_g, ln_b, tm=m_s, seq_rows=m_s, gv_rows=m_s,
                                cast_weights=False)
        w0 = jnp.repeat(a_ws[l][:, 0, 0], LANES)[None]
        b0 = jnp.repeat(a_bs[l][:, 0], LANES)[None]
        ya = _gate_row(h_main, gv, w0, b0)
        qi3 = _col_tile(h_main, T_QI).reshape(db, IDX_HEADS, IDX_DIM)
        kidx_t = jnp.swapaxes(cache_kidx[l], 1, 2)
        scores = _sample_scores(page_table, qi3, ws.reshape(db, IDX_HEADS, 1), kidx_t)
        bias, bnew = _sample_select(scores, h_main, kie, ws, top_k=top_k_s)
        bias3 = bias.reshape(db, n_pages, PAGE_SIZE)
        slots3, rows3 = _sample_compact(bias3, jnp.swapaxes(bias3, 1, 2),
                                        page_table.reshape(db, 1, n_pages), n_slots=top_k_s)
        pool_rows = lambda c: c.reshape(-1, N_KV_HEADS, HEAD_DIM)
        ob = _sample_attn(rows3.reshape(db, top_k_s),
                          _col_tile(h_main, T_Q).reshape(db, N_HEADS, HEAD_DIM),
                          slots3, bnew.reshape(db, 1, LANES),
                          kb.reshape(db, 1, KV_WIDTH), vb.reshape(db, 1, KV_WIDTH),
                          _col_tile(h_main, T_ZB).reshape(db, N_HEADS, HEAD_DIM),
                          pool_rows(cache_k[l]), pool_rows(cache_v[l])).reshape(db, B_WIDTH)
        hs = _merge(hs, ya, ob, h_main, woa, wob, wout, g_post, tm=m_s)
        outs[4].append(k.reshape(db, ds, N_KV_HEADS, HEAD_DIM))
        outs[5].append(v.reshape(db, ds, N_KV_HEADS, HEAD_DIM))
        outs[6].append(ki.reshape(db, ds, IDX_DIM))
        outs[7].append(gv.reshape(db, ds, A_WIDTH))

    st = [jnp.stack(o, axis=0) for o in outs]
    return (hp.reshape(batch, seq, D_MODEL), hs.reshape(db, ds, D_MODEL),
            st[0], st[1], st[2], st[3], st[4], st[5], st[6], st[7])
```

```python
import functools

import jax
import jax.numpy as jnp
from jax import lax
from jax.experimental import pallas as pl
from jax.experimental.pallas import tpu as pltpu

F32 = jnp.float32
BF16 = jnp.bfloat16

D_MODEL = 2048
CHUNK = 128
A_GROUPS = 8
A_WIDTH = 1024
N_HEADS = 8
N_KV_HEADS = 2
HEAD_DIM = 128
B_WIDTH = 1024
KV_WIDTH = 256
IDX_HEADS = 16
IDX_DIM = 64
TOPK_MAX = 256
ROPE_THETA = 10000.0
EPS = 1e-6
PAGE_SIZE = 128
Q_BLOCK = 128
NEG = -1e30
LOG2_E = 1.4426950408889634

LANES = 128
BF16_SUBLANES = 16
MXU_WIDTH = 256
PROJ_ROWS = 512
MERGE_ROWS = 256
NORM_STREAMS = 4
_NT = (((1,), (1,)), ((), ()))
COL_TILE = 1024
T_UA, T_VA, T_ZA, T_Q, T_ZB, T_QI, T_GA, T_GB = 0, 1, 2, 3, 4, 5, 6, 8
N_MAIN_TILES = 10
ROW_K = 4 * COL_TILE
ROW_ZB = ROW_K + 2 * KV_WIDTH
ROW_KI = ROW_ZB + B_WIDTH + IDX_HEADS * IDX_DIM
ROW_GA = ROW_KI + IDX_DIM + IDX_HEADS
VMEM_LIMIT = 56 * 1024 * 1024


def _cparams(sem):
    return pltpu.CompilerParams(dimension_semantics=sem, vmem_limit_bytes=VMEM_LIMIT)


def _rope_tables(pos, dim):
    half = dim // 2
    inv = ROPE_THETA ** (-jnp.arange(half, dtype=F32) / half)
    ang = pos.astype(F32)[:, None] * inv[None, :]
    cos = jnp.cos(ang)
    sin = jnp.sin(ang)
    reps = LANES // dim
    cos_t = jnp.tile(jnp.concatenate([cos, cos], axis=-1), (1, reps))
    sin_t = jnp.tile(jnp.concatenate([-sin, sin], axis=-1), (1, reps))
    return cos_t, sin_t


def _rope128(x, cos, sin):
    return x * cos + pltpu.roll(x, 64, 1) * sin


def _rope64(x, cos, sin):
    lane = lax.broadcasted_iota(jnp.int32, x.shape, 1)
    first = (lane % IDX_DIM) < (IDX_DIM // 2)
    partner = jnp.where(first, pltpu.roll(x, LANES - 32, 1), pltpu.roll(x, 32, 1))
    return x * cos + partner * sin


def _rmsnorm_rows(xf, g):
    ms = jnp.mean(xf * xf, axis=-1, keepdims=True)
    return xf * lax.rsqrt(ms + EPS) * g


_TILE_KINDS = ("copy", "ln", "silu", "rope128", "silu", "rope64", "sigmoid", "sigmoid", "sigmoid", "sigmoid")


def _sigmoid(x):
    return 0.5 * jnp.tanh(0.5 * x) + 0.5


def _prenorm_rows(x_refs, g_ref, o_ref):
    xs = [r[...] for r in x_refs]
    ms = sum(jnp.sum(x * x, axis=-1, keepdims=True) for x in xs) * (1.0 / D_MODEL)
    scale = lax.rsqrt(ms + EPS)
    w = xs[0].shape[1]
    for c, x in enumerate(xs):
        cs = slice(c * w, (c + 1) * w)
        o_ref[:, cs] = (x * scale * g_ref[:, cs]).astype(BF16)


def _proj_main_body(xn_ref, *refs, gv_rows, n_row_tiles, cast_weights):
    n_chunks = COL_TILE // MXU_WIDTH
    w_refs, refs = refs[:n_chunks], refs[n_chunks:]
    cq_ref, sq_ref, ci_ref, si_ref, lng_ref, lnb_ref, h_ref, gv_ref = refs[:8]
    wbf_ref = refs[8] if cast_weights else None
    acc_ref = refs[-1]
    j = pl.program_id(0)
    i = pl.program_id(1)
    tm = xn_ref.shape[0]
    chunks = [slice(c * MXU_WIDTH, (c + 1) * MXU_WIDTH) for c in range(n_chunks)]
    weights = lambda c: wbf_ref[chunks[c], :] if cast_weights else w_refs[c][...]

    def finish_chunk(kind, cs, stats):
        acc = acc_ref[:, cs]
        if kind == "copy":
            h_ref[:, cs] = acc.astype(BF16)
        elif kind == "ln":
            mu, rstd = stats
            vn = (acc - mu) * rstd * lng_ref[:, cs] + lnb_ref[:, cs]
            h_ref[:, cs] = vn.astype(BF16)
            gv_ref[:, cs] = vn[tm - gv_rows:, :]
        elif kind == "silu":
            h_ref[:, cs] = (acc * _sigmoid(acc)).astype(BF16)
        elif kind == "sigmoid":
            h_ref[:, cs] = _sigmoid(acc).astype(BF16)
        else:
            rope, cos, sin = ((_rope128, cq_ref[...], sq_ref[...]) if kind == "rope128"
                              else (_rope64, ci_ref[...], si_ref[...]))
            for h in range(MXU_WIDTH // LANES):
                sl = slice(cs.start + h * LANES, cs.start + (h + 1) * LANES)
                h_ref[:, sl] = rope(acc[:, h * LANES:(h + 1) * LANES], cos, sin).astype(BF16)

    def run(kind, finish, matmul):
        stats = None
        if finish and kind == "ln":
            acc = acc_ref[...]
            mu = jnp.mean(acc, axis=-1, keepdims=True)
            d = acc - mu
            stats = (mu, lax.rsqrt(jnp.mean(d * d, axis=-1, keepdims=True) + EPS))
        for c, cs in enumerate(chunks):
            if finish:
                finish_chunk(kind, cs, stats)
            if matmul:
                acc_ref[:, cs] = lax.dot_general(xn_ref[...], weights(c), _NT, preferred_element_type=F32)

    @pl.when(i == 0)
    def _():
        if cast_weights:
            for c, cs in enumerate(chunks):
                wbf_ref[cs, :] = w_refs[c][...].astype(BF16)
        run(None, False, True)

    @pl.when((i == 0) & (j != T_VA))
    def _():
        gv_ref[...] = jnp.zeros(gv_ref.shape, F32)

    for kind in sorted(set(_TILE_KINDS)):
        is_kind = functools.reduce(jnp.logical_or, [j == t for t, k in enumerate(_TILE_KINDS) if k == kind])

        @pl.when(is_kind & (i > 0) & (i < n_row_tiles))
        def _(kind=kind):
            run(kind, True, True)

        @pl.when(is_kind & (i == n_row_tiles))
        def _(kind=kind):
            run(kind, True, False)


def _main_tile_row(j, chunk):
    g = BF16_SUBLANES
    skip_kv = (ROW_ZB - T_ZB * COL_TILE) // g
    skip_idx = (ROW_GA - ROW_ZB - (T_GA - T_ZB) * COL_TILE) // g
    return (j * (COL_TILE // g) + chunk * (MXU_WIDTH // g)
            + jnp.where(j >= T_ZB, skip_kv, 0) + jnp.where(j >= T_GA, skip_idx, 0)) * g


def _proj_main(xn, w, cq, sq, ci, si, ln_g, ln_b, *, tm, seq_rows, gv_rows, cast_weights):
    m = xn.shape[0]
    tiles_per_seq = seq_rows // tm
    n_seq = m // seq_rows
    n_row_tiles = m // tm
    n_chunks = COL_TILE // MXU_WIDTH
    body = functools.partial(_proj_main_body, gv_rows=gv_rows, n_row_tiles=n_row_tiles,
                             cast_weights=cast_weights)
    w_tile = lambda j, i: jnp.minimum(jnp.where(i == n_row_tiles, j + 1, j), N_MAIN_TILES - 1)
    if cast_weights:
        w_chunk = lambda c: pl.BlockSpec((pl.Element(MXU_WIDTH), pl.Element(D_MODEL)),
                                         lambda j, i: (_main_tile_row(w_tile(j, i), c), 0))
    else:
        w_chunk = lambda c: pl.BlockSpec((MXU_WIDTH, D_MODEL), lambda j, i: (w_tile(j, i) * n_chunks + c, 0))
    w_out_spec = [pl.BlockSpec((COL_TILE, D_MODEL), lambda j, i: (j, 0))] if cast_weights else []
    w_out_shape = [jax.ShapeDtypeStruct((N_MAIN_TILES * COL_TILE, D_MODEL), BF16)] if cast_weights else []
    prev = lambda i: jnp.maximum(i - 1, 0)
    tab = pl.BlockSpec((tm, LANES), lambda j, i: (prev(i) % tiles_per_seq, 0))
    row = lambda n: pl.BlockSpec((1, n), lambda j, i: (0, 0))
    gv_block = lambda j, i: (jnp.where(j == T_VA, prev(i) // tiles_per_seq, n_seq + (j > T_VA)), 0)
    h_main, gv, *w_bf = pl.pallas_call(
        body,
        grid=(N_MAIN_TILES, n_row_tiles + 1),
        in_specs=[pl.BlockSpec((tm, D_MODEL), lambda j, i: (jnp.minimum(i, n_row_tiles - 1), 0))]
                 + [w_chunk(c) for c in range(n_chunks)]
                 + [tab, tab, tab, tab, row(A_WIDTH), row(A_WIDTH)],
        out_specs=[
            pl.BlockSpec((tm, COL_TILE), lambda j, i: (prev(i), j)),
            pl.BlockSpec((gv_rows, A_WIDTH), gv_block),
        ] + w_out_spec,
        out_shape=[
            jax.ShapeDtypeStruct((m, N_MAIN_TILES * COL_TILE), BF16),
            jax.ShapeDtypeStruct(((n_seq + 2) * gv_rows, A_WIDTH), F32),
        ] + w_out_shape,
        scratch_shapes=[pltpu.VMEM((tm, COL_TILE), F32)],
        compiler_params=_cparams(("arbitrary", "arbitrary")),
        name="proj_main",
    )(xn, *([w] * n_chunks), cq, sq, ci, si, ln_g, ln_b)
    return (h_main, gv[:n_seq * gv_rows], *w_bf)


def _proj_tail_body(*refs, transposed):
    x_refs, refs = refs[:NORM_STREAMS], refs[NORM_STREAMS:]
    (g_ref, wkv_ref, wix_ref, ck_ref, sk_ref, ci_ref, si_ref,
     xn_ref, k_ref, v_ref, ki_ref, kb_ref, kie_ref, kio_ref, vb_ref, ws_ref) = refs
    _prenorm_rows(x_refs, g_ref, xn_ref)
    xn = xn_ref[...]
    acc = lax.dot_general(xn, wkv_ref[...].astype(BF16), _NT, preferred_element_type=F32)
    cos = ck_ref[...]
    sin = sk_ref[...]
    for kh in range(N_KV_HEADS):
        sl = slice(kh * HEAD_DIM, (kh + 1) * HEAD_DIM)
        r = _rope128(acc[:, sl], cos, sin)
        k_ref[:, kh, :] = r
        kb_ref[:, sl] = r.astype(BF16)
        v_ref[:, kh, :] = acc[:, KV_WIDTH + kh * HEAD_DIM:KV_WIDTH + (kh + 1) * HEAD_DIM]
    t = lax.dot_general(xn, wix_ref[...].astype(BF16), _NT, preferred_element_type=F32)
    r = _rope64(t, ci_ref[...], si_ref[...])
    ki_ref[...] = r[:, :IDX_DIM]
    lane = lax.broadcasted_iota(jnp.int32, r.shape, 1)
    ke = jnp.where(lane < IDX_DIM, r, 0.0)
    kie_ref[...] = ke.astype(BF16)
    kio_ref[...] = pltpu.roll(ke, IDX_DIM, 1).astype(BF16)
    w_scale = IDX_HEADS ** -0.5 * IDX_DIM ** -0.5
    v = acc[:, KV_WIDTH:2 * KV_WIDTH]
    if transposed:
        vb_ref[...] = v.T.astype(BF16)
        ws_ref[...] = t.T[IDX_DIM:IDX_DIM + IDX_HEADS, :] * w_scale
    else:
        vb_ref[...] = v.astype(BF16)
        ws_ref[...] = t[:, IDX_DIM:IDX_DIM + IDX_HEADS] * w_scale


def _proj_tail(x2, pre_g, w_t, ck, sk, ci, si, *, tm, seq_rows, transposed):
    m = x2.shape[0]
    x_col = lambda c: pl.BlockSpec((tm, D_MODEL // NORM_STREAMS), lambda i: (i, c))
    if transposed:
        vw_specs = [pl.BlockSpec((n, tm), lambda i: (0, i)) for n in (KV_WIDTH, IDX_HEADS)]
        vw_shapes = [jax.ShapeDtypeStruct((KV_WIDTH, m), BF16), jax.ShapeDtypeStruct((IDX_HEADS, m), F32)]
    else:
        vw_specs = [pl.BlockSpec((tm, n), lambda i: (i, 0)) for n in (KV_WIDTH, IDX_HEADS)]
        vw_shapes = [jax.ShapeDtypeStruct((m, KV_WIDTH), BF16), jax.ShapeDtypeStruct((m, IDX_HEADS), F32)]
    tiles_per_seq = seq_rows // tm
    tab = pl.BlockSpec((tm, LANES), lambda i: (i % tiles_per_seq, 0))
    blk = lambda n: pl.BlockSpec((tm, n), lambda i: (i, 0))
    kv_rows = pl.BlockSpec((tm, N_KV_HEADS, HEAD_DIM), lambda i: (i, 0, 0))
    return pl.pallas_call(
        functools.partial(_proj_tail_body, transposed=transposed),
        grid=(m // tm,),
        in_specs=[x_col(c) for c in range(NORM_STREAMS)] + [
            pl.BlockSpec((1, D_MODEL), lambda i: (0, 0)),
            pl.BlockSpec((2 * KV_WIDTH, D_MODEL), lambda i: (ROW_K // (2 * KV_WIDTH), 0)),
            pl.BlockSpec((LANES, D_MODEL), lambda i: (ROW_KI // LANES, 0)),
            tab, tab, tab, tab,
        ],
        out_specs=[blk(D_MODEL), kv_rows, kv_rows, blk(IDX_DIM), blk(KV_WIDTH),
                   blk(LANES), blk(LANES)] + vw_specs,
        out_shape=[
            jax.ShapeDtypeStruct((m, D_MODEL), BF16),
            jax.ShapeDtypeStruct((m, N_KV_HEADS, HEAD_DIM), F32),
            jax.ShapeDtypeStruct((m, N_KV_HEADS, HEAD_DIM), F32),
            jax.ShapeDtypeStruct((m, IDX_DIM), F32),
            jax.ShapeDtypeStruct((m, KV_WIDTH), BF16),
            jax.ShapeDtypeStruct((m, LANES), BF16),
            jax.ShapeDtypeStruct((m, LANES), BF16),
        ] + vw_shapes,
        compiler_params=_cparams(("arbitrary",)),
        name="proj_tail",
    )(*([x2] * NORM_STREAMS), pre_g, w_t, w_t, ck, sk, ci, si)


def _gate_body(u_ref, vn_ref, sz_ref, ws_ref, bst_ref, y_ref, *, n_chunks):
    rr = lax.broadcasted_iota(jnp.int32, (CHUNK, CHUNK), 0)
    cc = lax.broadcasted_iota(jnp.int32, (CHUNK, CHUNK), 1)
    tril = cc <= rr
    for g in range(A_GROUPS):
        wm = jnp.where(tril, ws_ref[g], 0.0).astype(BF16)
        b = bst_ref[:, g:g + 1]
        cs = slice(g * LANES, (g + 1) * LANES)
        for c in range(n_chunks):
            rs = slice(c * CHUNK, (c + 1) * CHUNK)
            s = jnp.dot(wm, vn_ref[rs, cs], preferred_element_type=F32) + b
            y = u_ref[rs, cs].astype(F32) * s * sz_ref[rs, cs].astype(F32)
            y_ref[rs, cs] = y.astype(BF16)


def _gate_row_body(u_ref, vn_ref, sz_ref, w0_ref, b0_ref, y_ref):
    s = vn_ref[...] * w0_ref[...] + b0_ref[...]
    y_ref[...] = (u_ref[...].astype(F32) * s * sz_ref[...].astype(F32)).astype(BF16)


def _gate_row(h_main, vn, w0, b0):
    m = h_main.shape[0]
    col = lambda t: pl.BlockSpec((m, COL_TILE), lambda i, t=t: (0, t))
    full = lambda r: pl.BlockSpec((r, A_WIDTH), lambda i: (0, 0))
    return pl.pallas_call(
        _gate_row_body,
        grid=(1,),
        in_specs=[col(T_UA), full(m), col(T_ZA), full(1), full(1)],
        out_specs=full(m),
        out_shape=jax.ShapeDtypeStruct((m, A_WIDTH), BF16),
        compiler_params=_cparams(("arbitrary",)),
        name="gate_sample",
    )(h_main, vn, h_main, w0, b0)


def _lane_bcast(col, rows):
    return jnp.broadcast_to(col, (rows, LANES))


def _select_bias(sc_ref, extra_ref, kp, row_min, row_max, *, n_cols):
    rows = sc_ref.shape[0]
    n_tiles = n_cols // LANES
    extra = None if extra_ref is None else extra_ref[...]

    def count(pred):
        acc = jnp.zeros((rows, LANES), F32)
        for c in range(n_tiles):
            acc = acc + jnp.where(pred(sc_ref[:, c * LANES:(c + 1) * LANES], c), 1.0, 0.0)
        tot = jnp.sum(acc, axis=1, keepdims=True)
        return _lane_bcast(tot, rows)

    def count_ge(x):
        c = count(lambda s, _: s >= x)
        if extra is not None:
            c = c + jnp.where(extra >= x, 1.0, 0.0)
        return c

    c_max = count_ge(row_max)
    top = c_max >= kp
    lo0 = jnp.where(top, row_max, row_min)
    c0 = jnp.where(top, c_max, count_ge(row_min))

    def count_ge3(x1, x2, x3):
        a1 = jnp.zeros((rows, LANES), F32)
        a2 = jnp.zeros((rows, LANES), F32)
        a3 = jnp.zeros((rows, LANES), F32)
        for c in range(n_tiles):
            s = sc_ref[:, c * LANES:(c + 1) * LANES]
            a1 = a1 + jnp.where(s >= x1, 1.0, 0.0)
            a2 = a2 + jnp.where(s >= x2, 1.0, 0.0)
            a3 = a3 + jnp.where(s >= x3, 1.0, 0.0)
        res = []
        for a, x in ((a1, x1), (a2, x2), (a3, x3)):
            tot = _lane_bcast(jnp.sum(a, axis=1, keepdims=True), rows)
            if extra is not None:
                tot = tot + jnp.where(extra >= x, 1.0, 0.0)
            res.append(tot)
        return res

    def step(st):
        lo, hi, c_lo, _, it = st
        mid = 0.5 * lo + 0.5 * hi
        act = (c_lo != kp) & (mid > lo) & (mid < hi)
        any_act = jnp.max(jnp.where(act, 1.0, 0.0))
        clamp = lambda x: jnp.minimum(jnp.maximum(x, lo), hi)
        q1 = clamp(0.75 * lo + 0.25 * hi)
        q3 = clamp(0.25 * lo + 0.75 * hi)
        c1, c2, c3 = count_ge3(q1, mid, q3)
        g1, g2, g3 = c1 >= kp, c2 >= kp, c3 >= kp
        lo_n = jnp.where(g3, q3, jnp.where(g2, mid, jnp.where(g1, q1, lo)))
        c_n = jnp.where(g3, c3, jnp.where(g2, c2, jnp.where(g1, c1, c_lo)))
        hi_n = jnp.where(g3, hi, jnp.where(g2, q3, jnp.where(g1, mid, jnp.minimum(q1, mid))))
        return (jnp.where(act, lo_n, lo), jnp.where(act, hi_n, hi), jnp.where(act, c_n, c_lo),
                any_act, it + 1)

    def cond(st):
        return (st[3] > 0.0) & (st[4] < 400)

    lo, _, c_lo, _, _ = lax.while_loop(cond, step, (lo0, row_max, c0, jnp.float32(1.0), jnp.int32(0)))

    exact = jnp.max(jnp.where(c_lo != kp, 1.0, 0.0)) == 0.0

    @pl.when(exact)
    def _():
        for c in range(n_tiles):
            sl = slice(c * LANES, (c + 1) * LANES)
            sc_ref[:, sl] = jnp.where(sc_ref[:, sl] >= lo, 0.0, NEG)
        if extra is not None:
            extra_ref[...] = jnp.where(extra >= lo, 0.0, NEG)

    @pl.when(jnp.logical_not(exact))
    def _():
        n_gt = count(lambda s, _: s > lo)
        if extra is not None:
            n_gt = n_gt + jnp.where(extra > lo, 1.0, 0.0)
        need = kp - n_gt
        lane = lax.broadcasted_iota(jnp.int32, (rows, LANES), 1).astype(F32)

        def count_eq_upto(jx):
            c = count(lambda s, c: (s == lo) & (lane + float(c * LANES) <= jx))
            if extra is not None:
                c = c + jnp.where((extra == lo) & (jx >= float(n_cols)), 1.0, 0.0)
            return c

        last = n_cols if extra is not None else n_cols - 1
        j_lo = jnp.full((rows, LANES), -1.0, F32)
        j_hi = jnp.full((rows, LANES), float(last), F32)

        def jstep(_, st):
            a, b = st
            mid = jnp.floor(0.5 * (a + b))
            ok = count_eq_upto(mid) >= need
            return jnp.where(ok, a, mid), jnp.where(ok, mid, b)

        n_steps = max(1, (n_cols + 1).bit_length())
        _, j_hi = lax.fori_loop(0, n_steps, jstep, (j_lo, j_hi))
        for c in range(n_tiles):
            sl = slice(c * LANES, (c + 1) * LANES)
            s = sc_ref[:, sl]
            keep = (s > lo) | ((s == lo) & (lane + float(c * LANES) <= j_hi))
            sc_ref[:, sl] = jnp.where(keep, 0.0, NEG)
        if extra is not None:
            keep = (extra > lo) | ((extra == lo) & (j_hi >= float(n_cols)))
            extra_ref[...] = jnp.where(keep, 0.0, NEG)


def _select_bias_cols(sc_ref, kp, col_min, col_max, *, n_rows):
    n_tiles = n_rows // LANES
    n_pivots = 3 if n_rows <= 512 else 2 if n_rows <= 1024 else 1

    def counts(preds):
        accs = [jnp.zeros((LANES, LANES), F32) for _ in preds]
        for r in range(n_tiles):
            s = sc_ref[r * LANES:(r + 1) * LANES, :]
            accs = [a + jnp.where(p(s, r), 1.0, 0.0) for a, p in zip(accs, preds)]
        return [jnp.sum(a, axis=0, keepdims=True) for a in accs]

    count = lambda pred: counts([pred])[0]
    count_ge = lambda x: count(lambda s, _: s >= x)
    c_max = count_ge(col_max)
    top = c_max >= kp
    lo0 = jnp.where(top, col_max, col_min)
    c0 = jnp.where(top, c_max, count_ge(col_min))

    def step(st):
        lo, hi, c_lo, _, it = st
        mid = 0.5 * lo + 0.5 * hi
        act = (c_lo != kp) & (mid > lo) & (mid < hi)
        any_act = jnp.max(jnp.where(act, 1.0, 0.0))
        fr = [(k + 1) / (n_pivots + 1) for k in range(n_pivots)]
        piv = [mid if f == 0.5 else jnp.minimum(jnp.maximum((1.0 - f) * lo + f * hi, lo), hi) for f in fr]
        cs = counts([lambda s, _, x=x: s >= x for x in piv])
        lo_n, c_n, hi_n = lo, c_lo, functools.reduce(jnp.minimum, piv)
        for k in range(n_pivots):
            ge = cs[k] >= kp
            nxt = piv[k + 1] if k + 1 < n_pivots else hi
            lo_n = jnp.where(ge, piv[k], lo_n)
            c_n = jnp.where(ge, cs[k], c_n)
            hi_n = jnp.where(ge, nxt, hi_n)
        return jnp.where(act, lo_n, lo), jnp.where(act, hi_n, hi), jnp.where(act, c_n, c_lo), any_act, it + 1

    def cond(st):
        return (st[3] > 0.0) & (st[4] < 400)

    lo, _, c_lo, _, _ = lax.while_loop(cond, step, (lo0, col_max, c0, jnp.float32(1.0), jnp.int32(0)))
    exact = jnp.max(jnp.where(c_lo != kp, 1.0, 0.0)) == 0.0

    @pl.when(exact)
    def _():
        for r in range(n_tiles):
            rs = slice(r * LANES, (r + 1) * LANES)
            sc_ref[rs, :] = jnp.where(sc_ref[rs, :] >= lo, 0.0, NEG)

    @pl.when(jnp.logical_not(exact))
    def _():
        need = kp - count(lambda s, _: s > lo)
        key = lax.broadcasted_iota(jnp.int32, (LANES, LANES), 0).astype(F32)
        count_eq_upto = lambda jx: count(lambda s, r: (s == lo) & (key + float(r * LANES) <= jx))

        def jstep(_, st):
            a, b = st
            mid = jnp.floor(0.5 * (a + b))
            ok = count_eq_upto(mid) >= need
            return jnp.where(ok, a, mid), jnp.where(ok, mid, b)

        j_lo = jnp.full((1, LANES), -1.0, F32)
        j_hi = jnp.full((1, LANES), float(n_rows - 1), F32)
        _, j_hi = lax.fori_loop(0, max(1, n_rows.bit_length()), jstep, (j_lo, j_hi))
        for r in range(n_tiles):
            rs = slice(r * LANES, (r + 1) * LANES)
            s = sc_ref[rs, :]
            keep = (s > lo) | ((s == lo) & (key + float(r * LANES) <= j_hi))
            sc_ref[rs, :] = jnp.where(keep, 0.0, NEG)


def _prompt_attn_block(nk, qi_ref, wst_ref, kie_ref, kio_ref, q_ref, kb_ref, vt_ref, sz_ref,
                       o_ref, sc_ref, *, top_k, key_chunk):
    qb = pl.program_id(1)
    n_pairs = IDX_HEADS // 2
    grp = N_HEADS // N_KV_HEADS
    qpos = qb * Q_BLOCK + lax.broadcasted_iota(jnp.int32, (LANES, LANES), 1)
    key0 = lax.broadcasted_iota(jnp.int32, (LANES, LANES), 0)

    qs = jnp.concatenate([qi_ref[:, p * LANES:(p + 1) * LANES] for p in range(n_pairs)], axis=0)
    wrows = [wst_ref[h:h + 1, :] for h in range(IDX_HEADS)]
    cmax = jnp.full((LANES, LANES), -jnp.inf, F32)
    cmin = jnp.full((LANES, LANES), jnp.inf, F32)
    for kc in range(0, nk, key_chunk):
        le = lax.dot_general(kie_ref[kc:kc + key_chunk, :], qs, _NT, preferred_element_type=F32)
        lo = lax.dot_general(kio_ref[kc:kc + key_chunk, :], qs, _NT, preferred_element_type=F32)
        for r in range(key_chunk // LANES):
            rs = slice(r * LANES, (r + 1) * LANES)
            acc = jnp.zeros((LANES, LANES), F32)
            for p in range(n_pairs):
                cs = slice(p * LANES, (p + 1) * LANES)
                acc = acc + jnp.maximum(le[rs, cs], 0.0) * wrows[2 * p]
                acc = acc + jnp.maximum(lo[rs, cs], 0.0) * wrows[2 * p + 1]
            causal = key0 + (kc + r * LANES) <= qpos
            cmax = jnp.maximum(cmax, jnp.where(causal, acc, -jnp.inf))
            cmin = jnp.minimum(cmin, jnp.where(causal, acc, jnp.inf))
            sc_ref[kc + r * LANES:kc + (r + 1) * LANES, :] = jnp.where(causal, acc, -jnp.inf)

    col_max = jnp.max(cmax, axis=0, keepdims=True)
    col_min = jnp.min(cmin, axis=0, keepdims=True)
    kp = jnp.minimum(qpos[0:1, :] + 1, top_k).astype(F32)
    _select_bias_cols(sc_ref, kp, col_min, col_max, n_rows=nk)

    bias = jnp.concatenate([sc_ref[0:nk, :]] * grp, axis=1)
    for kh in range(N_KV_HEADS):
        qh = jnp.concatenate(
            [q_ref[:, (kh * grp + g) * HEAD_DIM:(kh * grp + g + 1) * HEAD_DIM] for g in range(grp)], axis=0)
        ks = slice(kh * HEAD_DIM, (kh + 1) * HEAD_DIM)
        s = lax.dot_general(kb_ref[0:nk, ks], qh, _NT, preferred_element_type=F32) + bias
        m = jnp.max(s, axis=0, keepdims=True)
        p = jnp.exp2((s - m) * (HEAD_DIM ** -0.5 * LOG2_E))
        l = jnp.sum(p, axis=0, keepdims=True)
        ot = jnp.dot(vt_ref[ks, 0:nk], p.astype(BF16), preferred_element_type=F32) / l
        for g in range(grp):
            hs = slice((kh * grp + g) * HEAD_DIM, (kh * grp + g + 1) * HEAD_DIM)
            o = ot[:, g * LANES:(g + 1) * LANES].T
            o_ref[:, hs] = (o * sz_ref[:, hs].astype(F32)).astype(BF16)


def _prompt_attn_body(*refs, seq, top_k, n_buckets, key_chunk):
    qb = pl.program_id(1)
    span = seq // n_buckets
    per = span // Q_BLOCK
    for c in range(n_buckets):
        @pl.when(qb // per == c)
        def _(c=c):
            _prompt_attn_block((c + 1) * span, *refs, top_k=top_k, key_chunk=key_chunk)


def _prompt_attn(h_main, ws_t, kie, kio, kb, vb_t, *, batch, seq):
    top_k = min(TOPK_MAX, seq // 4)
    n_qb = seq // Q_BLOCK
    key_chunk = min(512, seq)
    n_buckets = max(1, seq // 512)
    body = functools.partial(_prompt_attn_body, seq=seq, top_k=top_k, n_buckets=n_buckets, key_chunk=key_chunk)
    col = lambda t: pl.BlockSpec((Q_BLOCK, COL_TILE), lambda b, q, t=t: (b * n_qb + q, t))
    seqblk = lambda n: pl.BlockSpec((seq, n), lambda b, q: (b, 0))
    return pl.pallas_call(
        body,
        grid=(batch, n_qb),
        in_specs=[col(T_QI),
                  pl.BlockSpec((IDX_HEADS, Q_BLOCK), lambda b, q: (0, b * n_qb + q)),
                  seqblk(LANES), seqblk(LANES),
                  col(T_Q), seqblk(KV_WIDTH),
                  pl.BlockSpec((KV_WIDTH, seq), lambda b, q: (0, b)),
                  col(T_ZB)],
        out_specs=pl.BlockSpec((Q_BLOCK, B_WIDTH), lambda b, q: (b * n_qb + q, 0)),
        out_shape=jax.ShapeDtypeStruct((batch * seq, B_WIDTH), BF16),
        scratch_shapes=[pltpu.VMEM((seq, Q_BLOCK), F32)],
        compiler_params=_cparams(("arbitrary", "arbitrary")),
        name="attn_prompt",
    )(h_main, ws_t, kie, kio, h_main, kb, vb_t, h_main)


def _merge_body(x_ref, *refs, fused_gate):
    if fused_gate:
        gate_refs, refs = refs[:5], refs[5:]
        ya_ref = refs[-1]
        _gate_body(*gate_refs, ya_ref, n_chunks=x_ref.shape[0] // CHUNK)
        ob_ref, ga_ref, gb_ref, woa_ref, wob_ref, wout_ref, pg_ref, o_ref = refs[:-1]
    else:
        ya_ref, ob_ref, ga_ref, gb_ref, woa_ref, wob_ref, wout_ref, pg_ref, o_ref = refs
    pa = jnp.dot(ya_ref[...], woa_ref[...], preferred_element_type=F32)
    pb = jnp.dot(ob_ref[...], wob_ref[...], preferred_element_type=F32)
    mix = ga_ref[...].astype(F32) * pa + gb_ref[...].astype(F32) * pb
    r = jnp.dot(mix.astype(BF16), wout_ref[...], preferred_element_type=F32)
    o_ref[...] = x_ref[...] + _rmsnorm_rows(r, pg_ref[...])


def _merge(x2, ya, ob, h_main, w_oa, w_ob, w_out, post_g, *, tm, gate=None):
    m = x2.shape[0]
    const = lambda r, c: pl.BlockSpec((r, c), lambda i: (0, 0), pipeline_mode=pl.Buffered(1))
    if gate is None:
        a_specs = [pl.BlockSpec((tm, A_WIDTH), lambda i: (i, 0))]
        a_args, scratch = [ya], []
    else:
        col = lambda t: pl.BlockSpec((tm, COL_TILE), lambda i, t=t: (i, t))
        a_specs = [col(T_UA), col(T_VA), col(T_ZA),
                   pl.BlockSpec((A_GROUPS, CHUNK, CHUNK), lambda i: (0, 0, 0)),
                   pl.BlockSpec((CHUNK, A_GROUPS), lambda i: (0, 0))]
        a_args, scratch = [h_main, h_main, h_main, *gate], [pltpu.VMEM((tm, A_WIDTH), BF16)]
    return pl.pallas_call(
        functools.partial(_merge_body, fused_gate=gate is not None),
        grid=(m // tm,),
        in_specs=[pl.BlockSpec((tm, D_MODEL), lambda i: (i, 0))] + a_specs + [
            pl.BlockSpec((tm, B_WIDTH), lambda i: (i, 0)),
            pl.BlockSpec((tm, D_MODEL), lambda i: (i, T_GA // 2)),
            pl.BlockSpec((tm, D_MODEL), lambda i: (i, T_GB // 2)),
            const(A_WIDTH, D_MODEL), const(B_WIDTH, D_MODEL), const(D_MODEL, D_MODEL),
            const(1, D_MODEL),
        ],
        out_specs=pl.BlockSpec((tm, D_MODEL), lambda i: (i, 0)),
        out_shape=jax.ShapeDtypeStruct((m, D_MODEL), F32),
        scratch_shapes=scratch,
        compiler_params=_cparams(("arbitrary",)),
        name="merge",
    )(x2, *a_args, ob, h_main, h_main, w_oa, w_ob, w_out, post_g)


SCORE_CHUNK = 2048
COMPACT_SEQS = 4


def _sample_scores_body(pt_ref, q_ref, w_ref, kidx_hbm, o_ref, buf, sem):
    db, n_pages = pt_ref.shape
    past = n_pages * PAGE_SIZE

    def page_copy(b, p, slot):
        dst = buf.at[slot, :, pl.ds(pl.multiple_of(p * PAGE_SIZE, PAGE_SIZE), PAGE_SIZE)]
        return pltpu.make_async_copy(kidx_hbm.at[pt_ref[b, p]], dst, sem.at[slot])

    def start_all(b, slot):
        def f(p, c):
            page_copy(b, p, slot).start()
            return c
        lax.fori_loop(0, n_pages, f, 0, unroll=8)

    def wait_all(slot):
        for p in range(n_pages):
            dst = buf.at[slot, :, pl.ds(p * PAGE_SIZE, PAGE_SIZE)]
            pltpu.make_async_copy(kidx_hbm.at[0], dst, sem.at[slot]).wait()

    start_all(0, 0)

    def per_seq(b, c):
        slot = b % 2

        @pl.when(b + 1 < db)
        def _():
            start_all(b + 1, 1 - slot)

        wait_all(slot)
        q = q_ref[b]
        w = w_ref[b]
        for ch in range(past // SCORE_CHUNK):
            cs = slice(ch * SCORE_CHUNK, (ch + 1) * SCORE_CHUNK)
            logit = jnp.dot(q, buf[slot, :, cs].astype(BF16), preferred_element_type=F32)
            o_ref[pl.ds(b, 1), cs] = jnp.sum(jnp.maximum(logit, 0.0) * w, axis=0, keepdims=True)
        return c

    lax.fori_loop(0, db, per_seq, 0)


def _sample_scores(page_table, qi3, ws3, kidx_pages_t):
    db, n_pages = page_table.shape
    past = n_pages * PAGE_SIZE
    grid_spec = pltpu.PrefetchScalarGridSpec(
        num_scalar_prefetch=1,
        grid=(1,),
        in_specs=[pl.BlockSpec((db, IDX_HEADS, IDX_DIM), lambda i, pt: (0, 0, 0)),
                  pl.BlockSpec((db, IDX_HEADS, 1), lambda i, pt: (0, 0, 0)),
                  pl.BlockSpec(memory_space=pl.ANY)],
        out_specs=pl.BlockSpec((db, past), lambda i, pt: (0, 0)),
        scratch_shapes=[pltpu.VMEM((2, IDX_DIM, past), F32), pltpu.SemaphoreType.DMA((2,))],
    )
    return pl.pallas_call(
        _sample_scores_body,
        grid_spec=grid_spec,
        out_shape=jax.ShapeDtypeStruct((db, past), F32),
        compiler_params=_cparams(("arbitrary",)),
        name="scores_sample",
    )(page_table, qi3, ws3, kidx_pages_t)


def _sample_select_body(sc_ref, qi_ref, kie_ref, ws_ref, bias_ref, bnew_ref, *, top_k):
    rows, past = sc_ref.shape
    lane = lax.broadcasted_iota(jnp.int32, (rows, LANES), 1)
    ki = kie_ref[...].astype(F32)
    ki = ki + pltpu.roll(ki, IDX_DIM, 1)
    s_new = jnp.zeros((rows, 1), F32)
    for p in range(IDX_HEADS // 2):
        prod = qi_ref[:, p * LANES:(p + 1) * LANES].astype(F32) * ki
        l_even = jnp.sum(jnp.where(lane < IDX_DIM, prod, 0.0), axis=1, keepdims=True)
        l_odd = jnp.sum(jnp.where(lane >= IDX_DIM, prod, 0.0), axis=1, keepdims=True)
        s_new = s_new + jnp.maximum(l_even, 0.0) * ws_ref[:, 2 * p:2 * p + 1]
        s_new = s_new + jnp.maximum(l_odd, 0.0) * ws_ref[:, 2 * p + 1:2 * p + 2]
    extra = _lane_bcast(s_new, rows)
    bnew_ref[...] = extra
    rmax = extra
    rmin = extra
    for c in range(past // LANES):
        sl = slice(c * LANES, (c + 1) * LANES)
        s = sc_ref[:, sl]
        bias_ref[:, sl] = s
        rmax = jnp.maximum(rmax, s)
        rmin = jnp.minimum(rmin, s)
    row_max = _lane_bcast(jnp.max(rmax, axis=1, keepdims=True), rows)
    row_min = _lane_bcast(jnp.min(rmin, axis=1, keepdims=True), rows)
    kp = jnp.full((rows, LANES), float(top_k), F32)
    _select_bias(bias_ref, bnew_ref, kp, row_min, row_max, n_cols=past)


def _sample_select(scores, h_main, kie, ws, *, top_k):
    db, past = scores.shape
    full = lambda r, c: pl.BlockSpec((r, c), lambda i: (0, 0))
    return pl.pallas_call(
        functools.partial(_sample_select_body, top_k=top_k),
        grid=(1,),
        in_specs=[full(db, past),
                  pl.BlockSpec((db, COL_TILE), lambda i: (0, T_QI)),
                  full(db, LANES), full(db, IDX_HEADS)],
        out_specs=[full(db, past), full(db, LANES)],
        out_shape=[jax.ShapeDtypeStruct((db, past), F32), jax.ShapeDtypeStruct((db, LANES), F32)],
        compiler_params=_cparams(("arbitrary",)),
        name="select_sample",
    )(scores, h_main, kie, ws)


def _sample_compact_body(m_ref, mt_ref, pt_ref, idx_ref, row_ref, *, n_slots):
    n_pages = m_ref.shape[1]
    for q in range(m_ref.shape[0]):
        _compact_one(m_ref.at[q], mt_ref.at[q], pt_ref.at[q], idx_ref.at[q], row_ref.at[q],
                     n_pages=n_pages, n_slots=n_slots)


def _compact_one(m_ref, mt_ref, pt_ref, idx_ref, row_ref, *, n_pages, n_slots):
    pt = jnp.broadcast_to(pt_ref[...], (8, n_pages))
    pt_hi = (pt // PAGE_SIZE).astype(F32).astype(BF16)
    pt_lo = (pt % PAGE_SIZE).astype(F32).astype(BF16)
    one = lambda pred: jnp.where(pred, 1.0, 0.0)
    kept = m_ref[...] == 0.0
    kept_t = mt_ref[...] == 0.0
    ri = lax.broadcasted_iota(jnp.int32, (PAGE_SIZE, PAGE_SIZE), 0)
    ci = lax.broadcasted_iota(jnp.int32, (PAGE_SIZE, PAGE_SIZE), 1)
    rp = lax.broadcasted_iota(jnp.int32, (n_pages, n_pages), 0)
    cp = lax.broadcasted_iota(jnp.int32, (n_pages, n_pages), 1)
    plt = jnp.dot(one(ci <= ri).astype(BF16), one(kept_t).astype(BF16), preferred_element_type=F32)
    n_row = plt[PAGE_SIZE - 1:PAGE_SIZE, :]
    n_col = _lane_bcast(jnp.sum(one(kept), axis=1, keepdims=True), n_pages)
    e_col = jnp.dot(one(cp <= rp).astype(BF16), n_col.astype(BF16), preferred_element_type=F32)
    n_row8 = jnp.broadcast_to(n_row, (8, n_pages))
    e_row8 = jnp.dot(n_row8.astype(BF16), one(rp <= cp).astype(BF16), preferred_element_type=F32)
    off_row8 = e_row8 - n_row8
    n_total = e_col[n_pages - 1:n_pages, :]
    page_id = lax.broadcasted_iota(jnp.int32, (n_pages, LANES), 0).astype(F32)
    for jt in range(n_slots // LANES):
        j = (lax.broadcasted_iota(jnp.int32, (1, LANES), 1) + jt * LANES).astype(F32)
        page_j = jnp.sum(one(e_col <= j), axis=0, keepdims=True)
        pick = one(page_id == page_j).astype(BF16)
        prefix_j = jnp.dot(plt.astype(BF16), pick, preferred_element_type=F32)
        off_j = jnp.dot(off_row8.astype(BF16), pick, preferred_element_type=F32)[0:1]
        local_j = jnp.sum(one(prefix_j <= j - off_j), axis=0, keepdims=True)
        pos = page_j * float(PAGE_SIZE) + local_j
        phys = (jnp.dot(pt_hi, pick, preferred_element_type=F32)[0:1] * float(PAGE_SIZE)
                + jnp.dot(pt_lo, pick, preferred_element_type=F32)[0:1])
        row = phys * float(PAGE_SIZE) + local_j
        used = j < n_total
        sl = slice(jt * LANES, (jt + 1) * LANES)
        idx_ref[:, sl] = jnp.where(used, pos, -1.0).astype(jnp.int32)
        row_ref[:, sl] = jnp.where(used, row, 0.0).astype(jnp.int32)


def _sample_compact(bias3, bias3_t, pt3, *, n_slots):
    db, n_pages, _ = bias3.shape
    per = COMPACT_SEQS if db % COMPACT_SEQS == 0 else 1
    out = pl.BlockSpec((per, 1, n_slots), lambda b: (b, 0, 0))
    return pl.pallas_call(
        functools.partial(_sample_compact_body, n_slots=n_slots),
        grid=(db // per,),
        in_specs=[pl.BlockSpec((per, n_pages, PAGE_SIZE), lambda b: (b, 0, 0)),
                  pl.BlockSpec((per, PAGE_SIZE, n_pages), lambda b: (b, 0, 0)),
                  pl.BlockSpec((per, 1, n_pages), lambda b: (b, 0, 0))],
        out_specs=[out, out],
        out_shape=[jax.ShapeDtypeStruct((db, 1, n_slots), jnp.int32)] * 2,
        compiler_params=_cparams(("arbitrary",)),
        name="compact_sample",
    )(bias3, bias3_t, pt3)


def _sample_attn_body(row_ref, q_ref, slot_ref, bnew_ref, kn_ref, vn_ref, sz_ref, k_hbm, v_hbm,
                      o_ref, kbuf, vbuf, sem):
    b = pl.program_id(0)
    nb = pl.num_programs(0)
    n_slots = kbuf.shape[1]
    buf = b % 2
    grp = N_HEADS // N_KV_HEADS
    scale = HEAD_DIM ** -0.5

    def row_copies(seq, j, to):
        r = row_ref[seq, j]
        return (pltpu.make_async_copy(k_hbm.at[r], kbuf.at[to, j], sem.at[0, to]),
                pltpu.make_async_copy(v_hbm.at[r], vbuf.at[to, j], sem.at[1, to]))

    def start_all(seq, to):
        def f(j, c):
            ck, cv = row_copies(seq, j, to)
            ck.start()
            cv.start()
            return c
        lax.fori_loop(0, n_slots, f, 0, unroll=8)

    def wait_all(to):
        for blk in range(n_slots // PAGE_SIZE):
            rows = pl.ds(blk * PAGE_SIZE, PAGE_SIZE)
            src = pl.ds(0, PAGE_SIZE)
            pltpu.make_async_copy(k_hbm.at[src], kbuf.at[to, rows], sem.at[0, to]).wait()
            pltpu.make_async_copy(v_hbm.at[src], vbuf.at[to, rows], sem.at[1, to]).wait()

    @pl.when(b == 0)
    def _():
        start_all(0, 0)

    @pl.when(b + 1 < nb)
    def _():
        start_all(b + 1, 1 - buf)

    wait_all(buf)

    q = q_ref[0]
    head_s = lax.broadcasted_iota(jnp.int32, (N_HEADS, n_slots), 0)
    head_o = lax.broadcasted_iota(jnp.int32, (N_HEADS, HEAD_DIM), 0)
    k0, k1 = (kbuf[buf, :, kh, :].astype(BF16) for kh in range(N_KV_HEADS))
    v0, v1 = (vbuf[buf, :, kh, :].astype(BF16) for kh in range(N_KV_HEADS))
    s0 = lax.dot_general(q, k0, _NT, preferred_element_type=F32)
    s1 = lax.dot_general(q, k1, _NT, preferred_element_type=F32)
    s = jnp.where(head_s < grp, s0, s1) * scale + jnp.where(slot_ref[0] >= 0, 0.0, NEG)
    kn = kn_ref[0].astype(F32)
    vn = vn_ref[0].astype(F32)
    kn8 = jnp.where(head_o < grp, kn[:, :HEAD_DIM], kn[:, HEAD_DIM:])
    vn8 = jnp.where(head_o < grp, vn[:, :HEAD_DIM], vn[:, HEAD_DIM:])
    s_new = _lane_bcast(jnp.sum(q.astype(F32) * kn8, axis=-1, keepdims=True), N_HEADS) * scale + bnew_ref[0]
    m = jnp.maximum(_lane_bcast(jnp.max(s, axis=-1, keepdims=True), N_HEADS), s_new)
    p = jnp.exp(s - m[:, :1])
    p_new = jnp.exp(s_new - m)
    l = _lane_bcast(jnp.sum(p, axis=-1, keepdims=True), N_HEADS) + p_new
    pb = p.astype(BF16)
    pv = jnp.where(head_o < grp,
                   jnp.dot(pb, v0, preferred_element_type=F32),
                   jnp.dot(pb, v1, preferred_element_type=F32))
    o = (pv + p_new * vn8) / l
    o_ref[0] = (o * sz_ref[0].astype(F32)).astype(BF16)


def _sample_attn(rows, q3, slots3, bnew3, kn3, vn3, sz3, k_rows, v_rows):
    db, n_slots = rows.shape
    per_b = lambda r, c: pl.BlockSpec((1, r, c), lambda b, rw: (b, 0, 0))
    grid_spec = pltpu.PrefetchScalarGridSpec(
        num_scalar_prefetch=1,
        grid=(db,),
        in_specs=[per_b(N_HEADS, HEAD_DIM), per_b(1, n_slots), per_b(1, LANES),
                  per_b(1, KV_WIDTH), per_b(1, KV_WIDTH), per_b(N_HEADS, HEAD_DIM),
                  pl.BlockSpec(memory_space=pl.ANY), pl.BlockSpec(memory_space=pl.ANY)],
        out_specs=per_b(N_HEADS, HEAD_DIM),
        scratch_shapes=[pltpu.VMEM((2, n_slots, N_KV_HEADS, HEAD_DIM), F32),
                        pltpu.VMEM((2, n_slots, N_KV_HEADS, HEAD_DIM), F32),
                        pltpu.SemaphoreType.DMA((2, 2))],
    )
    return pl.pallas_call(
        _sample_attn_body,
        grid_spec=grid_spec,
        out_shape=jax.ShapeDtypeStruct((db, N_HEADS, HEAD_DIM), BF16),
        compiler_params=_cparams(("arbitrary",)),
        name="attn_sample",
    )(rows, q3, slots3, bnew3, kn3, vn3, sz3, k_rows, v_rows)


def _col_tile(h_main, t, n=1):
    return h_main[:, t * COL_TILE:(t + n) * COL_TILE]


def kernel(x_prompt, x_sample, cache_k, cache_v, cache_kidx, page_table, pre_g, w_in, a_ln_g, a_ln_b,
           a_ws, a_bs, w_oa, w_ob, w_out, post_g):
    batch, seq, _ = x_prompt.shape
    db, ds, _ = x_sample.shape
    depth = w_in.shape[0]
    n_pages = page_table.shape[1]
    past = n_pages * PAGE_SIZE
    assert ds == 1 and seq % PROJ_ROWS == 0 and past % SCORE_CHUNK == 0 and Q_BLOCK == LANES
    top_k_s = min(TOPK_MAX, (past + ds) // 4)
    assert top_k_s <= past and top_k_s % LANES == 0

    pos_p = jnp.arange(seq)
    pos_s = past + (jnp.arange(db * ds) % ds)
    tabs_p = _rope_tables(pos_p, HEAD_DIM) + _rope_tables(pos_p, IDX_DIM)
    tabs_s = _rope_tables(pos_s, HEAD_DIM) + _rope_tables(pos_s, IDX_DIM)

    hp = x_prompt.reshape(batch * seq, D_MODEL)
    hs = x_sample.reshape(db * ds, D_MODEL)
    outs = [[] for _ in range(8)]
    for l in range(depth):
        w_t = w_in[l].T
        woa, wob, wout = w_oa[l].astype(BF16), w_ob[l].astype(BF16), w_out[l].astype(BF16)
        g_pre, g_post = pre_g[l][None], post_g[l][None]
        ln_g, ln_b = a_ln_g[l][None], a_ln_b[l][None]

        xn, k, v, ki, kb, kie, kio, vb_t, ws_t = _proj_tail(hp, g_pre, w_t, *tabs_p, tm=PROJ_ROWS, seq_rows=seq,
                                                            transposed=True)
        h_main, gv, w_tiles = _proj_main(xn, w_t, *tabs_p, ln_g, ln_b, tm=PROJ_ROWS, seq_rows=seq,
                                         gv_rows=CHUNK, cast_weights=True)
        ob = _prompt_attn(h_main, ws_t, kie, kio, kb, vb_t, batch=batch, seq=seq)
        hp = _merge(hp, None, ob, h_main, woa, wob, wout, g_post, tm=MERGE_ROWS, gate=(a_ws[l], a_bs[l].T))
        outs[0].append(k.reshape(batch, seq, N_KV_HEADS, HEAD_DIM))
        outs[1].append(v.reshape(batch, seq, N_KV_HEADS, HEAD_DIM))
        outs[2].append(ki.reshape(batch, seq, IDX_DIM))
        outs[3].append(gv.reshape(batch, CHUNK, A_WIDTH))

        m_s = db * ds
        xn, k, v, ki, kb, kie, kio, vb, ws = _proj_tail(hs, g_pre, w_t, *tabs_s, tm=m_s, seq_rows=m_s,
                                                        transposed=False)
        h_main, gv = _proj_main(xn, w_tiles, *tabs_s, ln_g, ln_b, tm=m_s, seq_rows=m_s, gv_rows=m_s,
                                cast_weights=False)
        w0 = jnp.repeat(a_ws[l][:, 0, 0], LANES)[None]
        b0 = jnp.repeat(a_bs[l][:, 0], LANES)[None]
        ya = _gate_row(h_main, gv, w0, b0)
        qi3 = _col_tile(h_main, T_QI).reshape(db, IDX_HEADS, IDX_DIM)
        kidx_t = jnp.swapaxes(cache_kidx[l], 1, 2)
        scores = _sample_scores(page_table, qi3, ws.reshape(db, IDX_HEADS, 1), kidx_t)
        bias, bnew = _sample_select(scores, h_main, kie, ws, top_k=top_k_s)
        bias3 = bias.reshape(db, n_pages, PAGE_SIZE)
        slots3, rows3 = _sample_compact(bias3, jnp.swapaxes(bias3, 1, 2),
                                        page_table.reshape(db, 1, n_pages), n_slots=top_k_s)
        pool_rows = lambda c: c.reshape(-1, N_KV_HEADS, HEAD_DIM)
        ob = _sample_attn(rows3.reshape(db, top_k_s),
                          _col_tile(h_main, T_Q).reshape(db, N_HEADS, HEAD_DIM),
                          slots3, bnew.reshape(db, 1, LANES),
                          kb.reshape(db, 1, KV_WIDTH), vb.reshape(db, 1, KV_WIDTH),
                          _col_tile(h_main, T_ZB).reshape(db, N_HEADS, HEAD_DIM),
                          pool_rows(cache_k[l]), pool_rows(cache_v[l])).reshape(db, B_WIDTH)
        hs = _merge(hs, ya, ob, h_main, woa, wob, wout, g_post, tm=m_s)
        outs[4].append(k.reshape(db, ds, N_KV_HEADS, HEAD_DIM))
        outs[5].append(v.reshape(db, ds, N_KV_HEADS, HEAD_DIM))
        outs[6].append(ki.reshape(db, ds, IDX_DIM))
        outs[7].append(gv.reshape(db, ds, A_WIDTH))

    st = [jnp.stack(o, axis=0) for o in outs]
    return (hp.reshape(batch, seq, D_MODEL), hs.reshape(db, ds, D_MODEL),
            st[0], st[1], st[2], st[3], st[4], st[5], st[6], st[7])
```

```python
import functools

import jax
import jax.numpy as jnp
from jax import lax
from jax.experimental import pallas as pl
from jax.experimental.pallas import tpu as pltpu

F32 = jnp.float32
BF16 = jnp.bfloat16

D_MODEL = 2048
CHUNK = 128
A_GROUPS = 8
A_WIDTH = 1024
N_HEADS = 8
N_KV_HEADS = 2
HEAD_DIM = 128
B_WIDTH = 1024
KV_WIDTH = 256
IDX_HEADS = 16
IDX_DIM = 64
TOPK_MAX = 256
ROPE_THETA = 10000.0
EPS = 1e-6
PAGE_SIZE = 128
Q_BLOCK = 128
NEG = -1e30
LOG2_E = 1.4426950408889634

LANES = 128
BF16_SUBLANES = 16
MXU_WIDTH = 256
KEY_STEP = 256
PROJ_ROWS = 512
MERGE_ROWS = 256
NORM_STREAMS = 4
_NT = (((1,), (1,)), ((), ()))
COL_TILE = 1024
T_UA, T_VA, T_ZA, T_Q, T_ZB, T_QI, T_GA, T_GB = 0, 1, 2, 3, 4, 5, 6, 8
N_MAIN_TILES = 10
ROW_K = 4 * COL_TILE
ROW_ZB = ROW_K + 2 * KV_WIDTH
ROW_KI = ROW_ZB + B_WIDTH + IDX_HEADS * IDX_DIM
ROW_GA = ROW_KI + IDX_DIM + IDX_HEADS
VMEM_LIMIT = 56 * 1024 * 1024


def _cparams(sem):
    return pltpu.CompilerParams(dimension_semantics=sem, vmem_limit_bytes=VMEM_LIMIT)


def _rope_tables(pos, dim):
    half = dim // 2
    inv = ROPE_THETA ** (-jnp.arange(half, dtype=F32) / half)
    ang = pos.astype(F32)[:, None] * inv[None, :]
    cos = jnp.cos(ang)
    sin = jnp.sin(ang)
    reps = LANES // dim
    cos_t = jnp.tile(jnp.concatenate([cos, cos], axis=-1), (1, reps))
    sin_t = jnp.tile(jnp.concatenate([-sin, sin], axis=-1), (1, reps))
    return cos_t, sin_t


def _rope128(x, cos, sin):
    return x * cos + pltpu.roll(x, 64, 1) * sin


def _rope64(x, cos, sin):
    lane = lax.broadcasted_iota(jnp.int32, x.shape, 1)
    first = (lane % IDX_DIM) < (IDX_DIM // 2)
    partner = jnp.where(first, pltpu.roll(x, LANES - 32, 1), pltpu.roll(x, 32, 1))
    return x * cos + partner * sin


def _rmsnorm_rows(xf, g):
    ms = jnp.mean(xf * xf, axis=-1, keepdims=True)
    return xf * lax.rsqrt(ms + EPS) * g


_TILE_KINDS = ("copy", "ln", "silu", "rope128", "silu", "rope64", "sigmoid", "sigmoid", "sigmoid", "sigmoid")


def _sigmoid(x):
    return 0.5 * jnp.tanh(0.5 * x) + 0.5


def _prenorm_rows(x_refs, g_ref, o_ref):
    xs = [r[...] for r in x_refs]
    ms = sum(jnp.sum(x * x, axis=-1, keepdims=True) for x in xs) * (1.0 / D_MODEL)
    scale = lax.rsqrt(ms + EPS)
    w = xs[0].shape[1]
    for c, x in enumerate(xs):
        cs = slice(c * w, (c + 1) * w)
        o_ref[:, cs] = (x * scale * g_ref[:, cs]).astype(BF16)


def _proj_main_body(xn_ref, *refs, gv_rows, n_row_tiles, cast_weights):
    n_chunks = COL_TILE // MXU_WIDTH
    w_refs, refs = refs[:n_chunks], refs[n_chunks:]
    cq_ref, sq_ref, ci_ref, si_ref, lng_ref, lnb_ref, h_ref, gv_ref = refs[:8]
    wbf_ref = refs[8] if cast_weights else None
    acc_ref = refs[-1]
    j = pl.program_id(0)
    i = pl.program_id(1)
    tm = xn_ref.shape[0]
    chunks = [slice(c * MXU_WIDTH, (c + 1) * MXU_WIDTH) for c in range(n_chunks)]
    weights = lambda c: wbf_ref[chunks[c], :] if cast_weights else w_refs[c][...]

    def finish_chunk(kind, cs, stats):
        acc = acc_ref[:, cs]
        if kind == "copy":
            h_ref[:, cs] = acc.astype(BF16)
        elif kind == "ln":
            mu, rstd = stats
            vn = (acc - mu) * rstd * lng_ref[:, cs] + lnb_ref[:, cs]
            h_ref[:, cs] = vn.astype(BF16)
            gv_ref[:, cs] = vn[tm - gv_rows:, :]
        elif kind == "silu":
            h_ref[:, cs] = (acc * _sigmoid(acc)).astype(BF16)
        elif kind == "sigmoid":
            h_ref[:, cs] = _sigmoid(acc).astype(BF16)
        else:
            rope, cos, sin = ((_rope128, cq_ref[...], sq_ref[...]) if kind == "rope128"
                              else (_rope64, ci_ref[...], si_ref[...]))
            for h in range(MXU_WIDTH // LANES):
                sl = slice(cs.start + h * LANES, cs.start + (h + 1) * LANES)
                h_ref[:, sl] = rope(acc[:, h * LANES:(h + 1) * LANES], cos, sin).astype(BF16)

    def run(kind, finish, matmul):
        stats = None
        if finish and kind == "ln":
            acc = acc_ref[...]
            mu = jnp.mean(acc, axis=-1, keepdims=True)
            d = acc - mu
            stats = (mu, lax.rsqrt(jnp.mean(d * d, axis=-1, keepdims=True) + EPS))
        for c, cs in enumerate(chunks):
            if finish:
                finish_chunk(kind, cs, stats)
            if matmul:
                acc_ref[:, cs] = lax.dot_general(xn_ref[...], weights(c), _NT, preferred_element_type=F32)

    @pl.when(i == 0)
    def _():
        if cast_weights:
            for c, cs in enumerate(chunks):
                wbf_ref[cs, :] = w_refs[c][...].astype(BF16)
        run(None, False, True)

    @pl.when((i == 0) & (j != T_VA))
    def _():
        gv_ref[...] = jnp.zeros(gv_ref.shape, F32)

    for kind in sorted(set(_TILE_KINDS)):
        is_kind = functools.reduce(jnp.logical_or, [j == t for t, k in enumerate(_TILE_KINDS) if k == kind])

        @pl.when(is_kind & (i > 0) & (i < n_row_tiles))
        def _(kind=kind):
            run(kind, True, True)

        @pl.when(is_kind & (i == n_row_tiles))
        def _(kind=kind):
            run(kind, True, False)


def _main_tile_row(j, chunk):
    g = BF16_SUBLANES
    skip_kv = (ROW_ZB - T_ZB * COL_TILE) // g
    skip_idx = (ROW_GA - ROW_ZB - (T_GA - T_ZB) * COL_TILE) // g
    return (j * (COL_TILE // g) + chunk * (MXU_WIDTH // g)
            + jnp.where(j >= T_ZB, skip_kv, 0) + jnp.where(j >= T_GA, skip_idx, 0)) * g


def _proj_main(xn, w, cq, sq, ci, si, ln_g, ln_b, *, tm, seq_rows, gv_rows, cast_weights):
    m = xn.shape[0]
    tiles_per_seq = seq_rows // tm
    n_seq = m // seq_rows
    n_row_tiles = m // tm
    n_chunks = COL_TILE // MXU_WIDTH
    body = functools.partial(_proj_main_body, gv_rows=gv_rows, n_row_tiles=n_row_tiles,
                             cast_weights=cast_weights)
    w_tile = lambda j, i: jnp.minimum(jnp.where(i == n_row_tiles, j + 1, j), N_MAIN_TILES - 1)
    if cast_weights:
        w_chunk = lambda c: pl.BlockSpec((pl.Element(MXU_WIDTH), pl.Element(D_MODEL)),
                                         lambda j, i: (_main_tile_row(w_tile(j, i), c), 0))
    else:
        w_chunk = lambda c: pl.BlockSpec((MXU_WIDTH, D_MODEL), lambda j, i: (w_tile(j, i) * n_chunks + c, 0))
    w_out_spec = [pl.BlockSpec((COL_TILE, D_MODEL), lambda j, i: (j, 0))] if cast_weights else []
    w_out_shape = [jax.ShapeDtypeStruct((N_MAIN_TILES * COL_TILE, D_MODEL), BF16)] if cast_weights else []
    prev = lambda i: jnp.maximum(i - 1, 0)
    tab = pl.BlockSpec((tm, LANES), lambda j, i: (prev(i) % tiles_per_seq, 0))
    row = lambda n: pl.BlockSpec((1, n), lambda j, i: (0, 0))
    gv_block = lambda j, i: (jnp.where(j == T_VA, prev(i) // tiles_per_seq, n_seq + (j > T_VA)), 0)
    h_main, gv, *w_bf = pl.pallas_call(
        body,
        grid=(N_MAIN_TILES, n_row_tiles + 1),
        in_specs=[pl.BlockSpec((tm, D_MODEL), lambda j, i: (jnp.minimum(i, n_row_tiles - 1), 0))]
                 + [w_chunk(c) for c in range(n_chunks)]
                 + [tab, tab, tab, tab, row(A_WIDTH), row(A_WIDTH)],
        out_specs=[
            pl.BlockSpec((tm, COL_TILE), lambda j, i: (prev(i), j)),
            pl.BlockSpec((gv_rows, A_WIDTH), gv_block),
        ] + w_out_spec,
        out_shape=[
            jax.ShapeDtypeStruct((m, N_MAIN_TILES * COL_TILE), BF16),
            jax.ShapeDtypeStruct(((n_seq + 2) * gv_rows, A_WIDTH), F32),
        ] + w_out_shape,
        scratch_shapes=[pltpu.VMEM((tm, COL_TILE), F32)],
        compiler_params=_cparams(("arbitrary", "arbitrary")),
        name="proj_main",
    )(xn, *([w] * n_chunks), cq, sq, ci, si, ln_g, ln_b)
    return (h_main, gv[:n_seq * gv_rows], *w_bf)


def _proj_tail_body(*refs, transposed):
    x_refs, refs = refs[:NORM_STREAMS], refs[NORM_STREAMS:]
    (g_ref, wkv_ref, wix_ref, ck_ref, sk_ref, ci_ref, si_ref,
     xn_ref, k_ref, v_ref, ki_ref, kb_ref, kie_ref, kio_ref, vb_ref, ws_ref) = refs
    _prenorm_rows(x_refs, g_ref, xn_ref)
    xn = xn_ref[...]
    acc = lax.dot_general(xn, wkv_ref[...].astype(BF16), _NT, preferred_element_type=F32)
    cos = ck_ref[...]
    sin = sk_ref[...]
    for kh in range(N_KV_HEADS):
        sl = slice(kh * HEAD_DIM, (kh + 1) * HEAD_DIM)
        r = _rope128(acc[:, sl], cos, sin)
        k_ref[:, kh, :] = r
        kb_ref[:, sl] = r.astype(BF16)
        v_ref[:, kh, :] = acc[:, KV_WIDTH + kh * HEAD_DIM:KV_WIDTH + (kh + 1) * HEAD_DIM]
    t = lax.dot_general(xn, wix_ref[...].astype(BF16), _NT, preferred_element_type=F32)
    r = _rope64(t, ci_ref[...], si_ref[...])
    ki_ref[...] = r[:, :IDX_DIM]
    lane = lax.broadcasted_iota(jnp.int32, r.shape, 1)
    ke = jnp.where(lane < IDX_DIM, r, 0.0)
    kie_ref[...] = ke.astype(BF16)
    kio_ref[...] = pltpu.roll(ke, IDX_DIM, 1).astype(BF16)
    w_scale = IDX_HEADS ** -0.5 * IDX_DIM ** -0.5
    v = acc[:, KV_WIDTH:2 * KV_WIDTH]
    if transposed:
        vb_ref[...] = v.T.astype(BF16)
        ws_ref[...] = t.T[IDX_DIM:IDX_DIM + IDX_HEADS, :] * w_scale
    else:
        vb_ref[...] = v.astype(BF16)
        ws_ref[...] = t[:, IDX_DIM:IDX_DIM + IDX_HEADS] * w_scale


def _proj_tail(x2, pre_g, w_t, ck, sk, ci, si, *, tm, seq_rows, transposed):
    m = x2.shape[0]
    x_col = lambda c: pl.BlockSpec((tm, D_MODEL // NORM_STREAMS), lambda i: (i, c))
    if transposed:
        vw_specs = [pl.BlockSpec((n, tm), lambda i: (0, i)) for n in (KV_WIDTH, IDX_HEADS)]
        vw_shapes = [jax.ShapeDtypeStruct((KV_WIDTH, m), BF16), jax.ShapeDtypeStruct((IDX_HEADS, m), F32)]
    else:
        vw_specs = [pl.BlockSpec((tm, n), lambda i: (i, 0)) for n in (KV_WIDTH, IDX_HEADS)]
        vw_shapes = [jax.ShapeDtypeStruct((m, KV_WIDTH), BF16), jax.ShapeDtypeStruct((m, IDX_HEADS), F32)]
    tiles_per_seq = seq_rows // tm
    tab = pl.BlockSpec((tm, LANES), lambda i: (i % tiles_per_seq, 0))
    blk = lambda n: pl.BlockSpec((tm, n), lambda i: (i, 0))
    kv_rows = pl.BlockSpec((tm, N_KV_HEADS, HEAD_DIM), lambda i: (i, 0, 0))
    return pl.pallas_call(
        functools.partial(_proj_tail_body, transposed=transposed),
        grid=(m // tm,),
        in_specs=[x_col(c) for c in range(NORM_STREAMS)] + [
            pl.BlockSpec((1, D_MODEL), lambda i: (0, 0)),
            pl.BlockSpec((2 * KV_WIDTH, D_MODEL), lambda i: (ROW_K // (2 * KV_WIDTH), 0)),
            pl.BlockSpec((LANES, D_MODEL), lambda i: (ROW_KI // LANES, 0)),
            tab, tab, tab, tab,
        ],
        out_specs=[blk(D_MODEL), kv_rows, kv_rows, blk(IDX_DIM), blk(KV_WIDTH),
                   blk(LANES), blk(LANES)] + vw_specs,
        out_shape=[
            jax.ShapeDtypeStruct((m, D_MODEL), BF16),
            jax.ShapeDtypeStruct((m, N_KV_HEADS, HEAD_DIM), F32),
            jax.ShapeDtypeStruct((m, N_KV_HEADS, HEAD_DIM), F32),
            jax.ShapeDtypeStruct((m, IDX_DIM), F32),
            jax.ShapeDtypeStruct((m, KV_WIDTH), BF16),
            jax.ShapeDtypeStruct((m, LANES), BF16),
            jax.ShapeDtypeStruct((m, LANES), BF16),
        ] + vw_shapes,
        compiler_params=_cparams(("arbitrary",)),
        name="proj_tail",
    )(*([x2] * NORM_STREAMS), pre_g, w_t, w_t, ck, sk, ci, si)


def _gate_body(u_ref, vn_ref, sz_ref, ws_ref, bst_ref, y_ref, *, n_chunks):
    rr = lax.broadcasted_iota(jnp.int32, (CHUNK, CHUNK), 0)
    cc = lax.broadcasted_iota(jnp.int32, (CHUNK, CHUNK), 1)
    tril = cc <= rr
    for g in range(A_GROUPS):
        wm = jnp.where(tril, ws_ref[g], 0.0).astype(BF16)
        b = bst_ref[:, g:g + 1]
        cs = slice(g * LANES, (g + 1) * LANES)
        for c in range(n_chunks):
            rs = slice(c * CHUNK, (c + 1) * CHUNK)
            s = jnp.dot(wm, vn_ref[rs, cs], preferred_element_type=F32) + b
            y = u_ref[rs, cs].astype(F32) * s * sz_ref[rs, cs].astype(F32)
            y_ref[rs, cs] = y.astype(BF16)


def _gate_row_body(u_ref, vn_ref, sz_ref, w0_ref, b0_ref, y_ref):
    s = vn_ref[...] * w0_ref[...] + b0_ref[...]
    y_ref[...] = (u_ref[...].astype(F32) * s * sz_ref[...].astype(F32)).astype(BF16)


def _gate_row(h_main, vn, w0, b0):
    m = h_main.shape[0]
    col = lambda t: pl.BlockSpec((m, COL_TILE), lambda i, t=t: (0, t))
    full = lambda r: pl.BlockSpec((r, A_WIDTH), lambda i: (0, 0))
    return pl.pallas_call(
        _gate_row_body,
        grid=(1,),
        in_specs=[col(T_UA), full(m), col(T_ZA), full(1), full(1)],
        out_specs=full(m),
        out_shape=jax.ShapeDtypeStruct((m, A_WIDTH), BF16),
        compiler_params=_cparams(("arbitrary",)),
        name="gate_sample",
    )(h_main, vn, h_main, w0, b0)


def _lane_bcast(col, rows):
    return jnp.broadcast_to(col, (rows, LANES))


def _select_bias(sc_ref, extra_ref, kp, row_min, row_max, *, n_cols):
    rows = sc_ref.shape[0]
    n_tiles = n_cols // LANES
    extra = None if extra_ref is None else extra_ref[...]

    def count(pred):
        acc = jnp.zeros((rows, LANES), F32)
        for c in range(n_tiles):
            acc = acc + jnp.where(pred(sc_ref[:, c * LANES:(c + 1) * LANES], c), 1.0, 0.0)
        tot = jnp.sum(acc, axis=1, keepdims=True)
        return _lane_bcast(tot, rows)

    def count_ge(x):
        c = count(lambda s, _: s >= x)
        if extra is not None:
            c = c + jnp.where(extra >= x, 1.0, 0.0)
        return c

    c_max = count_ge(row_max)
    top = c_max >= kp
    lo0 = jnp.where(top, row_max, row_min)
    c0 = jnp.where(top, c_max, count_ge(row_min))

    def count_ge3(x1, x2, x3):
        a1 = jnp.zeros((rows, LANES), F32)
        a2 = jnp.zeros((rows, LANES), F32)
        a3 = jnp.zeros((rows, LANES), F32)
        for c in range(n_tiles):
            s = sc_ref[:, c * LANES:(c + 1) * LANES]
            a1 = a1 + jnp.where(s >= x1, 1.0, 0.0)
            a2 = a2 + jnp.where(s >= x2, 1.0, 0.0)
            a3 = a3 + jnp.where(s >= x3, 1.0, 0.0)
        res = []
        for a, x in ((a1, x1), (a2, x2), (a3, x3)):
            tot = _lane_bcast(jnp.sum(a, axis=1, keepdims=True), rows)
            if extra is not None:
                tot = tot + jnp.where(extra >= x, 1.0, 0.0)
            res.append(tot)
        return res

    def step(st):
        lo, hi, c_lo, _, it = st
        mid = 0.5 * lo + 0.5 * hi
        act = (c_lo != kp) & (mid > lo) & (mid < hi)
        any_act = jnp.max(jnp.where(act, 1.0, 0.0))
        clamp = lambda x: jnp.minimum(jnp.maximum(x, lo), hi)
        q1 = clamp(0.75 * lo + 0.25 * hi)
        q3 = clamp(0.25 * lo + 0.75 * hi)
        c1, c2, c3 = count_ge3(q1, mid, q3)
        g1, g2, g3 = c1 >= kp, c2 >= kp, c3 >= kp
        lo_n = jnp.where(g3, q3, jnp.where(g2, mid, jnp.where(g1, q1, lo)))
        c_n = jnp.where(g3, c3, jnp.where(g2, c2, jnp.where(g1, c1, c_lo)))
        hi_n = jnp.where(g3, hi, jnp.where(g2, q3, jnp.where(g1, mid, jnp.minimum(q1, mid))))
        return (jnp.where(act, lo_n, lo), jnp.where(act, hi_n, hi), jnp.where(act, c_n, c_lo),
                any_act, it + 1)

    def cond(st):
        return (st[3] > 0.0) & (st[4] < 400)

    lo, _, c_lo, _, _ = lax.while_loop(cond, step, (lo0, row_max, c0, jnp.float32(1.0), jnp.int32(0)))

    exact = jnp.max(jnp.where(c_lo != kp, 1.0, 0.0)) == 0.0

    @pl.when(exact)
    def _():
        for c in range(n_tiles):
            sl = slice(c * LANES, (c + 1) * LANES)
            sc_ref[:, sl] = jnp.where(sc_ref[:, sl] >= lo, 0.0, NEG)
        if extra is not None:
            extra_ref[...] = jnp.where(extra >= lo, 0.0, NEG)

    @pl.when(jnp.logical_not(exact))
    def _():
        n_gt = count(lambda s, _: s > lo)
        if extra is not None:
            n_gt = n_gt + jnp.where(extra > lo, 1.0, 0.0)
        need = kp - n_gt
        lane = lax.broadcasted_iota(jnp.int32, (rows, LANES), 1).astype(F32)

        def count_eq_upto(jx):
            c = count(lambda s, c: (s == lo) & (lane + float(c * LANES) <= jx))
            if extra is not None:
                c = c + jnp.where((extra == lo) & (jx >= float(n_cols)), 1.0, 0.0)
            return c

        last = n_cols if extra is not None else n_cols - 1
        j_lo = jnp.full((rows, LANES), -1.0, F32)
        j_hi = jnp.full((rows, LANES), float(last), F32)

        def jstep(_, st):
            a, b = st
            mid = jnp.floor(0.5 * (a + b))
            ok = count_eq_upto(mid) >= need
            return jnp.where(ok, a, mid), jnp.where(ok, mid, b)

        n_steps = max(1, (n_cols + 1).bit_length())
        _, j_hi = lax.fori_loop(0, n_steps, jstep, (j_lo, j_hi))
        for c in range(n_tiles):
            sl = slice(c * LANES, (c + 1) * LANES)
            s = sc_ref[:, sl]
            keep = (s > lo) | ((s == lo) & (lane + float(c * LANES) <= j_hi))
            sc_ref[:, sl] = jnp.where(keep, 0.0, NEG)
        if extra is not None:
            keep = (extra > lo) | ((extra == lo) & (j_hi >= float(n_cols)))
            extra_ref[...] = jnp.where(keep, 0.0, NEG)


def _select_bias_cols(sc_ref, kp, col_min, col_max, *, n_rows):
    n_tiles = n_rows // LANES
    n_pivots = 3 if n_rows <= 512 else 2 if n_rows <= 1024 else 1

    def counts(preds):
        accs = [jnp.zeros((LANES, LANES), F32) for _ in preds]
        for r in range(n_tiles):
            s = sc_ref[r * LANES:(r + 1) * LANES, :]
            accs = [a + jnp.where(p(s, r), 1.0, 0.0) for a, p in zip(accs, preds)]
        return [jnp.sum(a, axis=0, keepdims=True) for a in accs]

    count = lambda pred: counts([pred])[0]
    count_ge = lambda x: count(lambda s, _: s >= x)
    c_max = count_ge(col_max)
    top = c_max >= kp
    lo0 = jnp.where(top, col_max, col_min)
    c0 = jnp.where(top, c_max, count_ge(col_min))

    def step(st):
        lo, hi, c_lo, _, it = st
        mid = 0.5 * lo + 0.5 * hi
        act = (c_lo != kp) & (mid > lo) & (mid < hi)
        any_act = jnp.max(jnp.where(act, 1.0, 0.0))
        fr = [(k + 1) / (n_pivots + 1) for k in range(n_pivots)]
        piv = [mid if f == 0.5 else jnp.minimum(jnp.maximum((1.0 - f) * lo + f * hi, lo), hi) for f in fr]
        cs = counts([lambda s, _, x=x: s >= x for x in piv])
        lo_n, c_n, hi_n = lo, c_lo, functools.reduce(jnp.minimum, piv)
        for k in range(n_pivots):
            ge = cs[k] >= kp
            nxt = piv[k + 1] if k + 1 < n_pivots else hi
            lo_n = jnp.where(ge, piv[k], lo_n)
            c_n = jnp.where(ge, cs[k], c_n)
            hi_n = jnp.where(ge, nxt, hi_n)
        return jnp.where(act, lo_n, lo), jnp.where(act, hi_n, hi), jnp.where(act, c_n, c_lo), any_act, it + 1

    def cond(st):
        return (st[3] > 0.0) & (st[4] < 400)

    lo, _, c_lo, _, _ = lax.while_loop(cond, step, (lo0, col_max, c0, jnp.float32(1.0), jnp.int32(0)))
    exact = jnp.max(jnp.where(c_lo != kp, 1.0, 0.0)) == 0.0

    @pl.when(exact)
    def _():
        for r in range(n_tiles):
            rs = slice(r * LANES, (r + 1) * LANES)
            sc_ref[rs, :] = jnp.where(sc_ref[rs, :] >= lo, 0.0, NEG)

    @pl.when(jnp.logical_not(exact))
    def _():
        need = kp - count(lambda s, _: s > lo)
        key = lax.broadcasted_iota(jnp.int32, (LANES, LANES), 0).astype(F32)
        count_eq_upto = lambda jx: count(lambda s, r: (s == lo) & (key + float(r * LANES) <= jx))

        def jstep(_, st):
            a, b = st
            mid = jnp.floor(0.5 * (a + b))
            ok = count_eq_upto(mid) >= need
            return jnp.where(ok, a, mid), jnp.where(ok, mid, b)

        j_lo = jnp.full((1, LANES), -1.0, F32)
        j_hi = jnp.full((1, LANES), float(n_rows - 1), F32)
        _, j_hi = lax.fori_loop(0, max(1, n_rows.bit_length()), jstep, (j_lo, j_hi))
        for r in range(n_tiles):
            rs = slice(r * LANES, (r + 1) * LANES)
            s = sc_ref[rs, :]
            keep = (s > lo) | ((s == lo) & (key + float(r * LANES) <= j_hi))
            sc_ref[rs, :] = jnp.where(keep, 0.0, NEG)


def _prompt_attn_block(nk, qi_ref, wst_ref, kie_ref, kio_ref, q_ref, kb_ref, vt_ref, sz_ref,
                       o_ref, sc_ref, *, top_k, key_chunk):
    qb = pl.program_id(1)
    n_pairs = IDX_HEADS // 2
    grp = N_HEADS // N_KV_HEADS
    qpos = qb * Q_BLOCK + lax.broadcasted_iota(jnp.int32, (LANES, LANES), 1)
    key0 = lax.broadcasted_iota(jnp.int32, (LANES, LANES), 0)

    qs = jnp.concatenate([qi_ref[:, p * LANES:(p + 1) * LANES] for p in range(n_pairs)], axis=0)
    wrows = [wst_ref[h:h + 1, :] for h in range(IDX_HEADS)]
    cmax = jnp.full((LANES, LANES), -jnp.inf, F32)
    cmin = jnp.full((LANES, LANES), jnp.inf, F32)
    for kc in range(0, nk, key_chunk):
        le = lax.dot_general(kie_ref[kc:kc + key_chunk, :], qs, _NT, preferred_element_type=F32)
        lo = lax.dot_general(kio_ref[kc:kc + key_chunk, :], qs, _NT, preferred_element_type=F32)
        for r in range(key_chunk // LANES):
            rs = slice(r * LANES, (r + 1) * LANES)
            acc = jnp.zeros((LANES, LANES), F32)
            for p in range(n_pairs):
                cs = slice(p * LANES, (p + 1) * LANES)
                acc = acc + jnp.maximum(le[rs, cs], 0.0) * wrows[2 * p]
                acc = acc + jnp.maximum(lo[rs, cs], 0.0) * wrows[2 * p + 1]
            causal = key0 + (kc + r * LANES) <= qpos
            cmax = jnp.maximum(cmax, jnp.where(causal, acc, -jnp.inf))
            cmin = jnp.minimum(cmin, jnp.where(causal, acc, jnp.inf))
            sc_ref[kc + r * LANES:kc + (r + 1) * LANES, :] = jnp.where(causal, acc, -jnp.inf)

    col_max = jnp.max(cmax, axis=0, keepdims=True)
    col_min = jnp.min(cmin, axis=0, keepdims=True)
    kp = jnp.minimum(qpos[0:1, :] + 1, top_k).astype(F32)
    _select_bias_cols(sc_ref, kp, col_min, col_max, n_rows=nk)

    bias = jnp.concatenate([sc_ref[0:nk, :]] * grp, axis=1)
    for kh in range(N_KV_HEADS):
        qh = jnp.concatenate(
            [q_ref[:, (kh * grp + g) * HEAD_DIM:(kh * grp + g + 1) * HEAD_DIM] for g in range(grp)], axis=0)
        ks = slice(kh * HEAD_DIM, (kh + 1) * HEAD_DIM)
        s = lax.dot_general(kb_ref[0:nk, ks], qh, _NT, preferred_element_type=F32) + bias
        m = jnp.max(s, axis=0, keepdims=True)
        p = jnp.exp2((s - m) * (HEAD_DIM ** -0.5 * LOG2_E))
        l = jnp.sum(p, axis=0, keepdims=True)
        ot = jnp.dot(vt_ref[ks, 0:nk], p.astype(BF16), preferred_element_type=F32) / l
        for g in range(grp):
            hs = slice((kh * grp + g) * HEAD_DIM, (kh * grp + g + 1) * HEAD_DIM)
            o = ot[:, g * LANES:(g + 1) * LANES].T
            o_ref[:, hs] = (o * sz_ref[:, hs].astype(F32)).astype(BF16)


def _key_ranges(seq):
    fine = list(range(KEY_STEP, min(seq, 4 * KEY_STEP) + 1, KEY_STEP))
    return fine + list(range(fine[-1] + 2 * KEY_STEP, seq + 1, 2 * KEY_STEP))


def _prompt_attn_body(*refs, seq, top_k):
    need = (pl.program_id(1) + 1) * Q_BLOCK
    lower = 0
    for nk in _key_ranges(seq):
        @pl.when((need > lower) & (need <= nk))
        def _(nk=nk):
            _prompt_attn_block(nk, *refs, top_k=top_k, key_chunk=KEY_STEP)
        lower = nk


def _prompt_attn(h_main, ws_t, kie, kio, kb, vb_t, *, batch, seq):
    top_k = min(TOPK_MAX, seq // 4)
    n_qb = seq // Q_BLOCK
    body = functools.partial(_prompt_attn_body, seq=seq, top_k=top_k)
    col = lambda t: pl.BlockSpec((Q_BLOCK, COL_TILE), lambda b, q, t=t: (b * n_qb + q, t))
    seqblk = lambda n: pl.BlockSpec((seq, n), lambda b, q: (b, 0))
    return pl.pallas_call(
        body,
        grid=(batch, n_qb),
        in_specs=[col(T_QI),
                  pl.BlockSpec((IDX_HEADS, Q_BLOCK), lambda b, q: (0, b * n_qb + q)),
                  seqblk(LANES), seqblk(LANES),
                  col(T_Q), seqblk(KV_WIDTH),
                  pl.BlockSpec((KV_WIDTH, seq), lambda b, q: (0, b)),
                  col(T_ZB)],
        out_specs=pl.BlockSpec((Q_BLOCK, B_WIDTH), lambda b, q: (b * n_qb + q, 0)),
        out_shape=jax.ShapeDtypeStruct((batch * seq, B_WIDTH), BF16),
        scratch_shapes=[pltpu.VMEM((seq, Q_BLOCK), F32)],
        compiler_params=_cparams(("arbitrary", "arbitrary")),
        name="attn_prompt",
    )(h_main, ws_t, kie, kio, h_main, kb, vb_t, h_main)


def _merge_body(x_ref, *refs, fused_gate):
    if fused_gate:
        gate_refs, refs = refs[:5], refs[5:]
        ya_ref = refs[-1]
        _gate_body(*gate_refs, ya_ref, n_chunks=x_ref.shape[0] // CHUNK)
        ob_ref, ga_ref, gb_ref, woa_ref, wob_ref, wout_ref, pg_ref, o_ref = refs[:-1]
    else:
        ya_ref, ob_ref, ga_ref, gb_ref, woa_ref, wob_ref, wout_ref, pg_ref, o_ref = refs
    pa = jnp.dot(ya_ref[...], woa_ref[...], preferred_element_type=F32)
    pb = jnp.dot(ob_ref[...], wob_ref[...], preferred_element_type=F32)
    mix = ga_ref[...].astype(F32) * pa + gb_ref[...].astype(F32) * pb
    r = jnp.dot(mix.astype(BF16), wout_ref[...], preferred_element_type=F32)
    o_ref[...] = x_ref[...] + _rmsnorm_rows(r, pg_ref[...])


def _merge(x2, ya, ob, h_main, w_oa, w_ob, w_out, post_g, *, tm, gate=None):
    m = x2.shape[0]
    const = lambda r, c: pl.BlockSpec((r, c), lambda i: (0, 0), pipeline_mode=pl.Buffered(1))
    if gate is None:
        a_specs = [pl.BlockSpec((tm, A_WIDTH), lambda i: (i, 0))]
        a_args, scratch = [ya], []
    else:
        col = lambda t: pl.BlockSpec((tm, COL_TILE), lambda i, t=t: (i, t))
        a_specs = [col(T_UA), col(T_VA), col(T_ZA),
                   pl.BlockSpec((A_GROUPS, CHUNK, CHUNK), lambda i: (0, 0, 0)),
                   pl.BlockSpec((CHUNK, A_GROUPS), lambda i: (0, 0))]
        a_args, scratch = [h_main, h_main, h_main, *gate], [pltpu.VMEM((tm, A_WIDTH), BF16)]
    return pl.pallas_call(
        functools.partial(_merge_body, fused_gate=gate is not None),
        grid=(m // tm,),
        in_specs=[pl.BlockSpec((tm, D_MODEL), lambda i: (i, 0))] + a_specs + [
            pl.BlockSpec((tm, B_WIDTH), lambda i: (i, 0)),
            pl.BlockSpec((tm, D_MODEL), lambda i: (i, T_GA // 2)),
            pl.BlockSpec((tm, D_MODEL), lambda i: (i, T_GB // 2)),
            const(A_WIDTH, D_MODEL), const(B_WIDTH, D_MODEL), const(D_MODEL, D_MODEL),
            const(1, D_MODEL),
        ],
        out_specs=pl.BlockSpec((tm, D_MODEL), lambda i: (i, 0)),
        out_shape=jax.ShapeDtypeStruct((m, D_MODEL), F32),
        scratch_shapes=scratch,
        compiler_params=_cparams(("arbitrary",)),
        name="merge",
    )(x2, *a_args, ob, h_main, h_main, w_oa, w_ob, w_out, post_g)


SCORE_CHUNK = 2048
COMPACT_SEQS = 4


def _sample_scores_body(pt_ref, q_ref, w_ref, kidx_hbm, o_ref, buf, sem):
    db, n_pages = pt_ref.shape
    past = n_pages * PAGE_SIZE

    def page_copy(b, p, slot):
        dst = buf.at[slot, :, pl.ds(pl.multiple_of(p * PAGE_SIZE, PAGE_SIZE), PAGE_SIZE)]
        return pltpu.make_async_copy(kidx_hbm.at[pt_ref[b, p]], dst, sem.at[slot])

    def start_all(b, slot):
        def f(p, c):
            page_copy(b, p, slot).start()
            return c
        lax.fori_loop(0, n_pages, f, 0, unroll=8)

    def wait_all(slot):
        for p in range(n_pages):
            dst = buf.at[slot, :, pl.ds(p * PAGE_SIZE, PAGE_SIZE)]
            pltpu.make_async_copy(kidx_hbm.at[0], dst, sem.at[slot]).wait()

    start_all(0, 0)

    def per_seq(b, c):
        slot = b % 2

        @pl.when(b + 1 < db)
        def _():
            start_all(b + 1, 1 - slot)

        wait_all(slot)
        q = q_ref[b]
        w = w_ref[b]
        for ch in range(past // SCORE_CHUNK):
            cs = slice(ch * SCORE_CHUNK, (ch + 1) * SCORE_CHUNK)
            logit = jnp.dot(q, buf[slot, :, cs].astype(BF16), preferred_element_type=F32)
            o_ref[pl.ds(b, 1), cs] = jnp.sum(jnp.maximum(logit, 0.0) * w, axis=0, keepdims=True)
        return c

    lax.fori_loop(0, db, per_seq, 0)


def _sample_scores(page_table, qi3, ws3, kidx_pages_t):
    db, n_pages = page_table.shape
    past = n_pages * PAGE_SIZE
    grid_spec = pltpu.PrefetchScalarGridSpec(
        num_scalar_prefetch=1,
        grid=(1,),
        in_specs=[pl.BlockSpec((db, IDX_HEADS, IDX_DIM), lambda i, pt: (0, 0, 0)),
                  pl.BlockSpec((db, IDX_HEADS, 1), lambda i, pt: (0, 0, 0)),
                  pl.BlockSpec(memory_space=pl.ANY)],
        out_specs=pl.BlockSpec((db, past), lambda i, pt: (0, 0)),
        scratch_shapes=[pltpu.VMEM((2, IDX_DIM, past), F32), pltpu.SemaphoreType.DMA((2,))],
    )
    return pl.pallas_call(
        _sample_scores_body,
        grid_spec=grid_spec,
        out_shape=jax.ShapeDtypeStruct((db, past), F32),
        compiler_params=_cparams(("arbitrary",)),
        name="scores_sample",
    )(page_table, qi3, ws3, kidx_pages_t)


def _sample_select_body(sc_ref, qi_ref, kie_ref, ws_ref, bias_ref, bnew_ref, *, top_k):
    rows, past = sc_ref.shape
    lane = lax.broadcasted_iota(jnp.int32, (rows, LANES), 1)
    ki = kie_ref[...].astype(F32)
    ki = ki + pltpu.roll(ki, IDX_DIM, 1)
    s_new = jnp.zeros((rows, 1), F32)
    for p in range(IDX_HEADS // 2):
        prod = qi_ref[:, p * LANES:(p + 1) * LANES].astype(F32) * ki
        l_even = jnp.sum(jnp.where(lane < IDX_DIM, prod, 0.0), axis=1, keepdims=True)
        l_odd = jnp.sum(jnp.where(lane >= IDX_DIM, prod, 0.0), axis=1, keepdims=True)
        s_new = s_new + jnp.maximum(l_even, 0.0) * ws_ref[:, 2 * p:2 * p + 1]
        s_new = s_new + jnp.maximum(l_odd, 0.0) * ws_ref[:, 2 * p + 1:2 * p + 2]
    extra = _lane_bcast(s_new, rows)
    bnew_ref[...] = extra
    rmax = extra
    rmin = extra
    for c in range(past // LANES):
        sl = slice(c * LANES, (c + 1) * LANES)
        s = sc_ref[:, sl]
        bias_ref[:, sl] = s
        rmax = jnp.maximum(rmax, s)
        rmin = jnp.minimum(rmin, s)
    row_max = _lane_bcast(jnp.max(rmax, axis=1, keepdims=True), rows)
    row_min = _lane_bcast(jnp.min(rmin, axis=1, keepdims=True), rows)
    kp = jnp.full((rows, LANES), float(top_k), F32)
    _select_bias(bias_ref, bnew_ref, kp, row_min, row_max, n_cols=past)


def _sample_select(scores, h_main, kie, ws, *, top_k):
    db, past = scores.shape
    full = lambda r, c: pl.BlockSpec((r, c), lambda i: (0, 0))
    return pl.pallas_call(
        functools.partial(_sample_select_body, top_k=top_k),
        grid=(1,),
        in_specs=[full(db, past),
                  pl.BlockSpec((db, COL_TILE), lambda i: (0, T_QI)),
                  full(db, LANES), full(db, IDX_HEADS)],
        out_specs=[full(db, past), full(db, LANES)],
        out_shape=[jax.ShapeDtypeStruct((db, past), F32), jax.ShapeDtypeStruct((db, LANES), F32)],
        compiler_params=_cparams(("arbitrary",)),
        name="select_sample",
    )(scores, h_main, kie, ws)


def _sample_compact_body(m_ref, mt_ref, pt_ref, idx_ref, row_ref, *, n_slots):
    n_pages = m_ref.shape[1]
    for q in range(m_ref.shape[0]):
        _compact_one(m_ref.at[q], mt_ref.at[q], pt_ref.at[q], idx_ref.at[q], row_ref.at[q],
                     n_pages=n_pages, n_slots=n_slots)


def _compact_one(m_ref, mt_ref, pt_ref, idx_ref, row_ref, *, n_pages, n_slots):
    pt = jnp.broadcast_to(pt_ref[...], (8, n_pages))
    pt_hi = (pt // PAGE_SIZE).astype(F32).astype(BF16)
    pt_lo = (pt % PAGE_SIZE).astype(F32).astype(BF16)
    one = lambda pred: jnp.where(pred, 1.0, 0.0)
    kept = m_ref[...] == 0.0
    kept_t = mt_ref[...] == 0.0
    ri = lax.broadcasted_iota(jnp.int32, (PAGE_SIZE, PAGE_SIZE), 0)
    ci = lax.broadcasted_iota(jnp.int32, (PAGE_SIZE, PAGE_SIZE), 1)
    rp = lax.broadcasted_iota(jnp.int32, (n_pages, n_pages), 0)
    cp = lax.broadcasted_iota(jnp.int32, (n_pages, n_pages), 1)
    plt = jnp.dot(one(ci <= ri).astype(BF16), one(kept_t).astype(BF16), preferred_element_type=F32)
    n_row = plt[PAGE_SIZE - 1:PAGE_SIZE, :]
    n_col = _lane_bcast(jnp.sum(one(kept), axis=1, keepdims=True), n_pages)
    e_col = jnp.dot(one(cp <= rp).astype(BF16), n_col.astype(BF16), preferred_element_type=F32)
    n_row8 = jnp.broadcast_to(n_row, (8, n_pages))
    e_row8 = jnp.dot(n_row8.astype(BF16), one(rp <= cp).astype(BF16), preferred_element_type=F32)
    off_row8 = e_row8 - n_row8
    n_total = e_col[n_pages - 1:n_pages, :]
    page_id = lax.broadcasted_iota(jnp.int32, (n_pages, LANES), 0).astype(F32)
    for jt in range(n_slots // LANES):
        j = (lax.broadcasted_iota(jnp.int32, (1, LANES), 1) + jt * LANES).astype(F32)
        page_j = jnp.sum(one(e_col <= j), axis=0, keepdims=True)
        pick = one(page_id == page_j).astype(BF16)
        prefix_j = jnp.dot(plt.astype(BF16), pick, preferred_element_type=F32)
        off_j = jnp.dot(off_row8.astype(BF16), pick, preferred_element_type=F32)[0:1]
        local_j = jnp.sum(one(prefix_j <= j - off_j), axis=0, keepdims=True)
        pos = page_j * float(PAGE_SIZE) + local_j
        phys = (jnp.dot(pt_hi, pick, preferred_element_type=F32)[0:1] * float(PAGE_SIZE)
                + jnp.dot(pt_lo, pick, preferred_element_type=F32)[0:1])
        row = phys * float(PAGE_SIZE) + local_j
        used = j < n_total
        sl = slice(jt * LANES, (jt + 1) * LANES)
        idx_ref[:, sl] = jnp.where(used, pos, -1.0).astype(jnp.int32)
        row_ref[:, sl] = jnp.where(used, row, 0.0).astype(jnp.int32)


def _sample_compact(bias3, bias3_t, pt3, *, n_slots):
    db, n_pages, _ = bias3.shape
    per = COMPACT_SEQS if db % COMPACT_SEQS == 0 else 1
    out = pl.BlockSpec((per, 1, n_slots), lambda b: (b, 0, 0))
    return pl.pallas_call(
        functools.partial(_sample_compact_body, n_slots=n_slots),
        grid=(db // per,),
        in_specs=[pl.BlockSpec((per, n_pages, PAGE_SIZE), lambda b: (b, 0, 0)),
                  pl.BlockSpec((per, PAGE_SIZE, n_pages), lambda b: (b, 0, 0)),
                  pl.BlockSpec((per, 1, n_pages), lambda b: (b, 0, 0))],
        out_specs=[out, out],
        out_shape=[jax.ShapeDtypeStruct((db, 1, n_slots), jnp.int32)] * 2,
        compiler_params=_cparams(("arbitrary",)),
        name="compact_sample",
    )(bias3, bias3_t, pt3)


def _sample_attn_body(row_ref, q_ref, slot_ref, bnew_ref, kn_ref, vn_ref, sz_ref, k_hbm, v_hbm,
                      o_ref, kbuf, vbuf, sem):
    b = pl.program_id(0)
    nb = pl.num_programs(0)
    n_slots = kbuf.shape[1]
    buf = b % 2
    grp = N_HEADS // N_KV_HEADS
    scale = HEAD_DIM ** -0.5

    def row_copies(seq, j, to):
        r = row_ref[seq, j]
        return (pltpu.make_async_copy(k_hbm.at[r], kbuf.at[to, j], sem.at[0, to]),
                pltpu.make_async_copy(v_hbm.at[r], vbuf.at[to, j], sem.at[1, to]))

    def start_all(seq, to):
        def f(j, c):
            ck, cv = row_copies(seq, j, to)
            ck.start()
            cv.start()
            return c
        lax.fori_loop(0, n_slots, f, 0, unroll=8)

    def wait_all(to):
        for blk in range(n_slots // PAGE_SIZE):
            rows = pl.ds(blk * PAGE_SIZE, PAGE_SIZE)
            src = pl.ds(0, PAGE_SIZE)
            pltpu.make_async_copy(k_hbm.at[src], kbuf.at[to, rows], sem.at[0, to]).wait()
            pltpu.make_async_copy(v_hbm.at[src], vbuf.at[to, rows], sem.at[1, to]).wait()

    @pl.when(b == 0)
    def _():
        start_all(0, 0)

    @pl.when(b + 1 < nb)
    def _():
        start_all(b + 1, 1 - buf)

    wait_all(buf)

    q = q_ref[0]
    head_s = lax.broadcasted_iota(jnp.int32, (N_HEADS, n_slots), 0)
    head_o = lax.broadcasted_iota(jnp.int32, (N_HEADS, HEAD_DIM), 0)
    k0, k1 = (kbuf[buf, :, kh, :].astype(BF16) for kh in range(N_KV_HEADS))
    v0, v1 = (vbuf[buf, :, kh, :].astype(BF16) for kh in range(N_KV_HEADS))
    s0 = lax.dot_general(q, k0, _NT, preferred_element_type=F32)
    s1 = lax.dot_general(q, k1, _NT, preferred_element_type=F32)
    s = jnp.where(head_s < grp, s0, s1) * scale + jnp.where(slot_ref[0] >= 0, 0.0, NEG)
    kn = kn_ref[0].astype(F32)
    vn = vn_ref[0].astype(F32)
    kn8 = jnp.where(head_o < grp, kn[:, :HEAD_DIM], kn[:, HEAD_DIM:])
    vn8 = jnp.where(head_o < grp, vn[:, :HEAD_DIM], vn[:, HEAD_DIM:])
    s_new = _lane_bcast(jnp.sum(q.astype(F32) * kn8, axis=-1, keepdims=True), N_HEADS) * scale + bnew_ref[0]
    m = jnp.maximum(_lane_bcast(jnp.max(s, axis=-1, keepdims=True), N_HEADS), s_new)
    p = jnp.exp(s - m[:, :1])
    p_new = jnp.exp(s_new - m)
    l = _lane_bcast(jnp.sum(p, axis=-1, keepdims=True), N_HEADS) + p_new
    pb = p.astype(BF16)
    pv = jnp.where(head_o < grp,
                   jnp.dot(pb, v0, preferred_element_type=F32),
                   jnp.dot(pb, v1, preferred_element_type=F32))
    o = (pv + p_new * vn8) / l
    o_ref[0] = (o * sz_ref[0].astype(F32)).astype(BF16)


def _sample_attn(rows, q3, slots3, bnew3, kn3, vn3, sz3, k_rows, v_rows):
    db, n_slots = rows.shape
    per_b = lambda r, c: pl.BlockSpec((1, r, c), lambda b, rw: (b, 0, 0))
    grid_spec = pltpu.PrefetchScalarGridSpec(
        num_scalar_prefetch=1,
        grid=(db,),
        in_specs=[per_b(N_HEADS, HEAD_DIM), per_b(1, n_slots), per_b(1, LANES),
                  per_b(1, KV_WIDTH), per_b(1, KV_WIDTH), per_b(N_HEADS, HEAD_DIM),
                  pl.BlockSpec(memory_space=pl.ANY), pl.BlockSpec(memory_space=pl.ANY)],
        out_specs=per_b(N_HEADS, HEAD_DIM),
        scratch_shapes=[pltpu.VMEM((2, n_slots, N_KV_HEADS, HEAD_DIM), F32),
                        pltpu.VMEM((2, n_slots, N_KV_HEADS, HEAD_DIM), F32),
                        pltpu.SemaphoreType.DMA((2, 2))],
    )
    return pl.pallas_call(
        _sample_attn_body,
        grid_spec=grid_spec,
        out_shape=jax.ShapeDtypeStruct((db, N_HEADS, HEAD_DIM), BF16),
        compiler_params=_cparams(("arbitrary",)),
        name="attn_sample",
    )(rows, q3, slots3, bnew3, kn3, vn3, sz3, k_rows, v_rows)


def _col_tile(h_main, t, n=1):
    return h_main[:, t * COL_TILE:(t + n) * COL_TILE]


def kernel(x_prompt, x_sample, cache_k, cache_v, cache_kidx, page_table, pre_g, w_in, a_ln_g, a_ln_b,
           a_ws, a_bs, w_oa, w_ob, w_out, post_g):
    batch, seq, _ = x_prompt.shape
    db, ds, _ = x_sample.shape
    depth = w_in.shape[0]
    n_pages = page_table.shape[1]
    past = n_pages * PAGE_SIZE
    assert ds == 1 and seq % PROJ_ROWS == 0 and past % SCORE_CHUNK == 0 and Q_BLOCK == LANES
    top_k_s = min(TOPK_MAX, (past + ds) // 4)
    assert top_k_s <= past and top_k_s % LANES == 0

    pos_p = jnp.arange(seq)
    pos_s = past + (jnp.arange(db * ds) % ds)
    tabs_p = _rope_tables(pos_p, HEAD_DIM) + _rope_tables(pos_p, IDX_DIM)
    tabs_s = _rope_tables(pos_s, HEAD_DIM) + _rope_tables(pos_s, IDX_DIM)

    hp = x_prompt.reshape(batch * seq, D_MODEL)
    hs = x_sample.reshape(db * ds, D_MODEL)
    outs = [[] for _ in range(8)]
    for l in range(depth):
        w_t = w_in[l].T
        woa, wob, wout = w_oa[l].astype(BF16), w_ob[l].astype(BF16), w_out[l].astype(BF16)
        g_pre, g_post = pre_g[l][None], post_g[l][None]
        ln_g, ln_b = a_ln_g[l][None], a_ln_b[l][None]

        xn, k, v, ki, kb, kie, kio, vb_t, ws_t = _proj_tail(hp, g_pre, w_t, *tabs_p, tm=PROJ_ROWS, seq_rows=seq,
                                                            transposed=True)
        h_main, gv, w_tiles = _proj_main(xn, w_t, *tabs_p, ln_g, ln_b, tm=PROJ_ROWS, seq_rows=seq,
                                         gv_rows=CHUNK, cast_weights=True)
        ob = _prompt_attn(h_main, ws_t, kie, kio, kb, vb_t, batch=batch, seq=seq)
        hp = _merge(hp, None, ob, h_main, woa, wob, wout, g_post, tm=MERGE_ROWS, gate=(a_ws[l], a_bs[l].T))
        outs[0].append(k.reshape(batch, seq, N_KV_HEADS, HEAD_DIM))
        outs[1].append(v.reshape(batch, seq, N_KV_HEADS, HEAD_DIM))
        outs[2].append(ki.reshape(batch, seq, IDX_DIM))
        outs[3].append(gv.reshape(batch, CHUNK, A_WIDTH))

        m_s = db * ds
        xn, k, v, ki, kb, kie, kio, vb, ws = _proj_tail(hs, g_pre, w_t, *tabs_s, tm=m_s, seq_rows=m_s,
                                                        transposed=False)
        h_main, gv = _proj_main(xn, w_tiles, *tabs_s, ln_g, ln_b, tm=m_s, seq_rows=m_s, gv_rows=m_s,
                                cast_weights=False)
        w0 = jnp.repeat(a_ws[l][:, 0, 0], LANES)[None]
        b0 = jnp.repeat(a_bs[l][:, 0], LANES)[None]
        ya = _gate_row(h_main, gv, w0, b0)
        qi3 = _col_tile(h_main, T_QI).reshape(db, IDX_HEADS, IDX_DIM)
        kidx_t = jnp.swapaxes(cache_kidx[l], 1, 2)
        scores = _sample_scores(page_table, qi3, ws.reshape(db, IDX_HEADS, 1), kidx_t)
        bias, bnew = _sample_select(scores, h_main, kie, ws, top_k=top_k_s)
        bias3 = bias.reshape(db, n_pages, PAGE_SIZE)
        slots3, rows3 = _sample_compact(bias3, jnp.swapaxes(bias3, 1, 2),
                                        page_table.reshape(db, 1, n_pages), n_slots=top_k_s)
        pool_rows = lambda c: c.reshape(-1, N_KV_HEADS, HEAD_DIM)
        ob = _sample_attn(rows3.reshape(db, top_k_s),
                          _col_tile(h_main, T_Q).reshape(db, N_HEADS, HEAD_DIM),
                          slots3, bnew.reshape(db, 1, LANES),
                          kb.reshape(db, 1, KV_WIDTH), vb.reshape(db, 1, KV_WIDTH),
                          _col_tile(h_main, T_ZB).reshape(db, N_HEADS, HEAD_DIM),
                          pool_rows(cache_k[l]), pool_rows(cache_v[l])).reshape(db, B_WIDTH)
        hs = _merge(hs, ya, ob, h_main, woa, wob, wout, g_post, tm=m_s)
        outs[4].append(k.reshape(db, ds, N_KV_HEADS, HEAD_DIM))
        outs[5].append(v.reshape(db, ds, N_KV_HEADS, HEAD_DIM))
        outs[6].append(ki.reshape(db, ds, IDX_DIM))
        outs[7].append(gv.reshape(db, ds, A_WIDTH))

    st = [jnp.stack(o, axis=0) for o in outs]
    return (hp.reshape(batch, seq, D_MODEL), hs.reshape(db, ds, D_MODEL),
            st[0], st[1], st[2], st[3], st[4], st[5], st[6], st[7])
```

```python
import functools

import jax
import jax.numpy as jnp
from jax import lax
from jax.experimental import pallas as pl
from jax.experimental.pallas import tpu as pltpu

F32 = jnp.float32
BF16 = jnp.bfloat16

D_MODEL = 2048
CHUNK = 128
A_GROUPS = 8
A_WIDTH = 1024
N_HEADS = 8
N_KV_HEADS = 2
HEAD_DIM = 128
B_WIDTH = 1024
KV_WIDTH = 256
IDX_HEADS = 16
IDX_DIM = 64
TOPK_MAX = 256
ROPE_THETA = 10000.0
EPS = 1e-6
PAGE_SIZE = 128
Q_BLOCK = 128
NEG = -1e30
LOG2_E = 1.4426950408889634

LANES = 128
BF16_SUBLANES = 16
MXU_WIDTH = 256
KEY_STEP = 256
PROJ_ROWS = 512
MERGE_ROWS = 256
NORM_STREAMS = 4
_NT = (((1,), (1,)), ((), ()))
COL_TILE = 1024
T_UA, T_VA, T_ZA, T_Q, T_ZB, T_QI, T_GA, T_GB = 0, 1, 2, 3, 4, 5, 6, 8
N_MAIN_TILES = 10
ROW_K = 4 * COL_TILE
ROW_ZB = ROW_K + 2 * KV_WIDTH
ROW_KI = ROW_ZB + B_WIDTH + IDX_HEADS * IDX_DIM
ROW_GA = ROW_KI + IDX_DIM + IDX_HEADS
VMEM_LIMIT = 56 * 1024 * 1024


def _cparams(sem):
    return pltpu.CompilerParams(dimension_semantics=sem, vmem_limit_bytes=VMEM_LIMIT)


def _rope_tables(pos, dim):
    half = dim // 2
    inv = ROPE_THETA ** (-jnp.arange(half, dtype=F32) / half)
    ang = pos.astype(F32)[:, None] * inv[None, :]
    cos = jnp.cos(ang)
    sin = jnp.sin(ang)
    reps = LANES // dim
    cos_t = jnp.tile(jnp.concatenate([cos, cos], axis=-1), (1, reps))
    sin_t = jnp.tile(jnp.concatenate([-sin, sin], axis=-1), (1, reps))
    return cos_t, sin_t


def _rope128(x, cos, sin):
    return x * cos + pltpu.roll(x, 64, 1) * sin


def _rope64(x, cos, sin):
    lane = lax.broadcasted_iota(jnp.int32, x.shape, 1)
    first = (lane % IDX_DIM) < (IDX_DIM // 2)
    partner = jnp.where(first, pltpu.roll(x, LANES - 32, 1), pltpu.roll(x, 32, 1))
    return x * cos + partner * sin


def _rmsnorm_rows(xf, g):
    ms = jnp.mean(xf * xf, axis=-1, keepdims=True)
    return xf * lax.rsqrt(ms + EPS) * g


_TILE_KINDS = ("copy", "ln", "silu", "rope128", "silu", "rope64", "sigmoid", "sigmoid", "sigmoid", "sigmoid")


def _sigmoid(x):
    return 0.5 * jnp.tanh(0.5 * x) + 0.5


def _prenorm_rows(x_refs, g_ref, o_ref):
    xs = [r[...] for r in x_refs]
    ms = sum(jnp.sum(x * x, axis=-1, keepdims=True) for x in xs) * (1.0 / D_MODEL)
    scale = lax.rsqrt(ms + EPS)
    w = xs[0].shape[1]
    for c, x in enumerate(xs):
        cs = slice(c * w, (c + 1) * w)
        o_ref[:, cs] = (x * scale * g_ref[:, cs]).astype(BF16)


def _proj_main_body(xn_ref, *refs, gv_rows, n_row_tiles, cast_weights):
    n_chunks = COL_TILE // MXU_WIDTH
    w_refs, refs = refs[:n_chunks], refs[n_chunks:]
    cq_ref, sq_ref, ci_ref, si_ref, lng_ref, lnb_ref, h_ref, gv_ref = refs[:8]
    wbf_ref = refs[8] if cast_weights else None
    acc_ref = refs[-1]
    j = pl.program_id(0)
    i = pl.program_id(1)
    tm = xn_ref.shape[0]
    chunks = [slice(c * MXU_WIDTH, (c + 1) * MXU_WIDTH) for c in range(n_chunks)]
    weights = lambda c: wbf_ref[chunks[c], :] if cast_weights else w_refs[c][...]

    def finish_chunk(kind, cs, stats):
        acc = acc_ref[:, cs]
        if kind == "copy":
            h_ref[:, cs] = acc.astype(BF16)
        elif kind == "ln":
            mu, rstd = stats
            vn = (acc - mu) * rstd * lng_ref[:, cs] + lnb_ref[:, cs]
            h_ref[:, cs] = vn.astype(BF16)
            gv_ref[:, cs] = vn[tm - gv_rows:, :]
        elif kind == "silu":
            h_ref[:, cs] = (acc * _sigmoid(acc)).astype(BF16)
        elif kind == "sigmoid":
            h_ref[:, cs] = _sigmoid(acc).astype(BF16)
        else:
            rope, cos, sin = ((_rope128, cq_ref[...], sq_ref[...]) if kind == "rope128"
                              else (_rope64, ci_ref[...], si_ref[...]))
            for h in range(MXU_WIDTH // LANES):
                sl = slice(cs.start + h * LANES, cs.start + (h + 1) * LANES)
                h_ref[:, sl] = rope(acc[:, h * LANES:(h + 1) * LANES], cos, sin).astype(BF16)

    def run(kind, finish, matmul):
        stats = None
        if finish and kind == "ln":
            acc = acc_ref[...]
            mu = jnp.mean(acc, axis=-1, keepdims=True)
            d = acc - mu
            stats = (mu, lax.rsqrt(jnp.mean(d * d, axis=-1, keepdims=True) + EPS))
        for c, cs in enumerate(chunks):
            if finish:
                finish_chunk(kind, cs, stats)
            if matmul:
                acc_ref[:, cs] = lax.dot_general(xn_ref[...], weights(c), _NT, preferred_element_type=F32)

    @pl.when(i == 0)
    def _():
        if cast_weights:
            for c, cs in enumerate(chunks):
                wbf_ref[cs, :] = w_refs[c][...].astype(BF16)
        run(None, False, True)

    @pl.when((i == 0) & (j != T_VA))
    def _():
        gv_ref[...] = jnp.zeros(gv_ref.shape, F32)

    for kind in sorted(set(_TILE_KINDS)):
        is_kind = functools.reduce(jnp.logical_or, [j == t for t, k in enumerate(_TILE_KINDS) if k == kind])

        @pl.when(is_kind & (i > 0) & (i < n_row_tiles))
        def _(kind=kind):
            run(kind, True, True)

        @pl.when(is_kind & (i == n_row_tiles))
        def _(kind=kind):
            run(kind, True, False)


def _main_tile_row(j, chunk):
    g = BF16_SUBLANES
    skip_kv = (ROW_ZB - T_ZB * COL_TILE) // g
    skip_idx = (ROW_GA - ROW_ZB - (T_GA - T_ZB) * COL_TILE) // g
    return (j * (COL_TILE // g) + chunk * (MXU_WIDTH // g)
            + jnp.where(j >= T_ZB, skip_kv, 0) + jnp.where(j >= T_GA, skip_idx, 0)) * g


def _proj_main(xn, w, cq, sq, ci, si, ln_g, ln_b, *, tm, seq_rows, gv_rows, cast_weights):
    m = xn.shape[0]
    tiles_per_seq = seq_rows // tm
    n_seq = m // seq_rows
    n_row_tiles = m // tm
    n_chunks = COL_TILE // MXU_WIDTH
    body = functools.partial(_proj_main_body, gv_rows=gv_rows, n_row_tiles=n_row_tiles,
                             cast_weights=cast_weights)
    w_tile = lambda j, i: jnp.minimum(jnp.where(i == n_row_tiles, j + 1, j), N_MAIN_TILES - 1)
    if cast_weights:
        w_chunk = lambda c: pl.BlockSpec((pl.Element(MXU_WIDTH), pl.Element(D_MODEL)),
                                         lambda j, i: (_main_tile_row(w_tile(j, i), c), 0))
    else:
        w_chunk = lambda c: pl.BlockSpec((MXU_WIDTH, D_MODEL), lambda j, i: (w_tile(j, i) * n_chunks + c, 0))
    w_out_spec = [pl.BlockSpec((COL_TILE, D_MODEL), lambda j, i: (j, 0))] if cast_weights else []
    w_out_shape = [jax.ShapeDtypeStruct((N_MAIN_TILES * COL_TILE, D_MODEL), BF16)] if cast_weights else []
    prev = lambda i: jnp.maximum(i - 1, 0)
    tab = pl.BlockSpec((tm, LANES), lambda j, i: (prev(i) % tiles_per_seq, 0))
    row = lambda n: pl.BlockSpec((1, n), lambda j, i: (0, 0))
    gv_block = lambda j, i: (jnp.where(j == T_VA, prev(i) // tiles_per_seq, n_seq + (j > T_VA)), 0)
    h_main, gv, *w_bf = pl.pallas_call(
        body,
        grid=(N_MAIN_TILES, n_row_tiles + 1),
        in_specs=[pl.BlockSpec((tm, D_MODEL), lambda j, i: (jnp.minimum(i, n_row_tiles - 1), 0))]
                 + [w_chunk(c) for c in range(n_chunks)]
                 + [tab, tab, tab, tab, row(A_WIDTH), row(A_WIDTH)],
        out_specs=[
            pl.BlockSpec((tm, COL_TILE), lambda j, i: (prev(i), j)),
            pl.BlockSpec((gv_rows, A_WIDTH), gv_block),
        ] + w_out_spec,
        out_shape=[
            jax.ShapeDtypeStruct((m, N_MAIN_TILES * COL_TILE), BF16),
            jax.ShapeDtypeStruct(((n_seq + 2) * gv_rows, A_WIDTH), F32),
        ] + w_out_shape,
        scratch_shapes=[pltpu.VMEM((tm, COL_TILE), F32)],
        compiler_params=_cparams(("arbitrary", "arbitrary")),
        name="proj_main",
    )(xn, *([w] * n_chunks), cq, sq, ci, si, ln_g, ln_b)
    return (h_main, gv[:n_seq * gv_rows], *w_bf)


def _proj_tail_body(*refs, transposed):
    x_refs, refs = refs[:NORM_STREAMS], refs[NORM_STREAMS:]
    (g_ref, wkv_ref, wix_ref, ck_ref, sk_ref, ci_ref, si_ref,
     xn_ref, k_ref, v_ref, ki_ref, kb_ref, kie_ref, kio_ref, vb_ref, ws_ref) = refs
    _prenorm_rows(x_refs, g_ref, xn_ref)
    xn = xn_ref[...]
    acc = lax.dot_general(xn, wkv_ref[...].astype(BF16), _NT, preferred_element_type=F32)
    cos = ck_ref[...]
    sin = sk_ref[...]
    for kh in range(N_KV_HEADS):
        sl = slice(kh * HEAD_DIM, (kh + 1) * HEAD_DIM)
        r = _rope128(acc[:, sl], cos, sin)
        k_ref[:, kh, :] = r
        kb_ref[:, sl] = r.astype(BF16)
        v_ref[:, kh, :] = acc[:, KV_WIDTH + kh * HEAD_DIM:KV_WIDTH + (kh + 1) * HEAD_DIM]
    t = lax.dot_general(xn, wix_ref[...].astype(BF16), _NT, preferred_element_type=F32)
    r = _rope64(t, ci_ref[...], si_ref[...])
    ki_ref[...] = r[:, :IDX_DIM]
    lane = lax.broadcasted_iota(jnp.int32, r.shape, 1)
    ke = jnp.where(lane < IDX_DIM, r, 0.0)
    kie_ref[...] = ke.astype(BF16)
    kio_ref[...] = pltpu.roll(ke, IDX_DIM, 1).astype(BF16)
    w_scale = IDX_HEADS ** -0.5 * IDX_DIM ** -0.5
    v = acc[:, KV_WIDTH:2 * KV_WIDTH]
    if transposed:
        vb_ref[...] = v.T.astype(BF16)
        ws_ref[...] = t.T[IDX_DIM:IDX_DIM + IDX_HEADS, :] * w_scale
    else:
        vb_ref[...] = v.astype(BF16)
        ws_ref[...] = t[:, IDX_DIM:IDX_DIM + IDX_HEADS] * w_scale


def _proj_tail(x2, pre_g, w_t, ck, sk, ci, si, *, tm, seq_rows, transposed):
    m = x2.shape[0]
    x_col = lambda c: pl.BlockSpec((tm, D_MODEL // NORM_STREAMS), lambda i: (i, c))
    if transposed:
        vw_specs = [pl.BlockSpec((n, tm), lambda i: (0, i)) for n in (KV_WIDTH, IDX_HEADS)]
        vw_shapes = [jax.ShapeDtypeStruct((KV_WIDTH, m), BF16), jax.ShapeDtypeStruct((IDX_HEADS, m), F32)]
    else:
        vw_specs = [pl.BlockSpec((tm, n), lambda i: (i, 0)) for n in (KV_WIDTH, IDX_HEADS)]
        vw_shapes = [jax.ShapeDtypeStruct((m, KV_WIDTH), BF16), jax.ShapeDtypeStruct((m, IDX_HEADS), F32)]
    tiles_per_seq = seq_rows // tm
    tab = pl.BlockSpec((tm, LANES), lambda i: (i % tiles_per_seq, 0))
    blk = lambda n: pl.BlockSpec((tm, n), lambda i: (i, 0))
    kv_rows = pl.BlockSpec((tm, N_KV_HEADS, HEAD_DIM), lambda i: (i, 0, 0))
    return pl.pallas_call(
        functools.partial(_proj_tail_body, transposed=transposed),
        grid=(m // tm,),
        in_specs=[x_col(c) for c in range(NORM_STREAMS)] + [
            pl.BlockSpec((1, D_MODEL), lambda i: (0, 0)),
            pl.BlockSpec((2 * KV_WIDTH, D_MODEL), lambda i: (ROW_K // (2 * KV_WIDTH), 0)),
            pl.BlockSpec((LANES, D_MODEL), lambda i: (ROW_KI // LANES, 0)),
            tab, tab, tab, tab,
        ],
        out_specs=[blk(D_MODEL), kv_rows, kv_rows, blk(IDX_DIM), blk(KV_WIDTH),
                   blk(LANES), blk(LANES)] + vw_specs,
        out_shape=[
            jax.ShapeDtypeStruct((m, D_MODEL), BF16),
            jax.ShapeDtypeStruct((m, N_KV_HEADS, HEAD_DIM), F32),
            jax.ShapeDtypeStruct((m, N_KV_HEADS, HEAD_DIM), F32),
            jax.ShapeDtypeStruct((m, IDX_DIM), F32),
            jax.ShapeDtypeStruct((m, KV_WIDTH), BF16),
            jax.ShapeDtypeStruct((m, LANES), BF16),
            jax.ShapeDtypeStruct((m, LANES), BF16),
        ] + vw_shapes,
        compiler_params=_cparams(("arbitrary",)),
        name="proj_tail",
    )(*([x2] * NORM_STREAMS), pre_g, w_t, w_t, ck, sk, ci, si)


def _gate_body(u_ref, vn_ref, sz_ref, ws_ref, bst_ref, y_ref, *, n_chunks):
    rr = lax.broadcasted_iota(jnp.int32, (CHUNK, CHUNK), 0)
    cc = lax.broadcasted_iota(jnp.int32, (CHUNK, CHUNK), 1)
    tril = cc <= rr
    for g in range(A_GROUPS):
        wm = jnp.where(tril, ws_ref[g], 0.0).astype(BF16)
        b = bst_ref[:, g:g + 1]
        cs = slice(g * LANES, (g + 1) * LANES)
        for c in range(n_chunks):
            rs = slice(c * CHUNK, (c + 1) * CHUNK)
            s = jnp.dot(wm, vn_ref[rs, cs], preferred_element_type=F32) + b
            y = u_ref[rs, cs].astype(F32) * s * sz_ref[rs, cs].astype(F32)
            y_ref[rs, cs] = y.astype(BF16)


def _gate_row_body(u_ref, vn_ref, sz_ref, w0_ref, b0_ref, y_ref):
    s = vn_ref[...] * w0_ref[...] + b0_ref[...]
    y_ref[...] = (u_ref[...].astype(F32) * s * sz_ref[...].astype(F32)).astype(BF16)


def _gate_row(h_main, vn, w0, b0):
    m = h_main.shape[0]
    col = lambda t: pl.BlockSpec((m, COL_TILE), lambda i, t=t: (0, t))
    full = lambda r: pl.BlockSpec((r, A_WIDTH), lambda i: (0, 0))
    return pl.pallas_call(
        _gate_row_body,
        grid=(1,),
        in_specs=[col(T_UA), full(m), col(T_ZA), full(1), full(1)],
        out_specs=full(m),
        out_shape=jax.ShapeDtypeStruct((m, A_WIDTH), BF16),
        compiler_params=_cparams(("arbitrary",)),
        name="gate_sample",
    )(h_main, vn, h_main, w0, b0)


def _lane_bcast(col, rows):
    return jnp.broadcast_to(col, (rows, LANES))


def _select_bias(sc_ref, extra_ref, kp, row_min, row_max, *, n_cols):
    rows = sc_ref.shape[0]
    n_tiles = n_cols // LANES
    extra = None if extra_ref is None else extra_ref[...]

    def count(pred):
        acc = jnp.zeros((rows, LANES), F32)
        for c in range(n_tiles):
            acc = acc + jnp.where(pred(sc_ref[:, c * LANES:(c + 1) * LANES], c), 1.0, 0.0)
        tot = jnp.sum(acc, axis=1, keepdims=True)
        return _lane_bcast(tot, rows)

    def count_ge(x):
        c = count(lambda s, _: s >= x)
        if extra is not None:
            c = c + jnp.where(extra >= x, 1.0, 0.0)
        return c

    c_max = count_ge(row_max)
    top = c_max >= kp
    lo0 = jnp.where(top, row_max, row_min)
    c0 = jnp.where(top, c_max, count_ge(row_min))

    def count_ge3(x1, x2, x3):
        a1 = jnp.zeros((rows, LANES), F32)
        a2 = jnp.zeros((rows, LANES), F32)
        a3 = jnp.zeros((rows, LANES), F32)
        for c in range(n_tiles):
            s = sc_ref[:, c * LANES:(c + 1) * LANES]
            a1 = a1 + jnp.where(s >= x1, 1.0, 0.0)
            a2 = a2 + jnp.where(s >= x2, 1.0, 0.0)
            a3 = a3 + jnp.where(s >= x3, 1.0, 0.0)
        res = []
        for a, x in ((a1, x1), (a2, x2), (a3, x3)):
            tot = _lane_bcast(jnp.sum(a, axis=1, keepdims=True), rows)
            if extra is not None:
                tot = tot + jnp.where(extra >= x, 1.0, 0.0)
            res.append(tot)
        return res

    def step(st):
        lo, hi, c_lo, _, it = st
        mid = 0.5 * lo + 0.5 * hi
        act = (c_lo != kp) & (mid > lo) & (mid < hi)
        any_act = jnp.max(jnp.where(act, 1.0, 0.0))
        clamp = lambda x: jnp.minimum(jnp.maximum(x, lo), hi)
        q1 = clamp(0.75 * lo + 0.25 * hi)
        q3 = clamp(0.25 * lo + 0.75 * hi)
        c1, c2, c3 = count_ge3(q1, mid, q3)
        g1, g2, g3 = c1 >= kp, c2 >= kp, c3 >= kp
        lo_n = jnp.where(g3, q3, jnp.where(g2, mid, jnp.where(g1, q1, lo)))
        c_n = jnp.where(g3, c3, jnp.where(g2, c2, jnp.where(g1, c1, c_lo)))
        hi_n = jnp.where(g3, hi, jnp.where(g2, q3, jnp.where(g1, mid, jnp.minimum(q1, mid))))
        return (jnp.where(act, lo_n, lo), jnp.where(act, hi_n, hi), jnp.where(act, c_n, c_lo),
                any_act, it + 1)

    def cond(st):
        return (st[3] > 0.0) & (st[4] < 400)

    lo, _, c_lo, _, _ = lax.while_loop(cond, step, (lo0, row_max, c0, jnp.float32(1.0), jnp.int32(0)))

    exact = jnp.max(jnp.where(c_lo != kp, 1.0, 0.0)) == 0.0

    @pl.when(exact)
    def _():
        for c in range(n_tiles):
            sl = slice(c * LANES, (c + 1) * LANES)
            sc_ref[:, sl] = jnp.where(sc_ref[:, sl] >= lo, 0.0, NEG)
        if extra is not None:
            extra_ref[...] = jnp.where(extra >= lo, 0.0, NEG)

    @pl.when(jnp.logical_not(exact))
    def _():
        n_gt = count(lambda s, _: s > lo)
        if extra is not None:
            n_gt = n_gt + jnp.where(extra > lo, 1.0, 0.0)
        need = kp - n_gt
        lane = lax.broadcasted_iota(jnp.int32, (rows, LANES), 1).astype(F32)

        def count_eq_upto(jx):
            c = count(lambda s, c: (s == lo) & (lane + float(c * LANES) <= jx))
            if extra is not None:
                c = c + jnp.where((extra == lo) & (jx >= float(n_cols)), 1.0, 0.0)
            return c

        last = n_cols if extra is not None else n_cols - 1
        j_lo = jnp.full((rows, LANES), -1.0, F32)
        j_hi = jnp.full((rows, LANES), float(last), F32)

        def jstep(_, st):
            a, b = st
            mid = jnp.floor(0.5 * (a + b))
            ok = count_eq_upto(mid) >= need
            return jnp.where(ok, a, mid), jnp.where(ok, mid, b)

        n_steps = max(1, (n_cols + 1).bit_length())
        _, j_hi = lax.fori_loop(0, n_steps, jstep, (j_lo, j_hi))
        for c in range(n_tiles):
            sl = slice(c * LANES, (c + 1) * LANES)
            s = sc_ref[:, sl]
            keep = (s > lo) | ((s == lo) & (lane + float(c * LANES) <= j_hi))
            sc_ref[:, sl] = jnp.where(keep, 0.0, NEG)
        if extra is not None:
            keep = (extra > lo) | ((extra == lo) & (j_hi >= float(n_cols)))
            extra_ref[...] = jnp.where(keep, 0.0, NEG)


def _select_bias_cols(sc_ref, kp, col_min, col_max, *, n_rows):
    n_tiles = n_rows // LANES
    n_pivots = 3 if n_rows <= 512 else 2 if n_rows <= 1024 else 1

    def counts(preds):
        accs = [jnp.zeros((LANES, LANES), F32) for _ in preds]
        for r in range(n_tiles):
            s = sc_ref[r * LANES:(r + 1) * LANES, :]
            accs = [a + jnp.where(p(s, r), 1.0, 0.0) for a, p in zip(accs, preds)]
        return [jnp.sum(a, axis=0, keepdims=True) for a in accs]

    count = lambda pred: counts([pred])[0]
    count_ge = lambda x: count(lambda s, _: s >= x)
    c_max = count_ge(col_max)
    top = c_max >= kp
    lo0 = jnp.where(top, col_max, col_min)
    c0 = jnp.where(top, c_max, count_ge(col_min))

    def step(st):
        lo, hi, c_lo, _, it = st
        mid = 0.5 * lo + 0.5 * hi
        act = (c_lo != kp) & (mid > lo) & (mid < hi)
        any_act = jnp.max(jnp.where(act, 1.0, 0.0))
        fr = [(k + 1) / (n_pivots + 1) for k in range(n_pivots)]
        piv = [mid if f == 0.5 else jnp.minimum(jnp.maximum((1.0 - f) * lo + f * hi, lo), hi) for f in fr]
        cs = counts([lambda s, _, x=x: s >= x for x in piv])
        lo_n, c_n, hi_n = lo, c_lo, functools.reduce(jnp.minimum, piv)
        for k in range(n_pivots):
            ge = cs[k] >= kp
            nxt = piv[k + 1] if k + 1 < n_pivots else hi
            lo_n = jnp.where(ge, piv[k], lo_n)
            c_n = jnp.where(ge, cs[k], c_n)
            hi_n = jnp.where(ge, nxt, hi_n)
        return jnp.where(act, lo_n, lo), jnp.where(act, hi_n, hi), jnp.where(act, c_n, c_lo), any_act, it + 1

    def cond(st):
        return (st[3] > 0.0) & (st[4] < 400)

    lo, _, c_lo, _, _ = lax.while_loop(cond, step, (lo0, col_max, c0, jnp.float32(1.0), jnp.int32(0)))
    exact = jnp.max(jnp.where(c_lo != kp, 1.0, 0.0)) == 0.0

    @pl.when(exact)
    def _():
        for r in range(n_tiles):
            rs = slice(r * LANES, (r + 1) * LANES)
            sc_ref[rs, :] = jnp.where(sc_ref[rs, :] >= lo, 0.0, NEG)

    @pl.when(jnp.logical_not(exact))
    def _():
        need = kp - count(lambda s, _: s > lo)
        key = lax.broadcasted_iota(jnp.int32, (LANES, LANES), 0).astype(F32)
        count_eq_upto = lambda jx: count(lambda s, r: (s == lo) & (key + float(r * LANES) <= jx))

        def jstep(_, st):
            a, b = st
            mid = jnp.floor(0.5 * (a + b))
            ok = count_eq_upto(mid) >= need
            return jnp.where(ok, a, mid), jnp.where(ok, mid, b)

        j_lo = jnp.full((1, LANES), -1.0, F32)
        j_hi = jnp.full((1, LANES), float(n_rows - 1), F32)
        _, j_hi = lax.fori_loop(0, max(1, n_rows.bit_length()), jstep, (j_lo, j_hi))
        for r in range(n_tiles):
            rs = slice(r * LANES, (r + 1) * LANES)
            s = sc_ref[rs, :]
            keep = (s > lo) | ((s == lo) & (key + float(r * LANES) <= j_hi))
            sc_ref[rs, :] = jnp.where(keep, 0.0, NEG)


def _prompt_attn_block(nk, qi_ref, wst_ref, kie_ref, kio_ref, q_ref, kb_ref, vt_ref, sz_ref,
                       o_ref, sc_ref, *, top_k, key_chunk):
    qb = pl.program_id(1)
    n_pairs = IDX_HEADS // 2
    grp = N_HEADS // N_KV_HEADS
    qpos = qb * Q_BLOCK + lax.broadcasted_iota(jnp.int32, (LANES, LANES), 1)
    key0 = lax.broadcasted_iota(jnp.int32, (LANES, LANES), 0)

    qs = jnp.concatenate([qi_ref[:, p * LANES:(p + 1) * LANES] for p in range(n_pairs)], axis=0)
    wrows = [wst_ref[h:h + 1, :] for h in range(IDX_HEADS)]
    cmax = jnp.full((LANES, LANES), -jnp.inf, F32)
    cmin = jnp.full((LANES, LANES), jnp.inf, F32)
    for kc in range(0, nk, key_chunk):
        le = lax.dot_general(kie_ref[kc:kc + key_chunk, :], qs, _NT, preferred_element_type=F32)
        lo = lax.dot_general(kio_ref[kc:kc + key_chunk, :], qs, _NT, preferred_element_type=F32)
        for r in range(key_chunk // LANES):
            rs = slice(r * LANES, (r + 1) * LANES)
            acc = jnp.zeros((LANES, LANES), F32)
            for p in range(n_pairs):
                cs = slice(p * LANES, (p + 1) * LANES)
                acc = acc + jnp.maximum(le[rs, cs], 0.0) * wrows[2 * p]
                acc = acc + jnp.maximum(lo[rs, cs], 0.0) * wrows[2 * p + 1]
            causal = key0 + (kc + r * LANES) <= qpos
            cmax = jnp.maximum(cmax, jnp.where(causal, acc, -jnp.inf))
            cmin = jnp.minimum(cmin, jnp.where(causal, acc, jnp.inf))
            sc_ref[kc + r * LANES:kc + (r + 1) * LANES, :] = jnp.where(causal, acc, -jnp.inf)

    col_max = jnp.max(cmax, axis=0, keepdims=True)
    col_min = jnp.min(cmin, axis=0, keepdims=True)
    kp = jnp.minimum(qpos[0:1, :] + 1, top_k).astype(F32)
    _select_bias_cols(sc_ref, kp, col_min, col_max, n_rows=nk)

    bias = jnp.concatenate([sc_ref[0:nk, :]] * grp, axis=1)
    for kh in range(N_KV_HEADS):
        qh = jnp.concatenate(
            [q_ref[:, (kh * grp + g) * HEAD_DIM:(kh * grp + g + 1) * HEAD_DIM] for g in range(grp)], axis=0)
        ks = slice(kh * HEAD_DIM, (kh + 1) * HEAD_DIM)
        s = lax.dot_general(kb_ref[0:nk, ks], qh, _NT, preferred_element_type=F32) + bias
        m = jnp.max(s, axis=0, keepdims=True)
        p = jnp.exp2((s - m) * (HEAD_DIM ** -0.5 * LOG2_E))
        l = jnp.sum(p, axis=0, keepdims=True)
        ot = jnp.dot(vt_ref[ks, 0:nk], p.astype(BF16), preferred_element_type=F32) / l
        for g in range(grp):
            hs = slice((kh * grp + g) * HEAD_DIM, (kh * grp + g + 1) * HEAD_DIM)
            o = ot[:, g * LANES:(g + 1) * LANES].T
            o_ref[:, hs] = (o * sz_ref[:, hs].astype(F32)).astype(BF16)


def _key_ranges(seq):
    fine = list(range(KEY_STEP, min(seq, 4 * KEY_STEP) + 1, KEY_STEP))
    return fine + list(range(fine[-1] + 2 * KEY_STEP, seq + 1, 2 * KEY_STEP))


def _prompt_attn_body(*refs, seq, top_k):
    need = (pl.program_id(1) + 1) * Q_BLOCK
    lower = 0
    for nk in _key_ranges(seq):
        @pl.when((need > lower) & (need <= nk))
        def _(nk=nk):
            _prompt_attn_block(nk, *refs, top_k=top_k, key_chunk=KEY_STEP)
        lower = nk


def _prompt_attn(h_main, ws_t, kie, kio, kb, vb_t, *, batch, seq):
    top_k = min(TOPK_MAX, seq // 4)
    n_qb = seq // Q_BLOCK
    body = functools.partial(_prompt_attn_body, seq=seq, top_k=top_k)
    col = lambda t: pl.BlockSpec((Q_BLOCK, COL_TILE), lambda b, q, t=t: (b * n_qb + q, t))
    seqblk = lambda n: pl.BlockSpec((seq, n), lambda b, q: (b, 0))
    return pl.pallas_call(
        body,
        grid=(batch, n_qb),
        in_specs=[col(T_QI),
                  pl.BlockSpec((IDX_HEADS, Q_BLOCK), lambda b, q: (0, b * n_qb + q)),
                  seqblk(LANES), seqblk(LANES),
                  col(T_Q), seqblk(KV_WIDTH),
                  pl.BlockSpec((KV_WIDTH, seq), lambda b, q: (0, b)),
                  col(T_ZB)],
        out_specs=pl.BlockSpec((Q_BLOCK, B_WIDTH), lambda b, q: (b * n_qb + q, 0)),
        out_shape=jax.ShapeDtypeStruct((batch * seq, B_WIDTH), BF16),
        scratch_shapes=[pltpu.VMEM((seq, Q_BLOCK), F32)],
        compiler_params=_cparams(("arbitrary", "arbitrary")),
        name="attn_prompt",
    )(h_main, ws_t, kie, kio, h_main, kb, vb_t, h_main)


def _merge_body(x_ref, *refs, fused_gate):
    if fused_gate:
        gate_refs, refs = refs[:5], refs[5:]
        ya_ref = refs[-1]
        _gate_body(*gate_refs, ya_ref, n_chunks=x_ref.shape[0] // CHUNK)
        ob_ref, ga_ref, gb_ref, woa_ref, wob_ref, wout_ref, pg_ref, o_ref = refs[:-1]
    else:
        ya_ref, ob_ref, ga_ref, gb_ref, woa_ref, wob_ref, wout_ref, pg_ref, o_ref = refs
    pa = jnp.dot(ya_ref[...], woa_ref[...], preferred_element_type=F32)
    pb = jnp.dot(ob_ref[...], wob_ref[...], preferred_element_type=F32)
    mix = ga_ref[...].astype(F32) * pa + gb_ref[...].astype(F32) * pb
    r = jnp.dot(mix.astype(BF16), wout_ref[...], preferred_element_type=F32)
    o_ref[...] = x_ref[...] + _rmsnorm_rows(r, pg_ref[...])


def _merge(x2, ya, ob, h_main, w_oa, w_ob, w_out, post_g, *, tm, gate=None):
    m = x2.shape[0]
    const = lambda r, c: pl.BlockSpec((r, c), lambda i: (0, 0), pipeline_mode=pl.Buffered(1))
    if gate is None:
        a_specs = [pl.BlockSpec((tm, A_WIDTH), lambda i: (i, 0))]
        a_args, scratch = [ya], []
    else:
        col = lambda t: pl.BlockSpec((tm, COL_TILE), lambda i, t=t: (i, t))
        a_specs = [col(T_UA), col(T_VA), col(T_ZA),
                   pl.BlockSpec((A_GROUPS, CHUNK, CHUNK), lambda i: (0, 0, 0)),
                   pl.BlockSpec((CHUNK, A_GROUPS), lambda i: (0, 0))]
        a_args, scratch = [h_main, h_main, h_main, *gate], [pltpu.VMEM((tm, A_WIDTH), BF16)]
    return pl.pallas_call(
        functools.partial(_merge_body, fused_gate=gate is not None),
        grid=(m // tm,),
        in_specs=[pl.BlockSpec((tm, D_MODEL), lambda i: (i, 0))] + a_specs + [
            pl.BlockSpec((tm, B_WIDTH), lambda i: (i, 0)),
            pl.BlockSpec((tm, D_MODEL), lambda i: (i, T_GA // 2)),
            pl.BlockSpec((tm, D_MODEL), lambda i: (i, T_GB // 2)),
            const(A_WIDTH, D_MODEL), const(B_WIDTH, D_MODEL), const(D_MODEL, D_MODEL),
            const(1, D_MODEL),
        ],
        out_specs=pl.BlockSpec((tm, D_MODEL), lambda i: (i, 0)),
        out_shape=jax.ShapeDtypeStruct((m, D_MODEL), F32),
        scratch_shapes=scratch,
        compiler_params=_cparams(("arbitrary",)),
        name="merge",
    )(x2, *a_args, ob, h_main, h_main, w_oa, w_ob, w_out, post_g)


SCORE_CHUNK = 2048
COMPACT_SEQS = 4


def _sample_scores_body(pt_ref, q_ref, w_ref, kidx_hbm, o_ref, buf, sem):
    db, n_pages = pt_ref.shape
    past = n_pages * PAGE_SIZE

    def page_copy(b, p, slot):
        dst = buf.at[slot, :, pl.ds(pl.multiple_of(p * PAGE_SIZE, PAGE_SIZE), PAGE_SIZE)]
        return pltpu.make_async_copy(kidx_hbm.at[pt_ref[b, p]], dst, sem.at[slot])

    def start_all(b, slot):
        def f(t, c):
            page_copy(b, 2 * t, slot).start(priority=0)
            page_copy(b, 2 * t + 1, slot).start(priority=1)
            return c
        lax.fori_loop(0, n_pages // 2, f, 0, unroll=4)

    def wait_all(slot):
        for p in range(n_pages):
            dst = buf.at[slot, :, pl.ds(p * PAGE_SIZE, PAGE_SIZE)]
            pltpu.make_async_copy(kidx_hbm.at[0], dst, sem.at[slot]).wait()

    start_all(0, 0)

    def per_seq(b, c):
        slot = b % 2

        @pl.when(b + 1 < db)
        def _():
            start_all(b + 1, 1 - slot)

        wait_all(slot)
        q = q_ref[b]
        w = w_ref[b]
        for ch in range(past // SCORE_CHUNK):
            cs = slice(ch * SCORE_CHUNK, (ch + 1) * SCORE_CHUNK)
            logit = jnp.dot(q, buf[slot, :, cs].astype(BF16), preferred_element_type=F32)
            o_ref[pl.ds(b, 1), cs] = jnp.sum(jnp.maximum(logit, 0.0) * w, axis=0, keepdims=True)
        return c

    lax.fori_loop(0, db, per_seq, 0)


def _sample_scores(page_table, qi3, ws3, kidx_pages_t):
    db, n_pages = page_table.shape
    past = n_pages * PAGE_SIZE
    grid_spec = pltpu.PrefetchScalarGridSpec(
        num_scalar_prefetch=1,
        grid=(1,),
        in_specs=[pl.BlockSpec((db, IDX_HEADS, IDX_DIM), lambda i, pt: (0, 0, 0)),
                  pl.BlockSpec((db, IDX_HEADS, 1), lambda i, pt: (0, 0, 0)),
                  pl.BlockSpec(memory_space=pl.ANY)],
        out_specs=pl.BlockSpec((db, past), lambda i, pt: (0, 0)),
        scratch_shapes=[pltpu.VMEM((2, IDX_DIM, past), F32), pltpu.SemaphoreType.DMA((2,))],
    )
    return pl.pallas_call(
        _sample_scores_body,
        grid_spec=grid_spec,
        out_shape=jax.ShapeDtypeStruct((db, past), F32),
        compiler_params=_cparams(("arbitrary",)),
        name="scores_sample",
    )(page_table, qi3, ws3, kidx_pages_t)


def _sample_select_body(sc_ref, qi_ref, kie_ref, ws_ref, bias_ref, bnew_ref, *, top_k):
    rows, past = sc_ref.shape
    lane = lax.broadcasted_iota(jnp.int32, (rows, LANES), 1)
    ki = kie_ref[...].astype(F32)
    ki = ki + pltpu.roll(ki, IDX_DIM, 1)
    s_new = jnp.zeros((rows, 1), F32)
    for p in range(IDX_HEADS // 2):
        prod = qi_ref[:, p * LANES:(p + 1) * LANES].astype(F32) * ki
        l_even = jnp.sum(jnp.where(lane < IDX_DIM, prod, 0.0), axis=1, keepdims=True)
        l_odd = jnp.sum(jnp.where(lane >= IDX_DIM, prod, 0.0), axis=1, keepdims=True)
        s_new = s_new + jnp.maximum(l_even, 0.0) * ws_ref[:, 2 * p:2 * p + 1]
        s_new = s_new + jnp.maximum(l_odd, 0.0) * ws_ref[:, 2 * p + 1:2 * p + 2]
    extra = _lane_bcast(s_new, rows)
    bnew_ref[...] = extra
    rmax = extra
    rmin = extra
    for c in range(past // LANES):
        sl = slice(c * LANES, (c + 1) * LANES)
        s = sc_ref[:, sl]
        bias_ref[:, sl] = s
        rmax = jnp.maximum(rmax, s)
        rmin = jnp.minimum(rmin, s)
    row_max = _lane_bcast(jnp.max(rmax, axis=1, keepdims=True), rows)
    row_min = _lane_bcast(jnp.min(rmin, axis=1, keepdims=True), rows)
    kp = jnp.full((rows, LANES), float(top_k), F32)
    _select_bias(bias_ref, bnew_ref, kp, row_min, row_max, n_cols=past)


def _sample_select(scores, h_main, kie, ws, *, top_k):
    db, past = scores.shape
    full = lambda r, c: pl.BlockSpec((r, c), lambda i: (0, 0))
    return pl.pallas_call(
        functools.partial(_sample_select_body, top_k=top_k),
        grid=(1,),
        in_specs=[full(db, past),
                  pl.BlockSpec((db, COL_TILE), lambda i: (0, T_QI)),
                  full(db, LANES), full(db, IDX_HEADS)],
        out_specs=[full(db, past), full(db, LANES)],
        out_shape=[jax.ShapeDtypeStruct((db, past), F32), jax.ShapeDtypeStruct((db, LANES), F32)],
        compiler_params=_cparams(("arbitrary",)),
        name="select_sample",
    )(scores, h_main, kie, ws)


def _sample_compact_body(m_ref, mt_ref, pt_ref, idx_ref, row_ref, *, n_slots):
    n_pages = m_ref.shape[1]
    for q in range(m_ref.shape[0]):
        _compact_one(m_ref.at[q], mt_ref.at[q], pt_ref.at[q], idx_ref.at[q], row_ref.at[q],
                     n_pages=n_pages, n_slots=n_slots)


def _compact_one(m_ref, mt_ref, pt_ref, idx_ref, row_ref, *, n_pages, n_slots):
    pt = jnp.broadcast_to(pt_ref[...], (8, n_pages))
    pt_hi = (pt // PAGE_SIZE).astype(F32).astype(BF16)
    pt_lo = (pt % PAGE_SIZE).astype(F32).astype(BF16)
    one = lambda pred: jnp.where(pred, 1.0, 0.0)
    kept = m_ref[...] == 0.0
    kept_t = mt_ref[...] == 0.0
    ri = lax.broadcasted_iota(jnp.int32, (PAGE_SIZE, PAGE_SIZE), 0)
    ci = lax.broadcasted_iota(jnp.int32, (PAGE_SIZE, PAGE_SIZE), 1)
    rp = lax.broadcasted_iota(jnp.int32, (n_pages, n_pages), 0)
    cp = lax.broadcasted_iota(jnp.int32, (n_pages, n_pages), 1)
    plt = jnp.dot(one(ci <= ri).astype(BF16), one(kept_t).astype(BF16), preferred_element_type=F32)
    n_row = plt[PAGE_SIZE - 1:PAGE_SIZE, :]
    n_col = _lane_bcast(jnp.sum(one(kept), axis=1, keepdims=True), n_pages)
    e_col = jnp.dot(one(cp <= rp).astype(BF16), n_col.astype(BF16), preferred_element_type=F32)
    n_row8 = jnp.broadcast_to(n_row, (8, n_pages))
    e_row8 = jnp.dot(n_row8.astype(BF16), one(rp <= cp).astype(BF16), preferred_element_type=F32)
    off_row8 = e_row8 - n_row8
    n_total = e_col[n_pages - 1:n_pages, :]
    page_id = lax.broadcasted_iota(jnp.int32, (n_pages, LANES), 0).astype(F32)
    for jt in range(n_slots // LANES):
        j = (lax.broadcasted_iota(jnp.int32, (1, LANES), 1) + jt * LANES).astype(F32)
        page_j = jnp.sum(one(e_col <= j), axis=0, keepdims=True)
        pick = one(page_id == page_j).astype(BF16)
        prefix_j = jnp.dot(plt.astype(BF16), pick, preferred_element_type=F32)
        off_j = jnp.dot(off_row8.astype(BF16), pick, preferred_element_type=F32)[0:1]
        local_j = jnp.sum(one(prefix_j <= j - off_j), axis=0, keepdims=True)
        pos = page_j * float(PAGE_SIZE) + local_j
        phys = (jnp.dot(pt_hi, pick, preferred_element_type=F32)[0:1] * float(PAGE_SIZE)
                + jnp.dot(pt_lo, pick, preferred_element_type=F32)[0:1])
        row = phys * float(PAGE_SIZE) + local_j
        used = j < n_total
        sl = slice(jt * LANES, (jt + 1) * LANES)
        idx_ref[:, sl] = jnp.where(used, pos, -1.0).astype(jnp.int32)
        row_ref[:, sl] = jnp.where(used, row, 0.0).astype(jnp.int32)


def _sample_compact(bias3, bias3_t, pt3, *, n_slots):
    db, n_pages, _ = bias3.shape
    per = COMPACT_SEQS if db % COMPACT_SEQS == 0 else 1
    out = pl.BlockSpec((per, 1, n_slots), lambda b: (b, 0, 0))
    return pl.pallas_call(
        functools.partial(_sample_compact_body, n_slots=n_slots),
        grid=(db // per,),
        in_specs=[pl.BlockSpec((per, n_pages, PAGE_SIZE), lambda b: (b, 0, 0)),
                  pl.BlockSpec((per, PAGE_SIZE, n_pages), lambda b: (b, 0, 0)),
                  pl.BlockSpec((per, 1, n_pages), lambda b: (b, 0, 0))],
        out_specs=[out, out],
        out_shape=[jax.ShapeDtypeStruct((db, 1, n_slots), jnp.int32)] * 2,
        compiler_params=_cparams(("arbitrary",)),
        name="compact_sample",
    )(bias3, bias3_t, pt3)


def _sample_attn_body(row_ref, q_ref, slot_ref, bnew_ref, kn_ref, vn_ref, sz_ref, k_hbm, v_hbm,
                      o_ref, kbuf, vbuf, sem):
    b = pl.program_id(0)
    nb = pl.num_programs(0)
    n_slots = kbuf.shape[1]
    buf = b % 2
    grp = N_HEADS // N_KV_HEADS
    scale = HEAD_DIM ** -0.5

    def row_copies(seq, j, to):
        r = row_ref[seq, j]
        return (pltpu.make_async_copy(k_hbm.at[r], kbuf.at[to, j], sem.at[0, to]),
                pltpu.make_async_copy(v_hbm.at[r], vbuf.at[to, j], sem.at[1, to]))

    def start_all(seq, to):
        def f(j, c):
            ck, cv = row_copies(seq, j, to)
            ck.start(priority=0)
            cv.start(priority=1)
            return c
        lax.fori_loop(0, n_slots, f, 0, unroll=8)

    def wait_all(to):
        for blk in range(n_slots // PAGE_SIZE):
            rows = pl.ds(blk * PAGE_SIZE, PAGE_SIZE)
            src = pl.ds(0, PAGE_SIZE)
            pltpu.make_async_copy(k_hbm.at[src], kbuf.at[to, rows], sem.at[0, to]).wait()
            pltpu.make_async_copy(v_hbm.at[src], vbuf.at[to, rows], sem.at[1, to]).wait()

    @pl.when(b == 0)
    def _():
        start_all(0, 0)

    @pl.when(b + 1 < nb)
    def _():
        start_all(b + 1, 1 - buf)

    wait_all(buf)

    q = q_ref[0]
    head_s = lax.broadcasted_iota(jnp.int32, (N_HEADS, n_slots), 0)
    head_o = lax.broadcasted_iota(jnp.int32, (N_HEADS, HEAD_DIM), 0)
    k0, k1 = (kbuf[buf, :, kh, :].astype(BF16) for kh in range(N_KV_HEADS))
    v0, v1 = (vbuf[buf, :, kh, :].astype(BF16) for kh in range(N_KV_HEADS))
    s0 = lax.dot_general(q, k0, _NT, preferred_element_type=F32)
    s1 = lax.dot_general(q, k1, _NT, preferred_element_type=F32)
    s = jnp.where(head_s < grp, s0, s1) * scale + jnp.where(slot_ref[0] >= 0, 0.0, NEG)
    kn = kn_ref[0].astype(F32)
    vn = vn_ref[0].astype(F32)
    kn8 = jnp.where(head_o < grp, kn[:, :HEAD_DIM], kn[:, HEAD_DIM:])
    vn8 = jnp.where(head_o < grp, vn[:, :HEAD_DIM], vn[:, HEAD_DIM:])
    s_new = _lane_bcast(jnp.sum(q.astype(F32) * kn8, axis=-1, keepdims=True), N_HEADS) * scale + bnew_ref[0]
    m = jnp.maximum(_lane_bcast(jnp.max(s, axis=-1, keepdims=True), N_HEADS), s_new)
    p = jnp.exp(s - m[:, :1])
    p_new = jnp.exp(s_new - m)
    l = _lane_bcast(jnp.sum(p, axis=-1, keepdims=True), N_HEADS) + p_new
    pb = p.astype(BF16)
    pv = jnp.where(head_o < grp,
                   jnp.dot(pb, v0, preferred_element_type=F32),
                   jnp.dot(pb, v1, preferred_element_type=F32))
    o = (pv + p_new * vn8) / l
    o_ref[0] = (o * sz_ref[0].astype(F32)).astype(BF16)


def _sample_attn(rows, q3, slots3, bnew3, kn3, vn3, sz3, k_rows, v_rows):
    db, n_slots = rows.shape
    per_b = lambda r, c: pl.BlockSpec((1, r, c), lambda b, rw: (b, 0, 0))
    grid_spec = pltpu.PrefetchScalarGridSpec(
        num_scalar_prefetch=1,
        grid=(db,),
        in_specs=[per_b(N_HEADS, HEAD_DIM), per_b(1, n_slots), per_b(1, LANES),
                  per_b(1, KV_WIDTH), per_b(1, KV_WIDTH), per_b(N_HEADS, HEAD_DIM),
                  pl.BlockSpec(memory_space=pl.ANY), pl.BlockSpec(memory_space=pl.ANY)],
        out_specs=per_b(N_HEADS, HEAD_DIM),
        scratch_shapes=[pltpu.VMEM((2, n_slots, N_KV_HEADS, HEAD_DIM), F32),
                        pltpu.VMEM((2, n_slots, N_KV_HEADS, HEAD_DIM), F32),
                        pltpu.SemaphoreType.DMA((2, 2))],
    )
    return pl.pallas_call(
        _sample_attn_body,
        grid_spec=grid_spec,
        out_shape=jax.ShapeDtypeStruct((db, N_HEADS, HEAD_DIM), BF16),
        compiler_params=_cparams(("arbitrary",)),
        name="attn_sample",
    )(rows, q3, slots3, bnew3, kn3, vn3, sz3, k_rows, v_rows)


def _col_tile(h_main, t, n=1):
    return h_main[:, t * COL_TILE:(t + n) * COL_TILE]


def kernel(x_prompt, x_sample, cache_k, cache_v, cache_kidx, page_table, pre_g, w_in, a_ln_g, a_ln_b,
           a_ws, a_bs, w_oa, w_ob, w_out, post_g):
    batch, seq, _ = x_prompt.shape
    db, ds, _ = x_sample.shape
    depth = w_in.shape[0]
    n_pages = page_table.shape[1]
    past = n_pages * PAGE_SIZE
    assert ds == 1 and seq % PROJ_ROWS == 0 and past % SCORE_CHUNK == 0 and Q_BLOCK == LANES
    top_k_s = min(TOPK_MAX, (past + ds) // 4)
    assert top_k_s <= past and top_k_s % LANES == 0

    pos_p = jnp.arange(seq)
    pos_s = past + (jnp.arange(db * ds) % ds)
    tabs_p = _rope_tables(pos_p, HEAD_DIM) + _rope_tables(pos_p, IDX_DIM)
    tabs_s = _rope_tables(pos_s, HEAD_DIM) + _rope_tables(pos_s, IDX_DIM)

    hp = x_prompt.reshape(batch * seq, D_MODEL)
    hs = x_sample.reshape(db * ds, D_MODEL)
    outs = [[] for _ in range(8)]
    for l in range(depth):
        w_t = w_in[l].T
        woa, wob, wout = w_oa[l].astype(BF16), w_ob[l].astype(BF16), w_out[l].astype(BF16)
        g_pre, g_post = pre_g[l][None], post_g[l][None]
        ln_g, ln_b = a_ln_g[l][None], a_ln_b[l][None]

        xn, k, v, ki, kb, kie, kio, vb_t, ws_t = _proj_tail(hp, g_pre, w_t, *tabs_p, tm=PROJ_ROWS, seq_rows=seq,
                                                            transposed=True)
        h_main, gv, w_tiles = _proj_main(xn, w_t, *tabs_p, ln_g, ln_b, tm=PROJ_ROWS, seq_rows=seq,
                                         gv_rows=CHUNK, cast_weights=True)
        ob = _prompt_attn(h_main, ws_t, kie, kio, kb, vb_t, batch=batch, seq=seq)
        hp = _merge(hp, None, ob, h_main, woa, wob, wout, g_post, tm=MERGE_ROWS, gate=(a_ws[l], a_bs[l].T))
        outs[0].append(k.reshape(batch, seq, N_KV_HEADS, HEAD_DIM))
        outs[1].append(v.reshape(batch, seq, N_KV_HEADS, HEAD_DIM))
        outs[2].append(ki.reshape(batch, seq, IDX_DIM))
        outs[3].append(gv.reshape(batch, CHUNK, A_WIDTH))

        m_s = db * ds
        xn, k, v, ki, kb, kie, kio, vb, ws = _proj_tail(hs, g_pre, w_t, *tabs_s, tm=m_s, seq_rows=m_s,
                                                        transposed=False)
        h_main, gv = _proj_main(xn, w_tiles, *tabs_s, ln_g, ln_b, tm=m_s, seq_rows=m_s, gv_rows=m_s,
                                cast_weights=False)
        w0 = jnp.repeat(a_ws[l][:, 0, 0], LANES)[None]
        b0 = jnp.repeat(a_bs[l][:, 0], LANES)[None]
        ya = _gate_row(h_main, gv, w0, b0)
        qi3 = _col_tile(h_main, T_QI).reshape(db, IDX_HEADS, IDX_DIM)
        kidx_t = jnp.swapaxes(cache_kidx[l], 1, 2)
        scores = _sample_scores(page_table, qi3, ws.reshape(db, IDX_HEADS, 1), kidx_t)
        bias, bnew = _sample_select(scores, h_main, kie, ws, top_k=top_k_s)
        bias3 = bias.reshape(db, n_pages, PAGE_SIZE)
        slots3, rows3 = _sample_compact(bias3, jnp.swapaxes(bias3, 1, 2),
                                        page_table.reshape(db, 1, n_pages), n_slots=top_k_s)
        pool_rows = lambda c: c.reshape(-1, N_KV_HEADS, HEAD_DIM)
        ob = _sample_attn(rows3.reshape(db, top_k_s),
                          _col_tile(h_main, T_Q).reshape(db, N_HEADS, HEAD_DIM),
                          slots3, bnew.reshape(db, 1, LANES),
                          kb.reshape(db, 1, KV_WIDTH), vb.reshape(db, 1, KV_WIDTH),
                          _col_tile(h_main, T_ZB).reshape(db, N_HEADS, HEAD_DIM),
                          pool_rows(cache_k[l]), pool_rows(cache_v[l])).reshape(db, B_WIDTH)
        hs = _merge(hs, ya, ob, h_main, woa, wob, wout, g_post, tm=m_s)
        outs[4].append(k.reshape(db, ds, N_KV_HEADS, HEAD_DIM))
        outs[5].append(v.reshape(db, ds, N_KV_HEADS, HEAD_DIM))
        outs[6].append(ki.reshape(db, ds, IDX_DIM))
        outs[7].append(gv.reshape(db, ds, A_WIDTH))

    st = [jnp.stack(o, axis=0) for o in outs]
    return (hp.reshape(batch, seq, D_MODEL), hs.reshape(db, ds, D_MODEL),
            st[0], st[1], st[2], st[3], st[4], st[5], st[6], st[7])
```

```python
import functools

import jax
import jax.numpy as jnp
from jax import lax
from jax.experimental import pallas as pl
from jax.experimental.pallas import tpu as pltpu

F32 = jnp.float32
BF16 = jnp.bfloat16

D_MODEL = 2048
CHUNK = 128
A_GROUPS = 8
A_WIDTH = 1024
N_HEADS = 8
N_KV_HEADS = 2
HEAD_DIM = 128
B_WIDTH = 1024
KV_WIDTH = 256
IDX_HEADS = 16
IDX_DIM = 64
TOPK_MAX = 256
ROPE_THETA = 10000.0
EPS = 1e-6
PAGE_SIZE = 128
Q_BLOCK = 128
NEG = -1e30
LOG2_E = 1.4426950408889634

LANES = 128
BF16_SUBLANES = 16
MXU_WIDTH = 256
KEY_STEP = 256
PROJ_ROWS = 512
MERGE_ROWS = 256
NORM_STREAMS = 4
_NT = (((1,), (1,)), ((), ()))
COL_TILE = 1024
T_UA, T_VA, T_ZA, T_Q, T_ZB, T_QI, T_GA, T_GB = 0, 1, 2, 3, 4, 5, 6, 8
N_MAIN_TILES = 10
ROW_K = 4 * COL_TILE
ROW_ZB = ROW_K + 2 * KV_WIDTH
ROW_KI = ROW_ZB + B_WIDTH + IDX_HEADS * IDX_DIM
ROW_GA = ROW_KI + IDX_DIM + IDX_HEADS
VMEM_LIMIT = 56 * 1024 * 1024


def _cparams(sem):
    return pltpu.CompilerParams(dimension_semantics=sem, vmem_limit_bytes=VMEM_LIMIT)


def _rope_tables(pos, dim):
    half = dim // 2
    inv = ROPE_THETA ** (-jnp.arange(half, dtype=F32) / half)
    ang = pos.astype(F32)[:, None] * inv[None, :]
    cos = jnp.cos(ang)
    sin = jnp.sin(ang)
    reps = LANES // dim
    cos_t = jnp.tile(jnp.concatenate([cos, cos], axis=-1), (1, reps))
    sin_t = jnp.tile(jnp.concatenate([-sin, sin], axis=-1), (1, reps))
    return cos_t, sin_t


def _rope128(x, cos, sin):
    return x * cos + pltpu.roll(x, 64, 1) * sin


def _rope64(x, cos, sin):
    lane = lax.broadcasted_iota(jnp.int32, x.shape, 1)
    first = (lane % IDX_DIM) < (IDX_DIM // 2)
    partner = jnp.where(first, pltpu.roll(x, LANES - 32, 1), pltpu.roll(x, 32, 1))
    return x * cos + partner * sin


def _rmsnorm_rows(xf, g):
    ms = jnp.mean(xf * xf, axis=-1, keepdims=True)
    return xf * lax.rsqrt(ms + EPS) * g


_TILE_KINDS = ("copy", "ln", "silu", "rope128", "silu", "rope64", "sigmoid", "sigmoid", "sigmoid", "sigmoid")


def _sigmoid(x):
    return 0.5 * jnp.tanh(0.5 * x) + 0.5


def _prenorm_rows(x_refs, g_ref, o_ref):
    xs = [r[...] for r in x_refs]
    ms = sum(jnp.sum(x * x, axis=-1, keepdims=True) for x in xs) * (1.0 / D_MODEL)
    scale = lax.rsqrt(ms + EPS)
    w = xs[0].shape[1]
    for c, x in enumerate(xs):
        cs = slice(c * w, (c + 1) * w)
        o_ref[:, cs] = (x * scale * g_ref[:, cs]).astype(BF16)


def _proj_main_body(xn_ref, *refs, gv_rows, n_row_tiles, cast_weights):
    n_chunks = COL_TILE // MXU_WIDTH
    w_refs, refs = refs[:n_chunks], refs[n_chunks:]
    cq_ref, sq_ref, ci_ref, si_ref, lng_ref, lnb_ref, h_ref, gv_ref = refs[:8]
    wbf_ref = refs[8] if cast_weights else None
    acc_ref = refs[-1]
    j = pl.program_id(0)
    i = pl.program_id(1)
    tm = xn_ref.shape[0]
    chunks = [slice(c * MXU_WIDTH, (c + 1) * MXU_WIDTH) for c in range(n_chunks)]
    weights = lambda c: wbf_ref[chunks[c], :] if cast_weights else w_refs[c][...]

    def finish_chunk(kind, cs, stats):
        acc = acc_ref[:, cs]
        if kind == "copy":
            h_ref[:, cs] = acc.astype(BF16)
        elif kind == "ln":
            mu, rstd = stats
            vn = (acc - mu) * rstd * lng_ref[:, cs] + lnb_ref[:, cs]
            h_ref[:, cs] = vn.astype(BF16)
            gv_ref[:, cs] = vn[tm - gv_rows:, :]
        elif kind == "silu":
            h_ref[:, cs] = (acc * _sigmoid(acc)).astype(BF16)
        elif kind == "sigmoid":
            h_ref[:, cs] = _sigmoid(acc).astype(BF16)
        else:
            rope, cos, sin = ((_rope128, cq_ref[...], sq_ref[...]) if kind == "rope128"
                              else (_rope64, ci_ref[...], si_ref[...]))
            for h in range(MXU_WIDTH // LANES):
                sl = slice(cs.start + h * LANES, cs.start + (h + 1) * LANES)
                h_ref[:, sl] = rope(acc[:, h * LANES:(h + 1) * LANES], cos, sin).astype(BF16)

    def run(kind, finish, matmul):
        stats = None
        if finish and kind == "ln":
            acc = acc_ref[...]
            mu = jnp.mean(acc, axis=-1, keepdims=True)
            d = acc - mu
            stats = (mu, lax.rsqrt(jnp.mean(d * d, axis=-1, keepdims=True) + EPS))
        for c, cs in enumerate(chunks):
            if finish:
                finish_chunk(kind, cs, stats)
            if matmul:
                acc_ref[:, cs] = lax.dot_general(xn_ref[...], weights(c), _NT, preferred_element_type=F32)

    @pl.when(i == 0)
    def _():
        if cast_weights:
            for c, cs in enumerate(chunks):
                wbf_ref[cs, :] = w_refs[c][...].astype(BF16)
        run(None, False, True)

    @pl.when((i == 0) & (j != T_VA))
    def _():
        gv_ref[...] = jnp.zeros(gv_ref.shape, F32)

    for kind in sorted(set(_TILE_KINDS)):
        is_kind = functools.reduce(jnp.logical_or, [j == t for t, k in enumerate(_TILE_KINDS) if k == kind])

        @pl.when(is_kind & (i > 0) & (i < n_row_tiles))
        def _(kind=kind):
            run(kind, True, True)

        @pl.when(is_kind & (i == n_row_tiles))
        def _(kind=kind):
            run(kind, True, False)


def _main_tile_row(j, chunk):
    g = BF16_SUBLANES
    skip_kv = (ROW_ZB - T_ZB * COL_TILE) // g
    skip_idx = (ROW_GA - ROW_ZB - (T_GA - T_ZB) * COL_TILE) // g
    return (j * (COL_TILE // g) + chunk * (MXU_WIDTH // g)
            + jnp.where(j >= T_ZB, skip_kv, 0) + jnp.where(j >= T_GA, skip_idx, 0)) * g


def _proj_main(xn, w, cq, sq, ci, si, ln_g, ln_b, *, tm, seq_rows, gv_rows, cast_weights):
    m = xn.shape[0]
    tiles_per_seq = seq_rows // tm
    n_seq = m // seq_rows
    n_row_tiles = m // tm
    n_chunks = COL_TILE // MXU_WIDTH
    body = functools.partial(_proj_main_body, gv_rows=gv_rows, n_row_tiles=n_row_tiles,
                             cast_weights=cast_weights)
    w_tile = lambda j, i: jnp.minimum(jnp.where(i == n_row_tiles, j + 1, j), N_MAIN_TILES - 1)
    if cast_weights:
        w_chunk = lambda c: pl.BlockSpec((pl.Element(MXU_WIDTH), pl.Element(D_MODEL)),
                                         lambda j, i: (_main_tile_row(w_tile(j, i), c), 0))
    else:
        w_chunk = lambda c: pl.BlockSpec((MXU_WIDTH, D_MODEL), lambda j, i: (w_tile(j, i) * n_chunks + c, 0))
    w_out_spec = [pl.BlockSpec((COL_TILE, D_MODEL), lambda j, i: (j, 0))] if cast_weights else []
    w_out_shape = [jax.ShapeDtypeStruct((N_MAIN_TILES * COL_TILE, D_MODEL), BF16)] if cast_weights else []
    prev = lambda i: jnp.maximum(i - 1, 0)
    tab = pl.BlockSpec((tm, LANES), lambda j, i: (prev(i) % tiles_per_seq, 0))
    row = lambda n: pl.BlockSpec((1, n), lambda j, i: (0, 0))
    gv_block = lambda j, i: (jnp.where(j == T_VA, prev(i) // tiles_per_seq, n_seq + (j > T_VA)), 0)
    h_main, gv, *w_bf = pl.pallas_call(
        body,
        grid=(N_MAIN_TILES, n_row_tiles + 1),
        in_specs=[pl.BlockSpec((tm, D_MODEL), lambda j, i: (jnp.minimum(i, n_row_tiles - 1), 0))]
                 + [w_chunk(c) for c in range(n_chunks)]
                 + [tab, tab, tab, tab, row(A_WIDTH), row(A_WIDTH)],
        out_specs=[
            pl.BlockSpec((tm, COL_TILE), lambda j, i: (prev(i), j)),
            pl.BlockSpec((gv_rows, A_WIDTH), gv_block),
        ] + w_out_spec,
        out_shape=[
            jax.ShapeDtypeStruct((m, N_MAIN_TILES * COL_TILE), BF16),
            jax.ShapeDtypeStruct(((n_seq + 2) * gv_rows, A_WIDTH), F32),
        ] + w_out_shape,
        scratch_shapes=[pltpu.VMEM((tm, COL_TILE), F32)],
        compiler_params=_cparams(("arbitrary", "arbitrary")),
        name="proj_main",
    )(xn, *([w] * n_chunks), cq, sq, ci, si, ln_g, ln_b)
    return (h_main, gv[:n_seq * gv_rows], *w_bf)


def _proj_tail_body(*refs, transposed):
    x_refs, refs = refs[:NORM_STREAMS], refs[NORM_STREAMS:]
    (g_ref, wkv_ref, wix_ref, ck_ref, sk_ref, ci_ref, si_ref,
     xn_ref, k_ref, v_ref, ki_ref, kb_ref, kie_ref, kio_ref, vb_ref, ws_ref) = refs
    _prenorm_rows(x_refs, g_ref, xn_ref)
    xn = xn_ref[...]
    acc = lax.dot_general(xn, wkv_ref[...].astype(BF16), _NT, preferred_element_type=F32)
    cos = ck_ref[...]
    sin = sk_ref[...]
    for kh in range(N_KV_HEADS):
        sl = slice(kh * HEAD_DIM, (kh + 1) * HEAD_DIM)
        r = _rope128(acc[:, sl], cos, sin)
        k_ref[:, kh, :] = r
        kb_ref[:, sl] = r.astype(BF16)
        v_ref[:, kh, :] = acc[:, KV_WIDTH + kh * HEAD_DIM:KV_WIDTH + (kh + 1) * HEAD_DIM]
    t = lax.dot_general(xn, wix_ref[...].astype(BF16), _NT, preferred_element_type=F32)
    r = _rope64(t, ci_ref[...], si_ref[...])
    ki_ref[...] = r[:, :IDX_DIM]
    lane = lax.broadcasted_iota(jnp.int32, r.shape, 1)
    ke = jnp.where(lane < IDX_DIM, r, 0.0)
    kie_ref[...] = ke.astype(BF16)
    kio_ref[...] = pltpu.roll(ke, IDX_DIM, 1).astype(BF16)
    w_scale = IDX_HEADS ** -0.5 * IDX_DIM ** -0.5
    v = acc[:, KV_WIDTH:2 * KV_WIDTH]
    if transposed:
        vb_ref[...] = v.T.astype(BF16)
        ws_ref[...] = t.T[IDX_DIM:IDX_DIM + IDX_HEADS, :] * w_scale
    else:
        vb_ref[...] = v.astype(BF16)
        ws_ref[...] = t[:, IDX_DIM:IDX_DIM + IDX_HEADS] * w_scale


def _proj_tail(x2, pre_g, w_t, ck, sk, ci, si, *, tm, seq_rows, transposed):
    m = x2.shape[0]
    x_col = lambda c: pl.BlockSpec((tm, D_MODEL // NORM_STREAMS), lambda i: (i, c))
    if transposed:
        vw_specs = [pl.BlockSpec((n, tm), lambda i: (0, i)) for n in (KV_WIDTH, IDX_HEADS)]
        vw_shapes = [jax.ShapeDtypeStruct((KV_WIDTH, m), BF16), jax.ShapeDtypeStruct((IDX_HEADS, m), F32)]
    else:
        vw_specs = [pl.BlockSpec((tm, n), lambda i: (i, 0)) for n in (KV_WIDTH, IDX_HEADS)]
        vw_shapes = [jax.ShapeDtypeStruct((m, KV_WIDTH), BF16), jax.ShapeDtypeStruct((m, IDX_HEADS), F32)]
    tiles_per_seq = seq_rows // tm
    tab = pl.BlockSpec((tm, LANES), lambda i: (i % tiles_per_seq, 0))
    blk = lambda n: pl.BlockSpec((tm, n), lambda i: (i, 0))
    kv_rows = pl.BlockSpec((tm, N_KV_HEADS, HEAD_DIM), lambda i: (i, 0, 0))
    return pl.pallas_call(
        functools.partial(_proj_tail_body, transposed=transposed),
        grid=(m // tm,),
        in_specs=[x_col(c) for c in range(NORM_STREAMS)] + [
            pl.BlockSpec((1, D_MODEL), lambda i: (0, 0)),
            pl.BlockSpec((2 * KV_WIDTH, D_MODEL), lambda i: (ROW_K // (2 * KV_WIDTH), 0)),
            pl.BlockSpec((LANES, D_MODEL), lambda i: (ROW_KI // LANES, 0)),
            tab, tab, tab, tab,
        ],
        out_specs=[blk(D_MODEL), kv_rows, kv_rows, blk(IDX_DIM), blk(KV_WIDTH),
                   blk(LANES), blk(LANES)] + vw_specs,
        out_shape=[
            jax.ShapeDtypeStruct((m, D_MODEL), BF16),
            jax.ShapeDtypeStruct((m, N_KV_HEADS, HEAD_DIM), F32),
            jax.ShapeDtypeStruct((m, N_KV_HEADS, HEAD_DIM), F32),
            jax.ShapeDtypeStruct((m, IDX_DIM), F32),
            jax.ShapeDtypeStruct((m, KV_WIDTH), BF16),
            jax.ShapeDtypeStruct((m, LANES), BF16),
            jax.ShapeDtypeStruct((m, LANES), BF16),
        ] + vw_shapes,
        compiler_params=_cparams(("arbitrary",)),
        name="proj_tail",
    )(*([x2] * NORM_STREAMS), pre_g, w_t, w_t, ck, sk, ci, si)


def _gate_body(u_ref, vn_ref, sz_ref, ws_ref, bst_ref, y_ref, *, n_chunks):
    rr = lax.broadcasted_iota(jnp.int32, (CHUNK, CHUNK), 0)
    cc = lax.broadcasted_iota(jnp.int32, (CHUNK, CHUNK), 1)
    tril = cc <= rr
    for g in range(A_GROUPS):
        wm = jnp.where(tril, ws_ref[g], 0.0).astype(BF16)
        b = bst_ref[:, g:g + 1]
        cs = slice(g * LANES, (g + 1) * LANES)
        for c in range(n_chunks):
            rs = slice(c * CHUNK, (c + 1) * CHUNK)
            s = jnp.dot(wm, vn_ref[rs, cs], preferred_element_type=F32) + b
            y = u_ref[rs, cs].astype(F32) * s * sz_ref[rs, cs].astype(F32)
            y_ref[rs, cs] = y.astype(BF16)


def _gate_row_body(u_ref, vn_ref, sz_ref, w0_ref, b0_ref, y_ref):
    s = vn_ref[...] * w0_ref[...] + b0_ref[...]
    y_ref[...] = (u_ref[...].astype(F32) * s * sz_ref[...].astype(F32)).astype(BF16)


def _gate_row(h_main, vn, w0, b0):
    m = h_main.shape[0]
    col = lambda t: pl.BlockSpec((m, COL_TILE), lambda i, t=t: (0, t))
    full = lambda r: pl.BlockSpec((r, A_WIDTH), lambda i: (0, 0))
    return pl.pallas_call(
        _gate_row_body,
        grid=(1,),
        in_specs=[col(T_UA), full(m), col(T_ZA), full(1), full(1)],
        out_specs=full(m),
        out_shape=jax.ShapeDtypeStruct((m, A_WIDTH), BF16),
        compiler_params=_cparams(("arbitrary",)),
        name="gate_sample",
    )(h_main, vn, h_main, w0, b0)


def _lane_bcast(col, rows):
    return jnp.broadcast_to(col, (rows, LANES))


def _select_bias(sc_ref, extra_ref, kp, row_min, row_max, *, n_cols):
    rows = sc_ref.shape[0]
    n_tiles = n_cols // LANES
    extra = None if extra_ref is None else extra_ref[...]

    def count(pred):
        acc = jnp.zeros((rows, LANES), F32)
        for c in range(n_tiles):
            acc = acc + jnp.where(pred(sc_ref[:, c * LANES:(c + 1) * LANES], c), 1.0, 0.0)
        tot = jnp.sum(acc, axis=1, keepdims=True)
        return _lane_bcast(tot, rows)

    def count_ge(x):
        c = count(lambda s, _: s >= x)
        if extra is not None:
            c = c + jnp.where(extra >= x, 1.0, 0.0)
        return c

    c_max = count_ge(row_max)
    top = c_max >= kp
    lo0 = jnp.where(top, row_max, row_min)
    c0 = jnp.where(top, c_max, count_ge(row_min))

    def count_ge3(x1, x2, x3):
        a1 = jnp.zeros((rows, LANES), F32)
        a2 = jnp.zeros((rows, LANES), F32)
        a3 = jnp.zeros((rows, LANES), F32)
        for c in range(n_tiles):
            s = sc_ref[:, c * LANES:(c + 1) * LANES]
            a1 = a1 + jnp.where(s >= x1, 1.0, 0.0)
            a2 = a2 + jnp.where(s >= x2, 1.0, 0.0)
            a3 = a3 + jnp.where(s >= x3, 1.0, 0.0)
        res = []
        for a, x in ((a1, x1), (a2, x2), (a3, x3)):
            tot = _lane_bcast(jnp.sum(a, axis=1, keepdims=True), rows)
            if extra is not None:
                tot = tot + jnp.where(extra >= x, 1.0, 0.0)
            res.append(tot)
        return res

    def step(st):
        lo, hi, c_lo, _, it = st
        mid = 0.5 * lo + 0.5 * hi
        act = (c_lo != kp) & (mid > lo) & (mid < hi)
        any_act = jnp.max(jnp.where(act, 1.0, 0.0))
        clamp = lambda x: jnp.minimum(jnp.maximum(x, lo), hi)
        q1 = clamp(0.75 * lo + 0.25 * hi)
        q3 = clamp(0.25 * lo + 0.75 * hi)
        c1, c2, c3 = count_ge3(q1, mid, q3)
        g1, g2, g3 = c1 >= kp, c2 >= kp, c3 >= kp
        lo_n = jnp.where(g3, q3, jnp.where(g2, mid, jnp.where(g1, q1, lo)))
        c_n = jnp.where(g3, c3, jnp.where(g2, c2, jnp.where(g1, c1, c_lo)))
        hi_n = jnp.where(g3, hi, jnp.where(g2, q3, jnp.where(g1, mid, jnp.minimum(q1, mid))))
        return (jnp.where(act, lo_n, lo), jnp.where(act, hi_n, hi), jnp.where(act, c_n, c_lo),
                any_act, it + 1)

    def cond(st):
        return (st[3] > 0.0) & (st[4] < 400)

    lo, _, c_lo, _, _ = lax.while_loop(cond, step, (lo0, row_max, c0, jnp.float32(1.0), jnp.int32(0)))

    exact = jnp.max(jnp.where(c_lo != kp, 1.0, 0.0)) == 0.0

    @pl.when(exact)
    def _():
        for c in range(n_tiles):
            sl = slice(c * LANES, (c + 1) * LANES)
            sc_ref[:, sl] = jnp.where(sc_ref[:, sl] >= lo, 0.0, NEG)
        if extra is not None:
            extra_ref[...] = jnp.where(extra >= lo, 0.0, NEG)

    @pl.when(jnp.logical_not(exact))
    def _():
        n_gt = count(lambda s, _: s > lo)
        if extra is not None:
            n_gt = n_gt + jnp.where(extra > lo, 1.0, 0.0)
        need = kp - n_gt
        lane = lax.broadcasted_iota(jnp.int32, (rows, LANES), 1).astype(F32)

        def count_eq_upto(jx):
            c = count(lambda s, c: (s == lo) & (lane + float(c * LANES) <= jx))
            if extra is not None:
                c = c + jnp.where((extra == lo) & (jx >= float(n_cols)), 1.0, 0.0)
            return c

        last = n_cols if extra is not None else n_cols - 1
        j_lo = jnp.full((rows, LANES), -1.0, F32)
        j_hi = jnp.full((rows, LANES), float(last), F32)

        def jstep(_, st):
            a, b = st
            mid = jnp.floor(0.5 * (a + b))
            ok = count_eq_upto(mid) >= need
            return jnp.where(ok, a, mid), jnp.where(ok, mid, b)

        n_steps = max(1, (n_cols + 1).bit_length())
        _, j_hi = lax.fori_loop(0, n_steps, jstep, (j_lo, j_hi))
        for c in range(n_tiles):
            sl = slice(c * LANES, (c + 1) * LANES)
            s = sc_ref[:, sl]
            keep = (s > lo) | ((s == lo) & (lane + float(c * LANES) <= j_hi))
            sc_ref[:, sl] = jnp.where(keep, 0.0, NEG)
        if extra is not None:
            keep = (extra > lo) | ((extra == lo) & (j_hi >= float(n_cols)))
            extra_ref[...] = jnp.where(keep, 0.0, NEG)


def _select_bias_cols(sc_ref, kp, col_min, col_max, *, n_rows):
    n_tiles = n_rows // LANES
    n_pivots = 3 if n_rows <= 512 else 2 if n_rows <= 1024 else 1

    def counts(preds):
        accs = [jnp.zeros((LANES, LANES), F32) for _ in preds]
        for r in range(n_tiles):
            s = sc_ref[r * LANES:(r + 1) * LANES, :]
            accs = [a + jnp.where(p(s, r), 1.0, 0.0) for a, p in zip(accs, preds)]
        return [jnp.sum(a, axis=0, keepdims=True) for a in accs]

    count = lambda pred: counts([pred])[0]
    count_ge = lambda x: count(lambda s, _: s >= x)
    c_max = count_ge(col_max)
    top = c_max >= kp
    lo0 = jnp.where(top, col_max, col_min)
    c0 = jnp.where(top, c_max, count_ge(col_min))

    def step(st):
        lo, hi, c_lo, _, it = st
        mid = 0.5 * lo + 0.5 * hi
        act = (c_lo != kp) & (mid > lo) & (mid < hi)
        any_act = jnp.max(jnp.where(act, 1.0, 0.0))
        fr = [(k + 1) / (n_pivots + 1) for k in range(n_pivots)]
        piv = [mid if f == 0.5 else jnp.minimum(jnp.maximum((1.0 - f) * lo + f * hi, lo), hi) for f in fr]
        cs = counts([lambda s, _, x=x: s >= x for x in piv])
        lo_n, c_n, hi_n = lo, c_lo, functools.reduce(jnp.minimum, piv)
        for k in range(n_pivots):
            ge = cs[k] >= kp
            nxt = piv[k + 1] if k + 1 < n_pivots else hi
            lo_n = jnp.where(ge, piv[k], lo_n)
            c_n = jnp.where(ge, cs[k], c_n)
            hi_n = jnp.where(ge, nxt, hi_n)
        return jnp.where(act, lo_n, lo), jnp.where(act, hi_n, hi), jnp.where(act, c_n, c_lo), any_act, it + 1

    def cond(st):
        return (st[3] > 0.0) & (st[4] < 400)

    lo, _, c_lo, _, _ = lax.while_loop(cond, step, (lo0, col_max, c0, jnp.float32(1.0), jnp.int32(0)))
    exact = jnp.max(jnp.where(c_lo != kp, 1.0, 0.0)) == 0.0

    @pl.when(exact)
    def _():
        for r in range(n_tiles):
            rs = slice(r * LANES, (r + 1) * LANES)
            sc_ref[rs, :] = jnp.where(sc_ref[rs, :] >= lo, 0.0, NEG)

    @pl.when(jnp.logical_not(exact))
    def _():
        need = kp - count(lambda s, _: s > lo)
        key = lax.broadcasted_iota(jnp.int32, (LANES, LANES), 0).astype(F32)
        count_eq_upto = lambda jx: count(lambda s, r: (s == lo) & (key + float(r * LANES) <= jx))

        def jstep(_, st):
            a, b = st
            mid = jnp.floor(0.5 * (a + b))
            ok = count_eq_upto(mid) >= need
            return jnp.where(ok, a, mid), jnp.where(ok, mid, b)

        j_lo = jnp.full((1, LANES), -1.0, F32)
        j_hi = jnp.full((1, LANES), float(n_rows - 1), F32)
        _, j_hi = lax.fori_loop(0, max(1, n_rows.bit_length()), jstep, (j_lo, j_hi))
        for r in range(n_tiles):
            rs = slice(r * LANES, (r + 1) * LANES)
            s = sc_ref[rs, :]
            keep = (s > lo) | ((s == lo) & (key + float(r * LANES) <= j_hi))
            sc_ref[rs, :] = jnp.where(keep, 0.0, NEG)


def _prompt_attn_block(nk, qi_ref, wst_ref, kie_ref, kio_ref, q_ref, kb_ref, vt_ref, sz_ref,
                       o_ref, sc_ref, *, top_k, key_chunk):
    qb = pl.program_id(1)
    n_pairs = IDX_HEADS // 2
    grp = N_HEADS // N_KV_HEADS
    qpos = qb * Q_BLOCK + lax.broadcasted_iota(jnp.int32, (LANES, LANES), 1)
    key0 = lax.broadcasted_iota(jnp.int32, (LANES, LANES), 0)

    qs = jnp.concatenate([qi_ref[:, p * LANES:(p + 1) * LANES] for p in range(n_pairs)], axis=0)
    wrows = [wst_ref[h:h + 1, :] for h in range(IDX_HEADS)]
    cmax = jnp.full((LANES, LANES), -jnp.inf, F32)
    cmin = jnp.full((LANES, LANES), jnp.inf, F32)
    for kc in range(0, nk, key_chunk):
        le = lax.dot_general(kie_ref[kc:kc + key_chunk, :], qs, _NT, preferred_element_type=F32)
        lo = lax.dot_general(kio_ref[kc:kc + key_chunk, :], qs, _NT, preferred_element_type=F32)
        for r in range(key_chunk // LANES):
            rs = slice(r * LANES, (r + 1) * LANES)
            acc = jnp.zeros((LANES, LANES), F32)
            for p in range(n_pairs):
                cs = slice(p * LANES, (p + 1) * LANES)
                acc = acc + jnp.maximum(le[rs, cs], 0.0) * wrows[2 * p]
                acc = acc + jnp.maximum(lo[rs, cs], 0.0) * wrows[2 * p + 1]
            causal = key0 + (kc + r * LANES) <= qpos
            cmax = jnp.maximum(cmax, jnp.where(causal, acc, -jnp.inf))
            cmin = jnp.minimum(cmin, jnp.where(causal, acc, jnp.inf))
            sc_ref[kc + r * LANES:kc + (r + 1) * LANES, :] = jnp.where(causal, acc, -jnp.inf)

    col_max = jnp.max(cmax, axis=0, keepdims=True)
    col_min = jnp.min(cmin, axis=0, keepdims=True)
    kp = jnp.minimum(qpos[0:1, :] + 1, top_k).astype(F32)
    _select_bias_cols(sc_ref, kp, col_min, col_max, n_rows=nk)

    bias = jnp.concatenate([sc_ref[0:nk, :]] * grp, axis=1)
    for kh in range(N_KV_HEADS):
        qh = jnp.concatenate(
            [q_ref[:, (kh * grp + g) * HEAD_DIM:(kh * grp + g + 1) * HEAD_DIM] for g in range(grp)], axis=0)
        ks = slice(kh * HEAD_DIM, (kh + 1) * HEAD_DIM)
        s = lax.dot_general(kb_ref[0:nk, ks], qh, _NT, preferred_element_type=F32) + bias
        m = jnp.max(s, axis=0, keepdims=True)
        p = jnp.exp2((s - m) * (HEAD_DIM ** -0.5 * LOG2_E))
        l = jnp.sum(p, axis=0, keepdims=True)
        ot = jnp.dot(vt_ref[ks, 0:nk], p.astype(BF16), preferred_element_type=F32) / l
        for g in range(grp):
            hs = slice((kh * grp + g) * HEAD_DIM, (kh * grp + g + 1) * HEAD_DIM)
            o = ot[:, g * LANES:(g + 1) * LANES].T
            o_ref[:, hs] = (o * sz_ref[:, hs].astype(F32)).astype(BF16)


def _key_ranges(seq):
    fine = list(range(KEY_STEP, min(seq, 4 * KEY_STEP) + 1, KEY_STEP))
    return fine + list(range(fine[-1] + 2 * KEY_STEP, seq + 1, 2 * KEY_STEP))


def _prompt_attn_body(*refs, seq, top_k):
    need = (pl.program_id(1) + 1) * Q_BLOCK
    lower = 0
    for nk in _key_ranges(seq):
        @pl.when((need > lower) & (need <= nk))
        def _(nk=nk):
            _prompt_attn_block(nk, *refs, top_k=top_k, key_chunk=KEY_STEP)
        lower = nk


def _prompt_attn(h_main, ws_t, kie, kio, kb, vb_t, *, batch, seq):
    top_k = min(TOPK_MAX, seq // 4)
    n_qb = seq // Q_BLOCK
    body = functools.partial(_prompt_attn_body, seq=seq, top_k=top_k)
    col = lambda t: pl.BlockSpec((Q_BLOCK, COL_TILE), lambda b, q, t=t: (b * n_qb + q, t))
    seqblk = lambda n: pl.BlockSpec((seq, n), lambda b, q: (b, 0))
    return pl.pallas_call(
        body,
        grid=(batch, n_qb),
        in_specs=[col(T_QI),
                  pl.BlockSpec((IDX_HEADS, Q_BLOCK), lambda b, q: (0, b * n_qb + q)),
                  seqblk(LANES), seqblk(LANES),
                  col(T_Q), seqblk(KV_WIDTH),
                  pl.BlockSpec((KV_WIDTH, seq), lambda b, q: (0, b)),
                  col(T_ZB)],
        out_specs=pl.BlockSpec((Q_BLOCK, B_WIDTH), lambda b, q: (b * n_qb + q, 0)),
        out_shape=jax.ShapeDtypeStruct((batch * seq, B_WIDTH), BF16),
        scratch_shapes=[pltpu.VMEM((seq, Q_BLOCK), F32)],
        compiler_params=_cparams(("arbitrary", "arbitrary")),
        name="attn_prompt",
    )(h_main, ws_t, kie, kio, h_main, kb, vb_t, h_main)


def _merge_body(x_ref, *refs, fused_gate):
    if fused_gate:
        gate_refs, refs = refs[:5], refs[5:]
        ya_ref = refs[-1]
        _gate_body(*gate_refs, ya_ref, n_chunks=x_ref.shape[0] // CHUNK)
        ob_ref, ga_ref, gb_ref, woa_ref, wob_ref, wout_ref, pg_ref, o_ref = refs[:-1]
    else:
        ya_ref, ob_ref, ga_ref, gb_ref, woa_ref, wob_ref, wout_ref, pg_ref, o_ref = refs
    pa = jnp.dot(ya_ref[...], woa_ref[...], preferred_element_type=F32)
    pb = jnp.dot(ob_ref[...], wob_ref[...], preferred_element_type=F32)
    mix = ga_ref[...].astype(F32) * pa + gb_ref[...].astype(F32) * pb
    r = jnp.dot(mix.astype(BF16), wout_ref[...], preferred_element_type=F32)
    o_ref[...] = x_ref[...] + _rmsnorm_rows(r, pg_ref[...])


def _merge(x2, ya, ob, h_main, w_oa, w_ob, w_out, post_g, *, tm, gate=None):
    m = x2.shape[0]
    const = lambda r, c: pl.BlockSpec((r, c), lambda i: (0, 0), pipeline_mode=pl.Buffered(1))
    if gate is None:
        a_specs = [pl.BlockSpec((tm, A_WIDTH), lambda i: (i, 0))]
        a_args, scratch = [ya], []
    else:
        col = lambda t: pl.BlockSpec((tm, COL_TILE), lambda i, t=t: (i, t))
        a_specs = [col(T_UA), col(T_VA), col(T_ZA),
                   pl.BlockSpec((A_GROUPS, CHUNK, CHUNK), lambda i: (0, 0, 0)),
                   pl.BlockSpec((CHUNK, A_GROUPS), lambda i: (0, 0))]
        a_args, scratch = [h_main, h_main, h_main, *gate], [pltpu.VMEM((tm, A_WIDTH), BF16)]
    return pl.pallas_call(
        functools.partial(_merge_body, fused_gate=gate is not None),
        grid=(m // tm,),
        in_specs=[pl.BlockSpec((tm, D_MODEL), lambda i: (i, 0))] + a_specs + [
            pl.BlockSpec((tm, B_WIDTH), lambda i: (i, 0)),
            pl.BlockSpec((tm, D_MODEL), lambda i: (i, T_GA // 2)),
            pl.BlockSpec((tm, D_MODEL), lambda i: (i, T_GB // 2)),
            const(A_WIDTH, D_MODEL), const(B_WIDTH, D_MODEL), const(D_MODEL, D_MODEL),
            const(1, D_MODEL),
        ],
        out_specs=pl.BlockSpec((tm, D_MODEL), lambda i: (i, 0)),
        out_shape=jax.ShapeDtypeStruct((m, D_MODEL), F32),
        scratch_shapes=scratch,
        compiler_params=_cparams(("arbitrary",)),
        name="merge",
    )(x2, *a_args, ob, h_main, h_main, w_oa, w_ob, w_out, post_g)


SCORE_CHUNK = 2048
COMPACT_SEQS = 4
GATHER_BUFS = 3


def _sample_scores_body(pt_ref, q_ref, w_ref, kidx_hbm, o_ref, buf, sem):
    db, n_pages = pt_ref.shape
    past = n_pages * PAGE_SIZE

    def page_copy(b, p, slot):
        dst = buf.at[slot, :, pl.ds(pl.multiple_of(p * PAGE_SIZE, PAGE_SIZE), PAGE_SIZE)]
        return pltpu.make_async_copy(kidx_hbm.at[pt_ref[b, p]], dst, sem.at[slot])

    def start_all(b, slot):
        def f(t, c):
            page_copy(b, 2 * t, slot).start(priority=0)
            page_copy(b, 2 * t + 1, slot).start(priority=1)
            return c
        lax.fori_loop(0, n_pages // 2, f, 0, unroll=4)

    def wait_all(slot):
        for p in range(n_pages):
            dst = buf.at[slot, :, pl.ds(p * PAGE_SIZE, PAGE_SIZE)]
            pltpu.make_async_copy(kidx_hbm.at[0], dst, sem.at[slot]).wait()

    start_all(0, 0)

    def per_seq(b, c):
        slot = b % 2

        @pl.when(b + 1 < db)
        def _():
            start_all(b + 1, 1 - slot)

        wait_all(slot)
        q = q_ref[b]
        w = w_ref[b]
        for ch in range(past // SCORE_CHUNK):
            cs = slice(ch * SCORE_CHUNK, (ch + 1) * SCORE_CHUNK)
            logit = jnp.dot(q, buf[slot, :, cs].astype(BF16), preferred_element_type=F32)
            o_ref[pl.ds(b, 1), cs] = jnp.sum(jnp.maximum(logit, 0.0) * w, axis=0, keepdims=True)
        return c

    lax.fori_loop(0, db, per_seq, 0)


def _sample_scores(page_table, qi3, ws3, kidx_pages_t):
    db, n_pages = page_table.shape
    past = n_pages * PAGE_SIZE
    grid_spec = pltpu.PrefetchScalarGridSpec(
        num_scalar_prefetch=1,
        grid=(1,),
        in_specs=[pl.BlockSpec((db, IDX_HEADS, IDX_DIM), lambda i, pt: (0, 0, 0)),
                  pl.BlockSpec((db, IDX_HEADS, 1), lambda i, pt: (0, 0, 0)),
                  pl.BlockSpec(memory_space=pl.ANY)],
        out_specs=pl.BlockSpec((db, past), lambda i, pt: (0, 0)),
        scratch_shapes=[pltpu.VMEM((2, IDX_DIM, past), F32), pltpu.SemaphoreType.DMA((2,))],
    )
    return pl.pallas_call(
        _sample_scores_body,
        grid_spec=grid_spec,
        out_shape=jax.ShapeDtypeStruct((db, past), F32),
        compiler_params=_cparams(("arbitrary",)),
        name="scores_sample",
    )(page_table, qi3, ws3, kidx_pages_t)


def _sample_select_body(sc_ref, qi_ref, kie_ref, ws_ref, bias_ref, bnew_ref, *, top_k):
    rows, past = sc_ref.shape
    lane = lax.broadcasted_iota(jnp.int32, (rows, LANES), 1)
    ki = kie_ref[...].astype(F32)
    ki = ki + pltpu.roll(ki, IDX_DIM, 1)
    s_new = jnp.zeros((rows, 1), F32)
    for p in range(IDX_HEADS // 2):
        prod = qi_ref[:, p * LANES:(p + 1) * LANES].astype(F32) * ki
        l_even = jnp.sum(jnp.where(lane < IDX_DIM, prod, 0.0), axis=1, keepdims=True)
        l_odd = jnp.sum(jnp.where(lane >= IDX_DIM, prod, 0.0), axis=1, keepdims=True)
        s_new = s_new + jnp.maximum(l_even, 0.0) * ws_ref[:, 2 * p:2 * p + 1]
        s_new = s_new + jnp.maximum(l_odd, 0.0) * ws_ref[:, 2 * p + 1:2 * p + 2]
    extra = _lane_bcast(s_new, rows)
    bnew_ref[...] = extra
    rmax = extra
    rmin = extra
    for c in range(past // LANES):
        sl = slice(c * LANES, (c + 1) * LANES)
        s = sc_ref[:, sl]
        bias_ref[:, sl] = s
        rmax = jnp.maximum(rmax, s)
        rmin = jnp.minimum(rmin, s)
    row_max = _lane_bcast(jnp.max(rmax, axis=1, keepdims=True), rows)
    row_min = _lane_bcast(jnp.min(rmin, axis=1, keepdims=True), rows)
    kp = jnp.full((rows, LANES), float(top_k), F32)
    _select_bias(bias_ref, bnew_ref, kp, row_min, row_max, n_cols=past)


def _sample_select(scores, h_main, kie, ws, *, top_k):
    db, past = scores.shape
    full = lambda r, c: pl.BlockSpec((r, c), lambda i: (0, 0))
    return pl.pallas_call(
        functools.partial(_sample_select_body, top_k=top_k),
        grid=(1,),
        in_specs=[full(db, past),
                  pl.BlockSpec((db, COL_TILE), lambda i: (0, T_QI)),
                  full(db, LANES), full(db, IDX_HEADS)],
        out_specs=[full(db, past), full(db, LANES)],
        out_shape=[jax.ShapeDtypeStruct((db, past), F32), jax.ShapeDtypeStruct((db, LANES), F32)],
        compiler_params=_cparams(("arbitrary",)),
        name="select_sample",
    )(scores, h_main, kie, ws)


def _sample_compact_body(m_ref, mt_ref, pt_ref, idx_ref, row_ref, *, n_slots):
    n_pages = m_ref.shape[1]
    for q in range(m_ref.shape[0]):
        _compact_one(m_ref.at[q], mt_ref.at[q], pt_ref.at[q], idx_ref.at[q], row_ref.at[q],
                     n_pages=n_pages, n_slots=n_slots)


def _compact_one(m_ref, mt_ref, pt_ref, idx_ref, row_ref, *, n_pages, n_slots):
    pt = jnp.broadcast_to(pt_ref[...], (8, n_pages))
    pt_hi = (pt // PAGE_SIZE).astype(F32).astype(BF16)
    pt_lo = (pt % PAGE_SIZE).astype(F32).astype(BF16)
    one = lambda pred: jnp.where(pred, 1.0, 0.0)
    kept = m_ref[...] == 0.0
    kept_t = mt_ref[...] == 0.0
    ri = lax.broadcasted_iota(jnp.int32, (PAGE_SIZE, PAGE_SIZE), 0)
    ci = lax.broadcasted_iota(jnp.int32, (PAGE_SIZE, PAGE_SIZE), 1)
    rp = lax.broadcasted_iota(jnp.int32, (n_pages, n_pages), 0)
    cp = lax.broadcasted_iota(jnp.int32, (n_pages, n_pages), 1)
    plt = jnp.dot(one(ci <= ri).astype(BF16), one(kept_t).astype(BF16), preferred_element_type=F32)
    n_row = plt[PAGE_SIZE - 1:PAGE_SIZE, :]
    n_col = _lane_bcast(jnp.sum(one(kept), axis=1, keepdims=True), n_pages)
    e_col = jnp.dot(one(cp <= rp).astype(BF16), n_col.astype(BF16), preferred_element_type=F32)
    n_row8 = jnp.broadcast_to(n_row, (8, n_pages))
    e_row8 = jnp.dot(n_row8.astype(BF16), one(rp <= cp).astype(BF16), preferred_element_type=F32)
    off_row8 = e_row8 - n_row8
    n_total = e_col[n_pages - 1:n_pages, :]
    page_id = lax.broadcasted_iota(jnp.int32, (n_pages, LANES), 0).astype(F32)
    for jt in range(n_slots // LANES):
        j = (lax.broadcasted_iota(jnp.int32, (1, LANES), 1) + jt * LANES).astype(F32)
        page_j = jnp.sum(one(e_col <= j), axis=0, keepdims=True)
        pick = one(page_id == page_j).astype(BF16)
        prefix_j = jnp.dot(plt.astype(BF16), pick, preferred_element_type=F32)
        off_j = jnp.dot(off_row8.astype(BF16), pick, preferred_element_type=F32)[0:1]
        local_j = jnp.sum(one(prefix_j <= j - off_j), axis=0, keepdims=True)
        pos = page_j * float(PAGE_SIZE) + local_j
        phys = (jnp.dot(pt_hi, pick, preferred_element_type=F32)[0:1] * float(PAGE_SIZE)
                + jnp.dot(pt_lo, pick, preferred_element_type=F32)[0:1])
        row = phys * float(PAGE_SIZE) + local_j
        used = j < n_total
        sl = slice(jt * LANES, (jt + 1) * LANES)
        idx_ref[:, sl] = jnp.where(used, pos, -1.0).astype(jnp.int32)
        row_ref[:, sl] = jnp.where(used, row, 0.0).astype(jnp.int32)


def _sample_compact(bias3, bias3_t, pt3, *, n_slots):
    db, n_pages, _ = bias3.shape
    per = COMPACT_SEQS if db % COMPACT_SEQS == 0 else 1
    out = pl.BlockSpec((per, 1, n_slots), lambda b: (b, 0, 0))
    return pl.pallas_call(
        functools.partial(_sample_compact_body, n_slots=n_slots),
        grid=(db // per,),
        in_specs=[pl.BlockSpec((per, n_pages, PAGE_SIZE), lambda b: (b, 0, 0)),
                  pl.BlockSpec((per, PAGE_SIZE, n_pages), lambda b: (b, 0, 0)),
                  pl.BlockSpec((per, 1, n_pages), lambda b: (b, 0, 0))],
        out_specs=[out, out],
        out_shape=[jax.ShapeDtypeStruct((db, 1, n_slots), jnp.int32)] * 2,
        compiler_params=_cparams(("arbitrary",)),
        name="compact_sample",
    )(bias3, bias3_t, pt3)


def _sample_attn_body(row_ref, q_ref, slot_ref, bnew_ref, kn_ref, vn_ref, sz_ref, k_hbm, v_hbm, o_ref, *scratch):
    kbufs, vbufs, sem = scratch[:GATHER_BUFS], scratch[GATHER_BUFS:2 * GATHER_BUFS], scratch[-1]
    b = pl.program_id(0)
    nb = pl.num_programs(0)
    n_slots = kbufs[0].shape[0]
    ahead = GATHER_BUFS - 1
    grp = N_HEADS // N_KV_HEADS
    scale = HEAD_DIM ** -0.5

    def row_copies(seq, j, to):
        r = row_ref[seq, j]
        return (pltpu.make_async_copy(k_hbm.at[r], kbufs[to].at[j], sem.at[0, to]),
                pltpu.make_async_copy(v_hbm.at[r], vbufs[to].at[j], sem.at[1, to]))

    def start_all(seq, to, inline):
        def f(j, c):
            ck, cv = row_copies(seq, j, to)
            ck.start(priority=0)
            cv.start(priority=1)
            return c
        if inline:
            for j in range(n_slots):
                f(j, 0)
        else:
            lax.fori_loop(0, n_slots, f, 0, unroll=8)

    def wait_all(to):
        for blk in range(n_slots // PAGE_SIZE):
            rows = pl.ds(blk * PAGE_SIZE, PAGE_SIZE)
            src = pl.ds(0, PAGE_SIZE)
            pltpu.make_async_copy(k_hbm.at[src], kbufs[to].at[rows], sem.at[0, to]).wait()
            pltpu.make_async_copy(v_hbm.at[src], vbufs[to].at[rows], sem.at[1, to]).wait()

    @pl.when(b == 0)
    def _():
        for first in range(ahead):
            @pl.when(first < nb)
            def _(first=first):
                start_all(first, first, inline=False)

    for buf in range(GATHER_BUFS):
        mine = b % GATHER_BUFS == buf

        @pl.when(mine & (b + ahead < nb))
        def _(buf=buf):
            wait_all(buf)
            start_all(b + ahead, (buf + ahead) % GATHER_BUFS, inline=True)
            _sample_attend(kbufs[buf], vbufs[buf], q_ref, slot_ref, bnew_ref, kn_ref, vn_ref, sz_ref, o_ref)

        @pl.when(mine & (b + ahead >= nb))
        def _(buf=buf):
            wait_all(buf)
            _sample_attend(kbufs[buf], vbufs[buf], q_ref, slot_ref, bnew_ref, kn_ref, vn_ref, sz_ref, o_ref)


def _sample_attend(kbuf, vbuf, q_ref, slot_ref, bnew_ref, kn_ref, vn_ref, sz_ref, o_ref):
    n_slots = kbuf.shape[0]
    grp = N_HEADS // N_KV_HEADS
    scale = HEAD_DIM ** -0.5
    q = q_ref[0]
    head_s = lax.broadcasted_iota(jnp.int32, (N_HEADS, n_slots), 0)
    head_o = lax.broadcasted_iota(jnp.int32, (N_HEADS, HEAD_DIM), 0)
    k0, k1 = (kbuf[:, kh, :].astype(BF16) for kh in range(N_KV_HEADS))
    v0, v1 = (vbuf[:, kh, :].astype(BF16) for kh in range(N_KV_HEADS))
    s0 = lax.dot_general(q, k0, _NT, preferred_element_type=F32)
    s1 = lax.dot_general(q, k1, _NT, preferred_element_type=F32)
    s = jnp.where(head_s < grp, s0, s1) * scale + jnp.where(slot_ref[0] >= 0, 0.0, NEG)
    kn = kn_ref[0].astype(F32)
    vn = vn_ref[0].astype(F32)
    kn8 = jnp.where(head_o < grp, kn[:, :HEAD_DIM], kn[:, HEAD_DIM:])
    vn8 = jnp.where(head_o < grp, vn[:, :HEAD_DIM], vn[:, HEAD_DIM:])
    s_new = _lane_bcast(jnp.sum(q.astype(F32) * kn8, axis=-1, keepdims=True), N_HEADS) * scale + bnew_ref[0]
    m = jnp.maximum(_lane_bcast(jnp.max(s, axis=-1, keepdims=True), N_HEADS), s_new)
    p = jnp.exp(s - m[:, :1])
    p_new = jnp.exp(s_new - m)
    l = _lane_bcast(jnp.sum(p, axis=-1, keepdims=True), N_HEADS) + p_new
    pb = p.astype(BF16)
    pv = jnp.where(head_o < grp,
                   jnp.dot(pb, v0, preferred_element_type=F32),
                   jnp.dot(pb, v1, preferred_element_type=F32))
    o = (pv + p_new * vn8) / l
    o_ref[0] = (o * sz_ref[0].astype(F32)).astype(BF16)


def _sample_attn(rows, q3, slots3, bnew3, kn3, vn3, sz3, k_rows, v_rows):
    db, n_slots = rows.shape
    per_b = lambda r, c: pl.BlockSpec((1, r, c), lambda b, rw: (b, 0, 0))
    grid_spec = pltpu.PrefetchScalarGridSpec(
        num_scalar_prefetch=1,
        grid=(db,),
        in_specs=[per_b(N_HEADS, HEAD_DIM), per_b(1, n_slots), per_b(1, LANES),
                  per_b(1, KV_WIDTH), per_b(1, KV_WIDTH), per_b(N_HEADS, HEAD_DIM),
                  pl.BlockSpec(memory_space=pl.ANY), pl.BlockSpec(memory_space=pl.ANY)],
        out_specs=per_b(N_HEADS, HEAD_DIM),
        scratch_shapes=[pltpu.VMEM((n_slots, N_KV_HEADS, HEAD_DIM), F32)] * (2 * GATHER_BUFS)
                       + [pltpu.SemaphoreType.DMA((2, GATHER_BUFS))],
    )
    return pl.pallas_call(
        _sample_attn_body,
        grid_spec=grid_spec,
        out_shape=jax.ShapeDtypeStruct((db, N_HEADS, HEAD_DIM), BF16),
        compiler_params=_cparams(("arbitrary",)),
        name="attn_sample",
    )(rows, q3, slots3, bnew3, kn3, vn3, sz3, k_rows, v_rows)


def _col_tile(h_main, t, n=1):
    return h_main[:, t * COL_TILE:(t + n) * COL_TILE]


def kernel(x_prompt, x_sample, cache_k, cache_v, cache_kidx, page_table, pre_g, w_in, a_ln_g, a_ln_b,
           a_ws, a_bs, w_oa, w_ob, w_out, post_g):
    batch, seq, _ = x_prompt.shape
    db, ds, _ = x_sample.shape
    depth = w_in.shape[0]
    n_pages = page_table.shape[1]
    past = n_pages * PAGE_SIZE
    assert ds == 1 and seq % PROJ_ROWS == 0 and past % SCORE_CHUNK == 0 and Q_BLOCK == LANES
    top_k_s = min(TOPK_MAX, (past + ds) // 4)
    assert top_k_s <= past and top_k_s % LANES == 0

    pos_p = jnp.arange(seq)
    pos_s = past + (jnp.arange(db * ds) % ds)
    tabs_p = _rope_tables(pos_p, HEAD_DIM) + _rope_tables(pos_p, IDX_DIM)
    tabs_s = _rope_tables(pos_s, HEAD_DIM) + _rope_tables(pos_s, IDX_DIM)

    hp = x_prompt.reshape(batch * seq, D_MODEL)
    hs = x_sample.reshape(db * ds, D_MODEL)
    outs = [[] for _ in range(8)]
    for l in range(depth):
        w_t = w_in[l].T
        woa, wob, wout = w_oa[l].astype(BF16), w_ob[l].astype(BF16), w_out[l].astype(BF16)
        g_pre, g_post = pre_g[l][None], post_g[l][None]
        ln_g, ln_b = a_ln_g[l][None], a_ln_b[l][None]

        xn, k, v, ki, kb, kie, kio, vb_t, ws_t = _proj_tail(hp, g_pre, w_t, *tabs_p, tm=PROJ_ROWS, seq_rows=seq,
                                                            transposed=True)
        h_main, gv, w_tiles = _proj_main(xn, w_t, *tabs_p, ln_g, ln_b, tm=PROJ_ROWS, seq_rows=seq,
                                         gv_rows=CHUNK, cast_weights=True)
        ob = _prompt_attn(h_main, ws_t, kie, kio, kb, vb_t, batch=batch, seq=seq)
        hp = _merge(hp, None, ob, h_main, woa, wob, wout, g_post, tm=MERGE_ROWS, gate=(a_ws[l], a_bs[l].T))
        outs[0].append(k.reshape(batch, seq, N_KV_HEADS, HEAD_DIM))
        outs[1].append(v.reshape(batch, seq, N_KV_HEADS, HEAD_DIM))
        outs[2].append(ki.reshape(batch, seq, IDX_DIM))
        outs[3].append(gv.reshape(batch, CHUNK, A_WIDTH))

        m_s = db * ds
        xn, k, v, ki, kb, kie, kio, vb, ws = _proj_tail(hs, g_pre, w_t, *tabs_s, tm=m_s, seq_rows=m_s,
                                                        transposed=False)
        h_main, gv = _proj_main(xn, w_tiles, *tabs_s, ln_g, ln_b, tm=m_s, seq_rows=m_s, gv_rows=m_s,
                                cast_weights=False)
        w0 = jnp.repeat(a_ws[l][:, 0, 0], LANES)[None]
        b0 = jnp.repeat(a_bs[l][:, 0], LANES)[None]
        ya = _gate_row(h_main, gv, w0, b0)
        qi3 = _col_tile(h_main, T_QI).reshape(db, IDX_HEADS, IDX_DIM)
        kidx_t = jnp.swapaxes(cache_kidx[l], 1, 2)
        scores = _sample_scores(page_table, qi3, ws.reshape(db, IDX_HEADS, 1), kidx_t)
        bias, bnew = _sample_select(scores, h_main, kie, ws, top_k=top_k_s)
        bias3 = bias.reshape(db, n_pages, PAGE_SIZE)
        slots3, rows3 = _sample_compact(bias3, jnp.swapaxes(bias3, 1, 2),
                                        page_table.reshape(db, 1, n_pages), n_slots=top_k_s)
        pool_rows = lambda c: c.reshape(-1, N_KV_HEADS, HEAD_DIM)
        ob = _sample_attn(rows3.reshape(db, top_k_s),
                          _col_tile(h_main, T_Q).reshape(db, N_HEADS, HEAD_DIM),
                          slots3, bnew.reshape(db, 1, LANES),
                          kb.reshape(db, 1, KV_WIDTH), vb.reshape(db, 1, KV_WIDTH),
                          _col_tile(h_main, T_ZB).reshape(db, N_HEADS, HEAD_DIM),
                          pool_rows(cache_k[l]), pool_rows(cache_v[l])).reshape(db, B_WIDTH)
        hs = _merge(hs, ya, ob, h_main, woa, wob, wout, g_post, tm=m_s)
        outs[4].append(k.reshape(db, ds, N_KV_HEADS, HEAD_DIM))
        outs[5].append(v.reshape(db, ds, N_KV_HEADS, HEAD_DIM))
        outs[6].append(ki.reshape(db, ds, IDX_DIM))
        outs[7].append(gv.reshape(db, ds, A_WIDTH))

    st = [jnp.stack(o, axis=0) for o in outs]
    return (hp.reshape(batch, seq, D_MODEL), hs.reshape(db, ds, D_MODEL),
            st[0], st[1], st[2], st[3], st[4], st[5], st[6], st[7])
```

```python
import functools

import jax
import jax.numpy as jnp
from jax import lax
from jax.experimental import pallas as pl
from jax.experimental.pallas import tpu as pltpu

F32 = jnp.float32
BF16 = jnp.bfloat16

D_MODEL = 2048
CHUNK = 128
A_GROUPS = 8
A_WIDTH = 1024
N_HEADS = 8
N_KV_HEADS = 2
HEAD_DIM = 128
B_WIDTH = 1024
KV_WIDTH = 256
IDX_HEADS = 16
IDX_DIM = 64
TOPK_MAX = 256
ROPE_THETA = 10000.0
EPS = 1e-6
PAGE_SIZE = 128
Q_BLOCK = 128
NEG = -1e30
LOG2_E = 1.4426950408889634

LANES = 128
BF16_SUBLANES = 16
MXU_WIDTH = 256
KEY_STEP = 256
PROJ_ROWS = 512
MERGE_ROWS = 256
NORM_STREAMS = 4
_NT = (((1,), (1,)), ((), ()))
COL_TILE = 1024
T_UA, T_VA, T_ZA, T_Q, T_ZB, T_QI, T_GA, T_GB = 0, 1, 2, 3, 4, 5, 6, 8
N_MAIN_TILES = 10
ROW_K = 4 * COL_TILE
ROW_ZB = ROW_K + 2 * KV_WIDTH
ROW_KI = ROW_ZB + B_WIDTH + IDX_HEADS * IDX_DIM
ROW_GA = ROW_KI + IDX_DIM + IDX_HEADS
VMEM_LIMIT = 56 * 1024 * 1024


def _cparams(sem):
    return pltpu.CompilerParams(dimension_semantics=sem, vmem_limit_bytes=VMEM_LIMIT)


def _rope_tables(pos, dim):
    half = dim // 2
    inv = ROPE_THETA ** (-jnp.arange(half, dtype=F32) / half)
    ang = pos.astype(F32)[:, None] * inv[None, :]
    cos = jnp.cos(ang)
    sin = jnp.sin(ang)
    reps = LANES // dim
    cos_t = jnp.tile(jnp.concatenate([cos, cos], axis=-1), (1, reps))
    sin_t = jnp.tile(jnp.concatenate([-sin, sin], axis=-1), (1, reps))
    return cos_t, sin_t


def _rope128(x, cos, sin):
    return x * cos + pltpu.roll(x, 64, 1) * sin


def _rope64(x, cos, sin):
    lane = lax.broadcasted_iota(jnp.int32, x.shape, 1)
    first = (lane % IDX_DIM) < (IDX_DIM // 2)
    partner = jnp.where(first, pltpu.roll(x, LANES - 32, 1), pltpu.roll(x, 32, 1))
    return x * cos + partner * sin


def _rmsnorm_rows(xf, g):
    ms = jnp.mean(xf * xf, axis=-1, keepdims=True)
    return xf * lax.rsqrt(ms + EPS) * g


_TILE_KINDS = ("copy", "ln", "silu", "rope128", "silu", "rope64", "sigmoid", "sigmoid", "sigmoid", "sigmoid")


def _sigmoid(x):
    return 0.5 * jnp.tanh(0.5 * x) + 0.5


def _prenorm_rows(x_refs, g_ref, o_ref):
    xs = [r[...] for r in x_refs]
    ms = sum(jnp.sum(x * x, axis=-1, keepdims=True) for x in xs) * (1.0 / D_MODEL)
    scale = lax.rsqrt(ms + EPS)
    w = xs[0].shape[1]
    for c, x in enumerate(xs):
        cs = slice(c * w, (c + 1) * w)
        o_ref[:, cs] = (x * scale * g_ref[:, cs]).astype(BF16)


def _proj_main_body(xn_ref, *refs, gv_rows, n_row_tiles, cast_weights):
    n_chunks = COL_TILE // MXU_WIDTH
    w_refs, refs = refs[:n_chunks], refs[n_chunks:]
    cq_ref, sq_ref, ci_ref, si_ref, lng_ref, lnb_ref, h_ref, gv_ref = refs[:8]
    wbf_ref = refs[8] if cast_weights else None
    acc_ref = refs[-1]
    j = pl.program_id(0)
    i = pl.program_id(1)
    tm = xn_ref.shape[0]
    chunks = [slice(c * MXU_WIDTH, (c + 1) * MXU_WIDTH) for c in range(n_chunks)]
    weights = lambda c: wbf_ref[chunks[c], :] if cast_weights else w_refs[c][...]

    def finish_chunk(kind, cs, stats):
        acc = acc_ref[:, cs]
        if kind == "copy":
            h_ref[:, cs] = acc.astype(BF16)
        elif kind == "ln":
            mu, rstd = stats
            vn = (acc - mu) * rstd * lng_ref[:, cs] + lnb_ref[:, cs]
            h_ref[:, cs] = vn.astype(BF16)
            gv_ref[:, cs] = vn[tm - gv_rows:, :]
        elif kind == "silu":
            h_ref[:, cs] = (acc * _sigmoid(acc)).astype(BF16)
        elif kind == "sigmoid":
            h_ref[:, cs] = _sigmoid(acc).astype(BF16)
        else:
            rope, cos, sin = ((_rope128, cq_ref[...], sq_ref[...]) if kind == "rope128"
                              else (_rope64, ci_ref[...], si_ref[...]))
            for h in range(MXU_WIDTH // LANES):
                sl = slice(cs.start + h * LANES, cs.start + (h + 1) * LANES)
                h_ref[:, sl] = rope(acc[:, h * LANES:(h + 1) * LANES], cos, sin).astype(BF16)

    def run(kind, finish, matmul):
        stats = None
        if finish and kind == "ln":
            acc = acc_ref[...]
            mu = jnp.mean(acc, axis=-1, keepdims=True)
            d = acc - mu
            stats = (mu, lax.rsqrt(jnp.mean(d * d, axis=-1, keepdims=True) + EPS))
        for c, cs in enumerate(chunks):
            if finish:
                finish_chunk(kind, cs, stats)
            if matmul:
                acc_ref[:, cs] = lax.dot_general(xn_ref[...], weights(c), _NT, preferred_element_type=F32)

    @pl.when(i == 0)
    def _():
        if cast_weights:
            for c, cs in enumerate(chunks):
                wbf_ref[cs, :] = w_refs[c][...].astype(BF16)
        run(None, False, True)

    @pl.when((i == 0) & (j != T_VA))
    def _():
        gv_ref[...] = jnp.zeros(gv_ref.shape, F32)

    for kind in sorted(set(_TILE_KINDS)):
        is_kind = functools.reduce(jnp.logical_or, [j == t for t, k in enumerate(_TILE_KINDS) if k == kind])

        @pl.when(is_kind & (i > 0) & (i < n_row_tiles))
        def _(kind=kind):
            run(kind, True, True)

        @pl.when(is_kind & (i == n_row_tiles))
        def _(kind=kind):
            run(kind, True, False)


def _main_tile_row(j, chunk):
    g = BF16_SUBLANES
    skip_kv = (ROW_ZB - T_ZB * COL_TILE) // g
    skip_idx = (ROW_GA - ROW_ZB - (T_GA - T_ZB) * COL_TILE) // g
    return (j * (COL_TILE // g) + chunk * (MXU_WIDTH // g)
            + jnp.where(j >= T_ZB, skip_kv, 0) + jnp.where(j >= T_GA, skip_idx, 0)) * g


def _proj_main(xn, w, cq, sq, ci, si, ln_g, ln_b, *, tm, seq_rows, gv_rows, cast_weights):
    m = xn.shape[0]
    tiles_per_seq = seq_rows // tm
    n_seq = m // seq_rows
    n_row_tiles = m // tm
    n_chunks = COL_TILE // MXU_WIDTH
    body = functools.partial(_proj_main_body, gv_rows=gv_rows, n_row_tiles=n_row_tiles,
                             cast_weights=cast_weights)
    w_tile = lambda j, i: jnp.minimum(jnp.where(i == n_row_tiles, j + 1, j), N_MAIN_TILES - 1)
    if cast_weights:
        w_chunk = lambda c: pl.BlockSpec((pl.Element(MXU_WIDTH), pl.Element(D_MODEL)),
                                         lambda j, i: (_main_tile_row(w_tile(j, i), c), 0))
    else:
        w_chunk = lambda c: pl.BlockSpec((MXU_WIDTH, D_MODEL), lambda j, i: (w_tile(j, i) * n_chunks + c, 0))
    w_out_spec = [pl.BlockSpec((COL_TILE, D_MODEL), lambda j, i: (j, 0))] if cast_weights else []
    w_out_shape = [jax.ShapeDtypeStruct((N_MAIN_TILES * COL_TILE, D_MODEL), BF16)] if cast_weights else []
    prev = lambda i: jnp.maximum(i - 1, 0)
    tab = pl.BlockSpec((tm, LANES), lambda j, i: (prev(i) % tiles_per_seq, 0))
    row = lambda n: pl.BlockSpec((1, n), lambda j, i: (0, 0))
    gv_block = lambda j, i: (jnp.where(j == T_VA, prev(i) // tiles_per_seq, n_seq + (j > T_VA)), 0)
    h_main, gv, *w_bf = pl.pallas_call(
        body,
        grid=(N_MAIN_TILES, n_row_tiles + 1),
        in_specs=[pl.BlockSpec((tm, D_MODEL), lambda j, i: (jnp.minimum(i, n_row_tiles - 1), 0))]
                 + [w_chunk(c) for c in range(n_chunks)]
                 + [tab, tab, tab, tab, row(A_WIDTH), row(A_WIDTH)],
        out_specs=[
            pl.BlockSpec((tm, COL_TILE), lambda j, i: (prev(i), j)),
            pl.BlockSpec((gv_rows, A_WIDTH), gv_block),
        ] + w_out_spec,
        out_shape=[
            jax.ShapeDtypeStruct((m, N_MAIN_TILES * COL_TILE), BF16),
            jax.ShapeDtypeStruct(((n_seq + 2) * gv_rows, A_WIDTH), F32),
        ] + w_out_shape,
        scratch_shapes=[pltpu.VMEM((tm, COL_TILE), F32)],
        compiler_params=_cparams(("arbitrary", "arbitrary")),
        name="proj_main",
    )(xn, *([w] * n_chunks), cq, sq, ci, si, ln_g, ln_b)
    return (h_main, gv[:n_seq * gv_rows], *w_bf)


def _proj_tail_body(*refs, transposed):
    x_refs, refs = refs[:NORM_STREAMS], refs[NORM_STREAMS:]
    (g_ref, wkv_ref, wix_ref, ck_ref, sk_ref, ci_ref, si_ref,
     xn_ref, k_ref, v_ref, ki_ref, kb_ref, kie_ref, kio_ref, vb_ref, ws_ref) = refs
    _prenorm_rows(x_refs, g_ref, xn_ref)
    xn = xn_ref[...]
    acc = lax.dot_general(xn, wkv_ref[...].astype(BF16), _NT, preferred_element_type=F32)
    cos = ck_ref[...]
    sin = sk_ref[...]
    for kh in range(N_KV_HEADS):
        sl = slice(kh * HEAD_DIM, (kh + 1) * HEAD_DIM)
        r = _rope128(acc[:, sl], cos, sin)
        k_ref[:, kh, :] = r
        kb_ref[:, sl] = r.astype(BF16)
        v_ref[:, kh, :] = acc[:, KV_WIDTH + kh * HEAD_DIM:KV_WIDTH + (kh + 1) * HEAD_DIM]
    t = lax.dot_general(xn, wix_ref[...].astype(BF16), _NT, preferred_element_type=F32)
    r = _rope64(t, ci_ref[...], si_ref[...])
    ki_ref[...] = r[:, :IDX_DIM]
    lane = lax.broadcasted_iota(jnp.int32, r.shape, 1)
    ke = jnp.where(lane < IDX_DIM, r, 0.0)
    kie_ref[...] = ke.astype(BF16)
    kio_ref[...] = pltpu.roll(ke, IDX_DIM, 1).astype(BF16)
    w_scale = IDX_HEADS ** -0.5 * IDX_DIM ** -0.5
    v = acc[:, KV_WIDTH:2 * KV_WIDTH]
    if transposed:
        vb_ref[...] = v.T.astype(BF16)
        ws_ref[...] = t.T[IDX_DIM:IDX_DIM + IDX_HEADS, :] * w_scale
    else:
        vb_ref[...] = v.astype(BF16)
        ws_ref[...] = t[:, IDX_DIM:IDX_DIM + IDX_HEADS] * w_scale


def _proj_tail(x2, pre_g, w_t, ck, sk, ci, si, *, tm, seq_rows, transposed):
    m = x2.shape[0]
    x_col = lambda c: pl.BlockSpec((tm, D_MODEL // NORM_STREAMS), lambda i: (i, c))
    if transposed:
        vw_specs = [pl.BlockSpec((n, tm), lambda i: (0, i)) for n in (KV_WIDTH, IDX_HEADS)]
        vw_shapes = [jax.ShapeDtypeStruct((KV_WIDTH, m), BF16), jax.ShapeDtypeStruct((IDX_HEADS, m), F32)]
    else:
        vw_specs = [pl.BlockSpec((tm, n), lambda i: (i, 0)) for n in (KV_WIDTH, IDX_HEADS)]
        vw_shapes = [jax.ShapeDtypeStruct((m, KV_WIDTH), BF16), jax.ShapeDtypeStruct((m, IDX_HEADS), F32)]
    tiles_per_seq = seq_rows // tm
    tab = pl.BlockSpec((tm, LANES), lambda i: (i % tiles_per_seq, 0))
    blk = lambda n: pl.BlockSpec((tm, n), lambda i: (i, 0))
    kv_rows = pl.BlockSpec((tm, N_KV_HEADS, HEAD_DIM), lambda i: (i, 0, 0))
    return pl.pallas_call(
        functools.partial(_proj_tail_body, transposed=transposed),
        grid=(m // tm,),
        in_specs=[x_col(c) for c in range(NORM_STREAMS)] + [
            pl.BlockSpec((1, D_MODEL), lambda i: (0, 0)),
            pl.BlockSpec((2 * KV_WIDTH, D_MODEL), lambda i: (ROW_K // (2 * KV_WIDTH), 0)),
            pl.BlockSpec((LANES, D_MODEL), lambda i: (ROW_KI // LANES, 0)),
            tab, tab, tab, tab,
        ],
        out_specs=[blk(D_MODEL), kv_rows, kv_rows, blk(IDX_DIM), blk(KV_WIDTH),
                   blk(LANES), blk(LANES)] + vw_specs,
        out_shape=[
            jax.ShapeDtypeStruct((m, D_MODEL), BF16),
            jax.ShapeDtypeStruct((m, N_KV_HEADS, HEAD_DIM), F32),
            jax.ShapeDtypeStruct((m, N_KV_HEADS, HEAD_DIM), F32),
            jax.ShapeDtypeStruct((m, IDX_DIM), F32),
            jax.ShapeDtypeStruct((m, KV_WIDTH), BF16),
            jax.ShapeDtypeStruct((m, LANES), BF16),
            jax.ShapeDtypeStruct((m, LANES), BF16),
        ] + vw_shapes,
        compiler_params=_cparams(("arbitrary",)),
        name="proj_tail",
    )(*([x2] * NORM_STREAMS), pre_g, w_t, w_t, ck, sk, ci, si)


def _gate_body(u_ref, vn_ref, sz_ref, ws_ref, bst_ref, y_ref, *, n_chunks):
    rr = lax.broadcasted_iota(jnp.int32, (CHUNK, CHUNK), 0)
    cc = lax.broadcasted_iota(jnp.int32, (CHUNK, CHUNK), 1)
    tril = cc <= rr
    for g in range(A_GROUPS):
        wm = jnp.where(tril, ws_ref[g], 0.0).astype(BF16)
        b = bst_ref[:, g:g + 1]
        cs = slice(g * LANES, (g + 1) * LANES)
        for c in range(n_chunks):
            rs = slice(c * CHUNK, (c + 1) * CHUNK)
            s = jnp.dot(wm, vn_ref[rs, cs], preferred_element_type=F32) + b
            y = u_ref[rs, cs].astype(F32) * s * sz_ref[rs, cs].astype(F32)
            y_ref[rs, cs] = y.astype(BF16)


def _gate_row_body(u_ref, vn_ref, sz_ref, w0_ref, b0_ref, y_ref):
    s = vn_ref[...] * w0_ref[...] + b0_ref[...]
    y_ref[...] = (u_ref[...].astype(F32) * s * sz_ref[...].astype(F32)).astype(BF16)


def _gate_row(h_main, vn, w0, b0):
    m = h_main.shape[0]
    col = lambda t: pl.BlockSpec((m, COL_TILE), lambda i, t=t: (0, t))
    full = lambda r: pl.BlockSpec((r, A_WIDTH), lambda i: (0, 0))
    return pl.pallas_call(
        _gate_row_body,
        grid=(1,),
        in_specs=[col(T_UA), full(m), col(T_ZA), full(1), full(1)],
        out_specs=full(m),
        out_shape=jax.ShapeDtypeStruct((m, A_WIDTH), BF16),
        compiler_params=_cparams(("arbitrary",)),
        name="gate_sample",
    )(h_main, vn, h_main, w0, b0)


def _lane_bcast(col, rows):
    return jnp.broadcast_to(col, (rows, LANES))


def _select_bias(sc_ref, extra_ref, kp, row_min, row_max, *, n_cols):
    rows = sc_ref.shape[0]
    n_tiles = n_cols // LANES
    extra = None if extra_ref is None else extra_ref[...]

    def count(pred):
        acc = jnp.zeros((rows, LANES), F32)
        for c in range(n_tiles):
            acc = acc + jnp.where(pred(sc_ref[:, c * LANES:(c + 1) * LANES], c), 1.0, 0.0)
        tot = jnp.sum(acc, axis=1, keepdims=True)
        return _lane_bcast(tot, rows)

    def count_ge(x):
        c = count(lambda s, _: s >= x)
        if extra is not None:
            c = c + jnp.where(extra >= x, 1.0, 0.0)
        return c

    c_max = count_ge(row_max)
    top = c_max >= kp
    lo0 = jnp.where(top, row_max, row_min)
    c0 = jnp.where(top, c_max, count_ge(row_min))

    def count_ge3(x1, x2, x3):
        a1 = jnp.zeros((rows, LANES), F32)
        a2 = jnp.zeros((rows, LANES), F32)
        a3 = jnp.zeros((rows, LANES), F32)
        for c in range(n_tiles):
            s = sc_ref[:, c * LANES:(c + 1) * LANES]
            a1 = a1 + jnp.where(s >= x1, 1.0, 0.0)
            a2 = a2 + jnp.where(s >= x2, 1.0, 0.0)
            a3 = a3 + jnp.where(s >= x3, 1.0, 0.0)
        res = []
        for a, x in ((a1, x1), (a2, x2), (a3, x3)):
            tot = _lane_bcast(jnp.sum(a, axis=1, keepdims=True), rows)
            if extra is not None:
                tot = tot + jnp.where(extra >= x, 1.0, 0.0)
            res.append(tot)
        return res

    def step(st):
        lo, hi, c_lo, _, it = st
        mid = 0.5 * lo + 0.5 * hi
        act = (c_lo != kp) & (mid > lo) & (mid < hi)
        any_act = jnp.max(jnp.where(act, 1.0, 0.0))
        clamp = lambda x: jnp.minimum(jnp.maximum(x, lo), hi)
        q1 = clamp(0.75 * lo + 0.25 * hi)
        q3 = clamp(0.25 * lo + 0.75 * hi)
        c1, c2, c3 = count_ge3(q1, mid, q3)
        g1, g2, g3 = c1 >= kp, c2 >= kp, c3 >= kp
        lo_n = jnp.where(g3, q3, jnp.where(g2, mid, jnp.where(g1, q1, lo)))
        c_n = jnp.where(g3, c3, jnp.where(g2, c2, jnp.where(g1, c1, c_lo)))
        hi_n = jnp.where(g3, hi, jnp.where(g2, q3, jnp.where(g1, mid, jnp.minimum(q1, mid))))
        return (jnp.where(act, lo_n, lo), jnp.where(act, hi_n, hi), jnp.where(act, c_n, c_lo),
                any_act, it + 1)

    def cond(st):
        return (st[3] > 0.0) & (st[4] < 400)

    lo, _, c_lo, _, _ = lax.while_loop(cond, step, (lo0, row_max, c0, jnp.float32(1.0), jnp.int32(0)))

    exact = jnp.max(jnp.where(c_lo != kp, 1.0, 0.0)) == 0.0

    @pl.when(exact)
    def _():
        for c in range(n_tiles):
            sl = slice(c * LANES, (c + 1) * LANES)
            sc_ref[:, sl] = jnp.where(sc_ref[:, sl] >= lo, 0.0, NEG)
        if extra is not None:
            extra_ref[...] = jnp.where(extra >= lo, 0.0, NEG)

    @pl.when(jnp.logical_not(exact))
    def _():
        n_gt = count(lambda s, _: s > lo)
        if extra is not None:
            n_gt = n_gt + jnp.where(extra > lo, 1.0, 0.0)
        need = kp - n_gt
        lane = lax.broadcasted_iota(jnp.int32, (rows, LANES), 1).astype(F32)

        def count_eq_upto(jx):
            c = count(lambda s, c: (s == lo) & (lane + float(c * LANES) <= jx))
            if extra is not None:
                c = c + jnp.where((extra == lo) & (jx >= float(n_cols)), 1.0, 0.0)
            return c

        last = n_cols if extra is not None else n_cols - 1
        j_lo = jnp.full((rows, LANES), -1.0, F32)
        j_hi = jnp.full((rows, LANES), float(last), F32)

        def jstep(_, st):
            a, b = st
            mid = jnp.floor(0.5 * (a + b))
            ok = count_eq_upto(mid) >= need
            return jnp.where(ok, a, mid), jnp.where(ok, mid, b)

        n_steps = max(1, (n_cols + 1).bit_length())
        _, j_hi = lax.fori_loop(0, n_steps, jstep, (j_lo, j_hi))
        for c in range(n_tiles):
            sl = slice(c * LANES, (c + 1) * LANES)
            s = sc_ref[:, sl]
            keep = (s > lo) | ((s == lo) & (lane + float(c * LANES) <= j_hi))
            sc_ref[:, sl] = jnp.where(keep, 0.0, NEG)
        if extra is not None:
            keep = (extra > lo) | ((extra == lo) & (j_hi >= float(n_cols)))
            extra_ref[...] = jnp.where(keep, 0.0, NEG)


def _select_bias_cols(sc_ref, kp, col_min, col_max, *, n_rows):
    n_tiles = n_rows // LANES
    n_pivots = 3 if n_rows <= 512 else 2 if n_rows <= 1024 else 1

    def counts(preds):
        accs = [jnp.zeros((LANES, LANES), F32) for _ in preds]
        for r in range(n_tiles):
            s = sc_ref[r * LANES:(r + 1) * LANES, :]
            accs = [a + jnp.where(p(s, r), 1.0, 0.0) for a, p in zip(accs, preds)]
        return [jnp.sum(a, axis=0, keepdims=True) for a in accs]

    count = lambda pred: counts([pred])[0]
    count_ge = lambda x: count(lambda s, _: s >= x)
    c_max = count_ge(col_max)
    top = c_max >= kp
    lo0 = jnp.where(top, col_max, col_min)
    c0 = jnp.where(top, c_max, count_ge(col_min))

    def step(st):
        lo, hi, c_lo, _, it = st
        mid = 0.5 * lo + 0.5 * hi
        act = (c_lo != kp) & (mid > lo) & (mid < hi)
        any_act = jnp.max(jnp.where(act, 1.0, 0.0))
        fr = [(k + 1) / (n_pivots + 1) for k in range(n_pivots)]
        piv = [mid if f == 0.5 else jnp.minimum(jnp.maximum((1.0 - f) * lo + f * hi, lo), hi) for f in fr]
        cs = counts([lambda s, _, x=x: s >= x for x in piv])
        lo_n, c_n, hi_n = lo, c_lo, functools.reduce(jnp.minimum, piv)
        for k in range(n_pivots):
            ge = cs[k] >= kp
            nxt = piv[k + 1] if k + 1 < n_pivots else hi
            lo_n = jnp.where(ge, piv[k], lo_n)
            c_n = jnp.where(ge, cs[k], c_n)
            hi_n = jnp.where(ge, nxt, hi_n)
        return jnp.where(act, lo_n, lo), jnp.where(act, hi_n, hi), jnp.where(act, c_n, c_lo), any_act, it + 1

    def cond(st):
        return (st[3] > 0.0) & (st[4] < 400)

    lo, _, c_lo, _, _ = lax.while_loop(cond, step, (lo0, col_max, c0, jnp.float32(1.0), jnp.int32(0)))
    exact = jnp.max(jnp.where(c_lo != kp, 1.0, 0.0)) == 0.0

    @pl.when(exact)
    def _():
        for r in range(n_tiles):
            rs = slice(r * LANES, (r + 1) * LANES)
            sc_ref[rs, :] = jnp.where(sc_ref[rs, :] >= lo, 0.0, NEG)

    @pl.when(jnp.logical_not(exact))
    def _():
        need = kp - count(lambda s, _: s > lo)
        key = lax.broadcasted_iota(jnp.int32, (LANES, LANES), 0).astype(F32)
        count_eq_upto = lambda jx: count(lambda s, r: (s == lo) & (key + float(r * LANES) <= jx))

        def jstep(_, st):
            a, b = st
            mid = jnp.floor(0.5 * (a + b))
            ok = count_eq_upto(mid) >= need
            return jnp.where(ok, a, mid), jnp.where(ok, mid, b)

        j_lo = jnp.full((1, LANES), -1.0, F32)
        j_hi = jnp.full((1, LANES), float(n_rows - 1), F32)
        _, j_hi = lax.fori_loop(0, max(1, n_rows.bit_length()), jstep, (j_lo, j_hi))
        for r in range(n_tiles):
            rs = slice(r * LANES, (r + 1) * LANES)
            s = sc_ref[rs, :]
            keep = (s > lo) | ((s == lo) & (key + float(r * LANES) <= j_hi))
            sc_ref[rs, :] = jnp.where(keep, 0.0, NEG)


def _prompt_attn_block(nk, qi_ref, wst_ref, kie_ref, kio_ref, q_ref, kb_ref, vt_ref, sz_ref,
                       o_ref, sc_ref, *, top_k, key_chunk):
    qb = pl.program_id(1)
    n_pairs = IDX_HEADS // 2
    grp = N_HEADS // N_KV_HEADS
    qpos = qb * Q_BLOCK + lax.broadcasted_iota(jnp.int32, (LANES, LANES), 1)
    key0 = lax.broadcasted_iota(jnp.int32, (LANES, LANES), 0)

    qs = jnp.concatenate([qi_ref[:, p * LANES:(p + 1) * LANES] for p in range(n_pairs)], axis=0)
    wrows = [wst_ref[h:h + 1, :] for h in range(IDX_HEADS)]
    cmax = jnp.full((LANES, LANES), -jnp.inf, F32)
    cmin = jnp.full((LANES, LANES), jnp.inf, F32)
    for kc in range(0, nk, key_chunk):
        le = lax.dot_general(kie_ref[kc:kc + key_chunk, :], qs, _NT, preferred_element_type=F32)
        lo = lax.dot_general(kio_ref[kc:kc + key_chunk, :], qs, _NT, preferred_element_type=F32)
        for r in range(key_chunk // LANES):
            rs = slice(r * LANES, (r + 1) * LANES)
            acc = jnp.zeros((LANES, LANES), F32)
            for p in range(n_pairs):
                cs = slice(p * LANES, (p + 1) * LANES)
                acc = acc + jnp.maximum(le[rs, cs], 0.0) * wrows[2 * p]
                acc = acc + jnp.maximum(lo[rs, cs], 0.0) * wrows[2 * p + 1]
            causal = key0 + (kc + r * LANES) <= qpos
            cmax = jnp.maximum(cmax, jnp.where(causal, acc, -jnp.inf))
            cmin = jnp.minimum(cmin, jnp.where(causal, acc, jnp.inf))
            sc_ref[kc + r * LANES:kc + (r + 1) * LANES, :] = jnp.where(causal, acc, -jnp.inf)

    col_max = jnp.max(cmax, axis=0, keepdims=True)
    col_min = jnp.min(cmin, axis=0, keepdims=True)
    if nk <= top_k:
        for r in range(nk // LANES):
            rs = slice(r * LANES, (r + 1) * LANES)
            sc_ref[rs, :] = jnp.where(key0 + r * LANES <= qpos, 0.0, NEG)
    else:
        kp = jnp.minimum(qpos[0:1, :] + 1, top_k).astype(F32)
        _select_bias_cols(sc_ref, kp, col_min, col_max, n_rows=nk)

    bias = jnp.concatenate([sc_ref[0:nk, :]] * grp, axis=1)
    for kh in range(N_KV_HEADS):
        qh = jnp.concatenate(
            [q_ref[:, (kh * grp + g) * HEAD_DIM:(kh * grp + g + 1) * HEAD_DIM] for g in range(grp)], axis=0)
        ks = slice(kh * HEAD_DIM, (kh + 1) * HEAD_DIM)
        s = lax.dot_general(kb_ref[0:nk, ks], qh, _NT, preferred_element_type=F32) + bias
        m = jnp.max(s, axis=0, keepdims=True)
        p = jnp.exp2((s - m) * (HEAD_DIM ** -0.5 * LOG2_E))
        l = jnp.sum(p, axis=0, keepdims=True)
        ot = jnp.dot(vt_ref[ks, 0:nk], p.astype(BF16), preferred_element_type=F32) / l
        for g in range(grp):
            hs = slice((kh * grp + g) * HEAD_DIM, (kh * grp + g + 1) * HEAD_DIM)
            o = ot[:, g * LANES:(g + 1) * LANES].T
            o_ref[:, hs] = (o * sz_ref[:, hs].astype(F32)).astype(BF16)


def _key_ranges(seq):
    fine = list(range(KEY_STEP, min(seq, 4 * KEY_STEP) + 1, KEY_STEP))
    return fine + list(range(fine[-1] + 2 * KEY_STEP, seq + 1, 2 * KEY_STEP))


def _prompt_attn_body(*refs, seq, top_k):
    need = (pl.program_id(1) + 1) * Q_BLOCK
    lower = 0
    for nk in _key_ranges(seq):
        @pl.when((need > lower) & (need <= nk))
        def _(nk=nk):
            _prompt_attn_block(nk, *refs, top_k=top_k, key_chunk=KEY_STEP)
        lower = nk


def _prompt_attn(h_main, ws_t, kie, kio, kb, vb_t, *, batch, seq):
    top_k = min(TOPK_MAX, seq // 4)
    n_qb = seq // Q_BLOCK
    body = functools.partial(_prompt_attn_body, seq=seq, top_k=top_k)
    col = lambda t: pl.BlockSpec((Q_BLOCK, COL_TILE), lambda b, q, t=t: (b * n_qb + q, t))
    seqblk = lambda n: pl.BlockSpec((seq, n), lambda b, q: (b, 0))
    return pl.pallas_call(
        body,
        grid=(batch, n_qb),
        in_specs=[col(T_QI),
                  pl.BlockSpec((IDX_HEADS, Q_BLOCK), lambda b, q: (0, b * n_qb + q)),
                  seqblk(LANES), seqblk(LANES),
                  col(T_Q), seqblk(KV_WIDTH),
                  pl.BlockSpec((KV_WIDTH, seq), lambda b, q: (0, b)),
                  col(T_ZB)],
        out_specs=pl.BlockSpec((Q_BLOCK, B_WIDTH), lambda b, q: (b * n_qb + q, 0)),
        out_shape=jax.ShapeDtypeStruct((batch * seq, B_WIDTH), BF16),
        scratch_shapes=[pltpu.VMEM((seq, Q_BLOCK), F32)],
        compiler_params=_cparams(("arbitrary", "arbitrary")),
        name="attn_prompt",
    )(h_main, ws_t, kie, kio, h_main, kb, vb_t, h_main)


def _merge_body(x_ref, *refs, fused_gate):
    if fused_gate:
        gate_refs, refs = refs[:5], refs[5:]
        ya_ref = refs[-1]
        _gate_body(*gate_refs, ya_ref, n_chunks=x_ref.shape[0] // CHUNK)
        ob_ref, ga_ref, gb_ref, woa_ref, wob_ref, wout_ref, pg_ref, o_ref = refs[:-1]
    else:
        ya_ref, ob_ref, ga_ref, gb_ref, woa_ref, wob_ref, wout_ref, pg_ref, o_ref = refs
    pa = jnp.dot(ya_ref[...], woa_ref[...], preferred_element_type=F32)
    pb = jnp.dot(ob_ref[...], wob_ref[...], preferred_element_type=F32)
    mix = ga_ref[...].astype(F32) * pa + gb_ref[...].astype(F32) * pb
    r = jnp.dot(mix.astype(BF16), wout_ref[...], preferred_element_type=F32)
    o_ref[...] = x_ref[...] + _rmsnorm_rows(r, pg_ref[...])


def _merge(x2, ya, ob, h_main, w_oa, w_ob, w_out, post_g, *, tm, gate=None):
    m = x2.shape[0]
    const = lambda r, c: pl.BlockSpec((r, c), lambda i: (0, 0), pipeline_mode=pl.Buffered(1))
    if gate is None:
        a_specs = [pl.BlockSpec((tm, A_WIDTH), lambda i: (i, 0))]
        a_args, scratch = [ya], []
    else:
        col = lambda t: pl.BlockSpec((tm, COL_TILE), lambda i, t=t: (i, t))
        a_specs = [col(T_UA), col(T_VA), col(T_ZA),
                   pl.BlockSpec((A_GROUPS, CHUNK, CHUNK), lambda i: (0, 0, 0)),
                   pl.BlockSpec((CHUNK, A_GROUPS), lambda i: (0, 0))]
        a_args, scratch = [h_main, h_main, h_main, *gate], [pltpu.VMEM((tm, A_WIDTH), BF16)]
    return pl.pallas_call(
        functools.partial(_merge_body, fused_gate=gate is not None),
        grid=(m // tm,),
        in_specs=[pl.BlockSpec((tm, D_MODEL), lambda i: (i, 0))] + a_specs + [
            pl.BlockSpec((tm, B_WIDTH), lambda i: (i, 0)),
            pl.BlockSpec((tm, D_MODEL), lambda i: (i, T_GA // 2)),
            pl.BlockSpec((tm, D_MODEL), lambda i: (i, T_GB // 2)),
            const(A_WIDTH, D_MODEL), const(B_WIDTH, D_MODEL), const(D_MODEL, D_MODEL),
            const(1, D_MODEL),
        ],
        out_specs=pl.BlockSpec((tm, D_MODEL), lambda i: (i, 0)),
        out_shape=jax.ShapeDtypeStruct((m, D_MODEL), F32),
        scratch_shapes=scratch,
        compiler_params=_cparams(("arbitrary",)),
        name="merge",
    )(x2, *a_args, ob, h_main, h_main, w_oa, w_ob, w_out, post_g)


SCORE_CHUNK = 2048
COMPACT_SEQS = 8
GATHER_BUFS = 3


def _sample_scores_body(pt_ref, q_ref, w_ref, kidx_hbm, o_ref, buf, sem):
    db, n_pages = pt_ref.shape
    past = n_pages * PAGE_SIZE

    def page_copy(b, p, slot):
        dst = buf.at[slot, :, pl.ds(pl.multiple_of(p * PAGE_SIZE, PAGE_SIZE), PAGE_SIZE)]
        return pltpu.make_async_copy(kidx_hbm.at[pt_ref[b, p]], dst, sem.at[slot])

    def start_all(b, slot):
        def f(t, c):
            page_copy(b, 2 * t, slot).start(priority=0)
            page_copy(b, 2 * t + 1, slot).start(priority=1)
            return c
        lax.fori_loop(0, n_pages // 2, f, 0, unroll=4)

    def wait_all(slot):
        for p in range(n_pages):
            dst = buf.at[slot, :, pl.ds(p * PAGE_SIZE, PAGE_SIZE)]
            pltpu.make_async_copy(kidx_hbm.at[0], dst, sem.at[slot]).wait()

    start_all(0, 0)

    def per_seq(b, c):
        slot = b % 2

        @pl.when(b + 1 < db)
        def _():
            start_all(b + 1, 1 - slot)

        wait_all(slot)
        q = q_ref[b]
        w = w_ref[b]
        for ch in range(past // SCORE_CHUNK):
            cs = slice(ch * SCORE_CHUNK, (ch + 1) * SCORE_CHUNK)
            logit = jnp.dot(q, buf[slot, :, cs].astype(BF16), preferred_element_type=F32)
            o_ref[pl.ds(b, 1), cs] = jnp.sum(jnp.maximum(logit, 0.0) * w, axis=0, keepdims=True)
        return c

    lax.fori_loop(0, db, per_seq, 0)


def _sample_scores(page_table, qi3, ws3, kidx_pages_t):
    db, n_pages = page_table.shape
    past = n_pages * PAGE_SIZE
    grid_spec = pltpu.PrefetchScalarGridSpec(
        num_scalar_prefetch=1,
        grid=(1,),
        in_specs=[pl.BlockSpec((db, IDX_HEADS, IDX_DIM), lambda i, pt: (0, 0, 0)),
                  pl.BlockSpec((db, IDX_HEADS, 1), lambda i, pt: (0, 0, 0)),
                  pl.BlockSpec(memory_space=pl.ANY)],
        out_specs=pl.BlockSpec((db, past), lambda i, pt: (0, 0)),
        scratch_shapes=[pltpu.VMEM((2, IDX_DIM, past), F32), pltpu.SemaphoreType.DMA((2,))],
    )
    return pl.pallas_call(
        _sample_scores_body,
        grid_spec=grid_spec,
        out_shape=jax.ShapeDtypeStruct((db, past), F32),
        compiler_params=_cparams(("arbitrary",)),
        name="scores_sample",
    )(page_table, qi3, ws3, kidx_pages_t)


def _sample_select_body(sc_ref, qi_ref, kie_ref, ws_ref, bias_ref, bnew_ref, *, top_k):
    rows, past = sc_ref.shape
    lane = lax.broadcasted_iota(jnp.int32, (rows, LANES), 1)
    ki = kie_ref[...].astype(F32)
    ki = ki + pltpu.roll(ki, IDX_DIM, 1)
    s_new = jnp.zeros((rows, 1), F32)
    for p in range(IDX_HEADS // 2):
        prod = qi_ref[:, p * LANES:(p + 1) * LANES].astype(F32) * ki
        l_even = jnp.sum(jnp.where(lane < IDX_DIM, prod, 0.0), axis=1, keepdims=True)
        l_odd = jnp.sum(jnp.where(lane >= IDX_DIM, prod, 0.0), axis=1, keepdims=True)
        s_new = s_new + jnp.maximum(l_even, 0.0) * ws_ref[:, 2 * p:2 * p + 1]
        s_new = s_new + jnp.maximum(l_odd, 0.0) * ws_ref[:, 2 * p + 1:2 * p + 2]
    extra = _lane_bcast(s_new, rows)
    bnew_ref[...] = extra
    rmax = extra
    rmin = extra
    for c in range(past // LANES):
        sl = slice(c * LANES, (c + 1) * LANES)
        s = sc_ref[:, sl]
        bias_ref[:, sl] = s
        rmax = jnp.maximum(rmax, s)
        rmin = jnp.minimum(rmin, s)
    row_max = _lane_bcast(jnp.max(rmax, axis=1, keepdims=True), rows)
    row_min = _lane_bcast(jnp.min(rmin, axis=1, keepdims=True), rows)
    kp = jnp.full((rows, LANES), float(top_k), F32)
    _select_bias(bias_ref, bnew_ref, kp, row_min, row_max, n_cols=past)


def _sample_select(scores, h_main, kie, ws, *, top_k):
    db, past = scores.shape
    full = lambda r, c: pl.BlockSpec((r, c), lambda i: (0, 0))
    return pl.pallas_call(
        functools.partial(_sample_select_body, top_k=top_k),
        grid=(1,),
        in_specs=[full(db, past),
                  pl.BlockSpec((db, COL_TILE), lambda i: (0, T_QI)),
                  full(db, LANES), full(db, IDX_HEADS)],
        out_specs=[full(db, past), full(db, LANES)],
        out_shape=[jax.ShapeDtypeStruct((db, past), F32), jax.ShapeDtypeStruct((db, LANES), F32)],
        compiler_params=_cparams(("arbitrary",)),
        name="select_sample",
    )(scores, h_main, kie, ws)


def _sample_compact_body(m_ref, mt_ref, pt_ref, idx_ref, row_ref, *, n_slots):
    n_pages = m_ref.shape[1]
    for q in range(m_ref.shape[0]):
        _compact_one(m_ref.at[q], mt_ref.at[q], pt_ref.at[q], idx_ref.at[q], row_ref.at[q],
                     n_pages=n_pages, n_slots=n_slots)


def _compact_one(m_ref, mt_ref, pt_ref, idx_ref, row_ref, *, n_pages, n_slots):
    pt = jnp.broadcast_to(pt_ref[...], (8, n_pages))
    pt_hi = (pt // PAGE_SIZE).astype(F32).astype(BF16)
    pt_lo = (pt % PAGE_SIZE).astype(F32).astype(BF16)
    one = lambda pred: jnp.where(pred, 1.0, 0.0)
    kept = m_ref[...] == 0.0
    kept_t = mt_ref[...] == 0.0
    ri = lax.broadcasted_iota(jnp.int32, (PAGE_SIZE, PAGE_SIZE), 0)
    ci = lax.broadcasted_iota(jnp.int32, (PAGE_SIZE, PAGE_SIZE), 1)
    rp = lax.broadcasted_iota(jnp.int32, (n_pages, n_pages), 0)
    cp = lax.broadcasted_iota(jnp.int32, (n_pages, n_pages), 1)
    plt = jnp.dot(one(ci <= ri).astype(BF16), one(kept_t).astype(BF16), preferred_element_type=F32)
    n_row = plt[PAGE_SIZE - 1:PAGE_SIZE, :]
    n_col = _lane_bcast(jnp.sum(one(kept), axis=1, keepdims=True), n_pages)
    e_col = jnp.dot(one(cp <= rp).astype(BF16), n_col.astype(BF16), preferred_element_type=F32)
    n_row8 = jnp.broadcast_to(n_row, (8, n_pages))
    e_row8 = jnp.dot(n_row8.astype(BF16), one(rp <= cp).astype(BF16), preferred_element_type=F32)
    off_row8 = e_row8 - n_row8
    n_total = e_col[n_pages - 1:n_pages, :]
    page_id = lax.broadcasted_iota(jnp.int32, (n_pages, LANES), 0).astype(F32)
    for jt in range(n_slots // LANES):
        j = (lax.broadcasted_iota(jnp.int32, (1, LANES), 1) + jt * LANES).astype(F32)
        page_j = jnp.sum(one(e_col <= j), axis=0, keepdims=True)
        pick = one(page_id == page_j).astype(BF16)
        prefix_j = jnp.dot(plt.astype(BF16), pick, preferred_element_type=F32)
        off_j = jnp.dot(off_row8.astype(BF16), pick, preferred_element_type=F32)[0:1]
        local_j = jnp.sum(one(prefix_j <= j - off_j), axis=0, keepdims=True)
        pos = page_j * float(PAGE_SIZE) + local_j
        phys = (jnp.dot(pt_hi, pick, preferred_element_type=F32)[0:1] * float(PAGE_SIZE)
                + jnp.dot(pt_lo, pick, preferred_element_type=F32)[0:1])
        row = phys * float(PAGE_SIZE) + local_j
        used = j < n_total
        sl = slice(jt * LANES, (jt + 1) * LANES)
        idx_ref[:, sl] = jnp.where(used, pos, -1.0).astype(jnp.int32)
        row_ref[:, sl] = jnp.where(used, row, 0.0).astype(jnp.int32)


def _sample_compact(bias3, bias3_t, pt3, *, n_slots):
    db, n_pages, _ = bias3.shape
    per = COMPACT_SEQS if db % COMPACT_SEQS == 0 else 1
    out = pl.BlockSpec((per, 1, n_slots), lambda b: (b, 0, 0))
    return pl.pallas_call(
        functools.partial(_sample_compact_body, n_slots=n_slots),
        grid=(db // per,),
        in_specs=[pl.BlockSpec((per, n_pages, PAGE_SIZE), lambda b: (b, 0, 0)),
                  pl.BlockSpec((per, PAGE_SIZE, n_pages), lambda b: (b, 0, 0)),
                  pl.BlockSpec((per, 1, n_pages), lambda b: (b, 0, 0))],
        out_specs=[out, out],
        out_shape=[jax.ShapeDtypeStruct((db, 1, n_slots), jnp.int32)] * 2,
        compiler_params=_cparams(("arbitrary",)),
        name="compact_sample",
    )(bias3, bias3_t, pt3)


def _sample_attn_body(row_ref, q_ref, slot_ref, bnew_ref, kn_ref, vn_ref, sz_ref, k_hbm, v_hbm, o_ref, *scratch):
    kbufs, vbufs, sem = scratch[:GATHER_BUFS], scratch[GATHER_BUFS:2 * GATHER_BUFS], scratch[-1]
    b = pl.program_id(0)
    nb = pl.num_programs(0)
    n_slots = kbufs[0].shape[0]
    ahead = GATHER_BUFS - 1
    grp = N_HEADS // N_KV_HEADS
    scale = HEAD_DIM ** -0.5

    def row_copies(seq, j, to):
        r = row_ref[seq, j]
        return (pltpu.make_async_copy(k_hbm.at[r], kbufs[to].at[j], sem.at[0, to]),
                pltpu.make_async_copy(v_hbm.at[r], vbufs[to].at[j], sem.at[1, to]))

    def start_all(seq, to, inline):
        def f(j, c):
            ck, cv = row_copies(seq, j, to)
            ck.start(priority=0)
            cv.start(priority=1)
            return c
        if inline:
            for j in range(n_slots):
                f(j, 0)
        else:
            lax.fori_loop(0, n_slots, f, 0, unroll=8)

    def wait_all(to):
        for blk in range(n_slots // PAGE_SIZE):
            rows = pl.ds(blk * PAGE_SIZE, PAGE_SIZE)
            src = pl.ds(0, PAGE_SIZE)
            pltpu.make_async_copy(k_hbm.at[src], kbufs[to].at[rows], sem.at[0, to]).wait()
            pltpu.make_async_copy(v_hbm.at[src], vbufs[to].at[rows], sem.at[1, to]).wait()

    @pl.when(b == 0)
    def _():
        for first in range(ahead):
            @pl.when(first < nb)
            def _(first=first):
                start_all(first, first, inline=False)

    for buf in range(GATHER_BUFS):
        mine = b % GATHER_BUFS == buf

        @pl.when(mine & (b + ahead < nb))
        def _(buf=buf):
            wait_all(buf)
            start_all(b + ahead, (buf + ahead) % GATHER_BUFS, inline=True)
            _sample_attend(kbufs[buf], vbufs[buf], q_ref, slot_ref, bnew_ref, kn_ref, vn_ref, sz_ref, o_ref)

        @pl.when(mine & (b + ahead >= nb))
        def _(buf=buf):
            wait_all(buf)
            _sample_attend(kbufs[buf], vbufs[buf], q_ref, slot_ref, bnew_ref, kn_ref, vn_ref, sz_ref, o_ref)


def _sample_attend(kbuf, vbuf, q_ref, slot_ref, bnew_ref, kn_ref, vn_ref, sz_ref, o_ref):
    n_slots = kbuf.shape[0]
    grp = N_HEADS // N_KV_HEADS
    scale = HEAD_DIM ** -0.5
    q = q_ref[0]
    head_s = lax.broadcasted_iota(jnp.int32, (N_HEADS, n_slots), 0)
    head_o = lax.broadcasted_iota(jnp.int32, (N_HEADS, HEAD_DIM), 0)
    k0, k1 = (kbuf[:, kh, :].astype(BF16) for kh in range(N_KV_HEADS))
    v0, v1 = (vbuf[:, kh, :].astype(BF16) for kh in range(N_KV_HEADS))
    s0 = lax.dot_general(q, k0, _NT, preferred_element_type=F32)
    s1 = lax.dot_general(q, k1, _NT, preferred_element_type=F32)
    s = jnp.where(head_s < grp, s0, s1) * scale + jnp.where(slot_ref[0] >= 0, 0.0, NEG)
    kn = kn_ref[0].astype(F32)
    vn = vn_ref[0].astype(F32)
    kn8 = jnp.where(head_o < grp, kn[:, :HEAD_DIM], kn[:, HEAD_DIM:])
    vn8 = jnp.where(head_o < grp, vn[:, :HEAD_DIM], vn[:, HEAD_DIM:])
    s_new = _lane_bcast(jnp.sum(q.astype(F32) * kn8, axis=-1, keepdims=True), N_HEADS) * scale + bnew_ref[0]
    m = jnp.maximum(_lane_bcast(jnp.max(s, axis=-1, keepdims=True), N_HEADS), s_new)
    p = jnp.exp(s - m[:, :1])
    p_new = jnp.exp(s_new - m)
    l = _lane_bcast(jnp.sum(p, axis=-1, keepdims=True), N_HEADS) + p_new
    pb = p.astype(BF16)
    pv = jnp.where(head_o < grp,
                   jnp.dot(pb, v0, preferred_element_type=F32),
                   jnp.dot(pb, v1, preferred_element_type=F32))
    o = (pv + p_new * vn8) / l
    o_ref[0] = (o * sz_ref[0].astype(F32)).astype(BF16)


def _sample_attn(rows, q3, slots3, bnew3, kn3, vn3, sz3, k_rows, v_rows):
    db, n_slots = rows.shape
    per_b = lambda r, c: pl.BlockSpec((1, r, c), lambda b, rw: (b, 0, 0))
    grid_spec = pltpu.PrefetchScalarGridSpec(
        num_scalar_prefetch=1,
        grid=(db,),
        in_specs=[per_b(N_HEADS, HEAD_DIM), per_b(1, n_slots), per_b(1, LANES),
                  per_b(1, KV_WIDTH), per_b(1, KV_WIDTH), per_b(N_HEADS, HEAD_DIM),
                  pl.BlockSpec(memory_space=pl.ANY), pl.BlockSpec(memory_space=pl.ANY)],
        out_specs=per_b(N_HEADS, HEAD_DIM),
        scratch_shapes=[pltpu.VMEM((n_slots, N_KV_HEADS, HEAD_DIM), F32)] * (2 * GATHER_BUFS)
                       + [pltpu.SemaphoreType.DMA((2, GATHER_BUFS))],
    )
    return pl.pallas_call(
        _sample_attn_body,
        grid_spec=grid_spec,
        out_shape=jax.ShapeDtypeStruct((db, N_HEADS, HEAD_DIM), BF16),
        compiler_params=_cparams(("arbitrary",)),
        name="attn_sample",
    )(rows, q3, slots3, bnew3, kn3, vn3, sz3, k_rows, v_rows)


def _col_tile(h_main, t, n=1):
    return h_main[:, t * COL_TILE:(t + n) * COL_TILE]


def kernel(x_prompt, x_sample, cache_k, cache_v, cache_kidx, page_table, pre_g, w_in, a_ln_g, a_ln_b,
           a_ws, a_bs, w_oa, w_ob, w_out, post_g):
    batch, seq, _ = x_prompt.shape
    db, ds, _ = x_sample.shape
    depth = w_in.shape[0]
    n_pages = page_table.shape[1]
    past = n_pages * PAGE_SIZE
    assert ds == 1 and seq % PROJ_ROWS == 0 and past % SCORE_CHUNK == 0 and Q_BLOCK == LANES
    top_k_s = min(TOPK_MAX, (past + ds) // 4)
    assert top_k_s <= past and top_k_s % LANES == 0

    pos_p = jnp.arange(seq)
    pos_s = past + (jnp.arange(db * ds) % ds)
    tabs_p = _rope_tables(pos_p, HEAD_DIM) + _rope_tables(pos_p, IDX_DIM)
    tabs_s = _rope_tables(pos_s, HEAD_DIM) + _rope_tables(pos_s, IDX_DIM)

    hp = x_prompt.reshape(batch * seq, D_MODEL)
    hs = x_sample.reshape(db * ds, D_MODEL)
    outs = [[] for _ in range(8)]
    for l in range(depth):
        w_t = w_in[l].T
        woa, wob, wout = w_oa[l].astype(BF16), w_ob[l].astype(BF16), w_out[l].astype(BF16)
        g_pre, g_post = pre_g[l][None], post_g[l][None]
        ln_g, ln_b = a_ln_g[l][None], a_ln_b[l][None]

        xn, k, v, ki, kb, kie, kio, vb_t, ws_t = _proj_tail(hp, g_pre, w_t, *tabs_p, tm=PROJ_ROWS, seq_rows=seq,
                                                            transposed=True)
        h_main, gv, w_tiles = _proj_main(xn, w_t, *tabs_p, ln_g, ln_b, tm=PROJ_ROWS, seq_rows=seq,
                                         gv_rows=CHUNK, cast_weights=True)
        ob = _prompt_attn(h_main, ws_t, kie, kio, kb, vb_t, batch=batch, seq=seq)
        hp = _merge(hp, None, ob, h_main, woa, wob, wout, g_post, tm=MERGE_ROWS, gate=(a_ws[l], a_bs[l].T))
        outs[0].append(k.reshape(batch, seq, N_KV_HEADS, HEAD_DIM))
        outs[1].append(v.reshape(batch, seq, N_KV_HEADS, HEAD_DIM))
        outs[2].append(ki.reshape(batch, seq, IDX_DIM))
        outs[3].append(gv.reshape(batch, CHUNK, A_WIDTH))

        m_s = db * ds
        xn, k, v, ki, kb, kie, kio, vb, ws = _proj_tail(hs, g_pre, w_t, *tabs_s, tm=m_s, seq_rows=m_s,
                                                        transposed=False)
        h_main, gv = _proj_main(xn, w_tiles, *tabs_s, ln_g, ln_b, tm=m_s, seq_rows=m_s, gv_rows=m_s,
                                cast_weights=False)
        w0 = jnp.repeat(a_ws[l][:, 0, 0], LANES)[None]
        b0 = jnp.repeat(a_bs[l][:, 0], LANES)[None]
        ya = _gate_row(h_main, gv, w0, b0)
        qi3 = _col_tile(h_main, T_QI).reshape(db, IDX_HEADS, IDX_DIM)
        kidx_t = jnp.swapaxes(cache_kidx[l], 1, 2)
        scores = _sample_scores(page_table, qi3, ws.reshape(db, IDX_HEADS, 1), kidx_t)
        bias, bnew = _sample_select(scores, h_main, kie, ws, top_k=top_k_s)
        bias3 = bias.reshape(db, n_pages, PAGE_SIZE)
        slots3, rows3 = _sample_compact(bias3, jnp.swapaxes(bias3, 1, 2),
                                        page_table.reshape(db, 1, n_pages), n_slots=top_k_s)
        pool_rows = lambda c: c.reshape(-1, N_KV_HEADS, HEAD_DIM)
        ob = _sample_attn(rows3.reshape(db, top_k_s),
                          _col_tile(h_main, T_Q).reshape(db, N_HEADS, HEAD_DIM),
                          slots3, bnew.reshape(db, 1, LANES),
                          kb.reshape(db, 1, KV_WIDTH), vb.reshape(db, 1, KV_WIDTH),
                          _col_tile(h_main, T_ZB).reshape(db, N_HEADS, HEAD_DIM),
                          pool_rows(cache_k[l]), pool_rows(cache_v[l])).reshape(db, B_WIDTH)
        hs = _merge(hs, ya, ob, h_main, woa, wob, wout, g_post, tm=m_s)
        outs[4].append(k.reshape(db, ds, N_KV_HEADS, HEAD_DIM))
        outs[5].append(v.reshape(db, ds, N_KV_HEADS, HEAD_DIM))
        outs[6].append(ki.reshape(db, ds, IDX_DIM))
        outs[7].append(gv.reshape(db, ds, A_WIDTH))

    st = [jnp.stack(o, axis=0) for o in outs]
    return (hp.reshape(batch, seq, D_MODEL), hs.reshape(db, ds, D_MODEL),
            st[0], st[1], st[2], st[3], st[4], st[5], st[6], st[7])
```
